```python
import math
import jax, jax.numpy as jnp
from jax import lax
import numpy as np

D_MODEL = 2048
BATCH = 1
SEQ = 8192
DEPTH = 2
DEC_BATCH = 128
DEC_SEQ = 1
PAST_LEN = 2048
PAGE_SIZE = 128

RW_HEAD_DIM = 64
RW_WIDTH = D_MODEL // 2
RW_HEADS = RW_WIDTH // RW_HEAD_DIM
DECAY_LORA = 64
AAA_LORA = 64
GATE_LORA = 160
RW_COLS = 3 * RW_WIDTH + DECAY_LORA + AAA_LORA + GATE_LORA
GN_EPS = 64e-5
MB_HEAD_DIM = 64
MB_WIDTH = D_MODEL // 4
MB_HEADS = MB_WIDTH // MB_HEAD_DIM
MB_BLOCK = 256
MB_TOPK = 3
Q_BLOCK = 64
NUM_BUCKETS = 32
MAX_DISTANCE = 128
NEG_INF = -1e30
CV_WIDTH = D_MODEL // 4
CV_K = 31
LN_EPS = 1e-5
N_BRANCH = 3
N_IN = RW_COLS + 3 * MB_WIDTH + 2 * CV_WIDTH + N_BRANCH * D_MODEL
D_FF = 256 * ((8 * D_MODEL // 3 + 255) // 256)
N_EXPERTS = 8
TOP_K = 2
D_FF_EXPERT = D_FF // TOP_K
N_DENSE = (DEPTH + 1) // 2
N_MOE = DEPTH // 2
NORM_EPS = 1e-6

kernel_name = 'hybrid_rwkv7_moba_conformer_decode_step'


def rms_norm(x, g):
    xf = x.astype(jnp.float32)
    y = xf * lax.rsqrt(jnp.mean(xf * xf, axis=-1, keepdims=True) + NORM_EPS)
    return (y * g.astype(jnp.float32)).astype(x.dtype)


def swiglu(x, wg, wu, wd):
    return (jax.nn.silu(x @ wg) * (x @ wu)) @ wd


def moe_swiglu(x, w_router, b_router, wg, wu, wd):
    logits = x.astype(jnp.float32) @ w_router.astype(jnp.float32) + b_router.astype(jnp.float32)
    top_v, top_i = lax.top_k(logits, TOP_K)
    wts = jax.nn.softmax(top_v, axis=-1)
    gate = jnp.sum(jax.nn.one_hot(top_i, N_EXPERTS, dtype=jnp.float32) * wts[..., None], axis=-2)
    out = jnp.zeros_like(x)
    for e in range(N_EXPERTS):
        out = out + gate[..., e:e + 1].astype(x.dtype) * swiglu(x, wg[e], wu[e], wd[e])
    return out


def rwkv7_mix(p_rw, shift0, wkv0, mu, w0, w2, a0, a2, g2, k_k, k_a, r_k, ln_w, ln_b):
    f32 = jnp.float32
    B, T, _ = p_rw.shape
    prev = jnp.concatenate([shift0[:, None, :].astype(p_rw.dtype), p_rw[:, :-1]], axis=1)
    xs = p_rw + (prev - p_rw) * mu
    c = 3 * RW_WIDTH
    r, k, v, wd, ad, gd = jnp.split(xs, [RW_WIDTH, 2 * RW_WIDTH, c, c + DECAY_LORA, c + DECAY_LORA + AAA_LORA], axis=-1)
    w_log = -jax.nn.softplus(-(w0 + jnp.tanh(wd) @ w2).astype(f32)) - 0.5
    decay = jnp.exp(-jnp.exp(w_log))
    a = jax.nn.sigmoid((a0 + ad @ a2).astype(f32))
    g = (jax.nn.sigmoid(gd) @ g2).astype(f32)

    def heads(t):
        return t.astype(f32).reshape(B, T, RW_HEADS, RW_HEAD_DIM)

    kk = heads(k * k_k)
    kk = kk / jnp.maximum(jnp.sqrt(jnp.sum(kk * kk, axis=-1, keepdims=True)), 1e-12)
    k_mod = k.astype(f32) * (1.0 + (a - 1.0) * k_a.astype(f32))
    rh, kh, vh, wh, ah = heads(r), heads(k_mod), heads(v), heads(decay), heads(a)

    def step(S, inp):
        r_t, w_t, k_t, v_t, kk_t, a_t = inp
        sa = jnp.einsum('bhvk,bhk->bhv', S, -kk_t)
        S = (S * w_t[:, :, None, :] + sa[..., None] * (kk_t * a_t)[:, :, None, :]
             + v_t[..., None] * k_t[:, :, None, :])
        return S, jnp.einsum('bhvk,bhk->bhv', S, r_t)

    tm = lambda t: jnp.swapaxes(t, 0, 1)
    S_T, ys = lax.scan(step, wkv0.astype(f32), (tm(rh), tm(wh), tm(kh), tm(vh), tm(kk), tm(ah)))
    y = tm(ys)
    mean = jnp.mean(y, axis=-1, keepdims=True)
    var = jnp.mean(jnp.square(y - mean), axis=-1, keepdims=True)
    y = ((y - mean) * lax.rsqrt(var + GN_EPS)).reshape(B, T, RW_WIDTH) * ln_w.astype(f32) + ln_b.astype(f32)
    bonus = (jnp.sum(rh * kh * r_k.astype(f32), axis=-1, keepdims=True) * vh).reshape(B, T, RW_WIDTH)
    out = (y + bonus) * g
    return out.astype(p_rw.dtype), p_rw[:, -1], S_T


def t5_bucket(n):
    n = jnp.maximum(n, 0)
    max_exact = NUM_BUCKETS // 2
    nf = jnp.maximum(n, 1).astype(jnp.float32)
    large = max_exact + (jnp.log(nf / max_exact) / math.log(MAX_DISTANCE / max_exact)
                         * (NUM_BUCKETS - max_exact)).astype(jnp.int32)
    large = jnp.minimum(large, NUM_BUCKETS - 1)
    return jnp.where(n < max_exact, n, large)


def moba_attention(q, k, v, q_pos0, rel_bias):
    f32 = jnp.float32
    B, Tq, H, D = q.shape
    L = k.shape[1]
    nb = -(-L // MB_BLOCK)
    pad = nb * MB_BLOCK - L
    kb = jnp.pad(k, ((0, 0), (0, pad), (0, 0), (0, 0))).reshape(B, nb, MB_BLOCK, H, D).transpose(0, 3, 1, 2, 4)
    vb = jnp.pad(v, ((0, 0), (0, pad), (0, 0), (0, 0))).reshape(B, nb, MB_BLOCK, H, D).transpose(0, 3, 1, 2, 4)
    k_mean = jnp.mean(kb.astype(f32), axis=3)
    k_sel = min(MB_TOPK, nb)
    qc = min(Q_BLOCK, Tq)
    n_chunks = -(-Tq // qc)
    qpad = n_chunks * qc - Tq
    qh = jnp.pad(q, ((0, 0), (0, qpad), (0, 0), (0, 0))).reshape(B, n_chunks, qc, H, D).transpose(1, 0, 3, 2, 4)
    pos = (q_pos0 + jnp.arange(n_chunks * qc, dtype=jnp.int32)).reshape(n_chunks, qc)
    scale = D ** -0.5
    rb = rel_bias.T.astype(f32)
    head_ix = jnp.arange(H)[None, :, None, None, None]
    gather = jax.vmap(jax.vmap(lambda t, i: t[i]))

    def chunk(args):
        qch, pch = args
        blk = pch // MB_BLOCK
        s = jnp.einsum('bhqd,bhnd->bhqn', qch.astype(f32), k_mean)
        past = jnp.arange(nb, dtype=jnp.int32)[None, :] < blk[:, None]
        s = jnp.where(past[None, None], s, -jnp.inf)
        _, sel = lax.top_k(s, k_sel)
        own = jnp.broadcast_to(jnp.minimum(blk, nb - 1)[None, None, :, None], (B, H, qc, 1)).astype(sel.dtype)
        idx = jnp.concatenate([sel, own], axis=-1)
        valid = jnp.concatenate([jnp.arange(k_sel)[None, :] < blk[:, None], jnp.ones((qc, 1), bool)], axis=-1)
        gk = gather(kb, idx)
        gv = gather(vb, idx)
        kpos = idx[..., None] * MB_BLOCK + jnp.arange(MB_BLOCK, dtype=idx.dtype)
        qpos = pch[None, None, :, None, None]
        bias = rb[head_ix, t5_bucket(qpos - kpos)]
        logits = jnp.einsum('bhqd,bhqjkd->bhqjk', qch.astype(f32), gk.astype(f32)) * scale + bias
        mask = valid[None, None, :, :, None] & (kpos <= qpos)
        logits = jnp.where(mask, logits, NEG_INF)
        p = jax.nn.softmax(logits.reshape(B, H, qc, -1), axis=-1).reshape(logits.shape)
        return jnp.einsum('bhqjk,bhqjkd->bhqd', p, gv.astype(f32)).astype(qch.dtype)

    out = lax.map(chunk, (qh, pos))
    out = out.transpose(1, 0, 3, 2, 4).reshape(B, n_chunks * qc, H * D)
    return out[:, :Tq]


def conformer_conv(p_cv, buf0, cv_w, cv_b, ln_w, ln_b):
    f32 = jnp.float32
    u = p_cv[..., :CV_WIDTH] * jax.nn.sigmoid(p_cv[..., CV_WIDTH:])
    ext = jnp.concatenate([buf0.astype(u.dtype), u], axis=1)
    y = lax.conv_general_dilated(ext, cv_w[:, None, :].astype(ext.dtype), window_strides=(1,), padding='VALID',
                                 dimension_numbers=('NWC', 'WIO', 'NWC'), feature_group_count=CV_WIDTH) + cv_b
    yf = y.astype(f32)
    mean = jnp.mean(yf, axis=-1, keepdims=True)
    var = jnp.mean(jnp.square(yf - mean), axis=-1, keepdims=True)
    yn = (yf - mean) * lax.rsqrt(var + LN_EPS) * ln_w.astype(f32) + ln_b.astype(f32)
    return jax.nn.silu(yn).astype(p_cv.dtype), ext[:, -(CV_K - 1):]


def run_group(x, past_kv, wkv0, shift0, conv0, q_pos0, P):
    B, T, _ = x.shape
    o1 = RW_COLS
    o2 = o1 + MB_WIDTH
    o3 = o2 + MB_WIDTH
    o4 = o3 + MB_WIDTH
    o5 = o4 + 2 * CV_WIDTH
    h = x
    new_k, new_v, new_wkv, new_shift, new_conv = [], [], [], [], []
    for l in range(DEPTH):
        xn = rms_norm(h, P['norm_mix'][l])
        proj = xn @ P['w_in'][l]
        p_rw, p_q, p_k, p_v, p_cv, p_g = jnp.split(proj, [o1, o2, o3, o4, o5], axis=-1)
        y_rw, sh, S = rwkv7_mix(p_rw, shift0[l], wkv0[l], P['rw_mu'][l], P['rw_w0'][l], P['rw_w2'][l],
                                P['rw_a0'][l], P['rw_a2'][l], P['rw_g2'][l], P['rw_kk'][l], P['rw_ka'][l],
                                P['rw_rk'][l], P['rw_ln_w'][l], P['rw_ln_b'][l])
        q = p_q.reshape(B, T, MB_HEADS, MB_HEAD_DIM)
        k_new = p_k.reshape(B, T, MB_HEADS, MB_HEAD_DIM)
        v_new = p_v.reshape(B, T, MB_HEADS, MB_HEAD_DIM)
        k_past, v_past = past_kv(l)
        k_all = jnp.concatenate([k_past.astype(k_new.dtype), k_new], axis=1)
        v_all = jnp.concatenate([v_past.astype(v_new.dtype), v_new], axis=1)
        y_mb = moba_attention(q, k_all, v_all, q_pos0, P['rel_bias'])
        y_cv, cb = conformer_conv(p_cv, conv0[l], P['cv_w'][l], P['cv_b'][l], P['cv_ln_w'][l], P['cv_ln_b'][l])
        gate = jax.nn.sigmoid(p_g.astype(jnp.float32)).astype(h.dtype).reshape(B, T, N_BRANCH, D_MODEL)
        merged = (gate[:, :, 0] * (y_rw @ P['w_br_rw'][l])
                  + gate[:, :, 1] * (y_mb @ P['w_br_mb'][l])
                  + gate[:, :, 2] * (y_cv @ P['w_br_cv'][l]))
        h = h + merged @ P['w_out'][l]
        xn = rms_norm(h, P['norm_ffn'][l])
        i = l // 2
        if l % 2 == 0:
            h = h + swiglu(xn, P['ffn_wg'][i], P['ffn_wu'][i], P['ffn_wd'][i])
        else:
            h = h + moe_swiglu(xn, P['moe_router'][i], P['moe_router_b'][i], P['moe_wg'][i],
                               P['moe_wu'][i], P['moe_wd'][i])
        new_k.append(k_new)
        new_v.append(v_new)
        new_wkv.append(S)
        new_shift.append(sh)
        new_conv.append(cb)
    y = rms_norm(h, P['norm_final'])
    return (y, jnp.stack(new_k), jnp.stack(new_v), jnp.stack(new_wkv), jnp.stack(new_shift), jnp.stack(new_conv))


def setup_inputs(seed: int = 0) -> dict:
    f32 = jnp.float32
    keys = jax.random.split(jax.random.key(seed), 48)

    def nrm(i, shape, scale):
        return jax.random.normal(keys[i], shape, f32) * scale

    n_pages = PAST_LEN // PAGE_SIZE
    n_pool = (5 * DEC_BATCH * n_pages + 3) // 4
    page_table = jax.random.permutation(keys[0], n_pool)[:DEC_BATCH * n_pages].reshape(DEC_BATCH, n_pages).astype(jnp.int32)
    kv_shape = (DEPTH, n_pool, PAGE_SIZE, MB_HEADS, MB_HEAD_DIM)
    return {
        'x_prompt': nrm(1, (BATCH, SEQ, D_MODEL), 1.0),
        'x_sample': nrm(2, (DEC_BATCH, DEC_SEQ, D_MODEL), 1.0),
        'cache_k': nrm(3, kv_shape, 1.0),
        'cache_v': nrm(4, kv_shape, 1.0),
        'state_wkv': nrm(5, (DEPTH, DEC_BATCH, RW_HEADS, RW_HEAD_DIM, RW_HEAD_DIM), 0.5),
        'state_shift': nrm(6, (DEPTH, DEC_BATCH, RW_COLS), 1.0),
        'state_conv': nrm(7, (DEPTH, DEC_BATCH, CV_K - 1, CV_WIDTH), 0.5),
        'page_table': page_table,
        'norm_mix': 1.0 + nrm(8, (DEPTH, D_MODEL), 0.05),
        'norm_ffn': 1.0 + nrm(9, (DEPTH, D_MODEL), 0.05),
        'norm_final': 1.0 + nrm(10, (D_MODEL,), 0.05),
        'w_in': nrm(11, (DEPTH, D_MODEL, N_IN), D_MODEL ** -0.5),
        'rw_mu': jax.random.uniform(keys[12], (DEPTH, RW_COLS), f32, 0.0, 1.0),
        'rw_w0': jax.random.uniform(keys[13], (DEPTH, RW_WIDTH), f32, -6.0, 0.0),
        'rw_w2': nrm(14, (DEPTH, DECAY_LORA, RW_WIDTH), 0.1),
        'rw_a0': nrm(15, (DEPTH, RW_WIDTH), 0.5),
        'rw_a2': nrm(16, (DEPTH, AAA_LORA, RW_WIDTH), AAA_LORA ** -0.5),
        'rw_g2': nrm(17, (DEPTH, GATE_LORA, RW_WIDTH), GATE_LORA ** -0.5),
        'rw_kk': 0.85 + nrm(18, (DEPTH, RW_WIDTH), 0.05),
        'rw_ka': 1.0 + nrm(19, (DEPTH, RW_WIDTH), 0.05),
        'rw_rk': nrm(20, (DEPTH, RW_HEADS, RW_HEAD_DIM), 0.1),
        'rw_ln_w': 1.0 + nrm(21, (DEPTH, RW_WIDTH), 0.05),
        'rw_ln_b': nrm(22, (DEPTH, RW_WIDTH), 0.02),
        'rel_bias': nrm(23, (NUM_BUCKETS, MB_HEADS), 0.5),
        'cv_w': nrm(24, (DEPTH, CV_K, CV_WIDTH), CV_K ** -0.5),
        'cv_b': nrm(25, (DEPTH, CV_WIDTH), 0.02),
        'cv_ln_w': 1.0 + nrm(26, (DEPTH, CV_WIDTH), 0.05),
        'cv_ln_b': nrm(27, (DEPTH, CV_WIDTH), 0.02),
        'w_br_rw': nrm(28, (DEPTH, RW_WIDTH, D_MODEL), RW_WIDTH ** -0.5),
        'w_br_mb': nrm(29, (DEPTH, MB_WIDTH, D_MODEL), MB_WIDTH ** -0.5),
        'w_br_cv': nrm(30, (DEPTH, CV_WIDTH, D_MODEL), CV_WIDTH ** -0.5),
        'w_out': nrm(31, (DEPTH, D_MODEL, D_MODEL), D_MODEL ** -0.5),
        'ffn_wg': nrm(32, (N_DENSE, D_MODEL, D_FF), D_MODEL ** -0.5),
        'ffn_wu': nrm(33, (N_DENSE, D_MODEL, D_FF), D_MODEL ** -0.5),
        'ffn_wd': nrm(34, (N_DENSE, D_FF, D_MODEL), D_FF ** -0.5),
        'moe_router': nrm(35, (N_MOE, D_MODEL, N_EXPERTS), D_MODEL ** -0.5),
        'moe_router_b': nrm(36, (N_MOE, N_EXPERTS), 0.01),
        'moe_wg': nrm(37, (N_MOE, N_EXPERTS, D_MODEL, D_FF_EXPERT), D_MODEL ** -0.5),
        'moe_wu': nrm(38, (N_MOE, N_EXPERTS, D_MODEL, D_FF_EXPERT), D_MODEL ** -0.5),
        'moe_wd': nrm(39, (N_MOE, N_EXPERTS, D_FF_EXPERT, D_MODEL), D_FF_EXPERT ** -0.5),
    }


def reference(x_prompt, x_sample, cache_k, cache_v, state_wkv, state_shift, state_conv, page_table,
              norm_mix, norm_ffn, norm_final, w_in,
              rw_mu, rw_w0, rw_w2, rw_a0, rw_a2, rw_g2, rw_kk, rw_ka, rw_rk, rw_ln_w, rw_ln_b,
              rel_bias, cv_w, cv_b, cv_ln_w, cv_ln_b,
              w_br_rw, w_br_mb, w_br_cv, w_out,
              ffn_wg, ffn_wu, ffn_wd,
              moe_router, moe_router_b, moe_wg, moe_wu, moe_wd):
    P = dict(norm_mix=norm_mix, norm_ffn=norm_ffn, norm_final=norm_final, w_in=w_in,
             rw_mu=rw_mu, rw_w0=rw_w0, rw_w2=rw_w2, rw_a0=rw_a0, rw_a2=rw_a2, rw_g2=rw_g2,
             rw_kk=rw_kk, rw_ka=rw_ka, rw_rk=rw_rk, rw_ln_w=rw_ln_w, rw_ln_b=rw_ln_b,
             rel_bias=rel_bias, cv_w=cv_w, cv_b=cv_b, cv_ln_w=cv_ln_w, cv_ln_b=cv_ln_b,
             w_br_rw=w_br_rw, w_br_mb=w_br_mb, w_br_cv=w_br_cv, w_out=w_out,
             ffn_wg=ffn_wg, ffn_wu=ffn_wu, ffn_wd=ffn_wd,
             moe_router=moe_router, moe_router_b=moe_router_b, moe_wg=moe_wg, moe_wu=moe_wu, moe_wd=moe_wd)
    dt = x_prompt.dtype
    b_p = x_prompt.shape[0]
    empty_kv = jnp.zeros((b_p, 0, MB_HEADS, MB_HEAD_DIM), dt)
    y_prompt, prompt_k, prompt_v, prompt_wkv, prompt_shift, prompt_conv = run_group(
        x_prompt, lambda l: (empty_kv, empty_kv),
        jnp.zeros((DEPTH, b_p, RW_HEADS, RW_HEAD_DIM, RW_HEAD_DIM), jnp.float32),
        jnp.zeros((DEPTH, b_p, RW_COLS), dt),
        jnp.zeros((DEPTH, b_p, CV_K - 1, CV_WIDTH), dt), 0, P)

    b_s, n_pages = page_table.shape
    past_len = n_pages * PAGE_SIZE

    def sample_past(l):
        k = cache_k[l][page_table].reshape(b_s, past_len, MB_HEADS, MB_HEAD_DIM)
        v = cache_v[l][page_table].reshape(b_s, past_len, MB_HEADS, MB_HEAD_DIM)
        return k, v

    y_sample, sample_k, sample_v, sample_wkv, sample_shift, sample_conv = run_group(
        x_sample, sample_past, state_wkv, state_shift, state_conv, past_len, P)
    return (y_prompt, y_sample, prompt_k, prompt_v, prompt_wkv, prompt_shift, prompt_conv,
            sample_k, sample_v, sample_wkv, sample_shift, sample_conv)
```

```python
import functools
import math

import numpy as np
import jax
import jax.numpy as jnp
from jax import lax
from jax.experimental import pallas as pl
from jax.experimental.pallas import tpu as pltpu

F32 = jnp.float32
BF16 = jnp.bfloat16

D_MODEL = 2048
RW_HEAD_DIM = 64
RW_WIDTH = 1024
RW_HEADS = 16
DECAY_LORA = 64
AAA_LORA = 64
GATE_LORA = 160
RW_COLS = 3 * RW_WIDTH + DECAY_LORA + AAA_LORA + GATE_LORA
RW_PAD = 3584
LORA_PAD = RW_PAD - 3 * RW_WIDTH
GN_EPS = 64e-5
MB_HEAD_DIM = 64
MB_WIDTH = 512
MB_HEADS = 8
MB_BLOCK = 256
MB_TOPK = 3
NUM_BUCKETS = 32
MAX_DISTANCE = 128
NEG_INF = -1e30
CV_WIDTH = 512
CV_K = 31
LN_EPS = 1e-5
N_BRANCH = 3
N_EXPERTS = 8
NORM_EPS = 1e-6
PAGE_SIZE = 128

PROJ_TILE = 512
N_PROJ = RW_PAD + 3 * MB_WIDTH + 2 * CV_WIDTH + N_BRANCH * D_MODEL
COL_Q = RW_PAD // PROJ_TILE
COL_CV = (RW_PAD + 3 * MB_WIDTH) // (2 * CV_WIDTH)
COL_GATE = (RW_PAD + 3 * MB_WIDTH + 2 * CV_WIDTH) // D_MODEL
GATE_TILE0 = (RW_PAD + 3 * MB_WIDTH + 2 * CV_WIDTH) // PROJ_TILE

VMEM_LIMIT_BYTES = 56 * 1024 * 1024


def _params(*semantics):
    return pltpu.CompilerParams(dimension_semantics=semantics, vmem_limit_bytes=VMEM_LIMIT_BYTES)


def _sigmoid(x):
    return 1.0 / (1.0 + jnp.exp(-x))


def _silu(x):
    return x * _sigmoid(x)


def _split3(x):
    hi = x.astype(BF16)
    r1 = x - hi.astype(F32)
    mid = r1.astype(BF16)
    lo = (r1 - mid.astype(F32)).astype(BF16)
    return hi, mid, lo


def _dot(a, b):
    return jnp.dot(a, b, preferred_element_type=F32)


def _dot_exact_rhs(x, sel):
    hi, mid, lo = _split3(x)
    return _dot(hi, sel) + _dot(mid, sel) + _dot(lo, sel)


def _dot_f32(a, b):
    a_hi, a_mid, a_lo = _split3(a)
    b_hi, b_mid, b_lo = _split3(b)
    return (_dot(a_hi, b_hi) + (_dot(a_hi, b_mid) + _dot(a_mid, b_hi))
            + (_dot(a_mid, b_mid) + _dot(a_hi, b_lo) + _dot(a_lo, b_hi)))


def _norm_proj_kernel(x_ref, g_ref, w_ref, o_ref, xn_ref):
    j = pl.program_id(1)

    @pl.when(j == 0)
    def _():
        x = x_ref[...]
        ms = jnp.mean(x * x, axis=-1, keepdims=True)
        xn_ref[...] = (x * lax.rsqrt(ms + NORM_EPS) * g_ref[...]).astype(BF16)

    acc = _dot(xn_ref[...], w_ref[...])

    @pl.when(j < GATE_TILE0)
    def _():
        o_ref[...] = acc

    @pl.when(j >= GATE_TILE0)
    def _():
        o_ref[...] = _sigmoid(acc)


def _norm_proj(x, g, w, tm):
    m = x.shape[0]
    return pl.pallas_call(
        _norm_proj_kernel,
        grid=(m // tm, N_PROJ // PROJ_TILE),
        in_specs=[pl.BlockSpec((tm, D_MODEL), lambda i, j: (i, 0)),
                  pl.BlockSpec((1, D_MODEL), lambda i, j: (0, 0)),
                  pl.BlockSpec((D_MODEL, PROJ_TILE), lambda i, j: (0, j))],
        out_specs=pl.BlockSpec((tm, PROJ_TILE), lambda i, j: (i, j)),
        out_shape=jax.ShapeDtypeStruct((m, N_PROJ), F32),
        scratch_shapes=[pltpu.VMEM((tm, D_MODEL), BF16)],
        compiler_params=_params("parallel", "arbitrary"),
        name="norm_proj",
    )(x, g, w)


def _rwkv_prep_kernel(sequential, *refs):
    if sequential:
        (p_ref, mu_ref, w0_ref, a0_ref, kk_ref, ka_ref, rk_ref, w2_ref, a2_ref, g2_ref,
         r_o, w_o, k_o, v_o, kn_o, b_o, g_o, bonus_o, carry_ref) = refs
    else:
        (p_ref, prev_ref, mu_ref, w0_ref, a0_ref, kk_ref, ka_ref, rk_ref, w2_ref, a2_ref, g2_ref,
         r_o, w_o, k_o, v_o, kn_o, b_o, g_o, bonus_o) = refs
    p = p_ref[...]
    tm = p.shape[0]
    if sequential:
        @pl.when(pl.program_id(0) == 0)
        def _():
            carry_ref[...] = jnp.zeros_like(carry_ref)

        rows = lax.broadcasted_iota(jnp.int32, p.shape, 0)
        prev = jnp.where(rows == 0, carry_ref[...], pltpu.roll(p, 1, 0))
        carry_ref[...] = p[tm - 1:tm, :]
    else:
        prev = prev_ref[...]
    xs = p + (prev - p) * mu_ref[...]
    r = xs[:, 0:RW_WIDTH]
    k = xs[:, RW_WIDTH:2 * RW_WIDTH]
    v = xs[:, 2 * RW_WIDTH:3 * RW_WIDTH]
    lora = xs[:, 3 * RW_WIDTH:RW_PAD]
    wx = w0_ref[...] + _dot(jnp.tanh(lora).astype(BF16), w2_ref[...])
    w_log = jnp.minimum(wx, 0.0) - jnp.log(1.0 + jnp.exp(-jnp.abs(wx))) - 0.5
    decay = jnp.exp(-jnp.exp(w_log))
    a = _sigmoid(a0_ref[...] + _dot(lora.astype(BF16), a2_ref[...]))
    g_o[...] = _dot(_sigmoid(lora).astype(BF16), g2_ref[...])
    kk = k * kk_ref[...]
    k_mod = k * (1.0 + (a - 1.0) * ka_ref[...])
    rkr = r * k_mod * rk_ref[...]
    for h in range(RW_HEADS):
        sl = slice(h * RW_HEAD_DIM, (h + 1) * RW_HEAD_DIM)
        kk_h = kk[:, sl]
        nrm = jnp.sqrt(jnp.sum(kk_h * kk_h, axis=-1, keepdims=True))
        kn_h = kk_h / jnp.maximum(nrm, 1e-12)
        r_o[h] = r[:, sl]
        w_o[h] = decay[:, sl]
        k_o[h] = k_mod[:, sl]
        v_o[h] = v[:, sl]
        kn_o[h] = kn_h
        b_o[h] = kn_h * a[:, sl]
        bonus_o[:, sl] = jnp.sum(rkr[:, sl], axis=-1, keepdims=True) * v[:, sl]


def _rwkv_prep(proj, prev, vecs, mats, tm):
    m = proj.shape[0]
    sequential = prev is None
    row = lambda n: pl.BlockSpec((1, n), lambda i: (0, 0))
    in_specs = [pl.BlockSpec((tm, RW_PAD), lambda i: (i, 0))]
    args = [proj]
    if not sequential:
        in_specs.append(pl.BlockSpec((tm, RW_PAD), lambda i: (i, 0)))
        args.append(prev)
    in_specs += [row(RW_PAD)] + [row(RW_WIDTH)] * 5
    in_specs += [pl.BlockSpec((LORA_PAD, RW_WIDTH), lambda i: (0, 0))] * 3
    head = jax.ShapeDtypeStruct((RW_HEADS, m, RW_HEAD_DIM), F32)
    flat = jax.ShapeDtypeStruct((m, RW_WIDTH), F32)
    head_spec = pl.BlockSpec((RW_HEADS, tm, RW_HEAD_DIM), lambda i: (0, i, 0))
    flat_spec = pl.BlockSpec((tm, RW_WIDTH), lambda i: (i, 0))
    return pl.pallas_call(
        functools.partial(_rwkv_prep_kernel, sequential),
        grid=(m // tm,),
        in_specs=in_specs,
        out_specs=[head_spec] * 6 + [flat_spec] * 2,
        out_shape=[head] * 6 + [flat] * 2,
        scratch_shapes=[pltpu.VMEM((1, RW_PAD), F32)] if sequential else [],
        compiler_params=_params("arbitrary"),
        name="rwkv_prep",
    )(*args, *vecs, *mats)


def _rwkv_step(state, r_t, w_t, k_t, v_t, kn_t, b_t, eye):
    sa = jnp.sum(state * kn_t, axis=-1, keepdims=True)
    v_col = jnp.sum(eye * v_t, axis=-1, keepdims=True)
    state = state * w_t - sa * b_t + v_col * k_t
    y_col = jnp.sum(state * r_t, axis=-1, keepdims=True)
    y_row = jnp.sum(eye * y_col, axis=1, keepdims=True)
    return state, y_row


def _eye3():
    i = lax.broadcasted_iota(jnp.int32, (1, RW_HEAD_DIM, RW_HEAD_DIM), 1)
    j = lax.broadcasted_iota(jnp.int32, (1, RW_HEAD_DIM, RW_HEAD_DIM), 2)
    return (i == j).astype(F32)


def _rwkv_scan_kernel(r_ref, w_ref, k_ref, v_ref, kn_ref, b_ref, y_ref, s_out_ref, s_ref):
    @pl.when(pl.program_id(0) == 0)
    def _():
        s_ref[...] = jnp.zeros_like(s_ref)

    eye = _eye3()
    tt = r_ref.shape[1]

    def body(t, carry):
        sl = pl.ds(t, 1)
        state, y_row = _rwkv_step(s_ref[...], r_ref[:, sl, :], w_ref[:, sl, :], k_ref[:, sl, :],
                                  v_ref[:, sl, :], kn_ref[:, sl, :], b_ref[:, sl, :], eye)
        s_ref[...] = state
        y_ref[:, sl, :] = y_row
        return carry

    lax.fori_loop(0, tt, body, 0)
    s_out_ref[...] = s_ref[...]


def _rwkv_scan(ops, tt):
    m = ops[0].shape[1]
    spec = pl.BlockSpec((RW_HEADS, tt, RW_HEAD_DIM), lambda i: (0, i, 0))
    state_shape = (RW_HEADS, RW_HEAD_DIM, RW_HEAD_DIM)
    return pl.pallas_call(
        _rwkv_scan_kernel,
        grid=(m // tt,),
        in_specs=[spec] * 6,
        out_specs=[spec, pl.BlockSpec(state_shape, lambda i: (0, 0, 0))],
        out_shape=[jax.ShapeDtypeStruct((RW_HEADS, m, RW_HEAD_DIM), F32),
                   jax.ShapeDtypeStruct(state_shape, F32)],
        scratch_shapes=[pltpu.VMEM(state_shape, F32)],
        compiler_params=_params("arbitrary"),
        name="rwkv_scan",
    )(*ops)


def _rwkv_single_step_kernel(r_ref, w_ref, k_ref, v_ref, kn_ref, b_ref, s0_ref, y_ref, s_out_ref):
    eye = _eye3()
    tb = r_ref.shape[1]

    def body(t, carry):
        sl = pl.ds(t, 1)
        state, y_row = _rwkv_step(s0_ref[t], r_ref[:, sl, :], w_ref[:, sl, :], k_ref[:, sl, :],
                                  v_ref[:, sl, :], kn_ref[:, sl, :], b_ref[:, sl, :], eye)
        s_out_ref[t] = state
        y_ref[:, sl, :] = y_row
        return carry

    lax.fori_loop(0, tb, body, 0)


def _rwkv_single_step(ops, state0, tb):
    m = ops[0].shape[1]
    spec = pl.BlockSpec((RW_HEADS, tb, RW_HEAD_DIM), lambda i: (0, i, 0))
    st_spec = pl.BlockSpec((tb, RW_HEADS, RW_HEAD_DIM, RW_HEAD_DIM), lambda i: (i, 0, 0, 0))
    return pl.pallas_call(
        _rwkv_single_step_kernel,
        grid=(m // tb,),
        in_specs=[spec] * 6 + [st_spec],
        out_specs=[spec, st_spec],
        out_shape=[jax.ShapeDtypeStruct((RW_HEADS, m, RW_HEAD_DIM), F32),
                   jax.ShapeDtypeStruct(state0.shape, F32)],
        compiler_params=_params("parallel"),
        name="rwkv_single_step",
    )(*ops, state0)


def _rwkv_post_kernel(y_ref, bonus_ref, g_ref, lnw_ref, lnb_ref, o_ref):
    parts = []
    for h in range(RW_HEADS):
        y = y_ref[h]
        mean = jnp.mean(y, axis=-1, keepdims=True)
        var = jnp.mean(jnp.square(y - mean), axis=-1, keepdims=True)
        parts.append((y - mean) * lax.rsqrt(var + GN_EPS))
    yn = jnp.concatenate(parts, axis=-1)
    out = (yn * lnw_ref[...] + lnb_ref[...] + bonus_ref[...]) * g_ref[...]
    o_ref[...] = out.astype(BF16)


def _rwkv_post(y, bonus, g, ln_w, ln_b, tm):
    m = bonus.shape[0]
    flat_spec = pl.BlockSpec((tm, RW_WIDTH), lambda i: (i, 0))
    row = pl.BlockSpec((1, RW_WIDTH), lambda i: (0, 0))
    return pl.pallas_call(
        _rwkv_post_kernel,
        grid=(m // tm,),
        in_specs=[pl.BlockSpec((RW_HEADS, tm, RW_HEAD_DIM), lambda i: (0, i, 0)),
                  flat_spec, flat_spec, row, row],
        out_specs=flat_spec,
        out_shape=jax.ShapeDtypeStruct((m, RW_WIDTH), BF16),
        compiler_params=_params("parallel"),
        name="rwkv_post",
    )(y, bonus, g, ln_w, ln_b)


CV_HALO = 32


def _conv_finish(y, cb_ref, lnw_ref, lnb_ref):
    y = y + cb_ref[...]
    mean = jnp.mean(y, axis=-1, keepdims=True)
    var = jnp.mean(jnp.square(y - mean), axis=-1, keepdims=True)
    yn = (y - mean) * lax.rsqrt(var + LN_EPS) * lnw_ref[...] + lnb_ref[...]
    return _silu(yn).astype(BF16)


def _conv_seq_kernel(p_ref, cw_ref, cb_ref, lnw_ref, lnb_ref, o_ref, tail_ref, ext_ref):
    tm = p_ref.shape[0]
    pad = CV_HALO - (CV_K - 1)

    @pl.when(pl.program_id(0) == 0)
    def _():
        ext_ref[0:CV_HALO, :] = jnp.zeros((CV_HALO, CV_WIDTH), F32)

    p = p_ref[...]
    u = p[:, 0:CV_WIDTH] * _sigmoid(p[:, CV_WIDTH:2 * CV_WIDTH])
    ext_ref[CV_HALO:CV_HALO + tm, :] = u
    acc = jnp.zeros((tm, CV_WIDTH), F32)
    for j in range(CV_K):
        acc = acc + ext_ref[pad + j:pad + j + tm, :] * cw_ref[j:j + 1, :]
    o_ref[...] = _conv_finish(acc, cb_ref, lnw_ref, lnb_ref)
    tail = ext_ref[tm:tm + CV_HALO, :]
    ext_ref[0:CV_HALO, :] = tail
    tail_ref[...] = tail


def _conv_seq(proj, cw, cb, ln_w, ln_b, tm):
    m = proj.shape[0]
    row = pl.BlockSpec((1, CV_WIDTH), lambda i: (0, 0))
    return pl.pallas_call(
        _conv_seq_kernel,
        grid=(m // tm,),
        in_specs=[pl.BlockSpec((tm, 2 * CV_WIDTH), lambda i: (i, COL_CV)),
                  pl.BlockSpec((CV_K, CV_WIDTH), lambda i: (0, 0)), row, row, row],
        out_specs=[pl.BlockSpec((tm, CV_WIDTH), lambda i: (i, 0)),
                   pl.BlockSpec((CV_HALO, CV_WIDTH), lambda i: (0, 0))],
        out_shape=[jax.ShapeDtypeStruct((m, CV_WIDTH), BF16),
                   jax.ShapeDtypeStruct((CV_HALO, CV_WIDTH), F32)],
        scratch_shapes=[pltpu.VMEM((CV_HALO + tm, CV_WIDTH), F32)],
        compiler_params=_params("arbitrary"),
        name="conv_seq",
    )(proj, cw, cb, ln_w, ln_b)


def _conv_step_kernel(p_ref, buf_ref, cw_ref, cb_ref, lnw_ref, lnb_ref, o_ref, nbuf_ref):
    p = p_ref[...]
    u = p[:, :, 0:CV_WIDTH] * _sigmoid(p[:, :, CV_WIDTH:2 * CV_WIDTH])
    buf = buf_ref[...]
    acc = (jnp.sum(buf * cw_ref[0:CV_K - 1, :][None], axis=1, keepdims=True)
           + u * cw_ref[CV_K - 1:CV_K, :][None])
    o_ref[...] = _conv_finish(acc, cb_ref, lnw_ref, lnb_ref)
    nbuf_ref[:, 0:CV_K - 2, :] = buf[:, 1:CV_K - 1, :]
    nbuf_ref[:, CV_K - 2:CV_K - 1, :] = u


def _conv_step(proj, buf, cw, cb, ln_w, ln_b, tb):
    m = proj.shape[0]
    row = pl.BlockSpec((1, CV_WIDTH), lambda i: (0, 0))
    buf_spec = pl.BlockSpec((tb, CV_K - 1, CV_WIDTH), lambda i: (i, 0, 0))
    y = pl.pallas_call(
        _conv_step_kernel,
        grid=(m // tb,),
        in_specs=[pl.BlockSpec((tb, 1, 2 * CV_WIDTH), lambda i: (i, 0, COL_CV)), buf_spec,
                  pl.BlockSpec((CV_K, CV_WIDTH), lambda i: (0, 0)), row, row, row],
        out_specs=[pl.BlockSpec((tb, 1, CV_WIDTH), lambda i: (i, 0, 0)), buf_spec],
        out_shape=[jax.ShapeDtypeStruct((m, 1, CV_WIDTH), BF16),
                   jax.ShapeDtypeStruct(buf.shape, F32)],
        compiler_params=_params("parallel"),
        name="conv_step",
    )(proj.reshape(m, 1, N_PROJ), buf, cw, cb, ln_w, ln_b)
    return y[0].reshape(m, CV_WIDTH), y[1]


def _bucket_thresholds():
    n = np.arange(0, 4 * MAX_DISTANCE, dtype=np.int64)
    max_exact = NUM_BUCKETS // 2
    nf = np.maximum(n, 1).astype(np.float32)
    large = max_exact + (np.log(nf / np.float32(max_exact)) / np.float32(math.log(MAX_DISTANCE / max_exact))
                         * np.float32(NUM_BUCKETS - max_exact)).astype(np.int32)
    large = np.minimum(large, NUM_BUCKETS - 1)
    bucket = np.where(n < max_exact, n, large)
    assert np.all(np.diff(bucket) >= 0) and bucket[-1] == NUM_BUCKETS - 1
    return [int(np.argmax(bucket >= b)) for b in range(NUM_BUCKETS)]


_BUCKET_THR = _bucket_thresholds()


def _bias_from_distance(n, rb_of_bucket):
    bias = jnp.zeros(n.shape, F32) + rb_of_bucket(NUM_BUCKETS - 1)
    for b in range(NUM_BUCKETS - 2, -1, -1):
        bias = jnp.where(n < _BUCKET_THR[b + 1], rb_of_bucket(b), bias)
    return bias


def _rank_select(scores, n_rows, allowed):
    idx = lax.broadcasted_iota(jnp.int32, scores.shape, 0)
    beaten = jnp.zeros(scores.shape, F32)
    for j in range(n_rows):
        row = scores[j:j + 1, :]
        ahead = jnp.where(row > scores, 1.0, jnp.where(row == scores, jnp.where(idx > j, 1.0, 0.0), 0.0))
        beaten = beaten + ahead * allowed(j)
    return jnp.where(beaten < MB_TOPK, 1.0, 0.0)


def _block_mean_kernel(k_ref, o_ref):
    nb = o_ref.shape[1]
    k = k_ref[0].reshape(nb, MB_BLOCK, MB_HEAD_DIM)
    o_ref[0] = jnp.sum(k, axis=1) * (1.0 / MB_BLOCK)


def _block_mean(k_heads):
    h, t, d = k_heads.shape
    nb = t // MB_BLOCK
    return pl.pallas_call(
        _block_mean_kernel,
        grid=(h,),
        in_specs=[pl.BlockSpec((1, t, d), lambda i: (i, 0, 0))],
        out_specs=pl.BlockSpec((1, nb, d), lambda i: (i, 0, 0)),
        out_shape=jax.ShapeDtypeStruct((h, nb, d), F32),
        compiler_params=_params("parallel"),
        name="moba_block_mean",
    )(k_heads)


def _moba_seq_kernel(rb_ref, qt_ref, k_ref, vt_ref, km_ref, o_ref, bias_ref, sel_ref):
    h = pl.program_id(0)
    qi = pl.program_id(1)
    nb = km_ref.shape[1]
    scale = MB_HEAD_DIM ** -0.5
    kidx = lax.broadcasted_iota(jnp.int32, (MB_BLOCK, MB_BLOCK), 0)
    qidx = lax.broadcasted_iota(jnp.int32, (MB_BLOCK, MB_BLOCK), 1)

    @pl.when(qi == 0)
    def _():
        dist = qidx - kidx
        rb = lambda b: rb_ref[b, h]
        bias_ref[0] = _bias_from_distance(jnp.maximum(dist, 0), rb)
        bias_ref[1] = _bias_from_distance(dist + MB_BLOCK, rb)

    qt = qt_ref[0]
    block_scores = _dot_f32(km_ref[0], qt)
    sel_ref[...] = _rank_select(block_scores, nb, lambda j: jnp.where(j < qi, 1.0, 0.0))
    qtb = qt.astype(BF16)

    s = _dot(k_ref[0, qi], qtb) * scale + bias_ref[0]
    s = jnp.where(kidx <= qidx, s, NEG_INF)
    m0 = jnp.max(s, axis=0, keepdims=True)
    p = jnp.exp(s - m0)
    l0 = jnp.sum(p, axis=0, keepdims=True)
    acc0 = _dot(vt_ref[0, qi], p.astype(BF16))
    far_bias = rb_ref[NUM_BUCKETS - 1, h]

    def body(kj, carry):
        m, l, acc = carry
        bias = jnp.where(kj == qi - 1, bias_ref[1], far_bias)
        s = _dot(k_ref[0, kj], qtb) * scale + bias
        s = jnp.where(sel_ref[pl.ds(kj, 1), :] > 0.0, s, NEG_INF)
        m_new = jnp.maximum(m, jnp.max(s, axis=0, keepdims=True))
        alpha = jnp.exp(m - m_new)
        p = jnp.exp(s - m_new)
        l = alpha * l + jnp.sum(p, axis=0, keepdims=True)
        acc = alpha * acc + _dot(vt_ref[0, kj], p.astype(BF16))
        return m_new, l, acc

    m, l, acc = lax.fori_loop(0, qi, body, (m0, l0, acc0))
    o_ref[0] = acc / l


def _moba_seq(rel_bias, q, k, v):
    t = q.shape[0]
    nb = t // MB_BLOCK
    split = lambda x: x.reshape(nb, MB_BLOCK, MB_HEADS, MB_HEAD_DIM)
    qt = q.reshape(t, MB_HEADS, MB_HEAD_DIM).transpose(1, 2, 0)
    k_heads = k.reshape(t, MB_HEADS, MB_HEAD_DIM).transpose(1, 0, 2)
    kb = split(k).transpose(2, 0, 1, 3).astype(BF16)
    vt = split(v).transpose(2, 0, 3, 1).astype(BF16)
    km = _block_mean(k_heads)
    ot = pl.pallas_call(
        _moba_seq_kernel,
        grid=(MB_HEADS, nb),
        in_specs=[pl.BlockSpec(memory_space=pltpu.SMEM),
                  pl.BlockSpec((1, MB_HEAD_DIM, MB_BLOCK), lambda h, i: (h, 0, i)),
                  pl.BlockSpec((1, nb, MB_BLOCK, MB_HEAD_DIM), lambda h, i: (h, 0, 0, 0)),
                  pl.BlockSpec((1, nb, MB_HEAD_DIM, MB_BLOCK), lambda h, i: (h, 0, 0, 0)),
                  pl.BlockSpec((1, nb, MB_HEAD_DIM), lambda h, i: (h, 0, 0))],
        out_specs=pl.BlockSpec((1, MB_HEAD_DIM, MB_BLOCK), lambda h, i: (h, 0, i)),
        out_shape=jax.ShapeDtypeStruct((MB_HEADS, MB_HEAD_DIM, t), F32),
        scratch_shapes=[pltpu.VMEM((2, MB_BLOCK, MB_BLOCK), F32), pltpu.VMEM((nb, MB_BLOCK), F32)],
        compiler_params=_params("parallel", "arbitrary"),
        name="moba_seq",
    )(rel_bias, qt, kb, vt, km)
    return ot.transpose(2, 0, 1).reshape(t, MB_WIDTH).astype(BF16)


def _moba_paged_kernel(n_pages, pt_ref, q_ref, kn_ref, vn_ref, seg_ref, segt_ref, rb_ref, *refs):
    k_pages = refs[:n_pages]
    v_pages = refs[n_pages:2 * n_pages]
    o_ref, bias_ref, lg_ref = refs[2 * n_pages:]
    past = n_pages * PAGE_SIZE
    n_blocks = past // MB_BLOCK
    pages_per_block = MB_BLOCK // PAGE_SIZE
    scale = MB_HEAD_DIM ** -0.5
    seg = seg_ref[...]
    segt = segt_ref[...]

    @pl.when(pl.program_id(0) == 0)
    def _():
        dist = past - lax.broadcasted_iota(jnp.int32, (past, MB_HEADS), 0)
        bias_ref[...] = _bias_from_distance(dist, lambda b: rb_ref[b:b + 1, :])

    q = q_ref[0]
    means = []
    for j in range(n_blocks):
        tot = jnp.sum(k_pages[pages_per_block * j][...], axis=0, keepdims=True)
        for e in range(1, pages_per_block):
            tot = tot + jnp.sum(k_pages[pages_per_block * j + e][...], axis=0, keepdims=True)
        means.append(tot * (1.0 / MB_BLOCK))
    km = jnp.concatenate(means, axis=0)
    block_scores = _dot_exact_rhs(km * q, seg)
    sel = _rank_select(block_scores, n_blocks, lambda j: 1.0)

    for pg in range(n_pages):
        lg = _dot_exact_rhs(k_pages[pg][...] * q, seg) * scale + bias_ref[pg * PAGE_SIZE:(pg + 1) * PAGE_SIZE, :]
        blk = pg // pages_per_block
        lg_ref[pg * PAGE_SIZE:(pg + 1) * PAGE_SIZE, :] = jnp.where(sel[blk:blk + 1, :] > 0.0, lg, NEG_INF)
    lg_own = _dot_exact_rhs(jnp.broadcast_to(kn_ref[0] * q, (8, MB_WIDTH)), seg)[0:1, :] * scale + rb_ref[0:1, :]
    lg = lg_ref[...]
    m = jnp.maximum(jnp.max(lg, axis=0, keepdims=True), lg_own)
    p = jnp.exp(lg - m)
    p_own = jnp.exp(lg_own - m)
    denom = jnp.sum(p, axis=0, keepdims=True) + p_own
    acc = _dot(jnp.broadcast_to(p_own, (8, MB_HEADS)).astype(BF16), segt)[0:1, :] * vn_ref[0]
    for pg in range(n_pages):
        p_wide = _dot(p[pg * PAGE_SIZE:(pg + 1) * PAGE_SIZE, :].astype(BF16), segt)
        acc = acc + jnp.sum(p_wide * v_pages[pg][...], axis=0, keepdims=True)
    denom_wide = _dot_exact_rhs(jnp.broadcast_to(denom, (8, MB_HEADS)), segt)[0:1, :]
    o_ref[0] = (acc / denom_wide).astype(BF16)


def _moba_paged(rel_bias, q, k_new, v_new, cache_k, cache_v, layer, page_table):
    b, n_pages = page_table.shape
    past = n_pages * PAGE_SIZE
    seg = (np.arange(MB_WIDTH)[:, None] // MB_HEAD_DIM == np.arange(MB_HEADS)[None, :])
    seg_b = jnp.asarray(seg, BF16)
    segt_b = jnp.asarray(seg.T, BF16)
    row3 = pl.BlockSpec((1, 1, MB_WIDTH), lambda i, pt: (i, 0, 0))

    def page_spec(pg):
        return pl.BlockSpec((None, None, PAGE_SIZE, MB_WIDTH), lambda i, pt: (layer, pt[i, pg], 0, 0))

    full = lambda shape: pl.BlockSpec(shape, lambda i, pt: (0,) * len(shape))
    grid_spec = pltpu.PrefetchScalarGridSpec(
        num_scalar_prefetch=1,
        grid=(b,),
        in_specs=[row3, row3, row3, full((MB_WIDTH, MB_HEADS)), full((MB_HEADS, MB_WIDTH)),
                  full((NUM_BUCKETS, MB_HEADS))]
        + [page_spec(pg) for pg in range(n_pages)] * 2,
        out_specs=row3,
        scratch_shapes=[pltpu.VMEM((past, MB_HEADS), F32), pltpu.VMEM((past, MB_HEADS), F32)],
    )
    r3 = lambda x: x.reshape(b, 1, MB_WIDTH)
    out = pl.pallas_call(
        functools.partial(_moba_paged_kernel, n_pages),
        grid_spec=grid_spec,
        out_shape=jax.ShapeDtypeStruct((b, 1, MB_WIDTH), BF16),
        compiler_params=_params("arbitrary"),
        name="moba_paged",
    )(page_table, r3(q), r3(k_new), r3(v_new), seg_b, segt_b, rel_bias,
      *([cache_k] * n_pages), *([cache_v] * n_pages))
    return out.reshape(b, MB_WIDTH)


def _merge_kernel(yrw_ref, ymb_ref, ycv_ref, g0_ref, g1_ref, g2_ref, wrw_ref, wmb_ref, wcv_ref, o_ref):
    merged = (g0_ref[...] * _dot(yrw_ref[...], wrw_ref[...])
              + g1_ref[...] * _dot(ymb_ref[...], wmb_ref[...])
              + g2_ref[...] * _dot(ycv_ref[...], wcv_ref[...]))
    o_ref[...] = merged.astype(BF16)


def _merge(y_rw, y_mb, y_cv, proj, w_rw, w_mb, w_cv, tm):
    m = proj.shape[0]
    rows = lambda n: pl.BlockSpec((tm, n), lambda i: (i, 0))
    gate = lambda br: pl.BlockSpec((tm, D_MODEL), lambda i: (i, COL_GATE + br))
    full = lambda n: pl.BlockSpec((n, D_MODEL), lambda i: (0, 0))
    return pl.pallas_call(
        _merge_kernel,
        grid=(m // tm,),
        in_specs=[rows(RW_WIDTH), rows(MB_WIDTH), rows(CV_WIDTH), gate(0), gate(1), gate(2),
                  full(RW_WIDTH), full(MB_WIDTH), full(CV_WIDTH)],
        out_specs=rows(D_MODEL),
        out_shape=jax.ShapeDtypeStruct((m, D_MODEL), BF16),
        compiler_params=_params("parallel"),
        name="branch_merge",
    )(y_rw, y_mb, y_cv, proj, proj, proj, w_rw, w_mb, w_cv)


def _out_proj_kernel(h_ref, x_ref, w_ref, o_ref):
    o_ref[...] = h_ref[...] + _dot(x_ref[...], w_ref[...])


def _out_proj(h, merged, w_out, tm):
    m = h.shape[0]
    rows = pl.BlockSpec((tm, D_MODEL), lambda i: (i, 0))
    return pl.pallas_call(
        _out_proj_kernel,
        grid=(m // tm,),
        in_specs=[rows, rows, pl.BlockSpec((D_MODEL, D_MODEL), lambda i: (0, 0))],
        out_specs=rows,
        out_shape=jax.ShapeDtypeStruct((m, D_MODEL), F32),
        compiler_params=_params("parallel"),
        name="out_proj",
    )(h, merged, w_out)


def _ffn_kernel(h_ref, g_ref, gate_ref, wg_ref, wu_ref, wd_ref, o_ref, xn_ref):
    e = pl.program_id(1)
    f = pl.program_id(2)

    @pl.when((e == 0) & (f == 0))
    def _():
        x = h_ref[...]
        ms = jnp.mean(x * x, axis=-1, keepdims=True)
        xn_ref[...] = (x * lax.rsqrt(ms + NORM_EPS) * g_ref[...]).astype(BF16)
        o_ref[...] = x

    xn = xn_ref[...]
    gates = gate_ref[...]
    lane = lax.broadcasted_iota(jnp.int32, gates.shape, 1)
    gate_e = jnp.sum(jnp.where(lane == e, gates, 0.0), axis=-1, keepdims=True)
    mid = _silu(_dot(xn, wg_ref[0])) * _dot(xn, wu_ref[0]) * gate_e
    o_ref[...] += _dot(mid.astype(BF16), wd_ref[0])


def _ffn(h, g, gates, wg, wu, wd, tm, tf):
    m = h.shape[0]
    n_e, _, d_ff = wg.shape
    rows = pl.BlockSpec((tm, D_MODEL), lambda i, e, f: (i, 0))
    return pl.pallas_call(
        _ffn_kernel,
        grid=(m // tm, n_e, d_ff // tf),
        in_specs=[rows, pl.BlockSpec((1, D_MODEL), lambda i, e, f: (0, 0)),
                  pl.BlockSpec((tm, N_EXPERTS), lambda i, e, f: (i, 0)),
                  pl.BlockSpec((1, D_MODEL, tf), lambda i, e, f: (e, 0, f)),
                  pl.BlockSpec((1, D_MODEL, tf), lambda i, e, f: (e, 0, f)),
                  pl.BlockSpec((1, tf, D_MODEL), lambda i, e, f: (e, f, 0))],
        out_specs=rows,
        out_shape=jax.ShapeDtypeStruct((m, D_MODEL), F32),
        scratch_shapes=[pltpu.VMEM((tm, D_MODEL), BF16)],
        compiler_params=_params("parallel", "arbitrary", "arbitrary"),
        name="ffn",
    )(h, g, gates, wg, wu, wd)


def _router_kernel(h_ref, g_ref, w_ref, b_ref, o_ref):
    x = h_ref[...]
    ms = jnp.mean(x * x, axis=-1, keepdims=True)
    xn = x * lax.rsqrt(ms + NORM_EPS) * g_ref[...]
    logits = _dot_f32(xn, w_ref[...]) + b_ref[...]
    lane = lax.broadcasted_iota(jnp.int32, logits.shape, 1).astype(F32)
    m1 = jnp.max(logits, axis=-1, keepdims=True)
    i1 = jnp.min(jnp.where(logits == m1, lane, float(N_EXPERTS)), axis=-1, keepdims=True)
    rest = jnp.where(lane == i1, -jnp.inf, logits)
    m2 = jnp.max(rest, axis=-1, keepdims=True)
    i2 = jnp.min(jnp.where(rest == m2, lane, float(N_EXPERTS)), axis=-1, keepdims=True)
    e2 = jnp.exp(m2 - m1)
    w1 = 1.0 / (1.0 + e2)
    w2 = e2 / (1.0 + e2)
    o_ref[...] = jnp.where(lane == i1, w1, 0.0) + jnp.where(lane == i2, w2, 0.0)


def _router(h, g, w_router, b_router, tm):
    m = h.shape[0]
    return pl.pallas_call(
        _router_kernel,
        grid=(m // tm,),
        in_specs=[pl.BlockSpec((tm, D_MODEL), lambda i: (i, 0)),
                  pl.BlockSpec((1, D_MODEL), lambda i: (0, 0)),
                  pl.BlockSpec((D_MODEL, N_EXPERTS), lambda i: (0, 0)),
                  pl.BlockSpec((1, N_EXPERTS), lambda i: (0, 0))],
        out_specs=pl.BlockSpec((tm, N_EXPERTS), lambda i: (i, 0)),
        out_shape=jax.ShapeDtypeStruct((m, N_EXPERTS), F32),
        compiler_params=_params("parallel"),
        name="moe_router",
    )(h, g, w_router, b_router)


def _final_norm_kernel(h_ref, g_ref, o_ref):
    x = h_ref[...]
    ms = jnp.mean(x * x, axis=-1, keepdims=True)
    o_ref[...] = x * lax.rsqrt(ms + NORM_EPS) * g_ref[...]


def _final_norm(h, g, tm):
    m = h.shape[0]
    rows = pl.BlockSpec((tm, D_MODEL), lambda i: (i, 0))
    return pl.pallas_call(
        _final_norm_kernel,
        grid=(m // tm,),
        in_specs=[rows, pl.BlockSpec((1, D_MODEL), lambda i: (0, 0))],
        out_specs=rows,
        out_shape=jax.ShapeDtypeStruct((m, D_MODEL), F32),
        compiler_params=_params("parallel"),
        name="final_norm",
    )(h, g)


def _pad_rw_cols(x):
    pad = jnp.zeros(x.shape[:-1] + (RW_PAD - RW_COLS,), x.dtype)
    return jnp.concatenate([x[..., :RW_COLS], pad, x[..., RW_COLS:]], axis=-1)


def _lora_slab(w, row0):
    slab = jnp.zeros((LORA_PAD, RW_WIDTH), F32)
    return lax.dynamic_update_slice(slab, w, (row0, 0)).astype(BF16)


def _layer_params(l, P):
    row = lambda x: x.reshape(1, -1)
    lp = dict(
        norm_mix=row(P['norm_mix'][l]), norm_ffn=row(P['norm_ffn'][l]),
        w_in=_pad_rw_cols(P['w_in'][l]).astype(BF16),
        rw_vecs=[row(_pad_rw_cols(P['rw_mu'][l])), row(P['rw_w0'][l]), row(P['rw_a0'][l]),
                 row(P['rw_kk'][l]), row(P['rw_ka'][l]), row(P['rw_rk'][l])],
        rw_mats=[_lora_slab(P['rw_w2'][l], 0), _lora_slab(P['rw_a2'][l], DECAY_LORA),
                 _lora_slab(P['rw_g2'][l], DECAY_LORA + AAA_LORA)],
        rw_ln_w=row(P['rw_ln_w'][l]), rw_ln_b=row(P['rw_ln_b'][l]),
        cv_w=P['cv_w'][l], cv_b=row(P['cv_b'][l]), cv_ln_w=row(P['cv_ln_w'][l]), cv_ln_b=row(P['cv_ln_b'][l]),
        w_br_rw=P['w_br_rw'][l].astype(BF16), w_br_mb=P['w_br_mb'][l].astype(BF16),
        w_br_cv=P['w_br_cv'][l].astype(BF16), w_out=P['w_out'][l].astype(BF16),
    )
    i = l // 2
    if l % 2 == 0:
        lp.update(wg=P['ffn_wg'][i][None].astype(BF16), wu=P['ffn_wu'][i][None].astype(BF16),
                  wd=P['ffn_wd'][i][None].astype(BF16))
    else:
        lp.update(wg=P['moe_wg'][i].astype(BF16), wu=P['moe_wu'][i].astype(BF16), wd=P['moe_wd'][i].astype(BF16),
                  router=P['moe_router'][i], router_b=row(P['moe_router_b'][i]))
    return lp


def _channel_mix(h, lp, tm):
    m = h.shape[0]
    if 'router' in lp:
        gates = _router(h, lp['norm_ffn'], lp['router'], lp['router_b'], tm)
        tf = 256
    else:
        gates = jnp.ones((m, N_EXPERTS), F32)
        tf = 512
    return _ffn(h, lp['norm_ffn'], gates, lp['wg'], lp['wu'], lp['wd'], min(tm, 512), tf)


def _run_prompt(x, layers, norm_final, rel_bias):
    t = x.shape[0]
    h = x
    outs = dict(k=[], v=[], wkv=[], shift=[], conv=[])
    for lp in layers:
        proj = _norm_proj(h, lp['norm_mix'], lp['w_in'], 1024)
        *ops, g, bonus = _rwkv_prep(proj, None, lp['rw_vecs'], lp['rw_mats'], 128)
        y, s_final = _rwkv_scan(ops, 128)
        y_rw = _rwkv_post(y, bonus, g, lp['rw_ln_w'], lp['rw_ln_b'], 128)
        q = proj[:, RW_PAD:RW_PAD + MB_WIDTH]
        k_new = proj[:, RW_PAD + MB_WIDTH:RW_PAD + 2 * MB_WIDTH]
        v_new = proj[:, RW_PAD + 2 * MB_WIDTH:RW_PAD + 3 * MB_WIDTH]
        y_mb = _moba_seq(rel_bias, q, k_new, v_new)
        y_cv, tail = _conv_seq(proj, lp['cv_w'], lp['cv_b'], lp['cv_ln_w'], lp['cv_ln_b'], 512)
        merged = _merge(y_rw, y_mb, y_cv, proj, lp['w_br_rw'], lp['w_br_mb'], lp['w_br_cv'], 256)
        h = _out_proj(h, merged, lp['w_out'], 512)
        h = _channel_mix(h, lp, 512)
        outs['k'].append(k_new.reshape(1, t, MB_HEADS, MB_HEAD_DIM))
        outs['v'].append(v_new.reshape(1, t, MB_HEADS, MB_HEAD_DIM))
        outs['wkv'].append(s_final[None])
        outs['shift'].append(proj[t - 1:t, :RW_COLS])
        outs['conv'].append(tail[None, CV_HALO - (CV_K - 1):, :])
    y = _final_norm(h, norm_final, 512)
    return (y[None],) + tuple(jnp.stack(outs[n]) for n in ('k', 'v', 'wkv', 'shift', 'conv'))


def _run_sample(x, layers, norm_final, rel_bias, cache_k, cache_v, state_wkv, state_shift, state_conv, page_table):
    b = x.shape[0]
    h = x
    n_pool = cache_k.shape[1]
    ck = cache_k.reshape(cache_k.shape[0], n_pool, PAGE_SIZE, MB_WIDTH)
    cv = cache_v.reshape(cache_v.shape[0], n_pool, PAGE_SIZE, MB_WIDTH)
    outs = dict(k=[], v=[], wkv=[], shift=[], conv=[])
    for l, lp in enumerate(layers):
        proj = _norm_proj(h, lp['norm_mix'], lp['w_in'], b)
        *ops, g, bonus = _rwkv_prep(proj, _pad_rw_cols(state_shift[l]), lp['rw_vecs'], lp['rw_mats'], b)
        y, s_new = _rwkv_single_step(ops, state_wkv[l], 8)
        y_rw = _rwkv_post(y, bonus, g, lp['rw_ln_w'], lp['rw_ln_b'], b)
        q = proj[:, RW_PAD:RW_PAD + MB_WIDTH]
        k_new = proj[:, RW_PAD + MB_WIDTH:RW_PAD + 2 * MB_WIDTH]
        v_new = proj[:, RW_PAD + 2 * MB_WIDTH:RW_PAD + 3 * MB_WIDTH]
        y_mb = _moba_paged(rel_bias, q, k_new, v_new, ck, cv, l, page_table)
        y_cv, conv_new = _conv_step(proj, state_conv[l], lp['cv_w'], lp['cv_b'], lp['cv_ln_w'], lp['cv_ln_b'], 8)
        merged = _merge(y_rw, y_mb, y_cv, proj, lp['w_br_rw'], lp['w_br_mb'], lp['w_br_cv'], b)
        h = _out_proj(h, merged, lp['w_out'], b)
        h = _channel_mix(h, lp, b)
        outs['k'].append(k_new.reshape(b, 1, MB_HEADS, MB_HEAD_DIM))
        outs['v'].append(v_new.reshape(b, 1, MB_HEADS, MB_HEAD_DIM))
        outs['wkv'].append(s_new)
        outs['shift'].append(proj[:, :RW_COLS])
        outs['conv'].append(conv_new)
    y = _final_norm(h, norm_final, b)
    return (y[:, None, :],) + tuple(jnp.stack(outs[n]) for n in ('k', 'v', 'wkv', 'shift', 'conv'))


def kernel(x_prompt, x_sample, cache_k, cache_v, state_wkv, state_shift, state_conv, page_table, norm_mix, norm_ffn, norm_final, w_in, rw_mu, rw_w0, rw_w2, rw_a0, rw_a2, rw_g2, rw_kk, rw_ka, rw_rk, rw_ln_w, rw_ln_b, rel_bias, cv_w, cv_b, cv_ln_w, cv_ln_b, w_br_rw, w_br_mb, w_br_cv, w_out, ffn_wg, ffn_wu, ffn_wd, moe_router, moe_router_b, moe_wg, moe_wu, moe_wd):
    P = dict(norm_mix=norm_mix, norm_ffn=norm_ffn, w_in=w_in,
             rw_mu=rw_mu, rw_w0=rw_w0, rw_w2=rw_w2, rw_a0=rw_a0, rw_a2=rw_a2, rw_g2=rw_g2,
             rw_kk=rw_kk, rw_ka=rw_ka, rw_rk=rw_rk, rw_ln_w=rw_ln_w, rw_ln_b=rw_ln_b,
             cv_w=cv_w, cv_b=cv_b, cv_ln_w=cv_ln_w, cv_ln_b=cv_ln_b,
             w_br_rw=w_br_rw, w_br_mb=w_br_mb, w_br_cv=w_br_cv, w_out=w_out,
             ffn_wg=ffn_wg, ffn_wu=ffn_wu, ffn_wd=ffn_wd,
             moe_router=moe_router, moe_router_b=moe_router_b, moe_wg=moe_wg, moe_wu=moe_wu, moe_wd=moe_wd)
    depth = w_in.shape[0]
    layers = [_layer_params(l, P) for l in range(depth)]
    g_final = norm_final.reshape(1, -1)
    assert x_prompt.shape[0] == 1 and x_sample.shape[1] == 1
    prompt = _run_prompt(x_prompt[0], layers, g_final, rel_bias)
    sample = _run_sample(x_sample[:, 0, :], layers, g_final, rel_bias, cache_k, cache_v,
                         state_wkv, state_shift, state_conv, page_table)
    y_p, k_p, v_p, wkv_p, shift_p, conv_p = prompt
    y_s, k_s, v_s, wkv_s, shift_s, conv_s = sample
    return (y_p, y_s, k_p, v_p, wkv_p, shift_p, conv_p,
            k_s, v_s, wkv_s, shift_s, conv_s)
```

```python
import functools
import math

import numpy as np
import jax
import jax.numpy as jnp
from jax import lax
from jax.experimental import pallas as pl
from jax.experimental.pallas import tpu as pltpu

F32 = jnp.float32
BF16 = jnp.bfloat16

D_MODEL = 2048
RW_HEAD_DIM = 64
RW_WIDTH = 1024
RW_HEADS = 16
DECAY_LORA = 64
AAA_LORA = 64
GATE_LORA = 160
RW_COLS = 3 * RW_WIDTH + DECAY_LORA + AAA_LORA + GATE_LORA
RW_PAD = 3584
LORA_PAD = RW_PAD - 3 * RW_WIDTH
GN_EPS = 64e-5
MB_HEAD_DIM = 64
MB_WIDTH = 512
MB_HEADS = 8
MB_BLOCK = 256
MB_TOPK = 3
NUM_BUCKETS = 32
MAX_DISTANCE = 128
NEG_INF = -1e30
CV_WIDTH = 512
CV_K = 31
LN_EPS = 1e-5
N_BRANCH = 3
N_EXPERTS = 8
NORM_EPS = 1e-6
PAGE_SIZE = 128

PROJ_TILE = 512
N_PROJ = RW_PAD + 3 * MB_WIDTH + 2 * CV_WIDTH + N_BRANCH * D_MODEL
COL_Q = RW_PAD // PROJ_TILE
COL_CV = (RW_PAD + 3 * MB_WIDTH) // (2 * CV_WIDTH)
COL_GATE = (RW_PAD + 3 * MB_WIDTH + 2 * CV_WIDTH) // D_MODEL
GATE_TILE0 = (RW_PAD + 3 * MB_WIDTH + 2 * CV_WIDTH) // PROJ_TILE

VMEM_LIMIT_BYTES = 56 * 1024 * 1024


def _params(*semantics):
    return pltpu.CompilerParams(dimension_semantics=semantics, vmem_limit_bytes=VMEM_LIMIT_BYTES)


def _sigmoid(x):
    return 1.0 / (1.0 + jnp.exp(-x))


def _silu(x):
    return x * _sigmoid(x)


def _split3(x):
    hi = x.astype(BF16)
    r1 = x - hi.astype(F32)
    mid = r1.astype(BF16)
    lo = (r1 - mid.astype(F32)).astype(BF16)
    return hi, mid, lo


def _dot(a, b):
    return jnp.dot(a, b, preferred_element_type=F32)


def _dot_exact_rhs(x, sel):
    hi, mid, lo = _split3(x)
    return _dot(hi, sel) + _dot(mid, sel) + _dot(lo, sel)


def _dot_f32(a, b):
    a_hi, a_mid, a_lo = _split3(a)
    b_hi, b_mid, b_lo = _split3(b)
    return (_dot(a_hi, b_hi) + (_dot(a_hi, b_mid) + _dot(a_mid, b_hi))
            + (_dot(a_mid, b_mid) + _dot(a_hi, b_lo) + _dot(a_lo, b_hi)))


def _norm_proj_kernel(x_ref, g_ref, w_ref, o_ref, xn_ref):
    j = pl.program_id(1)

    @pl.when(j == 0)
    def _():
        x = x_ref[...]
        ms = jnp.mean(x * x, axis=-1, keepdims=True)
        xn_ref[...] = (x * lax.rsqrt(ms + NORM_EPS) * g_ref[...]).astype(BF16)

    acc = _dot(xn_ref[...], w_ref[...])

    @pl.when(j < GATE_TILE0)
    def _():
        o_ref[...] = acc

    @pl.when(j >= GATE_TILE0)
    def _():
        o_ref[...] = _sigmoid(acc)


def _norm_proj(x, g, w, tm):
    m = x.shape[0]
    return pl.pallas_call(
        _norm_proj_kernel,
        grid=(m // tm, N_PROJ // PROJ_TILE),
        in_specs=[pl.BlockSpec((tm, D_MODEL), lambda i, j: (i, 0)),
                  pl.BlockSpec((1, D_MODEL), lambda i, j: (0, 0)),
                  pl.BlockSpec((D_MODEL, PROJ_TILE), lambda i, j: (0, j))],
        out_specs=pl.BlockSpec((tm, PROJ_TILE), lambda i, j: (i, j)),
        out_shape=jax.ShapeDtypeStruct((m, N_PROJ), F32),
        scratch_shapes=[pltpu.VMEM((tm, D_MODEL), BF16)],
        compiler_params=_params("parallel", "arbitrary"),
        name="norm_proj",
    )(x, g, w)


def _rwkv_prep_kernel(sequential, *refs):
    if sequential:
        (p_ref, mu_ref, w0_ref, a0_ref, kk_ref, ka_ref, rk_ref, w2_ref, a2_ref, g2_ref,
         r_o, w_o, k_o, v_o, kn_o, b_o, g_o, bonus_o, carry_ref) = refs
    else:
        (p_ref, prev_ref, mu_ref, w0_ref, a0_ref, kk_ref, ka_ref, rk_ref, w2_ref, a2_ref, g2_ref,
         r_o, w_o, k_o, v_o, kn_o, b_o, g_o, bonus_o) = refs
    p = p_ref[...]
    tm = p.shape[0]
    if sequential:
        @pl.when(pl.program_id(0) == 0)
        def _():
            carry_ref[...] = jnp.zeros_like(carry_ref)

        rows = lax.broadcasted_iota(jnp.int32, p.shape, 0)
        prev = jnp.where(rows == 0, carry_ref[...], pltpu.roll(p, 1, 0))
        carry_ref[...] = p[tm - 1:tm, :]
    else:
        prev = prev_ref[...]
    xs = p + (prev - p) * mu_ref[...]
    r = xs[:, 0:RW_WIDTH]
    k = xs[:, RW_WIDTH:2 * RW_WIDTH]
    v = xs[:, 2 * RW_WIDTH:3 * RW_WIDTH]
    lora = xs[:, 3 * RW_WIDTH:RW_PAD]
    wx = w0_ref[...] + _dot(jnp.tanh(lora).astype(BF16), w2_ref[...])
    w_log = jnp.minimum(wx, 0.0) - jnp.log(1.0 + jnp.exp(-jnp.abs(wx))) - 0.5
    log_decay = -jnp.exp(w_log)
    a = _sigmoid(a0_ref[...] + _dot(lora.astype(BF16), a2_ref[...]))
    g_o[...] = _dot(_sigmoid(lora).astype(BF16), g2_ref[...])
    kk = k * kk_ref[...]
    k_mod = k * (1.0 + (a - 1.0) * ka_ref[...])
    rkr = r * k_mod * rk_ref[...]
    if sequential:
        r_o[...] = r
        w_o[...] = log_decay
        k_o[...] = k_mod
        v_o[...] = v
    else:
        decay = jnp.exp(log_decay)
    for h in range(RW_HEADS):
        sl = slice(h * RW_HEAD_DIM, (h + 1) * RW_HEAD_DIM)
        kk_h = kk[:, sl]
        nrm = jnp.sqrt(jnp.sum(kk_h * kk_h, axis=-1, keepdims=True))
        kn_h = kk_h / jnp.maximum(nrm, 1e-12)
        if sequential:
            kn_o[:, sl] = kn_h
            b_o[:, sl] = kn_h * a[:, sl]
        else:
            r_o[h] = r[:, sl]
            w_o[h] = decay[:, sl]
            k_o[h] = k_mod[:, sl]
            v_o[h] = v[:, sl]
            kn_o[h] = kn_h
            b_o[h] = kn_h * a[:, sl]
        bonus_o[:, sl] = jnp.sum(rkr[:, sl], axis=-1, keepdims=True) * v[:, sl]


def _rwkv_prep(proj, prev, vecs, mats, tm):
    m = proj.shape[0]
    sequential = prev is None
    row = lambda n: pl.BlockSpec((1, n), lambda i: (0, 0))
    in_specs = [pl.BlockSpec((tm, RW_PAD), lambda i: (i, 0))]
    args = [proj]
    if not sequential:
        in_specs.append(pl.BlockSpec((tm, RW_PAD), lambda i: (i, 0)))
        args.append(prev)
    in_specs += [row(RW_PAD)] + [row(RW_WIDTH)] * 5
    in_specs += [pl.BlockSpec((LORA_PAD, RW_WIDTH), lambda i: (0, 0))] * 3
    flat = jax.ShapeDtypeStruct((m, RW_WIDTH), F32)
    flat_spec = pl.BlockSpec((tm, RW_WIDTH), lambda i: (i, 0))
    if sequential:
        op_shape, op_spec = flat, flat_spec
    else:
        op_shape = jax.ShapeDtypeStruct((RW_HEADS, m, RW_HEAD_DIM), F32)
        op_spec = pl.BlockSpec((RW_HEADS, tm, RW_HEAD_DIM), lambda i: (0, i, 0))
    return pl.pallas_call(
        functools.partial(_rwkv_prep_kernel, sequential),
        grid=(m // tm,),
        in_specs=in_specs,
        out_specs=[op_spec] * 6 + [flat_spec] * 2,
        out_shape=[op_shape] * 6 + [flat] * 2,
        scratch_shapes=[pltpu.VMEM((1, RW_PAD), F32)] if sequential else [],
        compiler_params=_params("arbitrary"),
        name="rwkv_prep",
    )(*args, *vecs, *mats)


def _rwkv_step(state, r_t, w_t, k_t, v_t, kn_t, b_t, eye):
    sa = jnp.sum(state * kn_t, axis=-1, keepdims=True)
    v_col = jnp.sum(eye * v_t, axis=-1, keepdims=True)
    state = state * w_t - sa * b_t + v_col * k_t
    y_col = jnp.sum(state * r_t, axis=-1, keepdims=True)
    y_row = jnp.sum(eye * y_col, axis=1, keepdims=True)
    return state, y_row


def _eye3():
    i = lax.broadcasted_iota(jnp.int32, (1, RW_HEAD_DIM, RW_HEAD_DIM), 1)
    j = lax.broadcasted_iota(jnp.int32, (1, RW_HEAD_DIM, RW_HEAD_DIM), 2)
    return (i == j).astype(F32)


RW_CHUNK = 64
_NT = (((1,), (1,)), ((), ()))
_TN = (((0,), (0,)), ((), ()))


def _group_norm(y):
    mean = jnp.mean(y, axis=-1, keepdims=True)
    var = jnp.mean(jnp.square(y - mean), axis=-1, keepdims=True)
    return (y - mean) * lax.rsqrt(var + GN_EPS)


def _rwkv_chunk_kernel(r_ref, lw_ref, k_ref, v_ref, kn_ref, b_ref, bonus_ref, g_ref, lnw_ref, lnb_ref,
                       y_ref, s_out_ref, h_ref):
    c = RW_CHUNK
    n = RW_HEAD_DIM

    @pl.when(pl.program_id(0) == 0)
    def _():
        h_ref[...] = jnp.zeros_like(h_ref)

    row = lax.broadcasted_iota(jnp.int32, (c, c), 0)
    col = lax.broadcasted_iota(jnp.int32, (c, c), 1)
    eye = jnp.where(row == col, 1.0, 0.0)
    lower_ones = jnp.where(col <= row, 1.0, 0.0).astype(BF16)
    lw = lw_ref[...]
    lw_hi, lw_mid, lw_lo = _split3(lw)
    cum = _dot(lower_ones, lw_hi) + _dot(lower_ones, lw_mid) + _dot(lower_ones, lw_lo)
    total = cum[c - 1:c, :]
    k = k_ref[...]
    b = b_ref[...]
    v = v_ref[...]
    g_inv = jnp.exp(-cum)
    g_rest = jnp.exp(total - cum)
    decay_total = jnp.exp(total)
    qr = jnp.concatenate([kn_ref[...] * jnp.exp(cum - lw), r_ref[...] * jnp.exp(cum)], axis=0).astype(BF16)
    kdbd = jnp.concatenate([k * g_inv, b * g_inv], axis=0).astype(BF16)
    zk = jnp.concatenate([k * g_rest, b * g_rest], axis=0).astype(BF16)
    vb = v.astype(BF16)
    row2 = lax.broadcasted_iota(jnp.int32, (2 * c, 2 * c), 0)
    col2 = lax.broadcasted_iota(jnp.int32, (2 * c, 2 * c), 1)
    t_idx = jnp.where(row2 >= c, row2 - c, row2)
    s_idx = jnp.where(col2 >= c, col2 - c, col2)
    mask = s_idx < t_idx + jnp.where(row2 >= c, 1, 0)
    heads = range(RW_HEADS)
    sls = [slice(h * n, (h + 1) * n) for h in heads]
    gms = [jnp.where(mask, lax.dot_general(qr[:, sl], kdbd[:, sl], _NT, preferred_element_type=F32), 0.0)
           for sl in sls]
    h_old = [h_ref[h] for h in heads]
    qhs = [_dot(qr[:, sl], ho.astype(BF16)) for sl, ho in zip(sls, h_old)]
    akkv = [_dot(gm[0:c, 0:c].astype(BF16), vb[:, sl]) for gm, sl in zip(gms, sls)]
    powers = [gm[0:c, c:2 * c].astype(BF16) for gm in gms]
    t_invs = [eye - gm[0:c, c:2 * c] for gm in gms]
    for _ in range(int(math.log2(c)) - 1):
        powers = [_dot(p, p).astype(BF16) for p in powers]
        t_invs = [t + _dot(p, t.astype(BF16)) for p, t in zip(powers, t_invs)]
    us = [_dot(t.astype(BF16), (qh[0:c] + av).astype(BF16)) for t, qh, av in zip(t_invs, qhs, akkv)]
    vus = [jnp.concatenate([v[:, sl], -u], axis=0).astype(BF16) for sl, u in zip(sls, us)]
    outs = [_group_norm(qh[c:2 * c] + _dot(gm[c:2 * c, :].astype(BF16), vu)) for qh, gm, vu in zip(qhs, gms, vus)]
    for h in heads:
        decay_col = jnp.sum(eye * decay_total[:, sls[h]], axis=-1, keepdims=True)
        h_ref[h] = decay_col * h_old[h] + lax.dot_general(zk[:, sls[h]], vus[h], _TN, preferred_element_type=F32)
    yn = jnp.concatenate(outs, axis=-1)
    y_ref[...] = ((yn * lnw_ref[...] + lnb_ref[...] + bonus_ref[...]) * g_ref[...]).astype(BF16)

    @pl.when(pl.program_id(0) == pl.num_programs(0) - 1)
    def _():
        for h in range(RW_HEADS):
            s_out_ref[h] = h_ref[h].T


def _rwkv_chunked(ops, bonus, g, ln_w, ln_b):
    m = bonus.shape[0]
    spec = pl.BlockSpec((RW_CHUNK, RW_WIDTH), lambda i: (i, 0))
    row = pl.BlockSpec((1, RW_WIDTH), lambda i: (0, 0))
    state_shape = (RW_HEADS, RW_HEAD_DIM, RW_HEAD_DIM)
    return pl.pallas_call(
        _rwkv_chunk_kernel,
        grid=(m // RW_CHUNK,),
        in_specs=[spec] * 8 + [row, row],
        out_specs=[spec, pl.BlockSpec(state_shape, lambda i: (0, 0, 0))],
        out_shape=[jax.ShapeDtypeStruct((m, RW_WIDTH), BF16), jax.ShapeDtypeStruct(state_shape, F32)],
        scratch_shapes=[pltpu.VMEM(state_shape, F32)],
        compiler_params=_params("arbitrary"),
        name="rwkv_chunked",
    )(*ops, bonus, g, ln_w, ln_b)


def _rwkv_single_step_kernel(r_ref, w_ref, k_ref, v_ref, kn_ref, b_ref, s0_ref, y_ref, s_out_ref):
    eye = _eye3()
    tb = r_ref.shape[1]

    def body(t, carry):
        sl = pl.ds(t, 1)
        state, y_row = _rwkv_step(s0_ref[t], r_ref[:, sl, :], w_ref[:, sl, :], k_ref[:, sl, :],
                                  v_ref[:, sl, :], kn_ref[:, sl, :], b_ref[:, sl, :], eye)
        s_out_ref[t] = state
        y_ref[:, sl, :] = y_row
        return carry

    lax.fori_loop(0, tb, body, 0)


def _rwkv_single_step(ops, state0, tb):
    m = ops[0].shape[1]
    spec = pl.BlockSpec((RW_HEADS, tb, RW_HEAD_DIM), lambda i: (0, i, 0))
    st_spec = pl.BlockSpec((tb, RW_HEADS, RW_HEAD_DIM, RW_HEAD_DIM), lambda i: (i, 0, 0, 0))
    return pl.pallas_call(
        _rwkv_single_step_kernel,
        grid=(m // tb,),
        in_specs=[spec] * 6 + [st_spec],
        out_specs=[spec, st_spec],
        out_shape=[jax.ShapeDtypeStruct((RW_HEADS, m, RW_HEAD_DIM), F32),
                   jax.ShapeDtypeStruct(state0.shape, F32)],
        compiler_params=_params("parallel"),
        name="rwkv_single_step",
    )(*ops, state0)


def _rwkv_post_kernel(y_ref, bonus_ref, g_ref, lnw_ref, lnb_ref, o_ref):
    yn = jnp.concatenate([_group_norm(y_ref[h]) for h in range(RW_HEADS)], axis=-1)
    out = (yn * lnw_ref[...] + lnb_ref[...] + bonus_ref[...]) * g_ref[...]
    o_ref[...] = out.astype(BF16)


def _rwkv_post(y, bonus, g, ln_w, ln_b, tm):
    m = bonus.shape[0]
    flat_spec = pl.BlockSpec((tm, RW_WIDTH), lambda i: (i, 0))
    row = pl.BlockSpec((1, RW_WIDTH), lambda i: (0, 0))
    return pl.pallas_call(
        _rwkv_post_kernel,
        grid=(m // tm,),
        in_specs=[pl.BlockSpec((RW_HEADS, tm, RW_HEAD_DIM), lambda i: (0, i, 0)),
                  flat_spec, flat_spec, row, row],
        out_specs=flat_spec,
        out_shape=jax.ShapeDtypeStruct((m, RW_WIDTH), BF16),
        compiler_params=_params("parallel"),
        name="rwkv_post",
    )(y, bonus, g, ln_w, ln_b)


CV_HALO = 32


def _conv_finish(y, cb_ref, lnw_ref, lnb_ref):
    y = y + cb_ref[...]
    mean = jnp.mean(y, axis=-1, keepdims=True)
    var = jnp.mean(jnp.square(y - mean), axis=-1, keepdims=True)
    yn = (y - mean) * lax.rsqrt(var + LN_EPS) * lnw_ref[...] + lnb_ref[...]
    return _silu(yn).astype(BF16)


def _conv_seq_kernel(p_ref, cw_ref, cb_ref, lnw_ref, lnb_ref, o_ref, tail_ref, ext_ref):
    tm = p_ref.shape[0]
    pad = CV_HALO - (CV_K - 1)

    @pl.when(pl.program_id(0) == 0)
    def _():
        ext_ref[0:CV_HALO, :] = jnp.zeros((CV_HALO, CV_WIDTH), F32)

    p = p_ref[...]
    u = p[:, 0:CV_WIDTH] * _sigmoid(p[:, CV_WIDTH:2 * CV_WIDTH])
    ext_ref[CV_HALO:CV_HALO + tm, :] = u
    acc = jnp.zeros((tm, CV_WIDTH), F32)
    for j in range(CV_K):
        acc = acc + ext_ref[pad + j:pad + j + tm, :] * cw_ref[j:j + 1, :]
    o_ref[...] = _conv_finish(acc, cb_ref, lnw_ref, lnb_ref)
    tail = ext_ref[tm:tm + CV_HALO, :]
    ext_ref[0:CV_HALO, :] = tail
    tail_ref[...] = tail


def _conv_seq(proj, cw, cb, ln_w, ln_b, tm):
    m = proj.shape[0]
    row = pl.BlockSpec((1, CV_WIDTH), lambda i: (0, 0))
    return pl.pallas_call(
        _conv_seq_kernel,
        grid=(m // tm,),
        in_specs=[pl.BlockSpec((tm, 2 * CV_WIDTH), lambda i: (i, COL_CV)),
                  pl.BlockSpec((CV_K, CV_WIDTH), lambda i: (0, 0)), row, row, row],
        out_specs=[pl.BlockSpec((tm, CV_WIDTH), lambda i: (i, 0)),
                   pl.BlockSpec((CV_HALO, CV_WIDTH), lambda i: (0, 0))],
        out_shape=[jax.ShapeDtypeStruct((m, CV_WIDTH), BF16),
                   jax.ShapeDtypeStruct((CV_HALO, CV_WIDTH), F32)],
        scratch_shapes=[pltpu.VMEM((CV_HALO + tm, CV_WIDTH), F32)],
        compiler_params=_params("arbitrary"),
        name="conv_seq",
    )(proj, cw, cb, ln_w, ln_b)


def _conv_step_kernel(p_ref, buf_ref, cw_ref, cb_ref, lnw_ref, lnb_ref, o_ref, nbuf_ref):
    p = p_ref[...]
    u = p[:, :, 0:CV_WIDTH] * _sigmoid(p[:, :, CV_WIDTH:2 * CV_WIDTH])
    buf = buf_ref[...]
    acc = (jnp.sum(buf * cw_ref[0:CV_K - 1, :][None], axis=1, keepdims=True)
           + u * cw_ref[CV_K - 1:CV_K, :][None])
    o_ref[...] = _conv_finish(acc, cb_ref, lnw_ref, lnb_ref)
    nbuf_ref[:, 0:CV_K - 2, :] = buf[:, 1:CV_K - 1, :]
    nbuf_ref[:, CV_K - 2:CV_K - 1, :] = u


def _conv_step(proj, buf, cw, cb, ln_w, ln_b, tb):
    m = proj.shape[0]
    row = pl.BlockSpec((1, CV_WIDTH), lambda i: (0, 0))
    buf_spec = pl.BlockSpec((tb, CV_K - 1, CV_WIDTH), lambda i: (i, 0, 0))
    y = pl.pallas_call(
        _conv_step_kernel,
        grid=(m // tb,),
        in_specs=[pl.BlockSpec((tb, 1, 2 * CV_WIDTH), lambda i: (i, 0, COL_CV)), buf_spec,
                  pl.BlockSpec((CV_K, CV_WIDTH), lambda i: (0, 0)), row, row, row],
        out_specs=[pl.BlockSpec((tb, 1, CV_WIDTH), lambda i: (i, 0, 0)), buf_spec],
        out_shape=[jax.ShapeDtypeStruct((m, 1, CV_WIDTH), BF16),
                   jax.ShapeDtypeStruct(buf.shape, F32)],
        compiler_params=_params("parallel"),
        name="conv_step",
    )(proj.reshape(m, 1, N_PROJ), buf, cw, cb, ln_w, ln_b)
    return y[0].reshape(m, CV_WIDTH), y[1]


def _bucket_thresholds():
    n = np.arange(0, 4 * MAX_DISTANCE, dtype=np.int64)
    max_exact = NUM_BUCKETS // 2
    nf = np.maximum(n, 1).astype(np.float32)
    large = max_exact + (np.log(nf / np.float32(max_exact)) / np.float32(math.log(MAX_DISTANCE / max_exact))
                         * np.float32(NUM_BUCKETS - max_exact)).astype(np.int32)
    large = np.minimum(large, NUM_BUCKETS - 1)
    bucket = np.where(n < max_exact, n, large)
    assert np.all(np.diff(bucket) >= 0) and bucket[-1] == NUM_BUCKETS - 1
    return [int(np.argmax(bucket >= b)) for b in range(NUM_BUCKETS)]


_BUCKET_THR = _bucket_thresholds()


def _bias_from_distance(n, rb_of_bucket):
    bias = jnp.zeros(n.shape, F32) + rb_of_bucket(NUM_BUCKETS - 1)
    for b in range(NUM_BUCKETS - 2, -1, -1):
        bias = jnp.where(n < _BUCKET_THR[b + 1], rb_of_bucket(b), bias)
    return bias


def _rank_select(scores, n_rows, limit):
    idx = lax.broadcasted_iota(jnp.int32, scores.shape, 0)
    beaten = jnp.zeros(scores.shape, F32)
    for j in range(n_rows):
        row = scores[j:j + 1, :]
        ahead = jnp.where(row > scores, 1.0, jnp.where(row == scores, jnp.where(idx > j, 1.0, 0.0), 0.0))
        beaten = beaten + ahead * jnp.where(j < limit, 1.0, 0.0)
    return jnp.where(idx < limit, jnp.where(beaten < MB_TOPK, 1.0, 0.0), 0.0)


def _block_mean_kernel(k_ref, o_ref):
    nb = o_ref.shape[1]
    k = k_ref[0].reshape(nb, MB_BLOCK, MB_HEAD_DIM)
    o_ref[0] = jnp.sum(k, axis=1) * (1.0 / MB_BLOCK)


def _block_mean(k_heads):
    h, t, d = k_heads.shape
    nb = t // MB_BLOCK
    return pl.pallas_call(
        _block_mean_kernel,
        grid=(h,),
        in_specs=[pl.BlockSpec((1, t, d), lambda i: (i, 0, 0))],
        out_specs=pl.BlockSpec((1, nb, d), lambda i: (i, 0, 0)),
        out_shape=jax.ShapeDtypeStruct((h, nb, d), F32),
        compiler_params=_params("parallel"),
        name="moba_block_mean",
    )(k_heads)


MB_HEADS_PER_STEP = 4


def _softmax_blocks(carries, scores, vt_blks):
    m_new = [jnp.maximum(c[0], jnp.max(s, axis=0, keepdims=True)) for c, s in zip(carries, scores)]
    alpha = [jnp.exp(c[0] - mn) for c, mn in zip(carries, m_new)]
    p = [jnp.exp(s - mn) for s, mn in zip(scores, m_new)]
    pv = [_dot(vt, pp.astype(BF16)) for vt, pp in zip(vt_blks, p)]
    l = [a * c[1] + jnp.sum(pp, axis=0, keepdims=True) for a, c, pp in zip(alpha, carries, p)]
    acc = [a * c[2] + x for a, c, x in zip(alpha, carries, pv)]
    return list(zip(m_new, l, acc))


def _moba_seq_kernel(rb_ref, qt_ref, k_ref, vt_ref, km_ref, o_ref, bias_ref, rowterm_ref):
    hp = qt_ref.shape[0]
    h0 = pl.program_id(0) * hp
    qi = pl.program_id(1)
    nb = km_ref.shape[1]
    scale = MB_HEAD_DIM ** -0.5
    kidx = lax.broadcasted_iota(jnp.int32, (MB_BLOCK, MB_BLOCK), 0)
    qidx = lax.broadcasted_iota(jnp.int32, (MB_BLOCK, MB_BLOCK), 1)

    @pl.when(qi == 0)
    def _():
        dist = qidx - kidx
        for hh in range(hp):
            rb = lambda b: rb_ref[b, h0 + hh]
            bias_ref[hh, 0] = jnp.where(dist >= 0, _bias_from_distance(jnp.maximum(dist, 0), rb), NEG_INF)
            bias_ref[hh, 1] = _bias_from_distance(dist + MB_BLOCK, rb)

    near = jnp.maximum(qi - 1, 0)
    qtb, carries = [], []
    for hh in range(hp):
        qt = qt_ref[hh]
        block_scores = _dot_f32(km_ref[hh], qt)
        sel = _rank_select(block_scores, nb, qi)
        rowterm_ref[hh] = jnp.where(sel > 0.0, rb_ref[NUM_BUCKETS - 1, h0 + hh], NEG_INF)
        qtb.append((qt * scale).astype(BF16))
        s = _dot(k_ref[hh, qi], qtb[hh]) + bias_ref[hh, 0]
        m0 = jnp.max(s, axis=0, keepdims=True)
        p = jnp.exp(s - m0)
        carries.append((m0, jnp.sum(p, axis=0, keepdims=True), _dot(vt_ref[hh, qi], p.astype(BF16))))

    heads = range(hp)
    sel_near = [jnp.where(rowterm_ref[hh, pl.ds(near, 1), :] > 0.5 * NEG_INF, 0.0, NEG_INF) for hh in heads]
    scores = [_dot(k_ref[hh, near], qtb[hh]) + bias_ref[hh, 1] + sel_near[hh] for hh in heads]
    carries = _softmax_blocks(carries, scores, [vt_ref[hh, near] for hh in heads])

    def body(kj, carry):
        scores = [_dot(k_ref[hh, kj], qtb[hh]) + rowterm_ref[hh, pl.ds(kj, 1), :] for hh in heads]
        return tuple(_softmax_blocks(carry, scores, [vt_ref[hh, kj] for hh in heads]))

    carries = lax.fori_loop(0, near, body, tuple(carries))
    for hh in range(hp):
        m, l, acc = carries[hh]
        o_ref[hh] = acc / l


def _moba_seq(rel_bias, q, k, v):
    t = q.shape[0]
    nb = t // MB_BLOCK
    hp = MB_HEADS_PER_STEP
    split = lambda x: x.reshape(nb, MB_BLOCK, MB_HEADS, MB_HEAD_DIM)
    qt = q.reshape(t, MB_HEADS, MB_HEAD_DIM).transpose(1, 2, 0)
    k_heads = k.reshape(t, MB_HEADS, MB_HEAD_DIM).transpose(1, 0, 2)
    kb = split(k).transpose(2, 0, 1, 3).astype(BF16)
    vt = split(v).transpose(2, 0, 3, 1).astype(BF16)
    km = _block_mean(k_heads)
    ot = pl.pallas_call(
        _moba_seq_kernel,
        grid=(MB_HEADS // hp, nb),
        in_specs=[pl.BlockSpec(memory_space=pltpu.SMEM),
                  pl.BlockSpec((hp, MB_HEAD_DIM, MB_BLOCK), lambda h, i: (h, 0, i)),
                  pl.BlockSpec((hp, nb, MB_BLOCK, MB_HEAD_DIM), lambda h, i: (h, 0, 0, 0)),
                  pl.BlockSpec((hp, nb, MB_HEAD_DIM, MB_BLOCK), lambda h, i: (h, 0, 0, 0)),
                  pl.BlockSpec((hp, nb, MB_HEAD_DIM), lambda h, i: (h, 0, 0))],
        out_specs=pl.BlockSpec((hp, MB_HEAD_DIM, MB_BLOCK), lambda h, i: (h, 0, i)),
        out_shape=jax.ShapeDtypeStruct((MB_HEADS, MB_HEAD_DIM, t), F32),
        scratch_shapes=[pltpu.VMEM((hp, 2, MB_BLOCK, MB_BLOCK), F32), pltpu.VMEM((hp, nb, MB_BLOCK), F32)],
        compiler_params=_params("parallel", "arbitrary"),
        name="moba_seq",
    )(rel_bias, qt, kb, vt, km)
    return ot.transpose(2, 0, 1).reshape(t, MB_WIDTH).astype(BF16)


def _moba_paged_kernel(n_pages, pt_ref, q_ref, kn_ref, vn_ref, rbt_ref, *refs):
    k_pages = refs[:n_pages]
    v_pages = refs[n_pages:2 * n_pages]
    o_ref, mask_ref, bias_ref = refs[2 * n_pages:]
    past = n_pages * PAGE_SIZE
    n_blocks = past // MB_BLOCK
    pages_per_block = MB_BLOCK // PAGE_SIZE
    rows = PAGE_SIZE * MB_HEADS
    scale = MB_HEAD_DIM ** -0.5
    rb = lambda b: rbt_ref[:, b:b + 1]

    @pl.when(pl.program_id(0) == 0)
    def _():
        col = lax.broadcasted_iota(jnp.int32, (MB_HEADS, rows), 1)
        head = lax.broadcasted_iota(jnp.int32, (MB_HEADS, rows), 0)
        mask_ref[...] = jnp.where(col % MB_HEADS == head, 0.0, NEG_INF)
        bias_ref[...] = _bias_from_distance(PAGE_SIZE - col // MB_HEADS, rb)

    q = q_ref[0]
    scores = []
    for j in range(n_blocks):
        tot = jnp.sum(k_pages[pages_per_block * j][...], axis=0)
        for e in range(1, pages_per_block):
            tot = tot + jnp.sum(k_pages[pages_per_block * j + e][...], axis=0)
        scores.append(jnp.sum(tot * (1.0 / MB_BLOCK) * q, axis=-1, keepdims=True))
    far_bias = rb(NUM_BUCKETS - 1)
    rowterms = []
    for j in range(n_blocks):
        beaten = jnp.zeros((MB_HEADS, 1), F32)
        for i in range(n_blocks):
            if i != j:
                ahead = scores[i] > scores[j] if i > j else scores[i] >= scores[j]
                beaten = beaten + jnp.where(ahead, 1.0, 0.0)
        rowterms.append(jnp.where(beaten < MB_TOPK, far_bias, NEG_INF))
    qs = q * scale
    qsb = qs.astype(BF16)
    lg_own = jnp.sum(kn_ref[0] * qs, axis=-1, keepdims=True) + rb(0)
    logits = []
    for pg in range(n_pages):
        blk = pg // pages_per_block
        k2 = k_pages[pg][...].reshape(rows, MB_HEAD_DIM).astype(BF16)
        lg = lax.dot_general(qsb, k2, _NT, preferred_element_type=F32) + mask_ref[...]
        if pg == n_pages - 1:
            lg = lg + (bias_ref[...] + jnp.where(rowterms[blk] > 0.5 * NEG_INF, 0.0, NEG_INF))
        else:
            lg = lg + rowterms[blk]
        logits.append(lg)
    m_tile = logits[0]
    for lg in logits[1:]:
        m_tile = jnp.maximum(m_tile, lg)
    m = jnp.maximum(jnp.max(m_tile, axis=-1, keepdims=True), lg_own)
    p_own = jnp.exp(lg_own - m)
    l_tile = jnp.zeros((MB_HEADS, rows), F32)
    acc = p_own * vn_ref[0]
    for pg in range(n_pages):
        p = jnp.exp(logits[pg] - m)
        l_tile = l_tile + p
        v2 = v_pages[pg][...].reshape(rows, MB_HEAD_DIM).astype(BF16)
        acc = acc + _dot(p.astype(BF16), v2)
    l = jnp.sum(l_tile, axis=-1, keepdims=True) + p_own
    o_ref[0] = (acc / l).astype(BF16)


def _moba_paged(rel_bias, q, k_new, v_new, cache_k, cache_v, layer, page_table):
    b, n_pages = page_table.shape
    assert PAGE_SIZE >= _BUCKET_THR[NUM_BUCKETS - 1] and MB_BLOCK % PAGE_SIZE == 0
    head3 = pl.BlockSpec((1, MB_HEADS, MB_HEAD_DIM), lambda i, pt: (i, 0, 0))

    def page_spec(pg):
        return pl.BlockSpec((None, None, PAGE_SIZE, MB_HEADS, MB_HEAD_DIM),
                            lambda i, pt: (layer, pt[i, pg], 0, 0, 0))

    grid_spec = pltpu.PrefetchScalarGridSpec(
        num_scalar_prefetch=1,
        grid=(b,),
        in_specs=[head3, head3, head3, pl.BlockSpec((MB_HEADS, NUM_BUCKETS), lambda i, pt: (0, 0))]
        + [page_spec(pg) for pg in range(n_pages)] * 2,
        out_specs=head3,
        scratch_shapes=[pltpu.VMEM((MB_HEADS, PAGE_SIZE * MB_HEADS), F32)] * 2,
    )
    h3 = lambda x: x.reshape(b, MB_HEADS, MB_HEAD_DIM)
    out = pl.pallas_call(
        functools.partial(_moba_paged_kernel, n_pages),
        grid_spec=grid_spec,
        out_shape=jax.ShapeDtypeStruct((b, MB_HEADS, MB_HEAD_DIM), BF16),
        compiler_params=_params("arbitrary"),
        name="moba_paged",
    )(page_table, h3(q), h3(k_new), h3(v_new), rel_bias.T, *([cache_k] * n_pages), *([cache_v] * n_pages))
    return out.reshape(b, MB_WIDTH)


def _merge_kernel(yrw_ref, ymb_ref, ycv_ref, g0_ref, g1_ref, g2_ref, wrw_ref, wmb_ref, wcv_ref, o_ref):
    merged = (g0_ref[...] * _dot(yrw_ref[...], wrw_ref[...])
              + g1_ref[...] * _dot(ymb_ref[...], wmb_ref[...])
              + g2_ref[...] * _dot(ycv_ref[...], wcv_ref[...]))
    o_ref[...] = merged.astype(BF16)


def _merge(y_rw, y_mb, y_cv, proj, w_rw, w_mb, w_cv, tm):
    m = proj.shape[0]
    rows = lambda n: pl.BlockSpec((tm, n), lambda i: (i, 0))
    gate = lambda br: pl.BlockSpec((tm, D_MODEL), lambda i: (i, COL_GATE + br))
    full = lambda n: pl.BlockSpec((n, D_MODEL), lambda i: (0, 0))
    return pl.pallas_call(
        _merge_kernel,
        grid=(m // tm,),
        in_specs=[rows(RW_WIDTH), rows(MB_WIDTH), rows(CV_WIDTH), gate(0), gate(1), gate(2),
                  full(RW_WIDTH), full(MB_WIDTH), full(CV_WIDTH)],
        out_specs=rows(D_MODEL),
        out_shape=jax.ShapeDtypeStruct((m, D_MODEL), BF16),
        compiler_params=_params("parallel"),
        name="branch_merge",
    )(y_rw, y_mb, y_cv, proj, proj, proj, w_rw, w_mb, w_cv)


def _out_proj_kernel(h_ref, x_ref, w_ref, o_ref):
    o_ref[...] = h_ref[...] + _dot(x_ref[...], w_ref[...])


def _out_proj(h, merged, w_out, tm):
    m = h.shape[0]
    rows = pl.BlockSpec((tm, D_MODEL), lambda i: (i, 0))
    return pl.pallas_call(
        _out_proj_kernel,
        grid=(m // tm,),
        in_specs=[rows, rows, pl.BlockSpec((D_MODEL, D_MODEL), lambda i: (0, 0))],
        out_specs=rows,
        out_shape=jax.ShapeDtypeStruct((m, D_MODEL), F32),
        compiler_params=_params("parallel"),
        name="out_proj",
    )(h, merged, w_out)


def _ffn_kernel(h_ref, g_ref, gate_ref, wg_ref, wu_ref, wd_ref, o_ref, xn_ref):
    e = pl.program_id(1)
    f = pl.program_id(2)

    @pl.when((e == 0) & (f == 0))
    def _():
        x = h_ref[...]
        ms = jnp.mean(x * x, axis=-1, keepdims=True)
        xn_ref[...] = (x * lax.rsqrt(ms + NORM_EPS) * g_ref[...]).astype(BF16)
        o_ref[...] = x

    xn = xn_ref[...]
    gates = gate_ref[...]
    lane = lax.broadcasted_iota(jnp.int32, gates.shape, 1)
    gate_e = jnp.sum(jnp.where(lane == e, gates, 0.0), axis=-1, keepdims=True)
    mid = _silu(_dot(xn, wg_ref[0])) * _dot(xn, wu_ref[0]) * gate_e
    o_ref[...] += _dot(mid.astype(BF16), wd_ref[0])


def _ffn(h, g, gates, wg, wu, wd, tm, tf):
    m = h.shape[0]
    n_e, _, d_ff = wg.shape
    rows = pl.BlockSpec((tm, D_MODEL), lambda i, e, f: (i, 0))
    return pl.pallas_call(
        _ffn_kernel,
        grid=(m // tm, n_e, d_ff // tf),
        in_specs=[rows, pl.BlockSpec((1, D_MODEL), lambda i, e, f: (0, 0)),
                  pl.BlockSpec((tm, N_EXPERTS), lambda i, e, f: (i, 0)),
                  pl.BlockSpec((1, D_MODEL, tf), lambda i, e, f: (e, 0, f)),
                  pl.BlockSpec((1, D_MODEL, tf), lambda i, e, f: (e, 0, f)),
                  pl.BlockSpec((1, tf, D_MODEL), lambda i, e, f: (e, f, 0))],
        out_specs=rows,
        out_shape=jax.ShapeDtypeStruct((m, D_MODEL), F32),
        scratch_shapes=[pltpu.VMEM((tm, D_MODEL), BF16)],
        compiler_params=_params("parallel", "arbitrary", "arbitrary"),
        name="ffn",
    )(h, g, gates, wg, wu, wd)


def _router_kernel(h_ref, g_ref, w_ref, b_ref, o_ref):
    x = h_ref[...]
    ms = jnp.mean(x * x, axis=-1, keepdims=True)
    xn = x * lax.rsqrt(ms + NORM_EPS) * g_ref[...]
    logits = _dot_f32(xn, w_ref[...]) + b_ref[...]
    lane = lax.broadcasted_iota(jnp.int32, logits.shape, 1).astype(F32)
    m1 = jnp.max(logits, axis=-1, keepdims=True)
    i1 = jnp.min(jnp.where(logits == m1, lane, float(N_EXPERTS)), axis=-1, keepdims=True)
    rest = jnp.where(lane == i1, -jnp.inf, logits)
    m2 = jnp.max(rest, axis=-1, keepdims=True)
    i2 = jnp.min(jnp.where(rest == m2, lane, float(N_EXPERTS)), axis=-1, keepdims=True)
    e2 = jnp.exp(m2 - m1)
    w1 = 1.0 / (1.0 + e2)
    w2 = e2 / (1.0 + e2)
    o_ref[...] = jnp.where(lane == i1, w1, 0.0) + jnp.where(lane == i2, w2, 0.0)


def _router(h, g, w_router, b_router, tm):
    m = h.shape[0]
    return pl.pallas_call(
        _router_kernel,
        grid=(m // tm,),
        in_specs=[pl.BlockSpec((tm, D_MODEL), lambda i: (i, 0)),
                  pl.BlockSpec((1, D_MODEL), lambda i: (0, 0)),
                  pl.BlockSpec((D_MODEL, N_EXPERTS), lambda i: (0, 0)),
                  pl.BlockSpec((1, N_EXPERTS), lambda i: (0, 0))],
        out_specs=pl.BlockSpec((tm, N_EXPERTS), lambda i: (i, 0)),
        out_shape=jax.ShapeDtypeStruct((m, N_EXPERTS), F32),
        compiler_params=_params("parallel"),
        name="moe_router",
    )(h, g, w_router, b_router)


def _final_norm_kernel(h_ref, g_ref, o_ref):
    x = h_ref[...]
    ms = jnp.mean(x * x, axis=-1, keepdims=True)
    o_ref[...] = x * lax.rsqrt(ms + NORM_EPS) * g_ref[...]


def _final_norm(h, g, tm):
    m = h.shape[0]
    rows = pl.BlockSpec((tm, D_MODEL), lambda i: (i, 0))
    return pl.pallas_call(
        _final_norm_kernel,
        grid=(m // tm,),
        in_specs=[rows, pl.BlockSpec((1, D_MODEL), lambda i: (0, 0))],
        out_specs=rows,
        out_shape=jax.ShapeDtypeStruct((m, D_MODEL), F32),
        compiler_params=_params("parallel"),
        name="final_norm",
    )(h, g)


def _pad_rw_cols(x):
    pad = jnp.zeros(x.shape[:-1] + (RW_PAD - RW_COLS,), x.dtype)
    return jnp.concatenate([x[..., :RW_COLS], pad, x[..., RW_COLS:]], axis=-1)


def _lora_slab(w, row0):
    slab = jnp.zeros((LORA_PAD, RW_WIDTH), F32)
    return lax.dynamic_update_slice(slab, w, (row0, 0)).astype(BF16)


def _layer_params(l, P):
    row = lambda x: x.reshape(1, -1)
    lp = dict(
        norm_mix=row(P['norm_mix'][l]), norm_ffn=row(P['norm_ffn'][l]),
        w_in=_pad_rw_cols(P['w_in'][l]).astype(BF16),
        rw_vecs=[row(_pad_rw_cols(P['rw_mu'][l])), row(P['rw_w0'][l]), row(P['rw_a0'][l]),
                 row(P['rw_kk'][l]), row(P['rw_ka'][l]), row(P['rw_rk'][l])],
        rw_mats=[_lora_slab(P['rw_w2'][l], 0), _lora_slab(P['rw_a2'][l], DECAY_LORA),
                 _lora_slab(P['rw_g2'][l], DECAY_LORA + AAA_LORA)],
        rw_ln_w=row(P['rw_ln_w'][l]), rw_ln_b=row(P['rw_ln_b'][l]),
        cv_w=P['cv_w'][l], cv_b=row(P['cv_b'][l]), cv_ln_w=row(P['cv_ln_w'][l]), cv_ln_b=row(P['cv_ln_b'][l]),
        w_br_rw=P['w_br_rw'][l].astype(BF16), w_br_mb=P['w_br_mb'][l].astype(BF16),
        w_br_cv=P['w_br_cv'][l].astype(BF16), w_out=P['w_out'][l].astype(BF16),
    )
    i = l // 2
    if l % 2 == 0:
        lp.update(wg=P['ffn_wg'][i][None].astype(BF16), wu=P['ffn_wu'][i][None].astype(BF16),
                  wd=P['ffn_wd'][i][None].astype(BF16))
    else:
        lp.update(wg=P['moe_wg'][i].astype(BF16), wu=P['moe_wu'][i].astype(BF16), wd=P['moe_wd'][i].astype(BF16),
                  router=P['moe_router'][i], router_b=row(P['moe_router_b'][i]))
    return lp


def _channel_mix(h, lp, tm):
    m = h.shape[0]
    if 'router' in lp:
        gates = _router(h, lp['norm_ffn'], lp['router'], lp['router_b'], tm)
        tf = 256
    else:
        gates = jnp.ones((m, N_EXPERTS), F32)
        tf = 512
    return _ffn(h, lp['norm_ffn'], gates, lp['wg'], lp['wu'], lp['wd'], min(tm, 512), tf)


def _run_prompt(x, layers, norm_final, rel_bias):
    t = x.shape[0]
    h = x
    outs = dict(k=[], v=[], wkv=[], shift=[], conv=[])
    for lp in layers:
        proj = _norm_proj(h, lp['norm_mix'], lp['w_in'], 1024)
        *ops, g, bonus = _rwkv_prep(proj, None, lp['rw_vecs'], lp['rw_mats'], 128)
        y_rw, s_final = _rwkv_chunked(ops, bonus, g, lp['rw_ln_w'], lp['rw_ln_b'])
        q = proj[:, RW_PAD:RW_PAD + MB_WIDTH]
        k_new = proj[:, RW_PAD + MB_WIDTH:RW_PAD + 2 * MB_WIDTH]
        v_new = proj[:, RW_PAD + 2 * MB_WIDTH:RW_PAD + 3 * MB_WIDTH]
        y_mb = _moba_seq(rel_bias, q, k_new, v_new)
        y_cv, tail = _conv_seq(proj, lp['cv_w'], lp['cv_b'], lp['cv_ln_w'], lp['cv_ln_b'], 512)
        merged = _merge(y_rw, y_mb, y_cv, proj, lp['w_br_rw'], lp['w_br_mb'], lp['w_br_cv'], 256)
        h = _out_proj(h, merged, lp['w_out'], 512)
        h = _channel_mix(h, lp, 512)
        outs['k'].append(k_new.reshape(1, t, MB_HEADS, MB_HEAD_DIM))
        outs['v'].append(v_new.reshape(1, t, MB_HEADS, MB_HEAD_DIM))
        outs['wkv'].append(s_final[None])
        outs['shift'].append(proj[t - 1:t, :RW_COLS])
        outs['conv'].append(tail[None, CV_HALO - (CV_K - 1):, :])
    y = _final_norm(h, norm_final, 512)
    return (y[None],) + tuple(jnp.stack(outs[n]) for n in ('k', 'v', 'wkv', 'shift', 'conv'))


def _run_sample(x, layers, norm_final, rel_bias, cache_k, cache_v, state_wkv, state_shift, state_conv, page_table):
    b = x.shape[0]
    h = x
    outs = dict(k=[], v=[], wkv=[], shift=[], conv=[])
    for l, lp in enumerate(layers):
        proj = _norm_proj(h, lp['norm_mix'], lp['w_in'], b)
        *ops, g, bonus = _rwkv_prep(proj, _pad_rw_cols(state_shift[l]), lp['rw_vecs'], lp['rw_mats'], b)
        y, s_new = _rwkv_single_step(ops, state_wkv[l], 8)
        y_rw = _rwkv_post(y, bonus, g, lp['rw_ln_w'], lp['rw_ln_b'], b)
        q = proj[:, RW_PAD:RW_PAD + MB_WIDTH]
        k_new = proj[:, RW_PAD + MB_WIDTH:RW_PAD + 2 * MB_WIDTH]
        v_new = proj[:, RW_PAD + 2 * MB_WIDTH:RW_PAD + 3 * MB_WIDTH]
        y_mb = _moba_paged(rel_bias, q, k_new, v_new, cache_k, cache_v, l, page_table)
        y_cv, conv_new = _conv_step(proj, state_conv[l], lp['cv_w'], lp['cv_b'], lp['cv_ln_w'], lp['cv_ln_b'], 8)
        merged = _merge(y_rw, y_mb, y_cv, proj, lp['w_br_rw'], lp['w_br_mb'], lp['w_br_cv'], b)
        h = _out_proj(h, merged, lp['w_out'], b)
        h = _channel_mix(h, lp, b)
        outs['k'].append(k_new.reshape(b, 1, MB_HEADS, MB_HEAD_DIM))
        outs['v'].append(v_new.reshape(b, 1, MB_HEADS, MB_HEAD_DIM))
        outs['wkv'].append(s_new)
        outs['shift'].append(proj[:, :RW_COLS])
        outs['conv'].append(conv_new)
    y = _final_norm(h, norm_final, b)
    return (y[:, None, :],) + tuple(jnp.stack(outs[n]) for n in ('k', 'v', 'wkv', 'shift', 'conv'))


def kernel(x_prompt, x_sample, cache_k, cache_v, state_wkv, state_shift, state_conv, page_table, norm_mix, norm_ffn, norm_final, w_in, rw_mu, rw_w0, rw_w2, rw_a0, rw_a2, rw_g2, rw_kk, rw_ka, rw_rk, rw_ln_w, rw_ln_b, rel_bias, cv_w, cv_b, cv_ln_w, cv_ln_b, w_br_rw, w_br_mb, w_br_cv, w_out, ffn_wg, ffn_wu, ffn_wd, moe_router, moe_router_b, moe_wg, moe_wu, moe_wd):
    P = dict(norm_mix=norm_mix, norm_ffn=norm_ffn, w_in=w_in,
             rw_mu=rw_mu, rw_w0=rw_w0, rw_w2=rw_w2, rw_a0=rw_a0, rw_a2=rw_a2, rw_g2=rw_g2,
             rw_kk=rw_kk, rw_ka=rw_ka, rw_rk=rw_rk, rw_ln_w=rw_ln_w, rw_ln_b=rw_ln_b,
             cv_w=cv_w, cv_b=cv_b, cv_ln_w=cv_ln_w, cv_ln_b=cv_ln_b,
             w_br_rw=w_br_rw, w_br_mb=w_br_mb, w_br_cv=w_br_cv, w_out=w_out,
             ffn_wg=ffn_wg, ffn_wu=ffn_wu, ffn_wd=ffn_wd,
             moe_router=moe_router, moe_router_b=moe_router_b, moe_wg=moe_wg, moe_wu=moe_wu, moe_wd=moe_wd)
    depth = w_in.shape[0]
    layers = [_layer_params(l, P) for l in range(depth)]
    g_final = norm_final.reshape(1, -1)
    assert x_prompt.shape[0] == 1 and x_sample.shape[1] == 1
    prompt = _run_prompt(x_prompt[0], layers, g_final, rel_bias)
    sample = _run_sample(x_sample[:, 0, :], layers, g_final, rel_bias, cache_k, cache_v,
                         state_wkv, state_shift, state_conv, page_table)
    y_p, k_p, v_p, wkv_p, shift_p, conv_p = prompt
    y_s, k_s, v_s, wkv_s, shift_s, conv_s = sample
    return (y_p, y_s, k_p, v_p, wkv_p, shift_p, conv_p,
            k_s, v_s, wkv_s, shift_s, conv_s)
```

```python
import functools
import math

import numpy as np
import jax
import jax.numpy as jnp
from jax import lax
from jax.experimental import pallas as pl
from jax.experimental.pallas import tpu as pltpu

F32 = jnp.float32
BF16 = jnp.bfloat16

D_MODEL = 2048
RW_HEAD_DIM = 64
RW_WIDTH = 1024
RW_HEADS = 16
DECAY_LORA = 64
AAA_LORA = 64
GATE_LORA = 160
RW_COLS = 3 * RW_WIDTH + DECAY_LORA + AAA_LORA + GATE_LORA
RW_PAD = 3584
LORA_PAD = RW_PAD - 3 * RW_WIDTH
GN_EPS = 64e-5
MB_HEAD_DIM = 64
MB_WIDTH = 512
MB_HEADS = 8
MB_BLOCK = 256
MB_TOPK = 3
NUM_BUCKETS = 32
MAX_DISTANCE = 128
NEG_INF = -1e30
CV_WIDTH = 512
CV_K = 31
LN_EPS = 1e-5
N_BRANCH = 3
N_EXPERTS = 8
NORM_EPS = 1e-6
PAGE_SIZE = 128

PROJ_TILE = 512
N_PROJ = RW_PAD + 3 * MB_WIDTH + 2 * CV_WIDTH + N_BRANCH * D_MODEL
COL_Q = RW_PAD // PROJ_TILE
COL_CV = (RW_PAD + 3 * MB_WIDTH) // (2 * CV_WIDTH)
COL_GATE = (RW_PAD + 3 * MB_WIDTH + 2 * CV_WIDTH) // D_MODEL
GATE_TILE0 = (RW_PAD + 3 * MB_WIDTH + 2 * CV_WIDTH) // PROJ_TILE

VMEM_LIMIT_BYTES = 56 * 1024 * 1024


def _params(*semantics):
    return pltpu.CompilerParams(dimension_semantics=semantics, vmem_limit_bytes=VMEM_LIMIT_BYTES)


def _sigmoid(x):
    return 1.0 / (1.0 + jnp.exp(-x))


def _silu(x):
    return x * _sigmoid(x)


def _split3(x):
    hi = x.astype(BF16)
    r1 = x - hi.astype(F32)
    mid = r1.astype(BF16)
    lo = (r1 - mid.astype(F32)).astype(BF16)
    return hi, mid, lo


def _dot(a, b):
    return jnp.dot(a, b, preferred_element_type=F32)


def _dot_exact_rhs(x, sel):
    hi, mid, lo = _split3(x)
    return _dot(hi, sel) + _dot(mid, sel) + _dot(lo, sel)


def _dot_f32(a, b):
    a_hi, a_mid, a_lo = _split3(a)
    b_hi, b_mid, b_lo = _split3(b)
    return (_dot(a_hi, b_hi) + (_dot(a_hi, b_mid) + _dot(a_mid, b_hi))
            + (_dot(a_mid, b_mid) + _dot(a_hi, b_lo) + _dot(a_lo, b_hi)))


def _norm_proj_kernel(x_ref, g_ref, w_ref, o_ref, xn_ref):
    j = pl.program_id(1)

    @pl.when(j == 0)
    def _():
        x = x_ref[...]
        ms = jnp.mean(x * x, axis=-1, keepdims=True)
        xn_ref[...] = (x * lax.rsqrt(ms + NORM_EPS) * g_ref[...]).astype(BF16)

    acc = _dot(xn_ref[...], w_ref[...])

    @pl.when(j < GATE_TILE0)
    def _():
        o_ref[...] = acc

    @pl.when(j >= GATE_TILE0)
    def _():
        o_ref[...] = _sigmoid(acc)


def _norm_proj(x, g, w, tm):
    m = x.shape[0]
    return pl.pallas_call(
        _norm_proj_kernel,
        grid=(m // tm, N_PROJ // PROJ_TILE),
        in_specs=[pl.BlockSpec((tm, D_MODEL), lambda i, j: (i, 0)),
                  pl.BlockSpec((1, D_MODEL), lambda i, j: (0, 0)),
                  pl.BlockSpec((D_MODEL, PROJ_TILE), lambda i, j: (0, j))],
        out_specs=pl.BlockSpec((tm, PROJ_TILE), lambda i, j: (i, j)),
        out_shape=jax.ShapeDtypeStruct((m, N_PROJ), F32),
        scratch_shapes=[pltpu.VMEM((tm, D_MODEL), BF16)],
        compiler_params=_params("parallel", "arbitrary"),
        name="norm_proj",
    )(x, g, w)


def _rwkv_prep_kernel(sequential, *refs):
    if sequential:
        (p_ref, mu_ref, w0_ref, a0_ref, kk_ref, ka_ref, rk_ref, w2_ref, a2_ref, g2_ref,
         r_o, w_o, k_o, v_o, kn_o, b_o, g_o, bonus_o, carry_ref) = refs
    else:
        (p_ref, prev_ref, mu_ref, w0_ref, a0_ref, kk_ref, ka_ref, rk_ref, w2_ref, a2_ref, g2_ref,
         r_o, w_o, k_o, v_o, kn_o, b_o, g_o, bonus_o) = refs
    p = p_ref[...]
    tm = p.shape[0]
    if sequential:
        @pl.when(pl.program_id(0) == 0)
        def _():
            carry_ref[...] = jnp.zeros_like(carry_ref)

        rows = lax.broadcasted_iota(jnp.int32, p.shape, 0)
        prev = jnp.where(rows == 0, carry_ref[...], pltpu.roll(p, 1, 0))
        carry_ref[...] = p[tm - 1:tm, :]
    else:
        prev = prev_ref[...]
    xs = p + (prev - p) * mu_ref[...]
    r = xs[:, 0:RW_WIDTH]
    k = xs[:, RW_WIDTH:2 * RW_WIDTH]
    v = xs[:, 2 * RW_WIDTH:3 * RW_WIDTH]
    lora = xs[:, 3 * RW_WIDTH:RW_PAD]
    wx = w0_ref[...] + _dot(jnp.tanh(lora).astype(BF16), w2_ref[...])
    w_log = jnp.minimum(wx, 0.0) - jnp.log(1.0 + jnp.exp(-jnp.abs(wx))) - 0.5
    log_decay = -jnp.exp(w_log)
    a = _sigmoid(a0_ref[...] + _dot(lora.astype(BF16), a2_ref[...]))
    g_o[...] = _dot(_sigmoid(lora).astype(BF16), g2_ref[...])
    kk = k * kk_ref[...]
    k_mod = k * (1.0 + (a - 1.0) * ka_ref[...])
    rkr = r * k_mod * rk_ref[...]
    if sequential:
        r_o[...] = r
        w_o[...] = log_decay
        k_o[...] = k_mod
        v_o[...] = v
    else:
        decay = jnp.exp(log_decay)
    for h in range(RW_HEADS):
        sl = slice(h * RW_HEAD_DIM, (h + 1) * RW_HEAD_DIM)
        kk_h = kk[:, sl]
        nrm = jnp.sqrt(jnp.sum(kk_h * kk_h, axis=-1, keepdims=True))
        kn_h = kk_h / jnp.maximum(nrm, 1e-12)
        if sequential:
            kn_o[:, sl] = kn_h
            b_o[:, sl] = kn_h * a[:, sl]
        else:
            r_o[h] = r[:, sl]
            w_o[h] = decay[:, sl]
            k_o[h] = k_mod[:, sl]
            v_o[h] = v[:, sl]
            kn_o[h] = kn_h
            b_o[h] = kn_h * a[:, sl]
        bonus_o[:, sl] = jnp.sum(rkr[:, sl], axis=-1, keepdims=True) * v[:, sl]


def _rwkv_prep(proj, prev, vecs, mats, tm):
    m = proj.shape[0]
    sequential = prev is None
    row = lambda n: pl.BlockSpec((1, n), lambda i: (0, 0))
    in_specs = [pl.BlockSpec((tm, RW_PAD), lambda i: (i, 0))]
    args = [proj]
    if not sequential:
        in_specs.append(pl.BlockSpec((tm, RW_PAD), lambda i: (i, 0)))
        args.append(prev)
    in_specs += [row(RW_PAD)] + [row(RW_WIDTH)] * 5
    in_specs += [pl.BlockSpec((LORA_PAD, RW_WIDTH), lambda i: (0, 0))] * 3
    flat = jax.ShapeDtypeStruct((m, RW_WIDTH), F32)
    flat_spec = pl.BlockSpec((tm, RW_WIDTH), lambda i: (i, 0))
    if sequential:
        op_shape, op_spec = flat, flat_spec
    else:
        op_shape = jax.ShapeDtypeStruct((RW_HEADS, m, RW_HEAD_DIM), F32)
        op_spec = pl.BlockSpec((RW_HEADS, tm, RW_HEAD_DIM), lambda i: (0, i, 0))
    return pl.pallas_call(
        functools.partial(_rwkv_prep_kernel, sequential),
        grid=(m // tm,),
        in_specs=in_specs,
        out_specs=[op_spec] * 6 + [flat_spec] * 2,
        out_shape=[op_shape] * 6 + [flat] * 2,
        scratch_shapes=[pltpu.VMEM((1, RW_PAD), F32)] if sequential else [],
        compiler_params=_params("arbitrary"),
        name="rwkv_prep",
    )(*args, *vecs, *mats)


def _rwkv_step(state, r_t, w_t, k_t, v_t, kn_t, b_t, eye):
    sa = jnp.sum(state * kn_t, axis=-1, keepdims=True)
    v_col = jnp.sum(eye * v_t, axis=-1, keepdims=True)
    state = state * w_t - sa * b_t + v_col * k_t
    y_col = jnp.sum(state * r_t, axis=-1, keepdims=True)
    y_row = jnp.sum(eye * y_col, axis=1, keepdims=True)
    return state, y_row


def _eye3():
    i = lax.broadcasted_iota(jnp.int32, (1, RW_HEAD_DIM, RW_HEAD_DIM), 1)
    j = lax.broadcasted_iota(jnp.int32, (1, RW_HEAD_DIM, RW_HEAD_DIM), 2)
    return (i == j).astype(F32)


RW_CHUNK = 64
_NT = (((1,), (1,)), ((), ()))
_TN = (((0,), (0,)), ((), ()))


def _group_norm(y):
    mean = jnp.mean(y, axis=-1, keepdims=True)
    var = jnp.mean(jnp.square(y - mean), axis=-1, keepdims=True)
    return (y - mean) * lax.rsqrt(var + GN_EPS)


def _rwkv_chunk_kernel(r_ref, lw_ref, k_ref, v_ref, kn_ref, b_ref, bonus_ref, g_ref, lnw_ref, lnb_ref,
                       y_ref, s_out_ref, h_ref):
    c = RW_CHUNK
    n = RW_HEAD_DIM

    @pl.when(pl.program_id(0) == 0)
    def _():
        h_ref[...] = jnp.zeros_like(h_ref)

    row = lax.broadcasted_iota(jnp.int32, (c, c), 0)
    col = lax.broadcasted_iota(jnp.int32, (c, c), 1)
    eye = jnp.where(row == col, 1.0, 0.0)
    lower_ones = jnp.where(col <= row, 1.0, 0.0).astype(BF16)
    lw = lw_ref[...]
    lw_hi, lw_mid, lw_lo = _split3(lw)
    cum = _dot(lower_ones, lw_hi) + _dot(lower_ones, lw_mid) + _dot(lower_ones, lw_lo)
    total = cum[c - 1:c, :]
    k = k_ref[...]
    b = b_ref[...]
    v = v_ref[...]
    g_inv = jnp.exp(-cum)
    g_rest = jnp.exp(total - cum)
    decay_total = jnp.exp(total)
    qr = jnp.concatenate([kn_ref[...] * jnp.exp(cum - lw), r_ref[...] * jnp.exp(cum)], axis=0).astype(BF16)
    kdbd = jnp.concatenate([k * g_inv, b * g_inv], axis=0).astype(BF16)
    zk = jnp.concatenate([k * g_rest, b * g_rest], axis=0).astype(BF16)
    vb = v.astype(BF16)
    row2 = lax.broadcasted_iota(jnp.int32, (2 * c, 2 * c), 0)
    col2 = lax.broadcasted_iota(jnp.int32, (2 * c, 2 * c), 1)
    t_idx = jnp.where(row2 >= c, row2 - c, row2)
    s_idx = jnp.where(col2 >= c, col2 - c, col2)
    mask = s_idx < t_idx + jnp.where(row2 >= c, 1, 0)
    heads = range(RW_HEADS)
    sls = [slice(h * n, (h + 1) * n) for h in heads]
    gms = [jnp.where(mask, lax.dot_general(qr[:, sl], kdbd[:, sl], _NT, preferred_element_type=F32), 0.0)
           for sl in sls]
    h_old = [h_ref[h] for h in heads]
    qhs = [_dot(qr[:, sl], ho.astype(BF16)) for sl, ho in zip(sls, h_old)]
    akkv = [_dot(gm[0:c, 0:c].astype(BF16), vb[:, sl]) for gm, sl in zip(gms, sls)]
    powers = [gm[0:c, c:2 * c].astype(BF16) for gm in gms]
    t_invs = [eye - gm[0:c, c:2 * c] for gm in gms]
    for _ in range(int(math.log2(c)) - 1):
        powers = [_dot(p, p).astype(BF16) for p in powers]
        t_invs = [t + _dot(p, t.astype(BF16)) for p, t in zip(powers, t_invs)]
    us = [_dot(t.astype(BF16), (qh[0:c] + av).astype(BF16)) for t, qh, av in zip(t_invs, qhs, akkv)]
    vus = [jnp.concatenate([v[:, sl], -u], axis=0).astype(BF16) for sl, u in zip(sls, us)]
    outs = [_group_norm(qh[c:2 * c] + _dot(gm[c:2 * c, :].astype(BF16), vu)) for qh, gm, vu in zip(qhs, gms, vus)]
    for h in heads:
        decay_col = jnp.sum(eye * decay_total[:, sls[h]], axis=-1, keepdims=True)
        h_ref[h] = decay_col * h_old[h] + lax.dot_general(zk[:, sls[h]], vus[h], _TN, preferred_element_type=F32)
    yn = jnp.concatenate(outs, axis=-1)
    y_ref[...] = ((yn * lnw_ref[...] + lnb_ref[...] + bonus_ref[...]) * g_ref[...]).astype(BF16)

    @pl.when(pl.program_id(0) == pl.num_programs(0) - 1)
    def _():
        for h in range(RW_HEADS):
            s_out_ref[h] = h_ref[h].T


def _rwkv_chunked(ops, bonus, g, ln_w, ln_b):
    m = bonus.shape[0]
    spec = pl.BlockSpec((RW_CHUNK, RW_WIDTH), lambda i: (i, 0))
    row = pl.BlockSpec((1, RW_WIDTH), lambda i: (0, 0))
    state_shape = (RW_HEADS, RW_HEAD_DIM, RW_HEAD_DIM)
    return pl.pallas_call(
        _rwkv_chunk_kernel,
        grid=(m // RW_CHUNK,),
        in_specs=[spec] * 8 + [row, row],
        out_specs=[spec, pl.BlockSpec(state_shape, lambda i: (0, 0, 0))],
        out_shape=[jax.ShapeDtypeStruct((m, RW_WIDTH), BF16), jax.ShapeDtypeStruct(state_shape, F32)],
        scratch_shapes=[pltpu.VMEM(state_shape, F32)],
        compiler_params=_params("arbitrary"),
        name="rwkv_chunked",
    )(*ops, bonus, g, ln_w, ln_b)


def _rwkv_single_step_kernel(r_ref, w_ref, k_ref, v_ref, kn_ref, b_ref, s0_ref, y_ref, s_out_ref):
    eye = _eye3()
    tb = r_ref.shape[1]

    def body(t, carry):
        sl = pl.ds(t, 1)
        state, y_row = _rwkv_step(s0_ref[t], r_ref[:, sl, :], w_ref[:, sl, :], k_ref[:, sl, :],
                                  v_ref[:, sl, :], kn_ref[:, sl, :], b_ref[:, sl, :], eye)
        s_out_ref[t] = state
        y_ref[:, sl, :] = y_row
        return carry

    lax.fori_loop(0, tb, body, 0)


def _rwkv_single_step(ops, state0, tb):
    m = ops[0].shape[1]
    spec = pl.BlockSpec((RW_HEADS, tb, RW_HEAD_DIM), lambda i: (0, i, 0))
    st_spec = pl.BlockSpec((tb, RW_HEADS, RW_HEAD_DIM, RW_HEAD_DIM), lambda i: (i, 0, 0, 0))
    return pl.pallas_call(
        _rwkv_single_step_kernel,
        grid=(m // tb,),
        in_specs=[spec] * 6 + [st_spec],
        out_specs=[spec, st_spec],
        out_shape=[jax.ShapeDtypeStruct((RW_HEADS, m, RW_HEAD_DIM), F32),
                   jax.ShapeDtypeStruct(state0.shape, F32)],
        compiler_params=_params("parallel"),
        name="rwkv_single_step",
    )(*ops, state0)


def _rwkv_post_kernel(y_ref, bonus_ref, g_ref, lnw_ref, lnb_ref, o_ref):
    yn = jnp.concatenate([_group_norm(y_ref[h]) for h in range(RW_HEADS)], axis=-1)
    out = (yn * lnw_ref[...] + lnb_ref[...] + bonus_ref[...]) * g_ref[...]
    o_ref[...] = out.astype(BF16)


def _rwkv_post(y, bonus, g, ln_w, ln_b, tm):
    m = bonus.shape[0]
    flat_spec = pl.BlockSpec((tm, RW_WIDTH), lambda i: (i, 0))
    row = pl.BlockSpec((1, RW_WIDTH), lambda i: (0, 0))
    return pl.pallas_call(
        _rwkv_post_kernel,
        grid=(m // tm,),
        in_specs=[pl.BlockSpec((RW_HEADS, tm, RW_HEAD_DIM), lambda i: (0, i, 0)),
                  flat_spec, flat_spec, row, row],
        out_specs=flat_spec,
        out_shape=jax.ShapeDtypeStruct((m, RW_WIDTH), BF16),
        compiler_params=_params("parallel"),
        name="rwkv_post",
    )(y, bonus, g, ln_w, ln_b)


CV_HALO = 32


def _conv_finish(y, cb_ref, lnw_ref, lnb_ref):
    y = y + cb_ref[...]
    mean = jnp.mean(y, axis=-1, keepdims=True)
    var = jnp.mean(jnp.square(y - mean), axis=-1, keepdims=True)
    yn = (y - mean) * lax.rsqrt(var + LN_EPS) * lnw_ref[...] + lnb_ref[...]
    return _silu(yn).astype(BF16)


def _conv_seq_kernel(p_ref, cw_ref, cb_ref, lnw_ref, lnb_ref, o_ref, tail_ref, ext_ref):
    tm = p_ref.shape[0]
    pad = CV_HALO - (CV_K - 1)

    @pl.when(pl.program_id(0) == 0)
    def _():
        ext_ref[0:CV_HALO, :] = jnp.zeros((CV_HALO, CV_WIDTH), F32)

    p = p_ref[...]
    u = p[:, 0:CV_WIDTH] * _sigmoid(p[:, CV_WIDTH:2 * CV_WIDTH])
    ext_ref[CV_HALO:CV_HALO + tm, :] = u
    acc = jnp.zeros((tm, CV_WIDTH), F32)
    for j in range(CV_K):
        acc = acc + ext_ref[pad + j:pad + j + tm, :] * cw_ref[j:j + 1, :]
    o_ref[...] = _conv_finish(acc, cb_ref, lnw_ref, lnb_ref)
    tail = ext_ref[tm:tm + CV_HALO, :]
    ext_ref[0:CV_HALO, :] = tail
    tail_ref[...] = tail


def _conv_seq(proj, cw, cb, ln_w, ln_b, tm):
    m = proj.shape[0]
    row = pl.BlockSpec((1, CV_WIDTH), lambda i: (0, 0))
    return pl.pallas_call(
        _conv_seq_kernel,
        grid=(m // tm,),
        in_specs=[pl.BlockSpec((tm, 2 * CV_WIDTH), lambda i: (i, COL_CV)),
                  pl.BlockSpec((CV_K, CV_WIDTH), lambda i: (0, 0)), row, row, row],
        out_specs=[pl.BlockSpec((tm, CV_WIDTH), lambda i: (i, 0)),
                   pl.BlockSpec((CV_HALO, CV_WIDTH), lambda i: (0, 0))],
        out_shape=[jax.ShapeDtypeStruct((m, CV_WIDTH), BF16),
                   jax.ShapeDtypeStruct((CV_HALO, CV_WIDTH), F32)],
        scratch_shapes=[pltpu.VMEM((CV_HALO + tm, CV_WIDTH), F32)],
        compiler_params=_params("arbitrary"),
        name="conv_seq",
    )(proj, cw, cb, ln_w, ln_b)


def _conv_step_kernel(p_ref, buf_ref, cw_ref, cb_ref, lnw_ref, lnb_ref, o_ref, nbuf_ref):
    p = p_ref[...]
    u = p[:, :, 0:CV_WIDTH] * _sigmoid(p[:, :, CV_WIDTH:2 * CV_WIDTH])
    buf = buf_ref[...]
    acc = (jnp.sum(buf * cw_ref[0:CV_K - 1, :][None], axis=1, keepdims=True)
           + u * cw_ref[CV_K - 1:CV_K, :][None])
    o_ref[...] = _conv_finish(acc, cb_ref, lnw_ref, lnb_ref)
    nbuf_ref[:, 0:CV_K - 2, :] = buf[:, 1:CV_K - 1, :]
    nbuf_ref[:, CV_K - 2:CV_K - 1, :] = u


def _conv_step(proj, buf, cw, cb, ln_w, ln_b, tb):
    m = proj.shape[0]
    row = pl.BlockSpec((1, CV_WIDTH), lambda i: (0, 0))
    buf_spec = pl.BlockSpec((tb, CV_K - 1, CV_WIDTH), lambda i: (i, 0, 0))
    y = pl.pallas_call(
        _conv_step_kernel,
        grid=(m // tb,),
        in_specs=[pl.BlockSpec((tb, 1, 2 * CV_WIDTH), lambda i: (i, 0, COL_CV)), buf_spec,
                  pl.BlockSpec((CV_K, CV_WIDTH), lambda i: (0, 0)), row, row, row],
        out_specs=[pl.BlockSpec((tb, 1, CV_WIDTH), lambda i: (i, 0, 0)), buf_spec],
        out_shape=[jax.ShapeDtypeStruct((m, 1, CV_WIDTH), BF16),
                   jax.ShapeDtypeStruct(buf.shape, F32)],
        compiler_params=_params("parallel"),
        name="conv_step",
    )(proj.reshape(m, 1, N_PROJ), buf, cw, cb, ln_w, ln_b)
    return y[0].reshape(m, CV_WIDTH), y[1]


def _bucket_thresholds():
    n = np.arange(0, 4 * MAX_DISTANCE, dtype=np.int64)
    max_exact = NUM_BUCKETS // 2
    nf = np.maximum(n, 1).astype(np.float32)
    large = max_exact + (np.log(nf / np.float32(max_exact)) / np.float32(math.log(MAX_DISTANCE / max_exact))
                         * np.float32(NUM_BUCKETS - max_exact)).astype(np.int32)
    large = np.minimum(large, NUM_BUCKETS - 1)
    bucket = np.where(n < max_exact, n, large)
    assert np.all(np.diff(bucket) >= 0) and bucket[-1] == NUM_BUCKETS - 1
    return [int(np.argmax(bucket >= b)) for b in range(NUM_BUCKETS)]


_BUCKET_THR = _bucket_thresholds()


def _bias_from_distance(n, rb_of_bucket):
    bias = jnp.zeros(n.shape, F32) + rb_of_bucket(NUM_BUCKETS - 1)
    for b in range(NUM_BUCKETS - 2, -1, -1):
        bias = jnp.where(n < _BUCKET_THR[b + 1], rb_of_bucket(b), bias)
    return bias


def _rank_select(scores, n_rows, limit):
    idx = lax.broadcasted_iota(jnp.int32, scores.shape, 0)
    beaten = jnp.zeros(scores.shape, F32)
    for j in range(n_rows):
        row = scores[j:j + 1, :]
        ahead = jnp.where(row > scores, 1.0, jnp.where(row == scores, jnp.where(idx > j, 1.0, 0.0), 0.0))
        beaten = beaten + ahead * jnp.where(j < limit, 1.0, 0.0)
    return jnp.where(idx < limit, jnp.where(beaten < MB_TOPK, 1.0, 0.0), 0.0)


def _block_mean_kernel(k_ref, o_ref):
    nb = o_ref.shape[1]
    k = k_ref[0].reshape(nb, MB_BLOCK, MB_HEAD_DIM)
    o_ref[0] = jnp.sum(k, axis=1) * (1.0 / MB_BLOCK)


def _block_mean(k_heads):
    h, t, d = k_heads.shape
    nb = t // MB_BLOCK
    return pl.pallas_call(
        _block_mean_kernel,
        grid=(h,),
        in_specs=[pl.BlockSpec((1, t, d), lambda i: (i, 0, 0))],
        out_specs=pl.BlockSpec((1, nb, d), lambda i: (i, 0, 0)),
        out_shape=jax.ShapeDtypeStruct((h, nb, d), F32),
        compiler_params=_params("parallel"),
        name="moba_block_mean",
    )(k_heads)


MB_HEADS_PER_STEP = 4


def _softmax_blocks(carries, scores, vt_blks):
    m_new = [jnp.maximum(c[0], jnp.max(s, axis=0, keepdims=True)) for c, s in zip(carries, scores)]
    alpha = [jnp.exp(c[0] - mn) for c, mn in zip(carries, m_new)]
    p = [jnp.exp(s - mn) for s, mn in zip(scores, m_new)]
    pv = [_dot(vt, pp.astype(BF16)) for vt, pp in zip(vt_blks, p)]
    l = [a * c[1] + jnp.sum(pp, axis=0, keepdims=True) for a, c, pp in zip(alpha, carries, p)]
    acc = [a * c[2] + x for a, c, x in zip(alpha, carries, pv)]
    return list(zip(m_new, l, acc))


def _moba_seq_kernel(rb_ref, qt_ref, k_ref, vt_ref, km_ref, o_ref, bias_ref, rowterm_ref):
    hp = qt_ref.shape[0]
    h0 = pl.program_id(0) * hp
    qi = pl.program_id(1)
    nb = km_ref.shape[1]
    scale = MB_HEAD_DIM ** -0.5
    kidx = lax.broadcasted_iota(jnp.int32, (MB_BLOCK, MB_BLOCK), 0)
    qidx = lax.broadcasted_iota(jnp.int32, (MB_BLOCK, MB_BLOCK), 1)

    @pl.when(qi == 0)
    def _():
        dist = qidx - kidx
        for hh in range(hp):
            rb = lambda b: rb_ref[b, h0 + hh]
            bias_ref[hh, 0] = jnp.where(dist >= 0, _bias_from_distance(jnp.maximum(dist, 0), rb), NEG_INF)
            bias_ref[hh, 1] = _bias_from_distance(dist + MB_BLOCK, rb)

    near = jnp.maximum(qi - 1, 0)
    qtb, carries = [], []
    for hh in range(hp):
        qt = qt_ref[hh]
        block_scores = _dot_f32(km_ref[hh], qt)
        sel = _rank_select(block_scores, nb, qi)
        rowterm_ref[hh] = jnp.where(sel > 0.0, rb_ref[NUM_BUCKETS - 1, h0 + hh], NEG_INF)
        qtb.append((qt * scale).astype(BF16))
        s = _dot(k_ref[hh, qi], qtb[hh]) + bias_ref[hh, 0]
        m0 = jnp.max(s, axis=0, keepdims=True)
        p = jnp.exp(s - m0)
        carries.append((m0, jnp.sum(p, axis=0, keepdims=True), _dot(vt_ref[hh, qi], p.astype(BF16))))

    heads = range(hp)
    sel_near = [jnp.where(rowterm_ref[hh, pl.ds(near, 1), :] > 0.5 * NEG_INF, 0.0, NEG_INF) for hh in heads]
    scores = [_dot(k_ref[hh, near], qtb[hh]) + bias_ref[hh, 1] + sel_near[hh] for hh in heads]
    carries = _softmax_blocks(carries, scores, [vt_ref[hh, near] for hh in heads])

    def body(kj, carry):
        scores = [_dot(k_ref[hh, kj], qtb[hh]) + rowterm_ref[hh, pl.ds(kj, 1), :] for hh in heads]
        return tuple(_softmax_blocks(carry, scores, [vt_ref[hh, kj] for hh in heads]))

    carries = lax.fori_loop(0, near, body, tuple(carries))
    for hh in range(hp):
        m, l, acc = carries[hh]
        o_ref[hh] = acc / l


def _moba_seq(rel_bias, q, k, v):
    t = q.shape[0]
    nb = t // MB_BLOCK
    hp = MB_HEADS_PER_STEP
    split = lambda x: x.reshape(nb, MB_BLOCK, MB_HEADS, MB_HEAD_DIM)
    qt = q.reshape(t, MB_HEADS, MB_HEAD_DIM).transpose(1, 2, 0)
    k_heads = k.reshape(t, MB_HEADS, MB_HEAD_DIM).transpose(1, 0, 2)
    kb = split(k).transpose(2, 0, 1, 3).astype(BF16)
    vt = split(v).transpose(2, 0, 3, 1).astype(BF16)
    km = _block_mean(k_heads)
    ot = pl.pallas_call(
        _moba_seq_kernel,
        grid=(MB_HEADS // hp, nb),
        in_specs=[pl.BlockSpec(memory_space=pltpu.SMEM),
                  pl.BlockSpec((hp, MB_HEAD_DIM, MB_BLOCK), lambda h, i: (h, 0, i)),
                  pl.BlockSpec((hp, nb, MB_BLOCK, MB_HEAD_DIM), lambda h, i: (h, 0, 0, 0)),
                  pl.BlockSpec((hp, nb, MB_HEAD_DIM, MB_BLOCK), lambda h, i: (h, 0, 0, 0)),
                  pl.BlockSpec((hp, nb, MB_HEAD_DIM), lambda h, i: (h, 0, 0))],
        out_specs=pl.BlockSpec((hp, MB_HEAD_DIM, MB_BLOCK), lambda h, i: (h, 0, i)),
        out_shape=jax.ShapeDtypeStruct((MB_HEADS, MB_HEAD_DIM, t), F32),
        scratch_shapes=[pltpu.VMEM((hp, 2, MB_BLOCK, MB_BLOCK), F32), pltpu.VMEM((hp, nb, MB_BLOCK), F32)],
        compiler_params=_params("parallel", "arbitrary"),
        name="moba_seq",
    )(rel_bias, qt, kb, vt, km)
    return ot.transpose(2, 0, 1).reshape(t, MB_WIDTH).astype(BF16)


def _moba_paged_kernel(n_pages, pt_ref, q_ref, kn_ref, vn_ref, rbt_ref, *refs):
    k_pages = refs[:n_pages]
    v_pages = refs[n_pages:2 * n_pages]
    o_ref, bias_ref = refs[2 * n_pages:]
    past = n_pages * PAGE_SIZE
    n_blocks = past // MB_BLOCK
    pages_per_block = MB_BLOCK // PAGE_SIZE
    scale = MB_HEAD_DIM ** -0.5
    rb = lambda b: rbt_ref[:, :, b:b + 1]

    @pl.when(pl.program_id(0) == 0)
    def _():
        dist = PAGE_SIZE - lax.broadcasted_iota(jnp.int32, (MB_HEADS, 1, PAGE_SIZE), 2)
        bias_ref[...] = _bias_from_distance(dist, rb)

    di = lax.broadcasted_iota(jnp.int32, (MB_HEAD_DIM, MB_HEAD_DIM), 0)
    dj = lax.broadcasted_iota(jnp.int32, (MB_HEAD_DIM, MB_HEAD_DIM), 1)
    eye = jnp.where(di == dj, 1.0, 0.0)[None]
    q = q_ref[0]
    q_col = jnp.sum(eye * q, axis=-1, keepdims=True)
    raw = [jnp.sum(k_pages[pg][...] * q_col, axis=1, keepdims=True) for pg in range(n_pages)]
    scores = []
    for j in range(n_blocks):
        tot = raw[pages_per_block * j]
        for e in range(1, pages_per_block):
            tot = tot + raw[pages_per_block * j + e]
        scores.append(jnp.sum(tot, axis=-1, keepdims=True) * (1.0 / MB_BLOCK))
    far_bias = rb(NUM_BUCKETS - 1)
    rowterms = []
    for j in range(n_blocks):
        beaten = jnp.zeros((MB_HEADS, 1, 1), F32)
        for i in range(n_blocks):
            if i != j:
                ahead = scores[i] > scores[j] if i > j else scores[i] >= scores[j]
                beaten = beaten + jnp.where(ahead, 1.0, 0.0)
        rowterms.append(jnp.where(beaten < MB_TOPK, far_bias, NEG_INF))
    lg_own = jnp.sum(kn_ref[0] * q, axis=-1, keepdims=True) * scale + rb(0)
    logits = []
    for pg in range(n_pages):
        term = rowterms[pg // pages_per_block]
        if pg == n_pages - 1:
            term = bias_ref[...] + jnp.where(term > 0.5 * NEG_INF, 0.0, NEG_INF)
        logits.append(raw[pg] * scale + term)
    m_row = logits[0]
    for lg in logits[1:]:
        m_row = jnp.maximum(m_row, lg)
    m = jnp.maximum(jnp.max(m_row, axis=-1, keepdims=True), lg_own)
    p_own = jnp.exp(lg_own - m)
    l_row = jnp.zeros((MB_HEADS, 1, PAGE_SIZE), F32)
    acc = jnp.zeros((MB_HEADS, MB_HEAD_DIM, PAGE_SIZE), F32)
    for pg in range(n_pages):
        p = jnp.exp(logits[pg] - m)
        l_row = l_row + p
        acc = acc + v_pages[pg][...] * p
    out_col = jnp.sum(acc, axis=-1, keepdims=True)
    out = jnp.sum(eye * out_col, axis=1, keepdims=True) + p_own * vn_ref[0]
    l = jnp.sum(l_row, axis=-1, keepdims=True) + p_own
    o_ref[0] = (out / l).astype(BF16)


def _moba_paged(rel_bias, q, k_new, v_new, cache_k, cache_v, layer, page_table):
    b, n_pages = page_table.shape
    assert PAGE_SIZE >= _BUCKET_THR[NUM_BUCKETS - 1] and MB_BLOCK % PAGE_SIZE == 0
    head4 = pl.BlockSpec((1, MB_HEADS, 1, MB_HEAD_DIM), lambda i, pt: (i, 0, 0, 0))

    def page_spec(pg):
        return pl.BlockSpec((None, None, MB_HEADS, MB_HEAD_DIM, PAGE_SIZE),
                            lambda i, pt: (layer, pt[i, pg], 0, 0, 0))

    grid_spec = pltpu.PrefetchScalarGridSpec(
        num_scalar_prefetch=1,
        grid=(b,),
        in_specs=[head4, head4, head4, pl.BlockSpec((MB_HEADS, 1, NUM_BUCKETS), lambda i, pt: (0, 0, 0))]
        + [page_spec(pg) for pg in range(n_pages)] * 2,
        out_specs=head4,
        scratch_shapes=[pltpu.VMEM((MB_HEADS, 1, PAGE_SIZE), F32)],
    )
    h4 = lambda x: x.reshape(b, MB_HEADS, 1, MB_HEAD_DIM)
    kt = cache_k.transpose(0, 1, 3, 4, 2)
    vt = cache_v.transpose(0, 1, 3, 4, 2)
    out = pl.pallas_call(
        functools.partial(_moba_paged_kernel, n_pages),
        grid_spec=grid_spec,
        out_shape=jax.ShapeDtypeStruct((b, MB_HEADS, 1, MB_HEAD_DIM), BF16),
        compiler_params=_params("arbitrary"),
        name="moba_paged",
    )(page_table, h4(q), h4(k_new), h4(v_new), rel_bias.T.reshape(MB_HEADS, 1, NUM_BUCKETS),
      *([kt] * n_pages), *([vt] * n_pages))
    return out.reshape(b, MB_WIDTH)


def _merge_kernel(yrw_ref, ymb_ref, ycv_ref, g0_ref, g1_ref, g2_ref, wrw_ref, wmb_ref, wcv_ref, o_ref):
    merged = (g0_ref[...] * _dot(yrw_ref[...], wrw_ref[...])
              + g1_ref[...] * _dot(ymb_ref[...], wmb_ref[...])
              + g2_ref[...] * _dot(ycv_ref[...], wcv_ref[...]))
    o_ref[...] = merged.astype(BF16)


def _merge(y_rw, y_mb, y_cv, proj, w_rw, w_mb, w_cv, tm):
    m = proj.shape[0]
    rows = lambda n: pl.BlockSpec((tm, n), lambda i: (i, 0))
    gate = lambda br: pl.BlockSpec((tm, D_MODEL), lambda i: (i, COL_GATE + br))
    full = lambda n: pl.BlockSpec((n, D_MODEL), lambda i: (0, 0))
    return pl.pallas_call(
        _merge_kernel,
        grid=(m // tm,),
        in_specs=[rows(RW_WIDTH), rows(MB_WIDTH), rows(CV_WIDTH), gate(0), gate(1), gate(2),
                  full(RW_WIDTH), full(MB_WIDTH), full(CV_WIDTH)],
        out_specs=rows(D_MODEL),
        out_shape=jax.ShapeDtypeStruct((m, D_MODEL), BF16),
        compiler_params=_params("parallel"),
        name="branch_merge",
    )(y_rw, y_mb, y_cv, proj, proj, proj, w_rw, w_mb, w_cv)


def _out_proj_kernel(h_ref, x_ref, w_ref, o_ref):
    o_ref[...] = h_ref[...] + _dot(x_ref[...], w_ref[...])


def _out_proj(h, merged, w_out, tm):
    m = h.shape[0]
    rows = pl.BlockSpec((tm, D_MODEL), lambda i: (i, 0))
    return pl.pallas_call(
        _out_proj_kernel,
        grid=(m // tm,),
        in_specs=[rows, rows, pl.BlockSpec((D_MODEL, D_MODEL), lambda i: (0, 0))],
        out_specs=rows,
        out_shape=jax.ShapeDtypeStruct((m, D_MODEL), F32),
        compiler_params=_params("parallel"),
        name="out_proj",
    )(h, merged, w_out)


def _ffn_kernel(h_ref, g_ref, wg_ref, wu_ref, wd_ref, o_ref, xn_ref):
    @pl.when(pl.program_id(1) == 0)
    def _():
        x = h_ref[...]
        ms = jnp.mean(x * x, axis=-1, keepdims=True)
        xn_ref[...] = (x * lax.rsqrt(ms + NORM_EPS) * g_ref[...]).astype(BF16)
        o_ref[...] = x

    xn = xn_ref[...]
    mid = _silu(_dot(xn, wg_ref[...])) * _dot(xn, wu_ref[...])
    o_ref[...] += _dot(mid.astype(BF16), wd_ref[...])


def _ffn(h, g, wg, wu, wd, tm, tf):
    m = h.shape[0]
    d_ff = wg.shape[1]
    rows = pl.BlockSpec((tm, D_MODEL), lambda i, f: (i, 0))
    return pl.pallas_call(
        _ffn_kernel,
        grid=(m // tm, d_ff // tf),
        in_specs=[rows, pl.BlockSpec((1, D_MODEL), lambda i, f: (0, 0)),
                  pl.BlockSpec((D_MODEL, tf), lambda i, f: (0, f)),
                  pl.BlockSpec((D_MODEL, tf), lambda i, f: (0, f)),
                  pl.BlockSpec((tf, D_MODEL), lambda i, f: (f, 0))],
        out_specs=rows,
        out_shape=jax.ShapeDtypeStruct((m, D_MODEL), F32),
        scratch_shapes=[pltpu.VMEM((tm, D_MODEL), BF16)],
        compiler_params=_params("parallel", "arbitrary"),
        name="ffn",
    )(h, g, wg, wu, wd)


def _dot_f32_nt(a, b):
    nt = lambda x, y: lax.dot_general(x, y, _NT, preferred_element_type=F32)
    a_hi, a_mid, a_lo = _split3(a)
    b_hi, b_mid, b_lo = _split3(b)
    return (nt(a_hi, b_hi) + (nt(a_hi, b_mid) + nt(a_mid, b_hi))
            + (nt(a_mid, b_mid) + nt(a_hi, b_lo) + nt(a_lo, b_hi)))


def _router_kernel(h_ref, g_ref, wt_ref, b_ref, before_ref, xn_ref, gate_ref, pos_ref, cnt_ref):
    x = h_ref[...]
    ms = jnp.mean(x * x, axis=-1, keepdims=True)
    xn = x * lax.rsqrt(ms + NORM_EPS) * g_ref[...]
    xn_ref[...] = xn.astype(BF16)
    logits = _dot_f32_nt(wt_ref[...], xn) + b_ref[...]
    idx = lax.broadcasted_iota(jnp.int32, logits.shape, 0).astype(F32)
    m1 = jnp.max(logits, axis=0, keepdims=True)
    i1 = jnp.min(jnp.where(logits == m1, idx, float(N_EXPERTS)), axis=0, keepdims=True)
    rest = jnp.where(idx == i1, -jnp.inf, logits)
    m2 = jnp.max(rest, axis=0, keepdims=True)
    i2 = jnp.min(jnp.where(rest == m2, idx, float(N_EXPERTS)), axis=0, keepdims=True)
    e2 = jnp.exp(m2 - m1)
    gates = jnp.where(idx == i1, 1.0 / (1.0 + e2), 0.0) + jnp.where(idx == i2, e2 / (1.0 + e2), 0.0)
    member = jnp.where(gates > 0.0, 1.0, 0.0)
    gate_ref[0] = gates
    pos_ref[0] = _dot(member.astype(BF16), before_ref[...])
    cnt_ref[0] = jnp.sum(member, axis=-1, keepdims=True).astype(jnp.int32)


def _router(h, g, w_router, b_router, tm):
    m = h.shape[0]
    n_tiles = m // tm
    before = jnp.asarray(np.triu(np.ones((tm, tm), np.float32), 1), BF16)
    tile3 = pl.BlockSpec((1, N_EXPERTS, tm), lambda i: (i, 0, 0))
    per_tile = jax.ShapeDtypeStruct((n_tiles, N_EXPERTS, tm), F32)
    return pl.pallas_call(
        _router_kernel,
        grid=(n_tiles,),
        in_specs=[pl.BlockSpec((tm, D_MODEL), lambda i: (i, 0)),
                  pl.BlockSpec((1, D_MODEL), lambda i: (0, 0)),
                  pl.BlockSpec((N_EXPERTS, D_MODEL), lambda i: (0, 0)),
                  pl.BlockSpec((N_EXPERTS, 1), lambda i: (0, 0)),
                  pl.BlockSpec((tm, tm), lambda i: (0, 0))],
        out_specs=[pl.BlockSpec((tm, D_MODEL), lambda i: (i, 0)), tile3, tile3,
                   pl.BlockSpec((1, N_EXPERTS, 1), lambda i: (i, 0, 0))],
        out_shape=[jax.ShapeDtypeStruct((m, D_MODEL), BF16), per_tile, per_tile,
                   jax.ShapeDtypeStruct((n_tiles, N_EXPERTS, 1), jnp.int32)],
        compiler_params=_params("parallel"),
        name="moe_router",
    )(h, g, w_router.T, b_router.reshape(N_EXPERTS, 1), before)


def _moe_kernel(sub, cnt_ref, xn_ref, gate_ref, pos_ref, wg_ref, wu_ref, wd_ref, o_ref, xg_ref, gg_ref, acc_ref):
    i = pl.program_id(0)
    e = pl.program_id(1)
    f = pl.program_id(2)
    tm = xn_ref.shape[0]
    n_sub = (cnt_ref[i * N_EXPERTS + e] + (sub - 1)) // sub
    pos_row = pos_ref[0, pl.ds(e, 1), :]
    gate_row = gate_ref[0, pl.ds(e, 1), :]
    slot = lax.broadcasted_iota(jnp.int32, (sub, tm), 0)

    def one_hot(s):
        wanted = (slot + s * sub).astype(F32)
        return jnp.where(gate_row > 0.0, jnp.where(pos_row == wanted, 1.0, 0.0), 0.0)

    def rows_of(s):
        return pl.ds(pl.multiple_of(s * sub, sub), sub)

    @pl.when((e == 0) & (f == 0))
    def _():
        o_ref[...] = jnp.zeros_like(o_ref)

    @pl.when(f == 0)
    def _():
        def pack(s, carry):
            pick = one_hot(s)
            xg_ref[rows_of(s), :] = _dot(pick.astype(BF16), xn_ref[...]).astype(BF16)
            gg_ref[rows_of(s), :] = jnp.sum(pick * gate_row, axis=-1, keepdims=True)
            acc_ref[rows_of(s), :] = jnp.zeros((sub, D_MODEL), F32)
            return carry

        lax.fori_loop(0, n_sub, pack, 0)

    def expert(s, carry):
        x = xg_ref[rows_of(s), :]
        mid = _silu(_dot(x, wg_ref[0])) * _dot(x, wu_ref[0]) * gg_ref[rows_of(s), :]
        acc_ref[rows_of(s), :] += _dot(mid.astype(BF16), wd_ref[0])
        return carry

    lax.fori_loop(0, n_sub, expert, 0)

    @pl.when(f == pl.num_programs(2) - 1)
    def _():
        def unpack(s, carry):
            pick = one_hot(s).astype(BF16)
            a = acc_ref[rows_of(s), :]
            hi = a.astype(BF16)
            lo = (a - hi.astype(F32)).astype(BF16)
            o_ref[...] += (lax.dot_general(pick, hi, _TN, preferred_element_type=F32)
                           + lax.dot_general(pick, lo, _TN, preferred_element_type=F32))
            return carry

        lax.fori_loop(0, n_sub, unpack, 0)


def _moe(xn, gates, pos, counts, wg, wu, wd, tm, tf):
    m = xn.shape[0]
    n_e, _, d_ff = wg.shape
    sub = min(tm, 256)
    tile3 = pl.BlockSpec((1, N_EXPERTS, tm), lambda i, e, f, cnt: (i, 0, 0))
    grid_spec = pltpu.PrefetchScalarGridSpec(
        num_scalar_prefetch=1,
        grid=(m // tm, n_e, d_ff // tf),
        in_specs=[pl.BlockSpec((tm, D_MODEL), lambda i, e, f, cnt: (i, 0)), tile3, tile3,
                  pl.BlockSpec((1, D_MODEL, tf), lambda i, e, f, cnt: (e, 0, f)),
                  pl.BlockSpec((1, D_MODEL, tf), lambda i, e, f, cnt: (e, 0, f)),
                  pl.BlockSpec((1, tf, D_MODEL), lambda i, e, f, cnt: (e, f, 0))],
        out_specs=pl.BlockSpec((tm, D_MODEL), lambda i, e, f, cnt: (i, 0)),
        scratch_shapes=[pltpu.VMEM((tm, D_MODEL), BF16), pltpu.VMEM((tm, 1), F32), pltpu.VMEM((tm, D_MODEL), F32)],
    )
    return pl.pallas_call(
        functools.partial(_moe_kernel, sub),
        grid_spec=grid_spec,
        out_shape=jax.ShapeDtypeStruct((m, D_MODEL), F32),
        compiler_params=_params("parallel", "arbitrary", "arbitrary"),
        name="moe_experts",
    )(counts.reshape(-1), xn, gates, pos, wg, wu, wd)


def _final_norm_kernel(n_in, *refs):
    g_ref, o_ref = refs[n_in], refs[n_in + 1]
    x = refs[0][...]
    for r in refs[1:n_in]:
        x = x + r[...]
    ms = jnp.mean(x * x, axis=-1, keepdims=True)
    o_ref[...] = x * lax.rsqrt(ms + NORM_EPS) * g_ref[...]


def _final_norm(parts, g, tm):
    m = parts[0].shape[0]
    rows = pl.BlockSpec((tm, D_MODEL), lambda i: (i, 0))
    return pl.pallas_call(
        functools.partial(_final_norm_kernel, len(parts)),
        grid=(m // tm,),
        in_specs=[rows] * len(parts) + [pl.BlockSpec((1, D_MODEL), lambda i: (0, 0))],
        out_specs=rows,
        out_shape=jax.ShapeDtypeStruct((m, D_MODEL), F32),
        compiler_params=_params("parallel"),
        name="final_norm",
    )(*parts, g)


def _add_kernel(a_ref, b_ref, o_ref):
    o_ref[...] = a_ref[...] + b_ref[...]


def _add(a, b, tm):
    rows = pl.BlockSpec((tm, D_MODEL), lambda i: (i, 0))
    return pl.pallas_call(
        _add_kernel,
        grid=(a.shape[0] // tm,),
        in_specs=[rows, rows],
        out_specs=rows,
        out_shape=jax.ShapeDtypeStruct(a.shape, F32),
        compiler_params=_params("parallel"),
        name="residual_add",
    )(a, b)


def _pad_rw_cols(x):
    pad = jnp.zeros(x.shape[:-1] + (RW_PAD - RW_COLS,), x.dtype)
    return jnp.concatenate([x[..., :RW_COLS], pad, x[..., RW_COLS:]], axis=-1)


def _lora_slab(w, row0):
    slab = jnp.zeros((LORA_PAD, RW_WIDTH), F32)
    return lax.dynamic_update_slice(slab, w, (row0, 0)).astype(BF16)


def _layer_params(l, P):
    row = lambda x: x.reshape(1, -1)
    lp = dict(
        norm_mix=row(P['norm_mix'][l]), norm_ffn=row(P['norm_ffn'][l]),
        w_in=_pad_rw_cols(P['w_in'][l]).astype(BF16),
        rw_vecs=[row(_pad_rw_cols(P['rw_mu'][l])), row(P['rw_w0'][l]), row(P['rw_a0'][l]),
                 row(P['rw_kk'][l]), row(P['rw_ka'][l]), row(P['rw_rk'][l])],
        rw_mats=[_lora_slab(P['rw_w2'][l], 0), _lora_slab(P['rw_a2'][l], DECAY_LORA),
                 _lora_slab(P['rw_g2'][l], DECAY_LORA + AAA_LORA)],
        rw_ln_w=row(P['rw_ln_w'][l]), rw_ln_b=row(P['rw_ln_b'][l]),
        cv_w=P['cv_w'][l], cv_b=row(P['cv_b'][l]), cv_ln_w=row(P['cv_ln_w'][l]), cv_ln_b=row(P['cv_ln_b'][l]),
        w_br_rw=P['w_br_rw'][l].astype(BF16), w_br_mb=P['w_br_mb'][l].astype(BF16),
        w_br_cv=P['w_br_cv'][l].astype(BF16), w_out=P['w_out'][l].astype(BF16),
    )
    i = l // 2
    if l % 2 == 0:
        lp.update(wg=P['ffn_wg'][i].astype(BF16), wu=P['ffn_wu'][i].astype(BF16), wd=P['ffn_wd'][i].astype(BF16))
    else:
        lp.update(wg=P['moe_wg'][i].astype(BF16), wu=P['moe_wu'][i].astype(BF16), wd=P['moe_wd'][i].astype(BF16),
                  router=P['moe_router'][i], router_b=P['moe_router_b'][i])
    return lp


def _channel_mix(h, lp, tm_dense, tm_moe):
    if 'router' in lp:
        xn, gates, pos, counts = _router(h, lp['norm_ffn'], lp['router'], lp['router_b'], tm_moe)
        return [h, _moe(xn, gates, pos, counts, lp['wg'], lp['wu'], lp['wd'], tm_moe, 256)]
    return [_ffn(h, lp['norm_ffn'], lp['wg'], lp['wu'], lp['wd'], tm_dense, 512)]


def _as_one(parts, tm):
    return parts[0] if len(parts) == 1 else _add(parts[0], parts[1], tm)


def _run_prompt(x, layers, norm_final, rel_bias):
    t = x.shape[0]
    parts = [x]
    outs = dict(k=[], v=[], wkv=[], shift=[], conv=[])
    for lp in layers:
        h = _as_one(parts, 512)
        proj = _norm_proj(h, lp['norm_mix'], lp['w_in'], 1024)
        *ops, g, bonus = _rwkv_prep(proj, None, lp['rw_vecs'], lp['rw_mats'], 128)
        y_rw, s_final = _rwkv_chunked(ops, bonus, g, lp['rw_ln_w'], lp['rw_ln_b'])
        q = proj[:, RW_PAD:RW_PAD + MB_WIDTH]
        k_new = proj[:, RW_PAD + MB_WIDTH:RW_PAD + 2 * MB_WIDTH]
        v_new = proj[:, RW_PAD + 2 * MB_WIDTH:RW_PAD + 3 * MB_WIDTH]
        y_mb = _moba_seq(rel_bias, q, k_new, v_new)
        y_cv, tail = _conv_seq(proj, lp['cv_w'], lp['cv_b'], lp['cv_ln_w'], lp['cv_ln_b'], 512)
        merged = _merge(y_rw, y_mb, y_cv, proj, lp['w_br_rw'], lp['w_br_mb'], lp['w_br_cv'], 256)
        h = _out_proj(h, merged, lp['w_out'], 512)
        parts = _channel_mix(h, lp, 512, 1024)
        outs['k'].append(k_new.reshape(1, t, MB_HEADS, MB_HEAD_DIM))
        outs['v'].append(v_new.reshape(1, t, MB_HEADS, MB_HEAD_DIM))
        outs['wkv'].append(s_final[None])
        outs['shift'].append(proj[t - 1:t, :RW_COLS])
        outs['conv'].append(tail[None, CV_HALO - (CV_K - 1):, :])
    y = _final_norm(parts, norm_final, 512)
    return (y[None],) + tuple(jnp.stack(outs[n]) for n in ('k', 'v', 'wkv', 'shift', 'conv'))


def _run_sample(x, layers, norm_final, rel_bias, cache_k, cache_v, state_wkv, state_shift, state_conv, page_table):
    b = x.shape[0]
    parts = [x]
    outs = dict(k=[], v=[], wkv=[], shift=[], conv=[])
    for l, lp in enumerate(layers):
        h = _as_one(parts, b)
        proj = _norm_proj(h, lp['norm_mix'], lp['w_in'], b)
        *ops, g, bonus = _rwkv_prep(proj, _pad_rw_cols(state_shift[l]), lp['rw_vecs'], lp['rw_mats'], b)
        y, s_new = _rwkv_single_step(ops, state_wkv[l], 8)
        y_rw = _rwkv_post(y, bonus, g, lp['rw_ln_w'], lp['rw_ln_b'], b)
        q = proj[:, RW_PAD:RW_PAD + MB_WIDTH]
        k_new = proj[:, RW_PAD + MB_WIDTH:RW_PAD + 2 * MB_WIDTH]
        v_new = proj[:, RW_PAD + 2 * MB_WIDTH:RW_PAD + 3 * MB_WIDTH]
        y_mb = _moba_paged(rel_bias, q, k_new, v_new, cache_k, cache_v, l, page_table)
        y_cv, conv_new = _conv_step(proj, state_conv[l], lp['cv_w'], lp['cv_b'], lp['cv_ln_w'], lp['cv_ln_b'], 8)
        merged = _merge(y_rw, y_mb, y_cv, proj, lp['w_br_rw'], lp['w_br_mb'], lp['w_br_cv'], b)
        h = _out_proj(h, merged, lp['w_out'], b)
        parts = _channel_mix(h, lp, b, b)
        outs['k'].append(k_new.reshape(b, 1, MB_HEADS, MB_HEAD_DIM))
        outs['v'].append(v_new.reshape(b, 1, MB_HEADS, MB_HEAD_DIM))
        outs['wkv'].append(s_new)
        outs['shift'].append(proj[:, :RW_COLS])
        outs['conv'].append(conv_new)
    y = _final_norm(parts, norm_final, b)
    return (y[:, None, :],) + tuple(jnp.stack(outs[n]) for n in ('k', 'v', 'wkv', 'shift', 'conv'))


def kernel(x_prompt, x_sample, cache_k, cache_v, state_wkv, state_shift, state_conv, page_table, norm_mix, norm_ffn, norm_final, w_in, rw_mu, rw_w0, rw_w2, rw_a0, rw_a2, rw_g2, rw_kk, rw_ka, rw_rk, rw_ln_w, rw_ln_b, rel_bias, cv_w, cv_b, cv_ln_w, cv_ln_b, w_br_rw, w_br_mb, w_br_cv, w_out, ffn_wg, ffn_wu, ffn_wd, moe_router, moe_router_b, moe_wg, moe_wu, moe_wd):
    P = dict(norm_mix=norm_mix, norm_ffn=norm_ffn, w_in=w_in,
             rw_mu=rw_mu, rw_w0=rw_w0, rw_w2=rw_w2, rw_a0=rw_a0, rw_a2=rw_a2, rw_g2=rw_g2,
             rw_kk=rw_kk, rw_ka=rw_ka, rw_rk=rw_rk, rw_ln_w=rw_ln_w, rw_ln_b=rw_ln_b,
             cv_w=cv_w, cv_b=cv_b, cv_ln_w=cv_ln_w, cv_ln_b=cv_ln_b,
             w_br_rw=w_br_rw, w_br_mb=w_br_mb, w_br_cv=w_br_cv, w_out=w_out,
             ffn_wg=ffn_wg, ffn_wu=ffn_wu, ffn_wd=ffn_wd,
             moe_router=moe_router, moe_router_b=moe_router_b, moe_wg=moe_wg, moe_wu=moe_wu, moe_wd=moe_wd)
    depth = w_in.shape[0]
    layers = [_layer_params(l, P) for l in range(depth)]
    g_final = norm_final.reshape(1, -1)
    assert x_prompt.shape[0] == 1 and x_sample.shape[1] == 1
    prompt = _run_prompt(x_prompt[0], layers, g_final, rel_bias)
    sample = _run_sample(x_sample[:, 0, :], layers, g_final, rel_bias, cache_k, cache_v,
                         state_wkv, state_shift, state_conv, page_table)
    y_p, k_p, v_p, wkv_p, shift_p, conv_p = prompt
    y_s, k_s, v_s, wkv_s, shift_s, conv_s = sample
    return (y_p, y_s, k_p, v_p, wkv_p, shift_p, conv_p,
            k_s, v_s, wkv_s, shift_s, conv_s)
```

```python
import functools
import math

import numpy as np
import jax
import jax.numpy as jnp
from jax import lax
from jax.experimental import pallas as pl
from jax.experimental.pallas import tpu as pltpu

F32 = jnp.float32
BF16 = jnp.bfloat16

D_MODEL = 2048
RW_HEAD_DIM = 64
RW_WIDTH = 1024
RW_HEADS = 16
DECAY_LORA = 64
AAA_LORA = 64
GATE_LORA = 160
RW_COLS = 3 * RW_WIDTH + DECAY_LORA + AAA_LORA + GATE_LORA
RW_PAD = 3584
LORA_PAD = RW_PAD - 3 * RW_WIDTH
GN_EPS = 64e-5
MB_HEAD_DIM = 64
MB_WIDTH = 512
MB_HEADS = 8
MB_BLOCK = 256
MB_TOPK = 3
NUM_BUCKETS = 32
MAX_DISTANCE = 128
NEG_INF = -1e30
CV_WIDTH = 512
CV_K = 31
LN_EPS = 1e-5
N_BRANCH = 3
N_EXPERTS = 8
TOP_K = 2
NORM_EPS = 1e-6
PAGE_SIZE = 128

PROJ_TILE = 1024
N_PROJ = RW_PAD + 3 * MB_WIDTH + 2 * CV_WIDTH + N_BRANCH * D_MODEL
COL_CV = (RW_PAD + 3 * MB_WIDTH) // (2 * CV_WIDTH)
COL_GATE = (RW_PAD + 3 * MB_WIDTH + 2 * CV_WIDTH) // D_MODEL
GATE_TILE0 = (RW_PAD + 3 * MB_WIDTH + 2 * CV_WIDTH) // PROJ_TILE

VMEM_LIMIT_BYTES = 56 * 1024 * 1024


def _params(*semantics):
    return pltpu.CompilerParams(dimension_semantics=semantics, vmem_limit_bytes=VMEM_LIMIT_BYTES)


def _sigmoid(x):
    return 1.0 / (1.0 + jnp.exp(-x))


def _silu(x):
    return x * _sigmoid(x)


def _split3(x):
    hi = x.astype(BF16)
    r1 = x - hi.astype(F32)
    mid = r1.astype(BF16)
    lo = (r1 - mid.astype(F32)).astype(BF16)
    return hi, mid, lo


def _dot(a, b):
    return jnp.dot(a, b, preferred_element_type=F32)


def _dot_exact_rhs(x, sel):
    hi, mid, lo = _split3(x)
    return _dot(hi, sel) + _dot(mid, sel) + _dot(lo, sel)


def _dot_f32(a, b):
    a_hi, a_mid, a_lo = _split3(a)
    b_hi, b_mid, b_lo = _split3(b)
    return (_dot(a_hi, b_hi) + (_dot(a_hi, b_mid) + _dot(a_mid, b_hi))
            + (_dot(a_mid, b_mid) + _dot(a_hi, b_lo) + _dot(a_lo, b_hi)))


def _norm_proj_kernel(x_ref, g_ref, w_ref, o_ref, xn_ref):
    j = pl.program_id(1)

    @pl.when(j == 0)
    def _():
        x = x_ref[...]
        ms = jnp.mean(x * x, axis=-1, keepdims=True)
        xn_ref[...] = (x * lax.rsqrt(ms + NORM_EPS) * g_ref[...]).astype(BF16)

    acc = _dot(xn_ref[...], w_ref[...])

    @pl.when(j < GATE_TILE0)
    def _():
        o_ref[...] = acc

    @pl.when(j >= GATE_TILE0)
    def _():
        o_ref[...] = _sigmoid(acc)


def _norm_proj(x, g, w, tm):
    m = x.shape[0]
    return pl.pallas_call(
        _norm_proj_kernel,
        grid=(m // tm, N_PROJ // PROJ_TILE),
        in_specs=[pl.BlockSpec((tm, D_MODEL), lambda i, j: (i, 0)),
                  pl.BlockSpec((1, D_MODEL), lambda i, j: (0, 0)),
                  pl.BlockSpec((D_MODEL, PROJ_TILE), lambda i, j: (0, j))],
        out_specs=pl.BlockSpec((tm, PROJ_TILE), lambda i, j: (i, j)),
        out_shape=jax.ShapeDtypeStruct((m, N_PROJ), F32),
        scratch_shapes=[pltpu.VMEM((tm, D_MODEL), BF16)],
        compiler_params=_params("parallel", "arbitrary"),
        name="norm_proj",
    )(x, g, w)


def _rwkv_prep_kernel(sequential, *refs):
    if sequential:
        (p_ref, mu_ref, w0_ref, a0_ref, kk_ref, ka_ref, rk_ref, w2_ref, a2_ref, g2_ref,
         r_o, w_o, k_o, v_o, kn_o, b_o, g_o, bonus_o, carry_ref) = refs
    else:
        (p_ref, prev_ref, mu_ref, w0_ref, a0_ref, kk_ref, ka_ref, rk_ref, w2_ref, a2_ref, g2_ref,
         r_o, w_o, k_o, v_o, kn_o, b_o, g_o, bonus_o) = refs
    p = p_ref[...]
    tm = p.shape[0]
    if sequential:
        @pl.when(pl.program_id(0) == 0)
        def _():
            carry_ref[...] = jnp.zeros_like(carry_ref)

        rows = lax.broadcasted_iota(jnp.int32, p.shape, 0)
        prev = jnp.where(rows == 0, carry_ref[...], pltpu.roll(p, 1, 0))
        carry_ref[...] = p[tm - 1:tm, :]
    else:
        prev = prev_ref[...]
    xs = p + (prev - p) * mu_ref[...]
    r = xs[:, 0:RW_WIDTH]
    k = xs[:, RW_WIDTH:2 * RW_WIDTH]
    v = xs[:, 2 * RW_WIDTH:3 * RW_WIDTH]
    lora = xs[:, 3 * RW_WIDTH:RW_PAD]
    wx = w0_ref[...] + _dot(jnp.tanh(lora).astype(BF16), w2_ref[...])
    w_log = jnp.minimum(wx, 0.0) - jnp.log(1.0 + jnp.exp(-jnp.abs(wx))) - 0.5
    log_decay = -jnp.exp(w_log)
    a = _sigmoid(a0_ref[...] + _dot(lora.astype(BF16), a2_ref[...]))
    g_o[...] = _dot(_sigmoid(lora).astype(BF16), g2_ref[...])
    kk = k * kk_ref[...]
    k_mod = k * (1.0 + (a - 1.0) * ka_ref[...])
    rkr = r * k_mod * rk_ref[...]
    if sequential:
        r_o[...] = r
        w_o[...] = log_decay
        k_o[...] = k_mod
        v_o[...] = v
    else:
        decay = jnp.exp(log_decay)
    for h in range(RW_HEADS):
        sl = slice(h * RW_HEAD_DIM, (h + 1) * RW_HEAD_DIM)
        kk_h = kk[:, sl]
        nrm = jnp.sqrt(jnp.sum(kk_h * kk_h, axis=-1, keepdims=True))
        kn_h = kk_h / jnp.maximum(nrm, 1e-12)
        if sequential:
            kn_o[:, sl] = kn_h
            b_o[:, sl] = kn_h * a[:, sl]
        else:
            r_o[h] = r[:, sl]
            w_o[h] = decay[:, sl]
            k_o[h] = k_mod[:, sl]
            v_o[h] = v[:, sl]
            kn_o[h] = kn_h
            b_o[h] = kn_h * a[:, sl]
        bonus_o[:, sl] = jnp.sum(rkr[:, sl], axis=-1, keepdims=True) * v[:, sl]


def _rwkv_prep(proj, prev, vecs, mats, tm):
    m = proj.shape[0]
    sequential = prev is None
    row = lambda n: pl.BlockSpec((1, n), lambda i: (0, 0))
    in_specs = [pl.BlockSpec((tm, RW_PAD), lambda i: (i, 0))]
    args = [proj]
    if not sequential:
        in_specs.append(pl.BlockSpec((tm, RW_PAD), lambda i: (i, 0)))
        args.append(prev)
    in_specs += [row(RW_PAD)] + [row(RW_WIDTH)] * 5
    in_specs += [pl.BlockSpec((LORA_PAD, RW_WIDTH), lambda i: (0, 0))] * 3
    flat = jax.ShapeDtypeStruct((m, RW_WIDTH), F32)
    flat_spec = pl.BlockSpec((tm, RW_WIDTH), lambda i: (i, 0))
    if sequential:
        op_shape, op_spec = flat, flat_spec
    else:
        op_shape = jax.ShapeDtypeStruct((RW_HEADS, m, RW_HEAD_DIM), F32)
        op_spec = pl.BlockSpec((RW_HEADS, tm, RW_HEAD_DIM), lambda i: (0, i, 0))
    return pl.pallas_call(
        functools.partial(_rwkv_prep_kernel, sequential),
        grid=(m // tm,),
        in_specs=in_specs,
        out_specs=[op_spec] * 6 + [flat_spec] * 2,
        out_shape=[op_shape] * 6 + [flat] * 2,
        scratch_shapes=[pltpu.VMEM((1, RW_PAD), F32)] if sequential else [],
        compiler_params=_params("arbitrary"),
        name="rwkv_prep",
    )(*args, *vecs, *mats)


def _rwkv_step(state, r_t, w_t, k_t, v_t, kn_t, b_t, eye):
    sa = jnp.sum(state * kn_t, axis=-1, keepdims=True)
    v_col = jnp.sum(eye * v_t, axis=-1, keepdims=True)
    state = state * w_t - sa * b_t + v_col * k_t
    y_col = jnp.sum(state * r_t, axis=-1, keepdims=True)
    y_row = jnp.sum(eye * y_col, axis=1, keepdims=True)
    return state, y_row


def _eye3():
    i = lax.broadcasted_iota(jnp.int32, (1, RW_HEAD_DIM, RW_HEAD_DIM), 1)
    j = lax.broadcasted_iota(jnp.int32, (1, RW_HEAD_DIM, RW_HEAD_DIM), 2)
    return (i == j).astype(F32)


RW_CHUNK = 64
_NT = (((1,), (1,)), ((), ()))
_TN = (((0,), (0,)), ((), ()))


def _group_norm(y):
    mean = jnp.mean(y, axis=-1, keepdims=True)
    var = jnp.mean(jnp.square(y - mean), axis=-1, keepdims=True)
    return (y - mean) * lax.rsqrt(var + GN_EPS)


def _rwkv_chunk_kernel(r_ref, lw_ref, k_ref, v_ref, kn_ref, b_ref, bonus_ref, g_ref, lnw_ref, lnb_ref,
                       y_ref, s_out_ref, h_ref):
    c = RW_CHUNK
    n = RW_HEAD_DIM

    @pl.when(pl.program_id(0) == 0)
    def _():
        h_ref[...] = jnp.zeros_like(h_ref)

    row = lax.broadcasted_iota(jnp.int32, (c, c), 0)
    col = lax.broadcasted_iota(jnp.int32, (c, c), 1)
    eye = jnp.where(row == col, 1.0, 0.0)
    lower_ones = jnp.where(col <= row, 1.0, 0.0).astype(BF16)
    lw = lw_ref[...]
    lw_hi, lw_mid, lw_lo = _split3(lw)
    cum = _dot(lower_ones, lw_hi) + _dot(lower_ones, lw_mid) + _dot(lower_ones, lw_lo)
    total = cum[c - 1:c, :]
    k = k_ref[...]
    b = b_ref[...]
    v = v_ref[...]
    g_inv = jnp.exp(-cum)
    g_rest = jnp.exp(total - cum)
    decay_total = jnp.exp(total)
    qr = jnp.concatenate([kn_ref[...] * jnp.exp(cum - lw), r_ref[...] * jnp.exp(cum)], axis=0).astype(BF16)
    kdbd = jnp.concatenate([k * g_inv, b * g_inv], axis=0).astype(BF16)
    zk = jnp.concatenate([k * g_rest, b * g_rest], axis=0).astype(BF16)
    vb = v.astype(BF16)
    row2 = lax.broadcasted_iota(jnp.int32, (2 * c, 2 * c), 0)
    col2 = lax.broadcasted_iota(jnp.int32, (2 * c, 2 * c), 1)
    t_idx = jnp.where(row2 >= c, row2 - c, row2)
    s_idx = jnp.where(col2 >= c, col2 - c, col2)
    mask = s_idx < t_idx + jnp.where(row2 >= c, 1, 0)
    heads = range(RW_HEADS)
    sls = [slice(h * n, (h + 1) * n) for h in heads]
    gms = [jnp.where(mask, lax.dot_general(qr[:, sl], kdbd[:, sl], _NT, preferred_element_type=F32), 0.0)
           for sl in sls]
    h_old = [h_ref[h] for h in heads]
    qhs = [_dot(qr[:, sl], ho.astype(BF16)) for sl, ho in zip(sls, h_old)]
    akkv = [_dot(gm[0:c, 0:c].astype(BF16), vb[:, sl]) for gm, sl in zip(gms, sls)]
    powers = [gm[0:c, c:2 * c].astype(BF16) for gm in gms]
    t_invs = [eye - gm[0:c, c:2 * c] for gm in gms]
    for _ in range(int(math.log2(c)) - 1):
        powers = [_dot(p, p).astype(BF16) for p in powers]
        t_invs = [t + _dot(p, t.astype(BF16)) for p, t in zip(powers, t_invs)]
    us = [_dot(t.astype(BF16), (qh[0:c] + av).astype(BF16)) for t, qh, av in zip(t_invs, qhs, akkv)]
    vus = [jnp.concatenate([v[:, sl], -u], axis=0).astype(BF16) for sl, u in zip(sls, us)]
    outs = [_group_norm(qh[c:2 * c] + _dot(gm[c:2 * c, :].astype(BF16), vu)) for qh, gm, vu in zip(qhs, gms, vus)]
    for h in heads:
        decay_col = jnp.sum(eye * decay_total[:, sls[h]], axis=-1, keepdims=True)
        h_ref[h] = decay_col * h_old[h] + lax.dot_general(zk[:, sls[h]], vus[h], _TN, preferred_element_type=F32)
    yn = jnp.concatenate(outs, axis=-1)
    y_ref[...] = ((yn * lnw_ref[...] + lnb_ref[...] + bonus_ref[...]) * g_ref[...]).astype(BF16)

    @pl.when(pl.program_id(0) == pl.num_programs(0) - 1)
    def _():
        for h in range(RW_HEADS):
            s_out_ref[h] = h_ref[h].T


def _rwkv_chunked(ops, bonus, g, ln_w, ln_b):
    m = bonus.shape[0]
    spec = pl.BlockSpec((RW_CHUNK, RW_WIDTH), lambda i: (i, 0))
    row = pl.BlockSpec((1, RW_WIDTH), lambda i: (0, 0))
    state_shape = (RW_HEADS, RW_HEAD_DIM, RW_HEAD_DIM)
    return pl.pallas_call(
        _rwkv_chunk_kernel,
        grid=(m // RW_CHUNK,),
        in_specs=[spec] * 8 + [row, row],
        out_specs=[spec, pl.BlockSpec(state_shape, lambda i: (0, 0, 0))],
        out_shape=[jax.ShapeDtypeStruct((m, RW_WIDTH), BF16), jax.ShapeDtypeStruct(state_shape, F32)],
        scratch_shapes=[pltpu.VMEM(state_shape, F32)],
        compiler_params=_params("arbitrary"),
        name="rwkv_chunked",
    )(*ops, bonus, g, ln_w, ln_b)


def _rwkv_single_step_kernel(r_ref, w_ref, k_ref, v_ref, kn_ref, b_ref, s0_ref, y_ref, s_out_ref):
    eye = _eye3()
    tb = r_ref.shape[1]

    def body(t, carry):
        sl = pl.ds(t, 1)
        state, y_row = _rwkv_step(s0_ref[t], r_ref[:, sl, :], w_ref[:, sl, :], k_ref[:, sl, :],
                                  v_ref[:, sl, :], kn_ref[:, sl, :], b_ref[:, sl, :], eye)
        s_out_ref[t] = state
        y_ref[:, sl, :] = y_row
        return carry

    lax.fori_loop(0, tb, body, 0)


def _rwkv_single_step(ops, state0, tb):
    m = ops[0].shape[1]
    spec = pl.BlockSpec((RW_HEADS, tb, RW_HEAD_DIM), lambda i: (0, i, 0))
    st_spec = pl.BlockSpec((tb, RW_HEADS, RW_HEAD_DIM, RW_HEAD_DIM), lambda i: (i, 0, 0, 0))
    return pl.pallas_call(
        _rwkv_single_step_kernel,
        grid=(m // tb,),
        in_specs=[spec] * 6 + [st_spec],
        out_specs=[spec, st_spec],
        out_shape=[jax.ShapeDtypeStruct((RW_HEADS, m, RW_HEAD_DIM), F32),
                   jax.ShapeDtypeStruct(state0.shape, F32)],
        compiler_params=_params("parallel"),
        name="rwkv_single_step",
    )(*ops, state0)


def _rwkv_post_kernel(y_ref, bonus_ref, g_ref, lnw_ref, lnb_ref, o_ref):
    yn = jnp.concatenate([_group_norm(y_ref[h]) for h in range(RW_HEADS)], axis=-1)
    out = (yn * lnw_ref[...] + lnb_ref[...] + bonus_ref[...]) * g_ref[...]
    o_ref[...] = out.astype(BF16)


def _rwkv_post(y, bonus, g, ln_w, ln_b, tm):
    m = bonus.shape[0]
    flat_spec = pl.BlockSpec((tm, RW_WIDTH), lambda i: (i, 0))
    row = pl.BlockSpec((1, RW_WIDTH), lambda i: (0, 0))
    return pl.pallas_call(
        _rwkv_post_kernel,
        grid=(m // tm,),
        in_specs=[pl.BlockSpec((RW_HEADS, tm, RW_HEAD_DIM), lambda i: (0, i, 0)),
                  flat_spec, flat_spec, row, row],
        out_specs=flat_spec,
        out_shape=jax.ShapeDtypeStruct((m, RW_WIDTH), BF16),
        compiler_params=_params("parallel"),
        name="rwkv_post",
    )(y, bonus, g, ln_w, ln_b)


CV_HALO = 32


def _conv_finish(y, cb_ref, lnw_ref, lnb_ref):
    y = y + cb_ref[...]
    mean = jnp.mean(y, axis=-1, keepdims=True)
    var = jnp.mean(jnp.square(y - mean), axis=-1, keepdims=True)
    yn = (y - mean) * lax.rsqrt(var + LN_EPS) * lnw_ref[...] + lnb_ref[...]
    return _silu(yn).astype(BF16)


def _conv_seq_kernel(p_ref, cw_ref, cb_ref, lnw_ref, lnb_ref, o_ref, tail_ref, ext_ref):
    tm = p_ref.shape[0]
    pad = CV_HALO - (CV_K - 1)

    @pl.when(pl.program_id(0) == 0)
    def _():
        ext_ref[0:CV_HALO, :] = jnp.zeros((CV_HALO, CV_WIDTH), F32)

    p = p_ref[...]
    u = p[:, 0:CV_WIDTH] * _sigmoid(p[:, CV_WIDTH:2 * CV_WIDTH])
    ext_ref[CV_HALO:CV_HALO + tm, :] = u
    acc = jnp.zeros((tm, CV_WIDTH), F32)
    for j in range(CV_K):
        acc = acc + ext_ref[pad + j:pad + j + tm, :] * cw_ref[j:j + 1, :]
    o_ref[...] = _conv_finish(acc, cb_ref, lnw_ref, lnb_ref)
    tail = ext_ref[tm:tm + CV_HALO, :]
    ext_ref[0:CV_HALO, :] = tail
    tail_ref[...] = tail


def _conv_seq(proj, cw, cb, ln_w, ln_b, tm):
    m = proj.shape[0]
    row = pl.BlockSpec((1, CV_WIDTH), lambda i: (0, 0))
    return pl.pallas_call(
        _conv_seq_kernel,
        grid=(m // tm,),
        in_specs=[pl.BlockSpec((tm, 2 * CV_WIDTH), lambda i: (i, COL_CV)),
                  pl.BlockSpec((CV_K, CV_WIDTH), lambda i: (0, 0)), row, row, row],
        out_specs=[pl.BlockSpec((tm, CV_WIDTH), lambda i: (i, 0)),
                   pl.BlockSpec((CV_HALO, CV_WIDTH), lambda i: (0, 0))],
        out_shape=[jax.ShapeDtypeStruct((m, CV_WIDTH), BF16),
                   jax.ShapeDtypeStruct((CV_HALO, CV_WIDTH), F32)],
        scratch_shapes=[pltpu.VMEM((CV_HALO + tm, CV_WIDTH), F32)],
        compiler_params=_params("arbitrary"),
        name="conv_seq",
    )(proj, cw, cb, ln_w, ln_b)


def _conv_step_kernel(p_ref, buf_ref, cw_ref, cb_ref, lnw_ref, lnb_ref, o_ref, nbuf_ref):
    p = p_ref[...]
    u = p[:, :, 0:CV_WIDTH] * _sigmoid(p[:, :, CV_WIDTH:2 * CV_WIDTH])
    buf = buf_ref[...]
    acc = (jnp.sum(buf * cw_ref[0:CV_K - 1, :][None], axis=1, keepdims=True)
           + u * cw_ref[CV_K - 1:CV_K, :][None])
    o_ref[...] = _conv_finish(acc, cb_ref, lnw_ref, lnb_ref)
    nbuf_ref[:, 0:CV_K - 2, :] = buf[:, 1:CV_K - 1, :]
    nbuf_ref[:, CV_K - 2:CV_K - 1, :] = u


def _conv_step(proj, buf, cw, cb, ln_w, ln_b, tb):
    m = proj.shape[0]
    row = pl.BlockSpec((1, CV_WIDTH), lambda i: (0, 0))
    buf_spec = pl.BlockSpec((tb, CV_K - 1, CV_WIDTH), lambda i: (i, 0, 0))
    y = pl.pallas_call(
        _conv_step_kernel,
        grid=(m // tb,),
        in_specs=[pl.BlockSpec((tb, 1, 2 * CV_WIDTH), lambda i: (i, 0, COL_CV)), buf_spec,
                  pl.BlockSpec((CV_K, CV_WIDTH), lambda i: (0, 0)), row, row, row],
        out_specs=[pl.BlockSpec((tb, 1, CV_WIDTH), lambda i: (i, 0, 0)), buf_spec],
        out_shape=[jax.ShapeDtypeStruct((m, 1, CV_WIDTH), BF16),
                   jax.ShapeDtypeStruct(buf.shape, F32)],
        compiler_params=_params("parallel"),
        name="conv_step",
    )(proj.reshape(m, 1, N_PROJ), buf, cw, cb, ln_w, ln_b)
    return y[0].reshape(m, CV_WIDTH), y[1]


def _bucket_thresholds():
    n = np.arange(0, 4 * MAX_DISTANCE, dtype=np.int64)
    max_exact = NUM_BUCKETS // 2
    nf = np.maximum(n, 1).astype(np.float32)
    large = max_exact + (np.log(nf / np.float32(max_exact)) / np.float32(math.log(MAX_DISTANCE / max_exact))
                         * np.float32(NUM_BUCKETS - max_exact)).astype(np.int32)
    large = np.minimum(large, NUM_BUCKETS - 1)
    bucket = np.where(n < max_exact, n, large)
    assert np.all(np.diff(bucket) >= 0) and bucket[-1] == NUM_BUCKETS - 1
    return [int(np.argmax(bucket >= b)) for b in range(NUM_BUCKETS)]


_BUCKET_THR = _bucket_thresholds()


def _bias_from_distance(n, rb_of_bucket):
    bias = jnp.zeros(n.shape, F32) + rb_of_bucket(NUM_BUCKETS - 1)
    for b in range(NUM_BUCKETS - 2, -1, -1):
        bias = jnp.where(n < _BUCKET_THR[b + 1], rb_of_bucket(b), bias)
    return bias


def _rank_select(scores, n_rows, limit):
    idx = lax.broadcasted_iota(jnp.int32, scores.shape, 0)
    beaten = jnp.zeros(scores.shape, F32)
    for j in range(n_rows):
        row = scores[j:j + 1, :]
        ahead = jnp.where(row > scores, 1.0, jnp.where(row == scores, jnp.where(idx > j, 1.0, 0.0), 0.0))
        beaten = beaten + ahead * jnp.where(j < limit, 1.0, 0.0)
    return jnp.where(idx < limit, jnp.where(beaten < MB_TOPK, 1.0, 0.0), 0.0)


def _block_mean_kernel(k_ref, o_ref):
    nb = o_ref.shape[1]
    k = k_ref[0].reshape(nb, MB_BLOCK, MB_HEAD_DIM)
    o_ref[0] = jnp.sum(k, axis=1) * (1.0 / MB_BLOCK)


def _block_mean(k_heads):
    h, t, d = k_heads.shape
    nb = t // MB_BLOCK
    return pl.pallas_call(
        _block_mean_kernel,
        grid=(h,),
        in_specs=[pl.BlockSpec((1, t, d), lambda i: (i, 0, 0))],
        out_specs=pl.BlockSpec((1, nb, d), lambda i: (i, 0, 0)),
        out_shape=jax.ShapeDtypeStruct((h, nb, d), F32),
        compiler_params=_params("parallel"),
        name="moba_block_mean",
    )(k_heads)


MB_HEADS_PER_STEP = 4
MB_SUM_ROWS = 16


def _softmax_blocks(carries, scores, vt_blks):
    m_new = [jnp.maximum(c[0], jnp.max(s, axis=0, keepdims=True)) for c, s in zip(carries, scores)]
    alpha = [jnp.exp2(c[0] - mn) for c, mn in zip(carries, m_new)]
    p = [jnp.exp2(s - mn).astype(BF16) for s, mn in zip(scores, m_new)]
    pv = [_dot(_with_ones_rows(vt), pp) for vt, pp in zip(vt_blks, p)]
    return [(mn, a * c[1] + x) for mn, a, c, x in zip(m_new, alpha, carries, pv)]


def _with_ones_rows(vt_blk):
    return jnp.concatenate([vt_blk, jnp.ones((MB_SUM_ROWS, vt_blk.shape[1]), BF16)], axis=0)


def _moba_seq_kernel(rb_ref, qt_ref, k_ref, vt_ref, km_ref, o_ref, bias_ref, rowterm_ref):
    hp = qt_ref.shape[0]
    h0 = pl.program_id(0) * hp
    qi = pl.program_id(1)
    nb = km_ref.shape[1]
    log2e = math.log2(math.e)
    scale = MB_HEAD_DIM ** -0.5 * log2e
    kidx = lax.broadcasted_iota(jnp.int32, (MB_BLOCK, MB_BLOCK), 0)
    qidx = lax.broadcasted_iota(jnp.int32, (MB_BLOCK, MB_BLOCK), 1)

    @pl.when(qi == 0)
    def _():
        dist = qidx - kidx
        for hh in range(hp):
            rb = lambda b: rb_ref[b, h0 + hh] * log2e
            bias_ref[hh, 0] = jnp.where(dist >= 0, _bias_from_distance(jnp.maximum(dist, 0), rb), NEG_INF)
            bias_ref[hh, 1] = _bias_from_distance(dist + MB_BLOCK, rb)

    near = jnp.maximum(qi - 1, 0)
    qtb, carries = [], []
    for hh in range(hp):
        qt = qt_ref[hh]
        block_scores = _dot_f32(km_ref[hh], qt)
        sel = _rank_select(block_scores, nb, qi)
        rowterm_ref[hh] = jnp.where(sel > 0.0, rb_ref[NUM_BUCKETS - 1, h0 + hh] * log2e, NEG_INF)
        qtb.append((qt * scale).astype(BF16))
        s = _dot(k_ref[hh, qi], qtb[hh]) + bias_ref[hh, 0]
        m0 = jnp.max(s, axis=0, keepdims=True)
        p = jnp.exp2(s - m0).astype(BF16)
        carries.append((m0, _dot(_with_ones_rows(vt_ref[hh, qi]), p)))

    heads = range(hp)
    sel_near = [jnp.where(rowterm_ref[hh, pl.ds(near, 1), :] > 0.5 * NEG_INF, 0.0, NEG_INF) for hh in heads]
    scores = [_dot(k_ref[hh, near], qtb[hh]) + bias_ref[hh, 1] + sel_near[hh] for hh in heads]
    carries = _softmax_blocks(carries, scores, [vt_ref[hh, near] for hh in heads])

    def body(kj, carry):
        scores = [_dot(k_ref[hh, kj], qtb[hh]) + rowterm_ref[hh, pl.ds(kj, 1), :] for hh in heads]
        return tuple(_softmax_blocks(carry, scores, [vt_ref[hh, kj] for hh in heads]))

    carries = lax.fori_loop(0, near, body, tuple(carries))
    for hh in range(hp):
        acc = carries[hh][1]
        o_ref[hh] = acc[0:MB_HEAD_DIM] / acc[MB_HEAD_DIM:MB_HEAD_DIM + 1]


def _moba_seq(rel_bias, q, k, v):
    t = q.shape[0]
    nb = t // MB_BLOCK
    hp = MB_HEADS_PER_STEP
    split = lambda x: x.reshape(nb, MB_BLOCK, MB_HEADS, MB_HEAD_DIM)
    qt = q.reshape(t, MB_HEADS, MB_HEAD_DIM).transpose(1, 2, 0)
    k_heads = k.reshape(t, MB_HEADS, MB_HEAD_DIM).transpose(1, 0, 2)
    kb = split(k).transpose(2, 0, 1, 3).astype(BF16)
    vt = split(v).transpose(2, 0, 3, 1).astype(BF16)
    km = _block_mean(k_heads)
    ot = pl.pallas_call(
        _moba_seq_kernel,
        grid=(MB_HEADS // hp, nb),
        in_specs=[pl.BlockSpec(memory_space=pltpu.SMEM),
                  pl.BlockSpec((hp, MB_HEAD_DIM, MB_BLOCK), lambda h, i: (h, 0, i)),
                  pl.BlockSpec((hp, nb, MB_BLOCK, MB_HEAD_DIM), lambda h, i: (h, 0, 0, 0)),
                  pl.BlockSpec((hp, nb, MB_HEAD_DIM, MB_BLOCK), lambda h, i: (h, 0, 0, 0)),
                  pl.BlockSpec((hp, nb, MB_HEAD_DIM), lambda h, i: (h, 0, 0))],
        out_specs=pl.BlockSpec((hp, MB_HEAD_DIM, MB_BLOCK), lambda h, i: (h, 0, i)),
        out_shape=jax.ShapeDtypeStruct((MB_HEADS, MB_HEAD_DIM, t), F32),
        scratch_shapes=[pltpu.VMEM((hp, 2, MB_BLOCK, MB_BLOCK), F32), pltpu.VMEM((hp, nb, MB_BLOCK), F32)],
        compiler_params=_params("parallel", "arbitrary"),
        name="moba_seq",
    )(rel_bias, qt, kb, vt, km)
    return ot.transpose(2, 0, 1).reshape(t, MB_WIDTH).astype(BF16)


def _moba_paged_kernel(n_pages, pt_ref, q_ref, kn_ref, vn_ref, rbt_ref, *refs):
    k_pages = refs[:n_pages]
    v_pages = refs[n_pages:2 * n_pages]
    o_ref, bias_ref = refs[2 * n_pages:]
    past = n_pages * PAGE_SIZE
    n_blocks = past // MB_BLOCK
    pages_per_block = MB_BLOCK // PAGE_SIZE
    scale = MB_HEAD_DIM ** -0.5
    rb = lambda b: rbt_ref[:, :, b:b + 1]

    @pl.when(pl.program_id(0) == 0)
    def _():
        dist = PAGE_SIZE - lax.broadcasted_iota(jnp.int32, (MB_HEADS, 1, PAGE_SIZE), 2)
        bias_ref[...] = _bias_from_distance(dist, rb)

    di = lax.broadcasted_iota(jnp.int32, (MB_HEAD_DIM, MB_HEAD_DIM), 0)
    dj = lax.broadcasted_iota(jnp.int32, (MB_HEAD_DIM, MB_HEAD_DIM), 1)
    eye = jnp.where(di == dj, 1.0, 0.0)[None]
    q = q_ref[0]
    q_col = jnp.sum(eye * q, axis=-1, keepdims=True)
    raw = [jnp.sum(k_pages[pg][...] * q_col, axis=1, keepdims=True) for pg in range(n_pages)]
    scores = []
    for j in range(n_blocks):
        tot = raw[pages_per_block * j]
        for e in range(1, pages_per_block):
            tot = tot + raw[pages_per_block * j + e]
        scores.append(jnp.sum(tot, axis=-1, keepdims=True) * (1.0 / MB_BLOCK))
    far_bias = rb(NUM_BUCKETS - 1)
    rowterms = []
    for j in range(n_blocks):
        beaten = jnp.zeros((MB_HEADS, 1, 1), F32)
        for i in range(n_blocks):
            if i != j:
                ahead = scores[i] > scores[j] if i > j else scores[i] >= scores[j]
                beaten = beaten + jnp.where(ahead, 1.0, 0.0)
        rowterms.append(jnp.where(beaten < MB_TOPK, far_bias, NEG_INF))
    lg_own = jnp.sum(kn_ref[0] * q, axis=-1, keepdims=True) * scale + rb(0)
    logits = []
    for pg in range(n_pages):
        term = rowterms[pg // pages_per_block]
        if pg == n_pages - 1:
            term = bias_ref[...] + jnp.where(term > 0.5 * NEG_INF, 0.0, NEG_INF)
        logits.append(raw[pg] * scale + term)
    m_row = logits[0]
    for lg in logits[1:]:
        m_row = jnp.maximum(m_row, lg)
    m = jnp.maximum(jnp.max(m_row, axis=-1, keepdims=True), lg_own)
    p_own = jnp.exp(lg_own - m)
    l_row = jnp.zeros((MB_HEADS, 1, PAGE_SIZE), F32)
    acc = jnp.zeros((MB_HEADS, MB_HEAD_DIM, PAGE_SIZE), F32)
    for pg in range(n_pages):
        p = jnp.exp(logits[pg] - m)
        l_row = l_row + p
        acc = acc + v_pages[pg][...] * p
    out_col = jnp.sum(acc, axis=-1, keepdims=True)
    out = jnp.sum(eye * out_col, axis=1, keepdims=True) + p_own * vn_ref[0]
    l = jnp.sum(l_row, axis=-1, keepdims=True) + p_own
    o_ref[0] = (out / l).astype(BF16)


def _moba_paged(rel_bias, q, k_new, v_new, cache_k, cache_v, layer, page_table):
    b, n_pages = page_table.shape
    assert PAGE_SIZE >= _BUCKET_THR[NUM_BUCKETS - 1] and MB_BLOCK % PAGE_SIZE == 0
    head4 = pl.BlockSpec((1, MB_HEADS, 1, MB_HEAD_DIM), lambda i, pt: (i, 0, 0, 0))

    def page_spec(pg):
        return pl.BlockSpec((None, None, MB_HEADS, MB_HEAD_DIM, PAGE_SIZE),
                            lambda i, pt: (layer, pt[i, pg], 0, 0, 0))

    grid_spec = pltpu.PrefetchScalarGridSpec(
        num_scalar_prefetch=1,
        grid=(b,),
        in_specs=[head4, head4, head4, pl.BlockSpec((MB_HEADS, 1, NUM_BUCKETS), lambda i, pt: (0, 0, 0))]
        + [page_spec(pg) for pg in range(n_pages)] * 2,
        out_specs=head4,
        scratch_shapes=[pltpu.VMEM((MB_HEADS, 1, PAGE_SIZE), F32)],
    )
    h4 = lambda x: x.reshape(b, MB_HEADS, 1, MB_HEAD_DIM)
    kt = cache_k.transpose(0, 1, 3, 4, 2)
    vt = cache_v.transpose(0, 1, 3, 4, 2)
    out = pl.pallas_call(
        functools.partial(_moba_paged_kernel, n_pages),
        grid_spec=grid_spec,
        out_shape=jax.ShapeDtypeStruct((b, MB_HEADS, 1, MB_HEAD_DIM), BF16),
        compiler_params=_params("arbitrary"),
        name="moba_paged",
    )(page_table, h4(q), h4(k_new), h4(v_new), rel_bias.T.reshape(MB_HEADS, 1, NUM_BUCKETS),
      *([kt] * n_pages), *([vt] * n_pages))
    return out.reshape(b, MB_WIDTH)


def _merge_kernel(yrw_ref, ymb_ref, ycv_ref, g0_ref, g1_ref, g2_ref, wrw_ref, wmb_ref, wcv_ref, o_ref):
    merged = (g0_ref[...] * _dot(yrw_ref[...], wrw_ref[...])
              + g1_ref[...] * _dot(ymb_ref[...], wmb_ref[...])
              + g2_ref[...] * _dot(ycv_ref[...], wcv_ref[...]))
    o_ref[...] = merged.astype(BF16)


def _merge(y_rw, y_mb, y_cv, proj, w_rw, w_mb, w_cv, tm):
    m = proj.shape[0]
    rows = lambda n: pl.BlockSpec((tm, n), lambda i: (i, 0))
    gate = lambda br: pl.BlockSpec((tm, D_MODEL), lambda i: (i, COL_GATE + br))
    full = lambda n: pl.BlockSpec((n, D_MODEL), lambda i: (0, 0))
    return pl.pallas_call(
        _merge_kernel,
        grid=(m // tm,),
        in_specs=[rows(RW_WIDTH), rows(MB_WIDTH), rows(CV_WIDTH), gate(0), gate(1), gate(2),
                  full(RW_WIDTH), full(MB_WIDTH), full(CV_WIDTH)],
        out_specs=rows(D_MODEL),
        out_shape=jax.ShapeDtypeStruct((m, D_MODEL), BF16),
        compiler_params=_params("parallel"),
        name="branch_merge",
    )(y_rw, y_mb, y_cv, proj, proj, proj, w_rw, w_mb, w_cv)


def _out_proj_kernel(h_ref, x_ref, w_ref, o_ref):
    o_ref[...] = h_ref[...] + _dot(x_ref[...], w_ref[...])


def _out_proj(h, merged, w_out, tm):
    m = h.shape[0]
    rows = pl.BlockSpec((tm, D_MODEL), lambda i: (i, 0))
    return pl.pallas_call(
        _out_proj_kernel,
        grid=(m // tm,),
        in_specs=[rows, rows, pl.BlockSpec((D_MODEL, D_MODEL), lambda i: (0, 0))],
        out_specs=rows,
        out_shape=jax.ShapeDtypeStruct((m, D_MODEL), F32),
        compiler_params=_params("parallel"),
        name="out_proj",
    )(h, merged, w_out)


def _ffn_kernel(h_ref, g_ref, wg_ref, wu_ref, wd_ref, o_ref, xn_ref):
    @pl.when(pl.program_id(1) == 0)
    def _():
        x = h_ref[...]
        ms = jnp.mean(x * x, axis=-1, keepdims=True)
        xn_ref[...] = (x * lax.rsqrt(ms + NORM_EPS) * g_ref[...]).astype(BF16)
        o_ref[...] = x

    xn = xn_ref[...]
    mid = _silu(_dot(xn, wg_ref[...])) * _dot(xn, wu_ref[...])
    o_ref[...] += _dot(mid.astype(BF16), wd_ref[...])


def _ffn(h, g, wg, wu, wd, tm, tf):
    m = h.shape[0]
    d_ff = wg.shape[1]
    rows = pl.BlockSpec((tm, D_MODEL), lambda i, f: (i, 0))
    return pl.pallas_call(
        _ffn_kernel,
        grid=(m // tm, d_ff // tf),
        in_specs=[rows, pl.BlockSpec((1, D_MODEL), lambda i, f: (0, 0)),
                  pl.BlockSpec((D_MODEL, tf), lambda i, f: (0, f)),
                  pl.BlockSpec((D_MODEL, tf), lambda i, f: (0, f)),
                  pl.BlockSpec((tf, D_MODEL), lambda i, f: (f, 0))],
        out_specs=rows,
        out_shape=jax.ShapeDtypeStruct((m, D_MODEL), F32),
        scratch_shapes=[pltpu.VMEM((tm, D_MODEL), BF16)],
        compiler_params=_params("parallel", "arbitrary"),
        name="ffn",
    )(h, g, wg, wu, wd)


def _dot_f32_nt(a, b):
    nt = lambda x, y: lax.dot_general(x, y, _NT, preferred_element_type=F32)
    a_hi, a_mid, a_lo = _split3(a)
    b_hi, b_mid, b_lo = _split3(b)
    return (nt(a_hi, b_hi) + (nt(a_hi, b_mid) + nt(a_mid, b_hi))
            + (nt(a_mid, b_mid) + nt(a_hi, b_lo) + nt(a_lo, b_hi)))


def _router_kernel(h_ref, g_ref, wt_ref, b_ref, before_ref, xn_ref, gate_ref, pos_ref, cnt_ref):
    x = h_ref[...]
    ms = jnp.mean(x * x, axis=-1, keepdims=True)
    xn = x * lax.rsqrt(ms + NORM_EPS) * g_ref[...]
    xn_ref[...] = xn.astype(BF16)
    logits = _dot_f32_nt(wt_ref[...], xn) + b_ref[...]
    idx = lax.broadcasted_iota(jnp.int32, logits.shape, 0).astype(F32)
    m1 = jnp.max(logits, axis=0, keepdims=True)
    i1 = jnp.min(jnp.where(logits == m1, idx, float(N_EXPERTS)), axis=0, keepdims=True)
    rest = jnp.where(idx == i1, -jnp.inf, logits)
    m2 = jnp.max(rest, axis=0, keepdims=True)
    i2 = jnp.min(jnp.where(rest == m2, idx, float(N_EXPERTS)), axis=0, keepdims=True)
    e2 = jnp.exp(m2 - m1)
    gates = jnp.where(idx == i1, 1.0 / (1.0 + e2), 0.0) + jnp.where(idx == i2, e2 / (1.0 + e2), 0.0)
    member = jnp.where(gates > 0.0, 1.0, 0.0)
    gate_ref[0] = gates
    pos_ref[0] = _dot(member.astype(BF16), before_ref[...])
    cnt_ref[0] = jnp.sum(member, axis=-1, keepdims=True).astype(jnp.int32)


def _router(h, g, w_router, b_router, tm):
    m = h.shape[0]
    n_tiles = m // tm
    before = jnp.asarray(np.triu(np.ones((tm, tm), np.float32), 1), BF16)
    tile3 = pl.BlockSpec((1, N_EXPERTS, tm), lambda i: (i, 0, 0))
    per_tile = jax.ShapeDtypeStruct((n_tiles, N_EXPERTS, tm), F32)
    return pl.pallas_call(
        _router_kernel,
        grid=(n_tiles,),
        in_specs=[pl.BlockSpec((tm, D_MODEL), lambda i: (i, 0)),
                  pl.BlockSpec((1, D_MODEL), lambda i: (0, 0)),
                  pl.BlockSpec((N_EXPERTS, D_MODEL), lambda i: (0, 0)),
                  pl.BlockSpec((N_EXPERTS, 1), lambda i: (0, 0)),
                  pl.BlockSpec((tm, tm), lambda i: (0, 0))],
        out_specs=[pl.BlockSpec((tm, D_MODEL), lambda i: (i, 0)), tile3, tile3,
                   pl.BlockSpec((1, N_EXPERTS, 1), lambda i: (i, 0, 0))],
        out_shape=[jax.ShapeDtypeStruct((m, D_MODEL), BF16), per_tile, per_tile,
                   jax.ShapeDtypeStruct((n_tiles, N_EXPERTS, 1), jnp.int32)],
        compiler_params=_params("parallel"),
        name="moe_router",
    )(h, g, w_router.T, b_router.reshape(N_EXPERTS, 1), before)


def _moe_kernel(sub, cnt_ref, xn_ref, gate_ref, pos_ref, wg_ref, wu_ref, wd_ref, o_ref, xg_ref, gg_ref, acc_ref):
    i = pl.program_id(0)
    e = pl.program_id(1)
    f = pl.program_id(2)
    tm = xn_ref.shape[0]
    n_sub = (cnt_ref[i * N_EXPERTS + e] + (sub - 1)) // sub
    pos_row = pos_ref[0, pl.ds(e, 1), :]
    gate_row = gate_ref[0, pl.ds(e, 1), :]
    slot = lax.broadcasted_iota(jnp.int32, (sub, tm), 0)

    def one_hot(s):
        wanted = (slot + s * sub).astype(F32)
        return jnp.where(gate_row > 0.0, jnp.where(pos_row == wanted, 1.0, 0.0), 0.0)

    def rows_of(s):
        return pl.ds(pl.multiple_of(s * sub, sub), sub)

    @pl.when((e == 0) & (f == 0))
    def _():
        o_ref[...] = jnp.zeros_like(o_ref)

    @pl.when(f == 0)
    def _():
        def pack(s, carry):
            pick = one_hot(s)
            xg_ref[rows_of(s), :] = _dot(pick.astype(BF16), xn_ref[...]).astype(BF16)
            gg_ref[rows_of(s), :] = jnp.sum(pick * gate_row, axis=-1, keepdims=True)
            acc_ref[rows_of(s), :] = jnp.zeros((sub, D_MODEL), F32)
            return carry

        lax.fori_loop(0, n_sub, pack, 0)

    def expert(s, carry):
        x = xg_ref[rows_of(s), :]
        mid = _silu(_dot(x, wg_ref[0])) * _dot(x, wu_ref[0]) * gg_ref[rows_of(s), :]
        acc_ref[rows_of(s), :] += _dot(mid.astype(BF16), wd_ref[0])
        return carry

    lax.fori_loop(0, n_sub, expert, 0)

    @pl.when(f == pl.num_programs(2) - 1)
    def _():
        def unpack(s, carry):
            pick = one_hot(s).astype(BF16)
            a = acc_ref[rows_of(s), :]
            hi = a.astype(BF16)
            lo = (a - hi.astype(F32)).astype(BF16)
            o_ref[...] += (lax.dot_general(pick, hi, _TN, preferred_element_type=F32)
                           + lax.dot_general(pick, lo, _TN, preferred_element_type=F32))
            return carry

        lax.fori_loop(0, n_sub, unpack, 0)


def _moe(xn, gates, pos, counts, wg, wu, wd, tm, tf):
    m = xn.shape[0]
    n_e, _, d_ff = wg.shape
    sub = min(tm, -(-(tm * TOP_K * 5 // (N_EXPERTS * 4)) // 16) * 16)
    packed = -(-tm // sub) * sub
    tile3 = pl.BlockSpec((1, N_EXPERTS, tm), lambda i, e, f, cnt: (i, 0, 0))
    grid_spec = pltpu.PrefetchScalarGridSpec(
        num_scalar_prefetch=1,
        grid=(m // tm, n_e, d_ff // tf),
        in_specs=[pl.BlockSpec((tm, D_MODEL), lambda i, e, f, cnt: (i, 0)), tile3, tile3,
                  pl.BlockSpec((1, D_MODEL, tf), lambda i, e, f, cnt: (e, 0, f)),
                  pl.BlockSpec((1, D_MODEL, tf), lambda i, e, f, cnt: (e, 0, f)),
                  pl.BlockSpec((1, tf, D_MODEL), lambda i, e, f, cnt: (e, f, 0))],
        out_specs=pl.BlockSpec((tm, D_MODEL), lambda i, e, f, cnt: (i, 0)),
        scratch_shapes=[pltpu.VMEM((packed, D_MODEL), BF16), pltpu.VMEM((packed, 1), F32),
                        pltpu.VMEM((packed, D_MODEL), F32)],
    )
    return pl.pallas_call(
        functools.partial(_moe_kernel, sub),
        grid_spec=grid_spec,
        out_shape=jax.ShapeDtypeStruct((m, D_MODEL), F32),
        compiler_params=_params("parallel", "arbitrary", "arbitrary"),
        name="moe_experts",
    )(counts.reshape(-1), xn, gates, pos, wg, wu, wd)


def _final_norm_kernel(n_in, *refs):
    g_ref, o_ref = refs[n_in], refs[n_in + 1]
    x = refs[0][...]
    for r in refs[1:n_in]:
        x = x + r[...]
    ms = jnp.mean(x * x, axis=-1, keepdims=True)
    o_ref[...] = x * lax.rsqrt(ms + NORM_EPS) * g_ref[...]


def _final_norm(parts, g, tm):
    m = parts[0].shape[0]
    rows = pl.BlockSpec((tm, D_MODEL), lambda i: (i, 0))
    return pl.pallas_call(
        functools.partial(_final_norm_kernel, len(parts)),
        grid=(m // tm,),
        in_specs=[rows] * len(parts) + [pl.BlockSpec((1, D_MODEL), lambda i: (0, 0))],
        out_specs=rows,
        out_shape=jax.ShapeDtypeStruct((m, D_MODEL), F32),
        compiler_params=_params("parallel"),
        name="final_norm",
    )(*parts, g)


def _add_kernel(a_ref, b_ref, o_ref):
    o_ref[...] = a_ref[...] + b_ref[...]


def _add(a, b, tm):
    rows = pl.BlockSpec((tm, D_MODEL), lambda i: (i, 0))
    return pl.pallas_call(
        _add_kernel,
        grid=(a.shape[0] // tm,),
        in_specs=[rows, rows],
        out_specs=rows,
        out_shape=jax.ShapeDtypeStruct(a.shape, F32),
        compiler_params=_params("parallel"),
        name="residual_add",
    )(a, b)


def _pad_rw_cols(x):
    pad = jnp.zeros(x.shape[:-1] + (RW_PAD - RW_COLS,), x.dtype)
    return jnp.concatenate([x[..., :RW_COLS], pad, x[..., RW_COLS:]], axis=-1)


def _lora_slab(w, row0):
    slab = jnp.zeros((LORA_PAD, RW_WIDTH), F32)
    return lax.dynamic_update_slice(slab, w, (row0, 0)).astype(BF16)


def _layer_params(l, P):
    row = lambda x: x.reshape(1, -1)
    lp = dict(
        norm_mix=row(P['norm_mix'][l]), norm_ffn=row(P['norm_ffn'][l]),
        w_in=_pad_rw_cols(P['w_in'][l]).astype(BF16),
        rw_vecs=[row(_pad_rw_cols(P['rw_mu'][l])), row(P['rw_w0'][l]), row(P['rw_a0'][l]),
                 row(P['rw_kk'][l]), row(P['rw_ka'][l]), row(P['rw_rk'][l])],
        rw_mats=[_lora_slab(P['rw_w2'][l], 0), _lora_slab(P['rw_a2'][l], DECAY_LORA),
                 _lora_slab(P['rw_g2'][l], DECAY_LORA + AAA_LORA)],
        rw_ln_w=row(P['rw_ln_w'][l]), rw_ln_b=row(P['rw_ln_b'][l]),
        cv_w=P['cv_w'][l], cv_b=row(P['cv_b'][l]), cv_ln_w=row(P['cv_ln_w'][l]), cv_ln_b=row(P['cv_ln_b'][l]),
        w_br_rw=P['w_br_rw'][l].astype(BF16), w_br_mb=P['w_br_mb'][l].astype(BF16),
        w_br_cv=P['w_br_cv'][l].astype(BF16), w_out=P['w_out'][l].astype(BF16),
    )
    i = l // 2
    if l % 2 == 0:
        lp.update(wg=P['ffn_wg'][i].astype(BF16), wu=P['ffn_wu'][i].astype(BF16), wd=P['ffn_wd'][i].astype(BF16))
    else:
        lp.update(wg=P['moe_wg'][i].astype(BF16), wu=P['moe_wu'][i].astype(BF16), wd=P['moe_wd'][i].astype(BF16),
                  router=P['moe_router'][i], router_b=P['moe_router_b'][i])
    return lp


def _channel_mix(h, lp, tm_dense, tm_moe):
    if 'router' in lp:
        xn, gates, pos, counts = _router(h, lp['norm_ffn'], lp['router'], lp['router_b'], tm_moe)
        return [h, _moe(xn, gates, pos, counts, lp['wg'], lp['wu'], lp['wd'], tm_moe, 256)]
    return [_ffn(h, lp['norm_ffn'], lp['wg'], lp['wu'], lp['wd'], tm_dense, 512)]


def _as_one(parts, tm):
    return parts[0] if len(parts) == 1 else _add(parts[0], parts[1], tm)


def _run_prompt(x, layers, norm_final, rel_bias):
    t = x.shape[0]
    parts = [x]
    outs = dict(k=[], v=[], wkv=[], shift=[], conv=[])
    for lp in layers:
        h = _as_one(parts, 512)
        proj = _norm_proj(h, lp['norm_mix'], lp['w_in'], 1024)
        *ops, g, bonus = _rwkv_prep(proj, None, lp['rw_vecs'], lp['rw_mats'], 128)
        y_rw, s_final = _rwkv_chunked(ops, bonus, g, lp['rw_ln_w'], lp['rw_ln_b'])
        q = proj[:, RW_PAD:RW_PAD + MB_WIDTH]
        k_new = proj[:, RW_PAD + MB_WIDTH:RW_PAD + 2 * MB_WIDTH]
        v_new = proj[:, RW_PAD + 2 * MB_WIDTH:RW_PAD + 3 * MB_WIDTH]
        y_mb = _moba_seq(rel_bias, q, k_new, v_new)
        y_cv, tail = _conv_seq(proj, lp['cv_w'], lp['cv_b'], lp['cv_ln_w'], lp['cv_ln_b'], 512)
        merged = _merge(y_rw, y_mb, y_cv, proj, lp['w_br_rw'], lp['w_br_mb'], lp['w_br_cv'], 256)
        h = _out_proj(h, merged, lp['w_out'], 512)
        parts = _channel_mix(h, lp, 512, 1024)
        outs['k'].append(k_new.reshape(1, t, MB_HEADS, MB_HEAD_DIM))
        outs['v'].append(v_new.reshape(1, t, MB_HEADS, MB_HEAD_DIM))
        outs['wkv'].append(s_final[None])
        outs['shift'].append(proj[t - 1:t, :RW_COLS])
        outs['conv'].append(tail[None, CV_HALO - (CV_K - 1):, :])
    y = _final_norm(parts, norm_final, 512)
    return (y[None],) + tuple(jnp.stack(outs[n]) for n in ('k', 'v', 'wkv', 'shift', 'conv'))


def _run_sample(x, layers, norm_final, rel_bias, cache_k, cache_v, state_wkv, state_shift, state_conv, page_table):
    b = x.shape[0]
    parts = [x]
    outs = dict(k=[], v=[], wkv=[], shift=[], conv=[])
    for l, lp in enumerate(layers):
        h = _as_one(parts, b)
        proj = _norm_proj(h, lp['norm_mix'], lp['w_in'], b)
        *ops, g, bonus = _rwkv_prep(proj, _pad_rw_cols(state_shift[l]), lp['rw_vecs'], lp['rw_mats'], b)
        y, s_new = _rwkv_single_step(ops, state_wkv[l], 8)
        y_rw = _rwkv_post(y, bonus, g, lp['rw_ln_w'], lp['rw_ln_b'], b)
        q = proj[:, RW_PAD:RW_PAD + MB_WIDTH]
        k_new = proj[:, RW_PAD + MB_WIDTH:RW_PAD + 2 * MB_WIDTH]
        v_new = proj[:, RW_PAD + 2 * MB_WIDTH:RW_PAD + 3 * MB_WIDTH]
        y_mb = _moba_paged(rel_bias, q, k_new, v_new, cache_k, cache_v, l, page_table)
        y_cv, conv_new = _conv_step(proj, state_conv[l], lp['cv_w'], lp['cv_b'], lp['cv_ln_w'], lp['cv_ln_b'], 8)
        merged = _merge(y_rw, y_mb, y_cv, proj, lp['w_br_rw'], lp['w_br_mb'], lp['w_br_cv'], b)
        h = _out_proj(h, merged, lp['w_out'], b)
        parts = _channel_mix(h, lp, b, b)
        outs['k'].append(k_new.reshape(b, 1, MB_HEADS, MB_HEAD_DIM))
        outs['v'].append(v_new.reshape(b, 1, MB_HEADS, MB_HEAD_DIM))
        outs['wkv'].append(s_new)
        outs['shift'].append(proj[:, :RW_COLS])
        outs['conv'].append(conv_new)
    y = _final_norm(parts, norm_final, b)
    return (y[:, None, :],) + tuple(jnp.stack(outs[n]) for n in ('k', 'v', 'wkv', 'shift', 'conv'))


def kernel(x_prompt, x_sample, cache_k, cache_v, state_wkv, state_shift, state_conv, page_table, norm_mix, norm_ffn, norm_final, w_in, rw_mu, rw_w0, rw_w2, rw_a0, rw_a2, rw_g2, rw_kk, rw_ka, rw_rk, rw_ln_w, rw_ln_b, rel_bias, cv_w, cv_b, cv_ln_w, cv_ln_b, w_br_rw, w_br_mb, w_br_cv, w_out, ffn_wg, ffn_wu, ffn_wd, moe_router, moe_router_b, moe_wg, moe_wu, moe_wd):
    P = dict(norm_mix=norm_mix, norm_ffn=norm_ffn, w_in=w_in,
             rw_mu=rw_mu, rw_w0=rw_w0, rw_w2=rw_w2, rw_a0=rw_a0, rw_a2=rw_a2, rw_g2=rw_g2,
             rw_kk=rw_kk, rw_ka=rw_ka, rw_rk=rw_rk, rw_ln_w=rw_ln_w, rw_ln_b=rw_ln_b,
             cv_w=cv_w, cv_b=cv_b, cv_ln_w=cv_ln_w, cv_ln_b=cv_ln_b,
             w_br_rw=w_br_rw, w_br_mb=w_br_mb, w_br_cv=w_br_cv, w_out=w_out,
             ffn_wg=ffn_wg, ffn_wu=ffn_wu, ffn_wd=ffn_wd,
             moe_router=moe_router, moe_router_b=moe_router_b, moe_wg=moe_wg, moe_wu=moe_wu, moe_wd=moe_wd)
    depth = w_in.shape[0]
    layers = [_layer_params(l, P) for l in range(depth)]
    g_final = norm_final.reshape(1, -1)
    assert x_prompt.shape[0] == 1 and x_sample.shape[1] == 1
    prompt = _run_prompt(x_prompt[0], layers, g_final, rel_bias)
    sample = _run_sample(x_sample[:, 0, :], layers, g_final, rel_bias, cache_k, cache_v,
                         state_wkv, state_shift, state_conv, page_table)
    y_p, k_p, v_p, wkv_p, shift_p, conv_p = prompt
    y_s, k_s, v_s, wkv_s, shift_s, conv_s = sample
    return (y_p, y_s, k_p, v_p, wkv_p, shift_p, conv_p,
            k_s, v_s, wkv_s, shift_s, conv_s)
```

```python
import functools
import math

import numpy as np
import jax
import jax.numpy as jnp
from jax import lax
from jax.experimental import pallas as pl
from jax.experimental.pallas import tpu as pltpu

F32 = jnp.float32
BF16 = jnp.bfloat16

D_MODEL = 2048
RW_HEAD_DIM = 64
RW_WIDTH = 1024
RW_HEADS = 16
DECAY_LORA = 64
AAA_LORA = 64
GATE_LORA = 160
RW_COLS = 3 * RW_WIDTH + DECAY_LORA + AAA_LORA + GATE_LORA
RW_PAD = 3584
LORA_PAD = RW_PAD - 3 * RW_WIDTH
GN_EPS = 64e-5
MB_HEAD_DIM = 64
MB_WIDTH = 512
MB_HEADS = 8
MB_BLOCK = 256
MB_TOPK = 3
NUM_BUCKETS = 32
MAX_DISTANCE = 128
NEG_INF = -1e30
CV_WIDTH = 512
CV_K = 31
LN_EPS = 1e-5
N_BRANCH = 3
N_EXPERTS = 8
TOP_K = 2
NORM_EPS = 1e-6
PAGE_SIZE = 128

PROJ_TILE = 1024
N_PROJ = RW_PAD + 3 * MB_WIDTH + 2 * CV_WIDTH + N_BRANCH * D_MODEL
COL_CV = (RW_PAD + 3 * MB_WIDTH) // (2 * CV_WIDTH)
COL_GATE = (RW_PAD + 3 * MB_WIDTH + 2 * CV_WIDTH) // D_MODEL
GATE_TILE0 = (RW_PAD + 3 * MB_WIDTH + 2 * CV_WIDTH) // PROJ_TILE

VMEM_LIMIT_BYTES = 56 * 1024 * 1024


def _params(*semantics):
    return pltpu.CompilerParams(dimension_semantics=semantics, vmem_limit_bytes=VMEM_LIMIT_BYTES)


def _sigmoid(x):
    return 1.0 / (1.0 + jnp.exp(-x))


def _silu(x):
    return x * _sigmoid(x)


def _split3(x):
    hi = x.astype(BF16)
    r1 = x - hi.astype(F32)
    mid = r1.astype(BF16)
    lo = (r1 - mid.astype(F32)).astype(BF16)
    return hi, mid, lo


_NT = (((1,), (1,)), ((), ()))
_TN = (((0,), (0,)), ((), ()))


def _dot(a, b):
    return jnp.dot(a, b, preferred_element_type=F32)


def _dot_exact_rhs(x, sel):
    hi, mid, lo = _split3(x)
    return _dot(hi, sel) + _dot(mid, sel) + _dot(lo, sel)


def _dot_f32(a, b):
    a_hi, a_mid, a_lo = _split3(a)
    b_hi, b_mid, b_lo = _split3(b)
    return (_dot(a_hi, b_hi) + (_dot(a_hi, b_mid) + _dot(a_mid, b_hi))
            + (_dot(a_mid, b_mid) + _dot(a_hi, b_lo) + _dot(a_lo, b_hi)))


def _rms_norm_bf16(x, g):
    ms = jnp.mean(x * x, axis=-1, keepdims=True)
    return (x * lax.rsqrt(ms + NORM_EPS) * g).astype(BF16)


def _norm_proj_kernel(x_ref, g_ref, wt_ref, o_ref, xn_ref):
    j = pl.program_id(1)

    @pl.when(j == 0)
    def _():
        xn_ref[...] = _rms_norm_bf16(x_ref[...], g_ref[...])

    acc = lax.dot_general(xn_ref[...], wt_ref[...], _NT, preferred_element_type=F32)

    @pl.when(j < GATE_TILE0)
    def _():
        o_ref[...] = acc

    @pl.when(j >= GATE_TILE0)
    def _():
        o_ref[...] = _sigmoid(acc)


def _norm_proj(x, g, wt, tm):
    m = x.shape[0]
    return pl.pallas_call(
        _norm_proj_kernel,
        grid=(m // tm, N_PROJ // PROJ_TILE),
        in_specs=[pl.BlockSpec((tm, D_MODEL), lambda i, j: (i, 0)),
                  pl.BlockSpec((1, D_MODEL), lambda i, j: (0, 0)),
                  pl.BlockSpec((PROJ_TILE, D_MODEL), lambda i, j: (j, 0))],
        out_specs=pl.BlockSpec((tm, PROJ_TILE), lambda i, j: (i, j)),
        out_shape=jax.ShapeDtypeStruct((m, N_PROJ), F32),
        scratch_shapes=[pltpu.VMEM((tm, D_MODEL), BF16)],
        compiler_params=_params("parallel", "arbitrary"),
        name="norm_proj",
    )(x, g, wt)


def _qkv_t_kernel(x_ref, g_ref, wq_ref, wk_ref, wv_ref, q_ref, k_ref, v_ref, kb_ref, vb_ref, km_ref):
    xn = _rms_norm_bf16(x_ref[...], g_ref[...])
    nt = lambda w_ref: lax.dot_general(w_ref[...], xn, _NT, preferred_element_type=F32)
    q_ref[...] = nt(wq_ref)
    k = nt(wk_ref)
    v = nt(wv_ref)
    k_ref[...] = k
    v_ref[...] = v
    kb_ref[...] = k.astype(BF16)
    vb_ref[...] = v.astype(BF16)
    n_blk = k.shape[1] // MB_BLOCK
    sums = [jnp.sum(k[:, b * MB_BLOCK:(b + 1) * MB_BLOCK], axis=-1, keepdims=True) for b in range(n_blk)]
    km_ref[0] = jnp.concatenate(sums, axis=-1) * (1.0 / MB_BLOCK)


def _qkv_t(x, g, wt, tm):
    t = x.shape[0]
    q_row = RW_PAD // MB_WIDTH
    w_spec = lambda r: pl.BlockSpec((MB_WIDTH, D_MODEL), lambda i: (q_row + r, 0))
    col = pl.BlockSpec((MB_WIDTH, tm), lambda i: (0, i))
    f32_t = jax.ShapeDtypeStruct((MB_WIDTH, t), F32)
    bf16_t = jax.ShapeDtypeStruct((MB_WIDTH, t), BF16)
    return pl.pallas_call(
        _qkv_t_kernel,
        grid=(t // tm,),
        in_specs=[pl.BlockSpec((tm, D_MODEL), lambda i: (i, 0)), pl.BlockSpec((1, D_MODEL), lambda i: (0, 0)),
                  w_spec(0), w_spec(1), w_spec(2)],
        out_specs=[col] * 5 + [pl.BlockSpec((1, MB_WIDTH, tm // MB_BLOCK), lambda i: (i, 0, 0))],
        out_shape=[f32_t] * 3 + [bf16_t] * 2 + [jax.ShapeDtypeStruct((t // tm, MB_WIDTH, tm // MB_BLOCK), F32)],
        compiler_params=_params("parallel"),
        name="qkv_t",
    )(x, g, wt, wt, wt)


def _rwkv_prep_kernel(sequential, *refs):
    if sequential:
        (p_ref, mu_ref, w0_ref, a0_ref, kk_ref, ka_ref, rk_ref, w2_ref, a2_ref, g2_ref,
         r_o, w_o, k_o, v_o, kn_o, b_o, g_o, bonus_o, carry_ref) = refs
    else:
        (p_ref, prev_ref, mu_ref, w0_ref, a0_ref, kk_ref, ka_ref, rk_ref, w2_ref, a2_ref, g2_ref,
         r_o, w_o, k_o, v_o, kn_o, b_o, g_o, bonus_o) = refs
    p = p_ref[...]
    tm = p.shape[0]
    if sequential:
        @pl.when(pl.program_id(0) == 0)
        def _():
            carry_ref[...] = jnp.zeros_like(carry_ref)

        rows = lax.broadcasted_iota(jnp.int32, p.shape, 0)
        prev = jnp.where(rows == 0, carry_ref[...], pltpu.roll(p, 1, 0))
        carry_ref[...] = p[tm - 1:tm, :]
    else:
        prev = prev_ref[...]
    xs = p + (prev - p) * mu_ref[...]
    r = xs[:, 0:RW_WIDTH]
    k = xs[:, RW_WIDTH:2 * RW_WIDTH]
    v = xs[:, 2 * RW_WIDTH:3 * RW_WIDTH]
    lora = xs[:, 3 * RW_WIDTH:RW_PAD]
    wx = w0_ref[...] + _dot(jnp.tanh(lora).astype(BF16), w2_ref[...])
    w_log = jnp.minimum(wx, 0.0) - jnp.log(1.0 + jnp.exp(-jnp.abs(wx))) - 0.5
    log_decay = -jnp.exp(w_log)
    a = _sigmoid(a0_ref[...] + _dot(lora.astype(BF16), a2_ref[...]))
    g_o[...] = _dot(_sigmoid(lora).astype(BF16), g2_ref[...])
    kk = k * kk_ref[...]
    k_mod = k * (1.0 + (a - 1.0) * ka_ref[...])
    rkr = r * k_mod * rk_ref[...]
    if sequential:
        r_o[...] = r
        w_o[...] = log_decay
        k_o[...] = k_mod
        v_o[...] = v
    else:
        decay = jnp.exp(log_decay)
    for h in range(RW_HEADS):
        sl = slice(h * RW_HEAD_DIM, (h + 1) * RW_HEAD_DIM)
        kk_h = kk[:, sl]
        nrm = jnp.sqrt(jnp.sum(kk_h * kk_h, axis=-1, keepdims=True))
        kn_h = kk_h / jnp.maximum(nrm, 1e-12)
        if sequential:
            kn_o[:, sl] = kn_h
            b_o[:, sl] = kn_h * a[:, sl]
        else:
            r_o[h] = r[:, sl]
            w_o[h] = decay[:, sl]
            k_o[h] = k_mod[:, sl]
            v_o[h] = v[:, sl]
            kn_o[h] = kn_h
            b_o[h] = kn_h * a[:, sl]
        bonus_o[:, sl] = jnp.sum(rkr[:, sl], axis=-1, keepdims=True) * v[:, sl]


def _rwkv_prep(proj, prev, vecs, mats, tm):
    m = proj.shape[0]
    sequential = prev is None
    row = lambda n: pl.BlockSpec((1, n), lambda i: (0, 0))
    in_specs = [pl.BlockSpec((tm, RW_PAD), lambda i: (i, 0))]
    args = [proj]
    if not sequential:
        in_specs.append(pl.BlockSpec((tm, RW_PAD), lambda i: (i, 0)))
        args.append(prev)
    in_specs += [row(RW_PAD)] + [row(RW_WIDTH)] * 5
    in_specs += [pl.BlockSpec((LORA_PAD, RW_WIDTH), lambda i: (0, 0))] * 3
    flat = jax.ShapeDtypeStruct((m, RW_WIDTH), F32)
    flat_spec = pl.BlockSpec((tm, RW_WIDTH), lambda i: (i, 0))
    if sequential:
        op_shape, op_spec = flat, flat_spec
    else:
        op_shape = jax.ShapeDtypeStruct((RW_HEADS, m, RW_HEAD_DIM), F32)
        op_spec = pl.BlockSpec((RW_HEADS, tm, RW_HEAD_DIM), lambda i: (0, i, 0))
    return pl.pallas_call(
        functools.partial(_rwkv_prep_kernel, sequential),
        grid=(m // tm,),
        in_specs=in_specs,
        out_specs=[op_spec] * 6 + [flat_spec] * 2,
        out_shape=[op_shape] * 6 + [flat] * 2,
        scratch_shapes=[pltpu.VMEM((1, RW_PAD), F32)] if sequential else [],
        compiler_params=_params("arbitrary"),
        name="rwkv_prep",
    )(*args, *vecs, *mats)


def _rwkv_step(state, r_t, w_t, k_t, v_t, kn_t, b_t, eye):
    sa = jnp.sum(state * kn_t, axis=-1, keepdims=True)
    v_col = jnp.sum(eye * v_t, axis=-1, keepdims=True)
    state = state * w_t - sa * b_t + v_col * k_t
    y_col = jnp.sum(state * r_t, axis=-1, keepdims=True)
    y_row = jnp.sum(eye * y_col, axis=1, keepdims=True)
    return state, y_row


def _eye3():
    i = lax.broadcasted_iota(jnp.int32, (1, RW_HEAD_DIM, RW_HEAD_DIM), 1)
    j = lax.broadcasted_iota(jnp.int32, (1, RW_HEAD_DIM, RW_HEAD_DIM), 2)
    return (i == j).astype(F32)


RW_CHUNK = 64


def _group_norm(y):
    mean = jnp.mean(y, axis=-1, keepdims=True)
    var = jnp.mean(jnp.square(y - mean), axis=-1, keepdims=True)
    return (y - mean) * lax.rsqrt(var + GN_EPS)


def _rwkv_chunk_kernel(r_ref, lw_ref, k_ref, v_ref, kn_ref, b_ref, bonus_ref, g_ref, lnw_ref, lnb_ref,
                       y_ref, s_out_ref, h_ref):
    c = RW_CHUNK
    n = RW_HEAD_DIM

    @pl.when(pl.program_id(0) == 0)
    def _():
        h_ref[...] = jnp.zeros_like(h_ref)

    row = lax.broadcasted_iota(jnp.int32, (c, c), 0)
    col = lax.broadcasted_iota(jnp.int32, (c, c), 1)
    eye = jnp.where(row == col, 1.0, 0.0)
    lower_ones = jnp.where(col <= row, 1.0, 0.0).astype(BF16)
    lw = lw_ref[...]
    lw_hi, lw_mid, lw_lo = _split3(lw)
    cum = _dot(lower_ones, lw_hi) + _dot(lower_ones, lw_mid) + _dot(lower_ones, lw_lo)
    total = cum[c - 1:c, :]
    k = k_ref[...]
    b = b_ref[...]
    v = v_ref[...]
    g_inv = jnp.exp(-cum)
    g_rest = jnp.exp(total - cum)
    decay_total = jnp.exp(total)
    qr = jnp.concatenate([kn_ref[...] * jnp.exp(cum - lw), r_ref[...] * jnp.exp(cum)], axis=0).astype(BF16)
    kdbd = jnp.concatenate([k * g_inv, b * g_inv], axis=0).astype(BF16)
    zk = jnp.concatenate([k * g_rest, b * g_rest], axis=0).astype(BF16)
    vb = v.astype(BF16)
    row2 = lax.broadcasted_iota(jnp.int32, (2 * c, 2 * c), 0)
    col2 = lax.broadcasted_iota(jnp.int32, (2 * c, 2 * c), 1)
    t_idx = jnp.where(row2 >= c, row2 - c, row2)
    s_idx = jnp.where(col2 >= c, col2 - c, col2)
    mask = s_idx < t_idx + jnp.where(row2 >= c, 1, 0)
    heads = range(RW_HEADS)
    sls = [slice(h * n, (h + 1) * n) for h in heads]
    gms = [jnp.where(mask, lax.dot_general(qr[:, sl], kdbd[:, sl], _NT, preferred_element_type=F32), 0.0)
           for sl in sls]
    h_old = [h_ref[h] for h in heads]
    qhs = [_dot(qr[:, sl], ho.astype(BF16)) for sl, ho in zip(sls, h_old)]
    akkv = [_dot(gm[0:c, 0:c].astype(BF16), vb[:, sl]) for gm, sl in zip(gms, sls)]
    powers = [gm[0:c, c:2 * c].astype(BF16) for gm in gms]
    t_invs = [eye - gm[0:c, c:2 * c] for gm in gms]
    for _ in range(int(math.log2(c)) - 1):
        powers = [_dot(p, p).astype(BF16) for p in powers]
        t_invs = [t + _dot(p, t.astype(BF16)) for p, t in zip(powers, t_invs)]
    us = [_dot(t.astype(BF16), (qh[0:c] + av).astype(BF16)) for t, qh, av in zip(t_invs, qhs, akkv)]
    vus = [jnp.concatenate([v[:, sl], -u], axis=0).astype(BF16) for sl, u in zip(sls, us)]
    outs = [_group_norm(qh[c:2 * c] + _dot(gm[c:2 * c, :].astype(BF16), vu)) for qh, gm, vu in zip(qhs, gms, vus)]
    for h in heads:
        decay_col = jnp.sum(eye * decay_total[:, sls[h]], axis=-1, keepdims=True)
        h_ref[h] = decay_col * h_old[h] + lax.dot_general(zk[:, sls[h]], vus[h], _TN, preferred_element_type=F32)
    yn = jnp.concatenate(outs, axis=-1)
    y_ref[...] = ((yn * lnw_ref[...] + lnb_ref[...] + bonus_ref[...]) * g_ref[...]).astype(BF16)

    @pl.when(pl.program_id(0) == pl.num_programs(0) - 1)
    def _():
        for h in range(RW_HEADS):
            s_out_ref[h] = h_ref[h].T


def _rwkv_chunked(ops, bonus, g, ln_w, ln_b):
    m = bonus.shape[0]
    spec = pl.BlockSpec((RW_CHUNK, RW_WIDTH), lambda i: (i, 0))
    row = pl.BlockSpec((1, RW_WIDTH), lambda i: (0, 0))
    state_shape = (RW_HEADS, RW_HEAD_DIM, RW_HEAD_DIM)
    return pl.pallas_call(
        _rwkv_chunk_kernel,
        grid=(m // RW_CHUNK,),
        in_specs=[spec] * 8 + [row, row],
        out_specs=[spec, pl.BlockSpec(state_shape, lambda i: (0, 0, 0))],
        out_shape=[jax.ShapeDtypeStruct((m, RW_WIDTH), BF16), jax.ShapeDtypeStruct(state_shape, F32)],
        scratch_shapes=[pltpu.VMEM(state_shape, F32)],
        compiler_params=_params("arbitrary"),
        name="rwkv_chunked",
    )(*ops, bonus, g, ln_w, ln_b)


def _rwkv_single_step_kernel(r_ref, w_ref, k_ref, v_ref, kn_ref, b_ref, s0_ref, y_ref, s_out_ref):
    eye = _eye3()
    tb = r_ref.shape[1]

    def body(t, carry):
        sl = pl.ds(t, 1)
        state, y_row = _rwkv_step(s0_ref[t], r_ref[:, sl, :], w_ref[:, sl, :], k_ref[:, sl, :],
                                  v_ref[:, sl, :], kn_ref[:, sl, :], b_ref[:, sl, :], eye)
        s_out_ref[t] = state
        y_ref[:, sl, :] = y_row
        return carry

    lax.fori_loop(0, tb, body, 0)


def _rwkv_single_step(ops, state0, tb):
    m = ops[0].shape[1]
    spec = pl.BlockSpec((RW_HEADS, tb, RW_HEAD_DIM), lambda i: (0, i, 0))
    st_spec = pl.BlockSpec((tb, RW_HEADS, RW_HEAD_DIM, RW_HEAD_DIM), lambda i: (i, 0, 0, 0))
    return pl.pallas_call(
        _rwkv_single_step_kernel,
        grid=(m // tb,),
        in_specs=[spec] * 6 + [st_spec],
        out_specs=[spec, st_spec],
        out_shape=[jax.ShapeDtypeStruct((RW_HEADS, m, RW_HEAD_DIM), F32),
                   jax.ShapeDtypeStruct(state0.shape, F32)],
        compiler_params=_params("parallel"),
        name="rwkv_single_step",
    )(*ops, state0)


def _rwkv_post_kernel(y_ref, bonus_ref, g_ref, lnw_ref, lnb_ref, o_ref):
    yn = jnp.concatenate([_group_norm(y_ref[h]) for h in range(RW_HEADS)], axis=-1)
    out = (yn * lnw_ref[...] + lnb_ref[...] + bonus_ref[...]) * g_ref[...]
    o_ref[...] = out.astype(BF16)


def _rwkv_post(y, bonus, g, ln_w, ln_b, tm):
    m = bonus.shape[0]
    flat_spec = pl.BlockSpec((tm, RW_WIDTH), lambda i: (i, 0))
    row = pl.BlockSpec((1, RW_WIDTH), lambda i: (0, 0))
    return pl.pallas_call(
        _rwkv_post_kernel,
        grid=(m // tm,),
        in_specs=[pl.BlockSpec((RW_HEADS, tm, RW_HEAD_DIM), lambda i: (0, i, 0)),
                  flat_spec, flat_spec, row, row],
        out_specs=flat_spec,
        out_shape=jax.ShapeDtypeStruct((m, RW_WIDTH), BF16),
        compiler_params=_params("parallel"),
        name="rwkv_post",
    )(y, bonus, g, ln_w, ln_b)


CV_HALO = 32


def _conv_finish(y, cb_ref, lnw_ref, lnb_ref):
    y = y + cb_ref[...]
    mean = jnp.mean(y, axis=-1, keepdims=True)
    var = jnp.mean(jnp.square(y - mean), axis=-1, keepdims=True)
    yn = (y - mean) * lax.rsqrt(var + LN_EPS) * lnw_ref[...] + lnb_ref[...]
    return _silu(yn).astype(BF16)


def _conv_seq_kernel(p_ref, cw_ref, cb_ref, lnw_ref, lnb_ref, o_ref, tail_ref, ext_ref):
    tm = p_ref.shape[0]
    pad = CV_HALO - (CV_K - 1)

    @pl.when(pl.program_id(0) == 0)
    def _():
        ext_ref[0:CV_HALO, :] = jnp.zeros((CV_HALO, CV_WIDTH), F32)

    p = p_ref[...]
    u = p[:, 0:CV_WIDTH] * _sigmoid(p[:, CV_WIDTH:2 * CV_WIDTH])
    ext_ref[CV_HALO:CV_HALO + tm, :] = u
    acc = jnp.zeros((tm, CV_WIDTH), F32)
    for j in range(CV_K):
        acc = acc + ext_ref[pad + j:pad + j + tm, :] * cw_ref[j:j + 1, :]
    o_ref[...] = _conv_finish(acc, cb_ref, lnw_ref, lnb_ref)
    tail = ext_ref[tm:tm + CV_HALO, :]
    ext_ref[0:CV_HALO, :] = tail
    tail_ref[...] = tail


def _conv_seq(proj, cw, cb, ln_w, ln_b, tm):
    m = proj.shape[0]
    row = pl.BlockSpec((1, CV_WIDTH), lambda i: (0, 0))
    return pl.pallas_call(
        _conv_seq_kernel,
        grid=(m // tm,),
        in_specs=[pl.BlockSpec((tm, 2 * CV_WIDTH), lambda i: (i, COL_CV)),
                  pl.BlockSpec((CV_K, CV_WIDTH), lambda i: (0, 0)), row, row, row],
        out_specs=[pl.BlockSpec((tm, CV_WIDTH), lambda i: (i, 0)),
                   pl.BlockSpec((CV_HALO, CV_WIDTH), lambda i: (0, 0))],
        out_shape=[jax.ShapeDtypeStruct((m, CV_WIDTH), BF16),
                   jax.ShapeDtypeStruct((CV_HALO, CV_WIDTH), F32)],
        scratch_shapes=[pltpu.VMEM((CV_HALO + tm, CV_WIDTH), F32)],
        compiler_params=_params("arbitrary"),
        name="conv_seq",
    )(proj, cw, cb, ln_w, ln_b)


def _conv_step_kernel(p_ref, buf_ref, cw_ref, cb_ref, lnw_ref, lnb_ref, o_ref, nbuf_ref):
    p = p_ref[...]
    u = p[:, :, 0:CV_WIDTH] * _sigmoid(p[:, :, CV_WIDTH:2 * CV_WIDTH])
    buf = buf_ref[...]
    acc = (jnp.sum(buf * cw_ref[0:CV_K - 1, :][None], axis=1, keepdims=True)
           + u * cw_ref[CV_K - 1:CV_K, :][None])
    o_ref[...] = _conv_finish(acc, cb_ref, lnw_ref, lnb_ref)
    nbuf_ref[:, 0:CV_K - 2, :] = buf[:, 1:CV_K - 1, :]
    nbuf_ref[:, CV_K - 2:CV_K - 1, :] = u


def _conv_step(proj, buf, cw, cb, ln_w, ln_b, tb):
    m = proj.shape[0]
    row = pl.BlockSpec((1, CV_WIDTH), lambda i: (0, 0))
    buf_spec = pl.BlockSpec((tb, CV_K - 1, CV_WIDTH), lambda i: (i, 0, 0))
    y = pl.pallas_call(
        _conv_step_kernel,
        grid=(m // tb,),
        in_specs=[pl.BlockSpec((tb, 1, 2 * CV_WIDTH), lambda i: (i, 0, COL_CV)), buf_spec,
                  pl.BlockSpec((CV_K, CV_WIDTH), lambda i: (0, 0)), row, row, row],
        out_specs=[pl.BlockSpec((tb, 1, CV_WIDTH), lambda i: (i, 0, 0)), buf_spec],
        out_shape=[jax.ShapeDtypeStruct((m, 1, CV_WIDTH), BF16),
                   jax.ShapeDtypeStruct(buf.shape, F32)],
        compiler_params=_params("parallel"),
        name="conv_step",
    )(proj.reshape(m, 1, N_PROJ), buf, cw, cb, ln_w, ln_b)
    return y[0].reshape(m, CV_WIDTH), y[1]


def _bucket_thresholds():
    n = np.arange(0, 4 * MAX_DISTANCE, dtype=np.int64)
    max_exact = NUM_BUCKETS // 2
    nf = np.maximum(n, 1).astype(np.float32)
    large = max_exact + (np.log(nf / np.float32(max_exact)) / np.float32(math.log(MAX_DISTANCE / max_exact))
                         * np.float32(NUM_BUCKETS - max_exact)).astype(np.int32)
    large = np.minimum(large, NUM_BUCKETS - 1)
    bucket = np.where(n < max_exact, n, large)
    assert np.all(np.diff(bucket) >= 0) and bucket[-1] == NUM_BUCKETS - 1
    return [int(np.argmax(bucket >= b)) for b in range(NUM_BUCKETS)]


_BUCKET_THR = _bucket_thresholds()


def _bias_from_distance(n, rb_of_bucket):
    bias = jnp.zeros(n.shape, F32) + rb_of_bucket(NUM_BUCKETS - 1)
    for b in range(NUM_BUCKETS - 2, -1, -1):
        bias = jnp.where(n < _BUCKET_THR[b + 1], rb_of_bucket(b), bias)
    return bias


def _rank_select(scores, n_rows, limit):
    idx = lax.broadcasted_iota(jnp.int32, scores.shape, 0)
    beaten = jnp.zeros(scores.shape, F32)
    for j in range(n_rows):
        row = scores[j:j + 1, :]
        ahead = jnp.where(row > scores, 1.0, jnp.where(row == scores, jnp.where(idx > j, 1.0, 0.0), 0.0))
        beaten = beaten + ahead * jnp.where(j < limit, 1.0, 0.0)
    return jnp.where(idx < limit, jnp.where(beaten < MB_TOPK, 1.0, 0.0), 0.0)


MB_HEADS_PER_STEP = 4
MB_SUM_ROWS = 16
MB_BLOCKS_PER_TRIP = 2


def _softmax_blocks(carries, scores, vt_blks):
    m_new = [jnp.maximum(c[0], jnp.max(s, axis=0, keepdims=True)) for c, s in zip(carries, scores)]
    alpha = [jnp.exp2(c[0] - mn) for c, mn in zip(carries, m_new)]
    p = [jnp.exp2(s - mn).astype(BF16) for s, mn in zip(scores, m_new)]
    pv = [_dot(_with_ones_rows(vt), pp) for vt, pp in zip(vt_blks, p)]
    return [(mn, a * c[1] + x) for mn, a, c, x in zip(m_new, alpha, carries, pv)]


def _with_ones_rows(vt_blk):
    return jnp.concatenate([vt_blk, jnp.ones((MB_SUM_ROWS, vt_blk.shape[1]), BF16)], axis=0)


def _moba_seq_kernel(rb_ref, qt_ref, k_ref, vt_ref, km_ref, o_ref, bias_ref, rowterm_ref):
    hp = qt_ref.shape[0]
    h0 = pl.program_id(0) * hp
    qi = pl.program_id(1)
    nb = km_ref.shape[1]
    log2e = math.log2(math.e)
    scale = MB_HEAD_DIM ** -0.5 * log2e
    kidx = lax.broadcasted_iota(jnp.int32, (MB_BLOCK, MB_BLOCK), 0)
    qidx = lax.broadcasted_iota(jnp.int32, (MB_BLOCK, MB_BLOCK), 1)

    @pl.when(qi == 0)
    def _():
        dist = qidx - kidx
        for hh in range(hp):
            rb = lambda b: rb_ref[b, h0 + hh] * log2e
            bias_ref[hh, 0] = jnp.where(dist >= 0, _bias_from_distance(jnp.maximum(dist, 0), rb), NEG_INF)
            bias_ref[hh, 1] = _bias_from_distance(dist + MB_BLOCK, rb)

    near = jnp.maximum(qi - 1, 0)
    qtb, carries = [], []
    for hh in range(hp):
        qt = qt_ref[hh]
        block_scores = _dot_f32(km_ref[hh], qt)
        sel = _rank_select(block_scores, nb, qi)
        rowterm_ref[hh] = jnp.where(sel > 0.0, rb_ref[NUM_BUCKETS - 1, h0 + hh] * log2e, NEG_INF)
        qtb.append((qt * scale).astype(BF16))
        s = _dot(k_ref[hh, qi], qtb[hh]) + bias_ref[hh, 0]
        m0 = jnp.max(s, axis=0, keepdims=True)
        p = jnp.exp2(s - m0).astype(BF16)
        carries.append((m0, _dot(_with_ones_rows(vt_ref[hh, qi]), p)))

    heads = range(hp)
    sel_near = [jnp.where(rowterm_ref[hh, pl.ds(near, 1), :] > 0.5 * NEG_INF, 0.0, NEG_INF) for hh in heads]
    scores = [_dot(k_ref[hh, near], qtb[hh]) + bias_ref[hh, 1] + sel_near[hh] for hh in heads]
    carries = _softmax_blocks(carries, scores, [vt_ref[hh, near] for hh in heads])

    def body(t, carry):
        kjs = [t * MB_BLOCKS_PER_TRIP + u for u in range(MB_BLOCKS_PER_TRIP)]
        safe = [jnp.minimum(kj, nb - 1) for kj in kjs]
        raw = [[_dot(k_ref[hh, kc], qtb[hh]) for hh in heads] for kc in safe]
        for u, (kj, kc) in enumerate(zip(kjs, safe)):
            scores = [raw[u][hh] + jnp.where(kj < near, rowterm_ref[hh, pl.ds(kc, 1), :], NEG_INF) for hh in heads]
            carry = _softmax_blocks(carry, scores, [vt_ref[hh, kc] for hh in heads])
        return tuple(carry)

    n_trips = (near + MB_BLOCKS_PER_TRIP - 1) // MB_BLOCKS_PER_TRIP
    carries = lax.fori_loop(0, n_trips, body, tuple(carries))
    for hh in range(hp):
        acc = carries[hh][1]
        o_ref[hh] = (acc[0:MB_HEAD_DIM] / acc[MB_HEAD_DIM:MB_HEAD_DIM + 1]).astype(BF16)


def _moba_seq(rel_bias, q_t, k_t, v_t, km_tiles):
    t = q_t.shape[1]
    nb = t // MB_BLOCK
    hp = MB_HEADS_PER_STEP
    blocks = lambda x: x.reshape(MB_HEADS, MB_HEAD_DIM, nb, MB_BLOCK)
    qt = q_t.reshape(MB_HEADS, MB_HEAD_DIM, t)
    kb = blocks(k_t).transpose(0, 2, 3, 1)
    vt = blocks(v_t).transpose(0, 2, 1, 3)
    km = km_tiles.transpose(1, 0, 2).reshape(MB_HEADS, MB_HEAD_DIM, nb).transpose(0, 2, 1)
    ot = pl.pallas_call(
        _moba_seq_kernel,
        grid=(MB_HEADS // hp, nb),
        in_specs=[pl.BlockSpec(memory_space=pltpu.SMEM),
                  pl.BlockSpec((hp, MB_HEAD_DIM, MB_BLOCK), lambda h, i: (h, 0, i)),
                  pl.BlockSpec((hp, nb, MB_BLOCK, MB_HEAD_DIM), lambda h, i: (h, 0, 0, 0)),
                  pl.BlockSpec((hp, nb, MB_HEAD_DIM, MB_BLOCK), lambda h, i: (h, 0, 0, 0)),
                  pl.BlockSpec((hp, nb, MB_HEAD_DIM), lambda h, i: (h, 0, 0))],
        out_specs=pl.BlockSpec((hp, MB_HEAD_DIM, MB_BLOCK), lambda h, i: (h, 0, i)),
        out_shape=jax.ShapeDtypeStruct((MB_HEADS, MB_HEAD_DIM, t), BF16),
        scratch_shapes=[pltpu.VMEM((hp, 2, MB_BLOCK, MB_BLOCK), F32), pltpu.VMEM((hp, nb, MB_BLOCK), F32)],
        compiler_params=_params("parallel", "arbitrary"),
        name="moba_seq",
    )(rel_bias, qt, kb, vt, km)
    return ot.reshape(MB_WIDTH, t)


def _moba_paged_kernel(n_pages, pt_ref, q_ref, kn_ref, vn_ref, rbt_ref, *refs):
    k_pages = refs[:n_pages]
    v_pages = refs[n_pages:2 * n_pages]
    o_ref, bias_ref = refs[2 * n_pages:]
    past = n_pages * PAGE_SIZE
    n_blocks = past // MB_BLOCK
    pages_per_block = MB_BLOCK // PAGE_SIZE
    scale = MB_HEAD_DIM ** -0.5
    rb = lambda b: rbt_ref[:, :, b:b + 1]

    @pl.when(pl.program_id(0) == 0)
    def _():
        dist = PAGE_SIZE - lax.broadcasted_iota(jnp.int32, (MB_HEADS, 1, PAGE_SIZE), 2)
        bias_ref[...] = _bias_from_distance(dist, rb)

    di = lax.broadcasted_iota(jnp.int32, (MB_HEAD_DIM, MB_HEAD_DIM), 0)
    dj = lax.broadcasted_iota(jnp.int32, (MB_HEAD_DIM, MB_HEAD_DIM), 1)
    eye = jnp.where(di == dj, 1.0, 0.0)[None]
    q = q_ref[0]
    q_col = jnp.sum(eye * q, axis=-1, keepdims=True)
    raw = [jnp.sum(k_pages[pg][...] * q_col, axis=1, keepdims=True) for pg in range(n_pages)]
    scores = []
    for j in range(n_blocks):
        tot = raw[pages_per_block * j]
        for e in range(1, pages_per_block):
            tot = tot + raw[pages_per_block * j + e]
        scores.append(jnp.sum(tot, axis=-1, keepdims=True) * (1.0 / MB_BLOCK))
    far_bias = rb(NUM_BUCKETS - 1)
    rowterms = []
    for j in range(n_blocks):
        beaten = jnp.zeros((MB_HEADS, 1, 1), F32)
        for i in range(n_blocks):
            if i != j:
                ahead = scores[i] > scores[j] if i > j else scores[i] >= scores[j]
                beaten = beaten + jnp.where(ahead, 1.0, 0.0)
        rowterms.append(jnp.where(beaten < MB_TOPK, far_bias, NEG_INF))
    lg_own = jnp.sum(kn_ref[0] * q, axis=-1, keepdims=True) * scale + rb(0)
    logits = []
    for pg in range(n_pages):
        term = rowterms[pg // pages_per_block]
        if pg == n_pages - 1:
            term = bias_ref[...] + jnp.where(term > 0.5 * NEG_INF, 0.0, NEG_INF)
        logits.append(raw[pg] * scale + term)
    m_row = logits[0]
    for lg in logits[1:]:
        m_row = jnp.maximum(m_row, lg)
    m = jnp.maximum(jnp.max(m_row, axis=-1, keepdims=True), lg_own)
    p_own = jnp.exp(lg_own - m)
    l_row = jnp.zeros((MB_HEADS, 1, PAGE_SIZE), F32)
    acc = jnp.zeros((MB_HEADS, MB_HEAD_DIM, PAGE_SIZE), F32)
    for pg in range(n_pages):
        p = jnp.exp(logits[pg] - m)
        l_row = l_row + p
        acc = acc + v_pages[pg][...] * p
    out_col = jnp.sum(acc, axis=-1, keepdims=True)
    out = jnp.sum(eye * out_col, axis=1, keepdims=True) + p_own * vn_ref[0]
    l = jnp.sum(l_row, axis=-1, keepdims=True) + p_own
    o_ref[0] = (out / l).astype(BF16)


def _moba_paged(rel_bias, q, k_new, v_new, cache_k, cache_v, layer, page_table):
    b, n_pages = page_table.shape
    assert PAGE_SIZE >= _BUCKET_THR[NUM_BUCKETS - 1] and MB_BLOCK % PAGE_SIZE == 0
    head4 = pl.BlockSpec((1, MB_HEADS, 1, MB_HEAD_DIM), lambda i, pt: (i, 0, 0, 0))

    def page_spec(pg):
        return pl.BlockSpec((None, None, MB_HEADS, MB_HEAD_DIM, PAGE_SIZE),
                            lambda i, pt: (layer, pt[i, pg], 0, 0, 0))

    grid_spec = pltpu.PrefetchScalarGridSpec(
        num_scalar_prefetch=1,
        grid=(b,),
        in_specs=[head4, head4, head4, pl.BlockSpec((MB_HEADS, 1, NUM_BUCKETS), lambda i, pt: (0, 0, 0))]
        + [page_spec(pg) for pg in range(n_pages)] * 2,
        out_specs=head4,
        scratch_shapes=[pltpu.VMEM((MB_HEADS, 1, PAGE_SIZE), F32)],
    )
    h4 = lambda x: x.reshape(b, MB_HEADS, 1, MB_HEAD_DIM)
    kt = cache_k.transpose(0, 1, 3, 4, 2)
    vt = cache_v.transpose(0, 1, 3, 4, 2)
    out = pl.pallas_call(
        functools.partial(_moba_paged_kernel, n_pages),
        grid_spec=grid_spec,
        out_shape=jax.ShapeDtypeStruct((b, MB_HEADS, 1, MB_HEAD_DIM), BF16),
        compiler_params=_params("arbitrary"),
        name="moba_paged",
    )(page_table, h4(q), h4(k_new), h4(v_new), rel_bias.T.reshape(MB_HEADS, 1, NUM_BUCKETS),
      *([kt] * n_pages), *([vt] * n_pages))
    return out.reshape(b, MB_WIDTH)


def _merge_kernel(mb_feature_major, yrw_ref, ymb_ref, ycv_ref, g0_ref, g1_ref, g2_ref, wrw_ref, wmb_ref, wcv_ref,
                  o_ref):
    if mb_feature_major:
        y_mb = lax.dot_general(ymb_ref[...], wmb_ref[...], _TN, preferred_element_type=F32)
    else:
        y_mb = _dot(ymb_ref[...], wmb_ref[...])
    merged = (g0_ref[...] * _dot(yrw_ref[...], wrw_ref[...]) + g1_ref[...] * y_mb
              + g2_ref[...] * _dot(ycv_ref[...], wcv_ref[...]))
    o_ref[...] = merged.astype(BF16)


def _merge(y_rw, y_mb, y_cv, proj, w_rw, w_mb, w_cv, tm, mb_feature_major):
    m = proj.shape[0]
    rows = lambda n: pl.BlockSpec((tm, n), lambda i: (i, 0))
    mb_spec = pl.BlockSpec((MB_WIDTH, tm), lambda i: (0, i)) if mb_feature_major else rows(MB_WIDTH)
    gate = lambda br: pl.BlockSpec((tm, D_MODEL), lambda i: (i, COL_GATE + br))
    full = lambda n: pl.BlockSpec((n, D_MODEL), lambda i: (0, 0))
    return pl.pallas_call(
        functools.partial(_merge_kernel, mb_feature_major),
        grid=(m // tm,),
        in_specs=[rows(RW_WIDTH), mb_spec, rows(CV_WIDTH), gate(0), gate(1), gate(2),
                  full(RW_WIDTH), full(MB_WIDTH), full(CV_WIDTH)],
        out_specs=rows(D_MODEL),
        out_shape=jax.ShapeDtypeStruct((m, D_MODEL), BF16),
        compiler_params=_params("parallel"),
        name="branch_merge",
    )(y_rw, y_mb, y_cv, proj, proj, proj, w_rw, w_mb, w_cv)


def _out_proj_kernel(h_ref, x_ref, w_ref, o_ref):
    o_ref[...] = h_ref[...] + _dot(x_ref[...], w_ref[...])


def _out_proj(h, merged, w_out, tm):
    m = h.shape[0]
    rows = pl.BlockSpec((tm, D_MODEL), lambda i: (i, 0))
    return pl.pallas_call(
        _out_proj_kernel,
        grid=(m // tm,),
        in_specs=[rows, rows, pl.BlockSpec((D_MODEL, D_MODEL), lambda i: (0, 0))],
        out_specs=rows,
        out_shape=jax.ShapeDtypeStruct((m, D_MODEL), F32),
        compiler_params=_params("parallel"),
        name="out_proj",
    )(h, merged, w_out)


def _ffn_kernel(h_ref, g_ref, wg_ref, wu_ref, wd_ref, o_ref, xn_ref):
    @pl.when(pl.program_id(1) == 0)
    def _():
        x = h_ref[...]
        ms = jnp.mean(x * x, axis=-1, keepdims=True)
        xn_ref[...] = (x * lax.rsqrt(ms + NORM_EPS) * g_ref[...]).astype(BF16)
        o_ref[...] = x

    xn = xn_ref[...]
    mid = _silu(_dot(xn, wg_ref[...])) * _dot(xn, wu_ref[...])
    o_ref[...] += _dot(mid.astype(BF16), wd_ref[...])


def _ffn(h, g, wg, wu, wd, tm, tf):
    m = h.shape[0]
    d_ff = wg.shape[1]
    rows = pl.BlockSpec((tm, D_MODEL), lambda i, f: (i, 0))
    return pl.pallas_call(
        _ffn_kernel,
        grid=(m // tm, d_ff // tf),
        in_specs=[rows, pl.BlockSpec((1, D_MODEL), lambda i, f: (0, 0)),
                  pl.BlockSpec((D_MODEL, tf), lambda i, f: (0, f)),
                  pl.BlockSpec((D_MODEL, tf), lambda i, f: (0, f)),
                  pl.BlockSpec((tf, D_MODEL), lambda i, f: (f, 0))],
        out_specs=rows,
        out_shape=jax.ShapeDtypeStruct((m, D_MODEL), F32),
        scratch_shapes=[pltpu.VMEM((tm, D_MODEL), BF16)],
        compiler_params=_params("parallel", "arbitrary"),
        name="ffn",
    )(h, g, wg, wu, wd)


def _dot_f32_nt(a, b):
    nt = lambda x, y: lax.dot_general(x, y, _NT, preferred_element_type=F32)
    a_hi, a_mid, a_lo = _split3(a)
    b_hi, b_mid, b_lo = _split3(b)
    return (nt(a_hi, b_hi) + (nt(a_hi, b_mid) + nt(a_mid, b_hi))
            + (nt(a_mid, b_mid) + nt(a_hi, b_lo) + nt(a_lo, b_hi)))


def _router_kernel(h_ref, g_ref, wt_ref, b_ref, before_ref, xn_ref, gate_ref, pos_ref, cnt_ref):
    x = h_ref[...]
    ms = jnp.mean(x * x, axis=-1, keepdims=True)
    xn = x * lax.rsqrt(ms + NORM_EPS) * g_ref[...]
    xn_ref[...] = xn.astype(BF16)
    logits = _dot_f32_nt(wt_ref[...], xn) + b_ref[...]
    idx = lax.broadcasted_iota(jnp.int32, logits.shape, 0).astype(F32)
    m1 = jnp.max(logits, axis=0, keepdims=True)
    i1 = jnp.min(jnp.where(logits == m1, idx, float(N_EXPERTS)), axis=0, keepdims=True)
    rest = jnp.where(idx == i1, -jnp.inf, logits)
    m2 = jnp.max(rest, axis=0, keepdims=True)
    i2 = jnp.min(jnp.where(rest == m2, idx, float(N_EXPERTS)), axis=0, keepdims=True)
    e2 = jnp.exp(m2 - m1)
    gates = jnp.where(idx == i1, 1.0 / (1.0 + e2), 0.0) + jnp.where(idx == i2, e2 / (1.0 + e2), 0.0)
    member = jnp.where(gates > 0.0, 1.0, 0.0)
    gate_ref[0] = gates
    pos_ref[0] = _dot(member.astype(BF16), before_ref[...])
    cnt_ref[0] = jnp.sum(member, axis=-1, keepdims=True).astype(jnp.int32)


def _router(h, g, w_router, b_router, tm):
    m = h.shape[0]
    n_tiles = m // tm
    before = jnp.asarray(np.triu(np.ones((tm, tm), np.float32), 1), BF16)
    tile3 = pl.BlockSpec((1, N_EXPERTS, tm), lambda i: (i, 0, 0))
    per_tile = jax.ShapeDtypeStruct((n_tiles, N_EXPERTS, tm), F32)
    return pl.pallas_call(
        _router_kernel,
        grid=(n_tiles,),
        in_specs=[pl.BlockSpec((tm, D_MODEL), lambda i: (i, 0)),
                  pl.BlockSpec((1, D_MODEL), lambda i: (0, 0)),
                  pl.BlockSpec((N_EXPERTS, D_MODEL), lambda i: (0, 0)),
                  pl.BlockSpec((N_EXPERTS, 1), lambda i: (0, 0)),
                  pl.BlockSpec((tm, tm), lambda i: (0, 0))],
        out_specs=[pl.BlockSpec((tm, D_MODEL), lambda i: (i, 0)), tile3, tile3,
                   pl.BlockSpec((1, N_EXPERTS, 1), lambda i: (i, 0, 0))],
        out_shape=[jax.ShapeDtypeStruct((m, D_MODEL), BF16), per_tile, per_tile,
                   jax.ShapeDtypeStruct((n_tiles, N_EXPERTS, 1), jnp.int32)],
        compiler_params=_params("parallel"),
        name="moe_router",
    )(h, g, w_router.T, b_router.reshape(N_EXPERTS, 1), before)


def _moe_kernel(sub, cnt_ref, xn_ref, gate_ref, pos_ref, wg_ref, wu_ref, wd_ref, o_ref, xg_ref, gg_ref, acc_ref):
    i = pl.program_id(0)
    e = pl.program_id(1)
    f = pl.program_id(2)
    tm = xn_ref.shape[0]
    n_sub = (cnt_ref[i * N_EXPERTS + e] + (sub - 1)) // sub
    pos_row = pos_ref[0, pl.ds(e, 1), :]
    gate_row = gate_ref[0, pl.ds(e, 1), :]
    slot = lax.broadcasted_iota(jnp.int32, (sub, tm), 0)

    def one_hot(s):
        wanted = (slot + s * sub).astype(F32)
        return jnp.where(gate_row > 0.0, jnp.where(pos_row == wanted, 1.0, 0.0), 0.0)

    def rows_of(s):
        return pl.ds(pl.multiple_of(s * sub, sub), sub)

    @pl.when((e == 0) & (f == 0))
    def _():
        o_ref[...] = jnp.zeros_like(o_ref)

    @pl.when(f == 0)
    def _():
        def pack(s, carry):
            pick = one_hot(s)
            xg_ref[rows_of(s), :] = _dot(pick.astype(BF16), xn_ref[...]).astype(BF16)
            gg_ref[rows_of(s), :] = jnp.sum(pick * gate_row, axis=-1, keepdims=True)
            acc_ref[rows_of(s), :] = jnp.zeros((sub, D_MODEL), F32)
            return carry

        lax.fori_loop(0, n_sub, pack, 0)

    def expert(s, carry):
        x = xg_ref[rows_of(s), :]
        mid = _silu(_dot(x, wg_ref[0])) * _dot(x, wu_ref[0]) * gg_ref[rows_of(s), :]
        acc_ref[rows_of(s), :] += _dot(mid.astype(BF16), wd_ref[0])
        return carry

    lax.fori_loop(0, n_sub, expert, 0)

    @pl.when(f == pl.num_programs(2) - 1)
    def _():
        def unpack(s, carry):
            pick = one_hot(s).astype(BF16)
            a = acc_ref[rows_of(s), :]
            hi = a.astype(BF16)
            lo = (a - hi.astype(F32)).astype(BF16)
            o_ref[...] += (lax.dot_general(pick, hi, _TN, preferred_element_type=F32)
                           + lax.dot_general(pick, lo, _TN, preferred_element_type=F32))
            return carry

        lax.fori_loop(0, n_sub, unpack, 0)


def _moe(xn, gates, pos, counts, wg, wu, wd, tm, tf):
    m = xn.shape[0]
    n_e, _, d_ff = wg.shape
    sub = min(tm, -(-(tm * TOP_K * 5 // (N_EXPERTS * 4)) // 16) * 16)
    packed = -(-tm // sub) * sub
    tile3 = pl.BlockSpec((1, N_EXPERTS, tm), lambda i, e, f, cnt: (i, 0, 0))
    grid_spec = pltpu.PrefetchScalarGridSpec(
        num_scalar_prefetch=1,
        grid=(m // tm, n_e, d_ff // tf),
        in_specs=[pl.BlockSpec((tm, D_MODEL), lambda i, e, f, cnt: (i, 0)), tile3, tile3,
                  pl.BlockSpec((1, D_MODEL, tf), lambda i, e, f, cnt: (e, 0, f)),
                  pl.BlockSpec((1, D_MODEL, tf), lambda i, e, f, cnt: (e, 0, f)),
                  pl.BlockSpec((1, tf, D_MODEL), lambda i, e, f, cnt: (e, f, 0))],
        out_specs=pl.BlockSpec((tm, D_MODEL), lambda i, e, f, cnt: (i, 0)),
        scratch_shapes=[pltpu.VMEM((packed, D_MODEL), BF16), pltpu.VMEM((packed, 1), F32),
                        pltpu.VMEM((packed, D_MODEL), F32)],
    )
    return pl.pallas_call(
        functools.partial(_moe_kernel, sub),
        grid_spec=grid_spec,
        out_shape=jax.ShapeDtypeStruct((m, D_MODEL), F32),
        compiler_params=_params("parallel", "arbitrary", "arbitrary"),
        name="moe_experts",
    )(counts.reshape(-1), xn, gates, pos, wg, wu, wd)


def _final_norm_kernel(n_in, *refs):
    g_ref, o_ref = refs[n_in], refs[n_in + 1]
    x = refs[0][...]
    for r in refs[1:n_in]:
        x = x + r[...]
    ms = jnp.mean(x * x, axis=-1, keepdims=True)
    o_ref[...] = x * lax.rsqrt(ms + NORM_EPS) * g_ref[...]


def _final_norm(parts, g, tm):
    m = parts[0].shape[0]
    rows = pl.BlockSpec((tm, D_MODEL), lambda i: (i, 0))
    return pl.pallas_call(
        functools.partial(_final_norm_kernel, len(parts)),
        grid=(m // tm,),
        in_specs=[rows] * len(parts) + [pl.BlockSpec((1, D_MODEL), lambda i: (0, 0))],
        out_specs=rows,
        out_shape=jax.ShapeDtypeStruct((m, D_MODEL), F32),
        compiler_params=_params("parallel"),
        name="final_norm",
    )(*parts, g)


def _add_kernel(a_ref, b_ref, o_ref):
    o_ref[...] = a_ref[...] + b_ref[...]


def _add(a, b, tm):
    rows = pl.BlockSpec((tm, D_MODEL), lambda i: (i, 0))
    return pl.pallas_call(
        _add_kernel,
        grid=(a.shape[0] // tm,),
        in_specs=[rows, rows],
        out_specs=rows,
        out_shape=jax.ShapeDtypeStruct(a.shape, F32),
        compiler_params=_params("parallel"),
        name="residual_add",
    )(a, b)


def _pad_rw_cols(x, axis=-1):
    x = jnp.moveaxis(x, axis, 0)
    pad = jnp.zeros((RW_PAD - RW_COLS,) + x.shape[1:], x.dtype)
    return jnp.moveaxis(jnp.concatenate([x[:RW_COLS], pad, x[RW_COLS:]], axis=0), 0, axis)


def _lora_slab(w, row0):
    slab = jnp.zeros((LORA_PAD, RW_WIDTH), F32)
    return lax.dynamic_update_slice(slab, w, (row0, 0)).astype(BF16)


def _layer_params(l, P):
    row = lambda x: x.reshape(1, -1)
    lp = dict(
        norm_mix=row(P['norm_mix'][l]), norm_ffn=row(P['norm_ffn'][l]),
        w_in_t=_pad_rw_cols(P['w_in'][l].T.astype(BF16), axis=0),
        rw_vecs=[row(_pad_rw_cols(P['rw_mu'][l])), row(P['rw_w0'][l]), row(P['rw_a0'][l]),
                 row(P['rw_kk'][l]), row(P['rw_ka'][l]), row(P['rw_rk'][l])],
        rw_mats=[_lora_slab(P['rw_w2'][l], 0), _lora_slab(P['rw_a2'][l], DECAY_LORA),
                 _lora_slab(P['rw_g2'][l], DECAY_LORA + AAA_LORA)],
        rw_ln_w=row(P['rw_ln_w'][l]), rw_ln_b=row(P['rw_ln_b'][l]),
        cv_w=P['cv_w'][l], cv_b=row(P['cv_b'][l]), cv_ln_w=row(P['cv_ln_w'][l]), cv_ln_b=row(P['cv_ln_b'][l]),
        w_br_rw=P['w_br_rw'][l].astype(BF16), w_br_mb=P['w_br_mb'][l].astype(BF16),
        w_br_cv=P['w_br_cv'][l].astype(BF16), w_out=P['w_out'][l].astype(BF16),
    )
    i = l // 2
    if l % 2 == 0:
        lp.update(wg=P['ffn_wg'][i].astype(BF16), wu=P['ffn_wu'][i].astype(BF16), wd=P['ffn_wd'][i].astype(BF16))
    else:
        lp.update(wg=P['moe_wg'][i].astype(BF16), wu=P['moe_wu'][i].astype(BF16), wd=P['moe_wd'][i].astype(BF16),
                  router=P['moe_router'][i], router_b=P['moe_router_b'][i])
    return lp


def _channel_mix(h, lp, tm_dense, tm_moe):
    if 'router' in lp:
        xn, gates, pos, counts = _router(h, lp['norm_ffn'], lp['router'], lp['router_b'], tm_moe)
        return [h, _moe(xn, gates, pos, counts, lp['wg'], lp['wu'], lp['wd'], tm_moe, 256)]
    return [_ffn(h, lp['norm_ffn'], lp['wg'], lp['wu'], lp['wd'], tm_dense, 512)]


def _as_one(parts, tm):
    return parts[0] if len(parts) == 1 else _add(parts[0], parts[1], tm)


def _run_prompt(x, layers, norm_final, rel_bias):
    t = x.shape[0]
    parts = [x]
    outs = dict(k=[], v=[], wkv=[], shift=[], conv=[])
    for lp in layers:
        h = _as_one(parts, 512)
        proj = _norm_proj(h, lp['norm_mix'], lp['w_in_t'], 1024)
        *ops, g, bonus = _rwkv_prep(proj, None, lp['rw_vecs'], lp['rw_mats'], 128)
        y_rw, s_final = _rwkv_chunked(ops, bonus, g, lp['rw_ln_w'], lp['rw_ln_b'])
        q_t, k_t, v_t, k_tb, v_tb, km_tiles = _qkv_t(h, lp['norm_mix'], lp['w_in_t'], 1024)
        y_mb_t = _moba_seq(rel_bias, q_t, k_tb, v_tb, km_tiles)
        y_cv, tail = _conv_seq(proj, lp['cv_w'], lp['cv_b'], lp['cv_ln_w'], lp['cv_ln_b'], 512)
        merged = _merge(y_rw, y_mb_t, y_cv, proj, lp['w_br_rw'], lp['w_br_mb'], lp['w_br_cv'], 256, True)
        h = _out_proj(h, merged, lp['w_out'], 512)
        parts = _channel_mix(h, lp, 512, 1024)
        tokens_major = lambda x_t: x_t.reshape(MB_HEADS, MB_HEAD_DIM, t).transpose(2, 0, 1)[None]
        outs['k'].append(tokens_major(k_t))
        outs['v'].append(tokens_major(v_t))
        outs['wkv'].append(s_final[None])
        outs['shift'].append(proj[t - 1:t, :RW_COLS])
        outs['conv'].append(tail[None, CV_HALO - (CV_K - 1):, :])
    y = _final_norm(parts, norm_final, 512)
    return (y[None],) + tuple(jnp.stack(outs[n]) for n in ('k', 'v', 'wkv', 'shift', 'conv'))


def _run_sample(x, layers, norm_final, rel_bias, cache_k, cache_v, state_wkv, state_shift, state_conv, page_table):
    b = x.shape[0]
    parts = [x]
    outs = dict(k=[], v=[], wkv=[], shift=[], conv=[])
    for l, lp in enumerate(layers):
        h = _as_one(parts, b)
        proj = _norm_proj(h, lp['norm_mix'], lp['w_in_t'], b)
        *ops, g, bonus = _rwkv_prep(proj, _pad_rw_cols(state_shift[l]), lp['rw_vecs'], lp['rw_mats'], b)
        y, s_new = _rwkv_single_step(ops, state_wkv[l], 8)
        y_rw = _rwkv_post(y, bonus, g, lp['rw_ln_w'], lp['rw_ln_b'], b)
        q = proj[:, RW_PAD:RW_PAD + MB_WIDTH]
        k_new = proj[:, RW_PAD + MB_WIDTH:RW_PAD + 2 * MB_WIDTH]
        v_new = proj[:, RW_PAD + 2 * MB_WIDTH:RW_PAD + 3 * MB_WIDTH]
        y_mb = _moba_paged(rel_bias, q, k_new, v_new, cache_k, cache_v, l, page_table)
        y_cv, conv_new = _conv_step(proj, state_conv[l], lp['cv_w'], lp['cv_b'], lp['cv_ln_w'], lp['cv_ln_b'], 8)
        merged = _merge(y_rw, y_mb, y_cv, proj, lp['w_br_rw'], lp['w_br_mb'], lp['w_br_cv'], b, False)
        h = _out_proj(h, merged, lp['w_out'], b)
        parts = _channel_mix(h, lp, b, b)
        outs['k'].append(k_new.reshape(b, 1, MB_HEADS, MB_HEAD_DIM))
        outs['v'].append(v_new.reshape(b, 1, MB_HEADS, MB_HEAD_DIM))
        outs['wkv'].append(s_new)
        outs['shift'].append(proj[:, :RW_COLS])
        outs['conv'].append(conv_new)
    y = _final_norm(parts, norm_final, b)
    return (y[:, None, :],) + tuple(jnp.stack(outs[n]) for n in ('k', 'v', 'wkv', 'shift', 'conv'))


def kernel(x_prompt, x_sample, cache_k, cache_v, state_wkv, state_shift, state_conv, page_table, norm_mix, norm_ffn, norm_final, w_in, rw_mu, rw_w0, rw_w2, rw_a0, rw_a2, rw_g2, rw_kk, rw_ka, rw_rk, rw_ln_w, rw_ln_b, rel_bias, cv_w, cv_b, cv_ln_w, cv_ln_b, w_br_rw, w_br_mb, w_br_cv, w_out, ffn_wg, ffn_wu, ffn_wd, moe_router, moe_router_b, moe_wg, moe_wu, moe_wd):
    P = dict(norm_mix=norm_mix, norm_ffn=norm_ffn, w_in=w_in,
             rw_mu=rw_mu, rw_w0=rw_w0, rw_w2=rw_w2, rw_a0=rw_a0, rw_a2=rw_a2, rw_g2=rw_g2,
             rw_kk=rw_kk, rw_ka=rw_ka, rw_rk=rw_rk, rw_ln_w=rw_ln_w, rw_ln_b=rw_ln_b,
             cv_w=cv_w, cv_b=cv_b, cv_ln_w=cv_ln_w, cv_ln_b=cv_ln_b,
             w_br_rw=w_br_rw, w_br_mb=w_br_mb, w_br_cv=w_br_cv, w_out=w_out,
             ffn_wg=ffn_wg, ffn_wu=ffn_wu, ffn_wd=ffn_wd,
             moe_router=moe_router, moe_router_b=moe_router_b, moe_wg=moe_wg, moe_wu=moe_wu, moe_wd=moe_wd)
    depth = w_in.shape[0]
    layers = [_layer_params(l, P) for l in range(depth)]
    g_final = norm_final.reshape(1, -1)
    assert x_prompt.shape[0] == 1 and x_sample.shape[1] == 1
    prompt = _run_prompt(x_prompt[0], layers, g_final, rel_bias)
    sample = _run_sample(x_sample[:, 0, :], layers, g_final, rel_bias, cache_k, cache_v,
                         state_wkv, state_shift, state_conv, page_table)
    y_p, k_p, v_p, wkv_p, shift_p, conv_p = prompt
    y_s, k_s, v_s, wkv_s, shift_s, conv_s = sample
    return (y_p, y_s, k_p, v_p, wkv_p, shift_p, conv_p,
            k_s, v_s, wkv_s, shift_s, conv_s)
```

```python
import functools
import math

import numpy as np
import jax
import jax.numpy as jnp
from jax import lax
from jax.experimental import pallas as pl
from jax.experimental.pallas import tpu as pltpu

F32 = jnp.float32
BF16 = jnp.bfloat16

D_MODEL = 2048
RW_HEAD_DIM = 64
RW_WIDTH = 1024
RW_HEADS = 16
DECAY_LORA = 64
AAA_LORA = 64
GATE_LORA = 160
RW_COLS = 3 * RW_WIDTH + DECAY_LORA + AAA_LORA + GATE_LORA
RW_PAD = 3584
LORA_PAD = RW_PAD - 3 * RW_WIDTH
GN_EPS = 64e-5
MB_HEAD_DIM = 64
MB_WIDTH = 512
MB_HEADS = 8
MB_BLOCK = 256
MB_TOPK = 3
NUM_BUCKETS = 32
MAX_DISTANCE = 128
NEG_INF = -1e30
CV_WIDTH = 512
CV_K = 31
LN_EPS = 1e-5
N_BRANCH = 3
N_EXPERTS = 8
TOP_K = 2
NORM_EPS = 1e-6
PAGE_SIZE = 128

PROJ_TILE = 1024
N_PROJ = RW_PAD + 3 * MB_WIDTH + 2 * CV_WIDTH + N_BRANCH * D_MODEL
COL_CV = (RW_PAD + 3 * MB_WIDTH) // (2 * CV_WIDTH)
COL_GATE = (RW_PAD + 3 * MB_WIDTH + 2 * CV_WIDTH) // D_MODEL
GATE_TILE0 = (RW_PAD + 3 * MB_WIDTH + 2 * CV_WIDTH) // PROJ_TILE

VMEM_LIMIT_BYTES = 56 * 1024 * 1024


def _params(*semantics):
    return pltpu.CompilerParams(dimension_semantics=semantics, vmem_limit_bytes=VMEM_LIMIT_BYTES)


def _sigmoid(x):
    return 1.0 / (1.0 + jnp.exp(-x))


def _silu(x):
    return x * _sigmoid(x)


def _split3(x):
    hi = x.astype(BF16)
    r1 = x - hi.astype(F32)
    mid = r1.astype(BF16)
    lo = (r1 - mid.astype(F32)).astype(BF16)
    return hi, mid, lo


_NT = (((1,), (1,)), ((), ()))
_TN = (((0,), (0,)), ((), ()))


def _dot(a, b):
    return jnp.dot(a, b, preferred_element_type=F32)


def _dot_exact_rhs(x, sel):
    hi, mid, lo = _split3(x)
    return _dot(hi, sel) + _dot(mid, sel) + _dot(lo, sel)


def _dot_f32(a, b):
    a_hi, a_mid, a_lo = _split3(a)
    b_hi, b_mid, b_lo = _split3(b)
    return (_dot(a_hi, b_hi) + (_dot(a_hi, b_mid) + _dot(a_mid, b_hi))
            + (_dot(a_mid, b_mid) + _dot(a_hi, b_lo) + _dot(a_lo, b_hi)))


def _rms_norm_bf16(x, g):
    ms = jnp.mean(x * x, axis=-1, keepdims=True)
    return (x * lax.rsqrt(ms + NORM_EPS) * g).astype(BF16)


def _norm_proj_kernel(x_ref, g_ref, wt_ref, o_ref, xn_ref):
    j = pl.program_id(1)

    @pl.when(j == 0)
    def _():
        xn_ref[...] = _rms_norm_bf16(x_ref[...], g_ref[...])

    acc = lax.dot_general(xn_ref[...], wt_ref[...], _NT, preferred_element_type=F32)

    @pl.when(j < GATE_TILE0)
    def _():
        o_ref[...] = acc

    @pl.when(j >= GATE_TILE0)
    def _():
        o_ref[...] = _sigmoid(acc)


def _norm_proj(x, g, wt, tm):
    m = x.shape[0]
    return pl.pallas_call(
        _norm_proj_kernel,
        grid=(m // tm, N_PROJ // PROJ_TILE),
        in_specs=[pl.BlockSpec((tm, D_MODEL), lambda i, j: (i, 0)),
                  pl.BlockSpec((1, D_MODEL), lambda i, j: (0, 0)),
                  pl.BlockSpec((PROJ_TILE, D_MODEL), lambda i, j: (j, 0))],
        out_specs=pl.BlockSpec((tm, PROJ_TILE), lambda i, j: (i, j)),
        out_shape=jax.ShapeDtypeStruct((m, N_PROJ), F32),
        scratch_shapes=[pltpu.VMEM((tm, D_MODEL), BF16)],
        compiler_params=_params("parallel", "arbitrary"),
        name="norm_proj",
    )(x, g, wt)


def _qkv_t_kernel(x_ref, g_ref, wq_ref, wk_ref, wv_ref, q_ref, k_ref, v_ref, kb_ref, vb_ref, km_ref):
    xn = _rms_norm_bf16(x_ref[...], g_ref[...])
    nt = lambda w_ref: lax.dot_general(w_ref[...], xn, _NT, preferred_element_type=F32)
    q_ref[...] = nt(wq_ref)
    k = nt(wk_ref)
    v = nt(wv_ref)
    k_ref[...] = k
    v_ref[...] = v
    kb_ref[...] = k.astype(BF16)
    vb_ref[...] = v.astype(BF16)
    n_blk = k.shape[1] // MB_BLOCK
    sums = [jnp.sum(k[:, b * MB_BLOCK:(b + 1) * MB_BLOCK], axis=-1, keepdims=True) for b in range(n_blk)]
    km_ref[0] = jnp.concatenate(sums, axis=-1) * (1.0 / MB_BLOCK)


def _qkv_t(x, g, wt, tm):
    t = x.shape[0]
    q_row = RW_PAD // MB_WIDTH
    w_spec = lambda r: pl.BlockSpec((MB_WIDTH, D_MODEL), lambda i: (q_row + r, 0))
    col = pl.BlockSpec((MB_WIDTH, tm), lambda i: (0, i))
    f32_t = jax.ShapeDtypeStruct((MB_WIDTH, t), F32)
    bf16_t = jax.ShapeDtypeStruct((MB_WIDTH, t), BF16)
    return pl.pallas_call(
        _qkv_t_kernel,
        grid=(t // tm,),
        in_specs=[pl.BlockSpec((tm, D_MODEL), lambda i: (i, 0)), pl.BlockSpec((1, D_MODEL), lambda i: (0, 0)),
                  w_spec(0), w_spec(1), w_spec(2)],
        out_specs=[col] * 5 + [pl.BlockSpec((1, MB_WIDTH, tm // MB_BLOCK), lambda i: (i, 0, 0))],
        out_shape=[f32_t] * 3 + [bf16_t] * 2 + [jax.ShapeDtypeStruct((t // tm, MB_WIDTH, tm // MB_BLOCK), F32)],
        compiler_params=_params("parallel"),
        name="qkv_t",
    )(x, g, wt, wt, wt)


def _rwkv_prep_kernel(sequential, *refs):
    if sequential:
        (p_ref, mu_ref, w0_ref, a0_ref, kk_ref, ka_ref, rk_ref, w2_ref, a2_ref, g2_ref,
         r_o, w_o, k_o, v_o, kn_o, b_o, g_o, bonus_o, carry_ref) = refs
    else:
        (p_ref, prev_ref, mu_ref, w0_ref, a0_ref, kk_ref, ka_ref, rk_ref, w2_ref, a2_ref, g2_ref,
         r_o, w_o, k_o, v_o, kn_o, b_o, g_o, bonus_o) = refs
    p = p_ref[...]
    tm = p.shape[0]
    if sequential:
        @pl.when(pl.program_id(0) == 0)
        def _():
            carry_ref[...] = jnp.zeros_like(carry_ref)

        rows = lax.broadcasted_iota(jnp.int32, p.shape, 0)
        prev = jnp.where(rows == 0, carry_ref[...], pltpu.roll(p, 1, 0))
        carry_ref[...] = p[tm - 1:tm, :]
    else:
        prev = prev_ref[...]
    xs = p + (prev - p) * mu_ref[...]
    r = xs[:, 0:RW_WIDTH]
    k = xs[:, RW_WIDTH:2 * RW_WIDTH]
    v = xs[:, 2 * RW_WIDTH:3 * RW_WIDTH]
    lora = xs[:, 3 * RW_WIDTH:RW_PAD]
    wx = w0_ref[...] + _dot(jnp.tanh(lora).astype(BF16), w2_ref[...])
    w_log = jnp.minimum(wx, 0.0) - jnp.log(1.0 + jnp.exp(-jnp.abs(wx))) - 0.5
    log_decay = -jnp.exp(w_log)
    a = _sigmoid(a0_ref[...] + _dot(lora.astype(BF16), a2_ref[...]))
    g_o[...] = _dot(_sigmoid(lora).astype(BF16), g2_ref[...])
    kk = k * kk_ref[...]
    k_mod = k * (1.0 + (a - 1.0) * ka_ref[...])
    rkr = r * k_mod * rk_ref[...]
    if sequential:
        r_o[...] = r
        w_o[...] = log_decay
        k_o[...] = k_mod
        v_o[...] = v
    else:
        decay = jnp.exp(log_decay)
    for h in range(RW_HEADS):
        sl = slice(h * RW_HEAD_DIM, (h + 1) * RW_HEAD_DIM)
        kk_h = kk[:, sl]
        nrm = jnp.sqrt(jnp.sum(kk_h * kk_h, axis=-1, keepdims=True))
        kn_h = kk_h / jnp.maximum(nrm, 1e-12)
        if sequential:
            kn_o[:, sl] = kn_h
            b_o[:, sl] = kn_h * a[:, sl]
        else:
            r_o[h] = r[:, sl]
            w_o[h] = decay[:, sl]
            k_o[h] = k_mod[:, sl]
            v_o[h] = v[:, sl]
            kn_o[h] = kn_h
            b_o[h] = kn_h * a[:, sl]
        bonus_o[:, sl] = jnp.sum(rkr[:, sl], axis=-1, keepdims=True) * v[:, sl]


def _rwkv_prep(proj, prev, vecs, mats, tm):
    m = proj.shape[0]
    sequential = prev is None
    row = lambda n: pl.BlockSpec((1, n), lambda i: (0, 0))
    in_specs = [pl.BlockSpec((tm, RW_PAD), lambda i: (i, 0))]
    args = [proj]
    if not sequential:
        in_specs.append(pl.BlockSpec((tm, RW_PAD), lambda i: (i, 0)))
        args.append(prev)
    in_specs += [row(RW_PAD)] + [row(RW_WIDTH)] * 5
    in_specs += [pl.BlockSpec((LORA_PAD, RW_WIDTH), lambda i: (0, 0))] * 3
    flat = jax.ShapeDtypeStruct((m, RW_WIDTH), F32)
    flat_spec = pl.BlockSpec((tm, RW_WIDTH), lambda i: (i, 0))
    if sequential:
        op_shape, op_spec = flat, flat_spec
    else:
        op_shape = jax.ShapeDtypeStruct((RW_HEADS, m, RW_HEAD_DIM), F32)
        op_spec = pl.BlockSpec((RW_HEADS, tm, RW_HEAD_DIM), lambda i: (0, i, 0))
    return pl.pallas_call(
        functools.partial(_rwkv_prep_kernel, sequential),
        grid=(m // tm,),
        in_specs=in_specs,
        out_specs=[op_spec] * 6 + [flat_spec] * 2,
        out_shape=[op_shape] * 6 + [flat] * 2,
        scratch_shapes=[pltpu.VMEM((1, RW_PAD), F32)] if sequential else [],
        compiler_params=_params("arbitrary"),
        name="rwkv_prep",
    )(*args, *vecs, *mats)


def _rwkv_step(state, r_t, w_t, k_t, v_t, kn_t, b_t, eye):
    sa = jnp.sum(state * kn_t, axis=-1, keepdims=True)
    v_col = jnp.sum(eye * v_t, axis=-1, keepdims=True)
    state = state * w_t - sa * b_t + v_col * k_t
    y_col = jnp.sum(state * r_t, axis=-1, keepdims=True)
    y_row = jnp.sum(eye * y_col, axis=1, keepdims=True)
    return state, y_row


def _eye3():
    i = lax.broadcasted_iota(jnp.int32, (1, RW_HEAD_DIM, RW_HEAD_DIM), 1)
    j = lax.broadcasted_iota(jnp.int32, (1, RW_HEAD_DIM, RW_HEAD_DIM), 2)
    return (i == j).astype(F32)


RW_CHUNK = 64


def _group_norm(y):
    mean = jnp.mean(y, axis=-1, keepdims=True)
    var = jnp.mean(jnp.square(y - mean), axis=-1, keepdims=True)
    return (y - mean) * lax.rsqrt(var + GN_EPS)


def _rwkv_chunk_kernel(r_ref, lw_ref, k_ref, v_ref, kn_ref, b_ref, bonus_ref, g_ref, lnw_ref, lnb_ref,
                       y_ref, s_out_ref, h_ref):
    c = RW_CHUNK
    n = RW_HEAD_DIM

    @pl.when(pl.program_id(0) == 0)
    def _():
        h_ref[...] = jnp.zeros_like(h_ref)

    row = lax.broadcasted_iota(jnp.int32, (c, c), 0)
    col = lax.broadcasted_iota(jnp.int32, (c, c), 1)
    eye = jnp.where(row == col, 1.0, 0.0)
    lower_ones = jnp.where(col <= row, 1.0, 0.0).astype(BF16)
    lw = lw_ref[...]
    lw_hi, lw_mid, lw_lo = _split3(lw)
    cum = _dot(lower_ones, lw_hi) + _dot(lower_ones, lw_mid) + _dot(lower_ones, lw_lo)
    total = cum[c - 1:c, :]
    k = k_ref[...]
    b = b_ref[...]
    v = v_ref[...]
    g_inv = jnp.exp(-cum)
    g_rest = jnp.exp(total - cum)
    decay_total = jnp.exp(total)
    qr = jnp.concatenate([kn_ref[...] * jnp.exp(cum - lw), r_ref[...] * jnp.exp(cum)], axis=0).astype(BF16)
    kdbd = jnp.concatenate([k * g_inv, b * g_inv], axis=0).astype(BF16)
    zk = jnp.concatenate([k * g_rest, b * g_rest], axis=0).astype(BF16)
    vb = v.astype(BF16)
    row2 = lax.broadcasted_iota(jnp.int32, (2 * c, 2 * c), 0)
    col2 = lax.broadcasted_iota(jnp.int32, (2 * c, 2 * c), 1)
    t_idx = jnp.where(row2 >= c, row2 - c, row2)
    s_idx = jnp.where(col2 >= c, col2 - c, col2)
    mask = s_idx < t_idx + jnp.where(row2 >= c, 1, 0)
    heads = range(RW_HEADS)
    sls = [slice(h * n, (h + 1) * n) for h in heads]
    gms = [jnp.where(mask, lax.dot_general(qr[:, sl], kdbd[:, sl], _NT, preferred_element_type=F32), 0.0)
           for sl in sls]
    h_old = [h_ref[h] for h in heads]
    qhs = [_dot(qr[:, sl], ho.astype(BF16)) for sl, ho in zip(sls, h_old)]
    akkv = [_dot(gm[0:c, 0:c].astype(BF16), vb[:, sl]) for gm, sl in zip(gms, sls)]
    powers = [gm[0:c, c:2 * c].astype(BF16) for gm in gms]
    t_invs = [eye - gm[0:c, c:2 * c] for gm in gms]
    for _ in range(int(math.log2(c)) - 1):
        powers = [_dot(p, p).astype(BF16) for p in powers]
        t_invs = [t + _dot(p, t.astype(BF16)) for p, t in zip(powers, t_invs)]
    us = [_dot(t.astype(BF16), (qh[0:c] + av).astype(BF16)) for t, qh, av in zip(t_invs, qhs, akkv)]
    vus = [jnp.concatenate([v[:, sl], -u], axis=0).astype(BF16) for sl, u in zip(sls, us)]
    outs = [_group_norm(qh[c:2 * c] + _dot(gm[c:2 * c, :].astype(BF16), vu)) for qh, gm, vu in zip(qhs, gms, vus)]
    for h in heads:
        decay_col = jnp.sum(eye * decay_total[:, sls[h]], axis=-1, keepdims=True)
        h_ref[h] = decay_col * h_old[h] + lax.dot_general(zk[:, sls[h]], vus[h], _TN, preferred_element_type=F32)
    yn = jnp.concatenate(outs, axis=-1)
    y_ref[...] = ((yn * lnw_ref[...] + lnb_ref[...] + bonus_ref[...]) * g_ref[...]).astype(BF16)

    @pl.when(pl.program_id(0) == pl.num_programs(0) - 1)
    def _():
        for h in range(RW_HEADS):
            s_out_ref[h] = h_ref[h].T


def _rwkv_chunked(ops, bonus, g, ln_w, ln_b):
    m = bonus.shape[0]
    spec = pl.BlockSpec((RW_CHUNK, RW_WIDTH), lambda i: (i, 0))
    row = pl.BlockSpec((1, RW_WIDTH), lambda i: (0, 0))
    state_shape = (RW_HEADS, RW_HEAD_DIM, RW_HEAD_DIM)
    return pl.pallas_call(
        _rwkv_chunk_kernel,
        grid=(m // RW_CHUNK,),
        in_specs=[spec] * 8 + [row, row],
        out_specs=[spec, pl.BlockSpec(state_shape, lambda i: (0, 0, 0))],
        out_shape=[jax.ShapeDtypeStruct((m, RW_WIDTH), BF16), jax.ShapeDtypeStruct(state_shape, F32)],
        scratch_shapes=[pltpu.VMEM(state_shape, F32)],
        compiler_params=_params("arbitrary"),
        name="rwkv_chunked",
    )(*ops, bonus, g, ln_w, ln_b)


def _rwkv_single_step_kernel(r_ref, w_ref, k_ref, v_ref, kn_ref, b_ref, s0_ref, *rest):
    y_ref, s_out_ref = rest[-2:]
    eye = _eye3()
    tb = r_ref.shape[1]

    def body(t, carry):
        sl = pl.ds(t, 1)
        state, y_row = _rwkv_step(s0_ref[t], r_ref[:, sl, :], w_ref[:, sl, :], k_ref[:, sl, :],
                                  v_ref[:, sl, :], kn_ref[:, sl, :], b_ref[:, sl, :], eye)
        s_out_ref[t] = state
        y_ref[:, sl, :] = y_row
        return carry

    lax.fori_loop(0, tb, body, 0)


def _rwkv_single_step(ops, states, layer, new_states, tb):
    m = ops[0].shape[1]
    spec = pl.BlockSpec((RW_HEADS, tb, RW_HEAD_DIM), lambda i: (0, i, 0))
    st_spec = pl.BlockSpec((None, tb, RW_HEADS, RW_HEAD_DIM, RW_HEAD_DIM), lambda i: (layer, i, 0, 0, 0))
    carried = [] if new_states is None else [new_states]
    return pl.pallas_call(
        _rwkv_single_step_kernel,
        grid=(m // tb,),
        in_specs=[spec] * 6 + [st_spec] + [pl.BlockSpec(memory_space=pl.ANY)] * len(carried),
        out_specs=[spec, st_spec],
        out_shape=[jax.ShapeDtypeStruct((RW_HEADS, m, RW_HEAD_DIM), F32),
                   jax.ShapeDtypeStruct(states.shape, F32)],
        input_output_aliases={7: 1} if carried else {},
        compiler_params=_params("parallel"),
        name="rwkv_single_step",
    )(*ops, states, *carried)


def _rwkv_post_kernel(y_ref, bonus_ref, g_ref, lnw_ref, lnb_ref, o_ref):
    yn = jnp.concatenate([_group_norm(y_ref[h]) for h in range(RW_HEADS)], axis=-1)
    out = (yn * lnw_ref[...] + lnb_ref[...] + bonus_ref[...]) * g_ref[...]
    o_ref[...] = out.astype(BF16)


def _rwkv_post(y, bonus, g, ln_w, ln_b, tm):
    m = bonus.shape[0]
    flat_spec = pl.BlockSpec((tm, RW_WIDTH), lambda i: (i, 0))
    row = pl.BlockSpec((1, RW_WIDTH), lambda i: (0, 0))
    return pl.pallas_call(
        _rwkv_post_kernel,
        grid=(m // tm,),
        in_specs=[pl.BlockSpec((RW_HEADS, tm, RW_HEAD_DIM), lambda i: (0, i, 0)),
                  flat_spec, flat_spec, row, row],
        out_specs=flat_spec,
        out_shape=jax.ShapeDtypeStruct((m, RW_WIDTH), BF16),
        compiler_params=_params("parallel"),
        name="rwkv_post",
    )(y, bonus, g, ln_w, ln_b)


CV_HALO = 32


def _conv_finish(y, cb_ref, lnw_ref, lnb_ref):
    y = y + cb_ref[...]
    mean = jnp.mean(y, axis=-1, keepdims=True)
    var = jnp.mean(jnp.square(y - mean), axis=-1, keepdims=True)
    yn = (y - mean) * lax.rsqrt(var + LN_EPS) * lnw_ref[...] + lnb_ref[...]
    return _silu(yn).astype(BF16)


def _conv_seq_kernel(p_ref, cw_ref, cb_ref, lnw_ref, lnb_ref, o_ref, tail_ref, ext_ref):
    tm = p_ref.shape[0]
    pad = CV_HALO - (CV_K - 1)

    @pl.when(pl.program_id(0) == 0)
    def _():
        ext_ref[0:CV_HALO, :] = jnp.zeros((CV_HALO, CV_WIDTH), F32)

    p = p_ref[...]
    u = p[:, 0:CV_WIDTH] * _sigmoid(p[:, CV_WIDTH:2 * CV_WIDTH])
    ext_ref[CV_HALO:CV_HALO + tm, :] = u
    acc = jnp.zeros((tm, CV_WIDTH), F32)
    for j in range(CV_K):
        acc = acc + ext_ref[pad + j:pad + j + tm, :] * cw_ref[j:j + 1, :]
    o_ref[...] = _conv_finish(acc, cb_ref, lnw_ref, lnb_ref)
    tail = ext_ref[tm:tm + CV_HALO, :]
    ext_ref[0:CV_HALO, :] = tail
    tail_ref[...] = tail


def _conv_seq(proj, cw, cb, ln_w, ln_b, tm):
    m = proj.shape[0]
    row = pl.BlockSpec((1, CV_WIDTH), lambda i: (0, 0))
    return pl.pallas_call(
        _conv_seq_kernel,
        grid=(m // tm,),
        in_specs=[pl.BlockSpec((tm, 2 * CV_WIDTH), lambda i: (i, COL_CV)),
                  pl.BlockSpec((CV_K, CV_WIDTH), lambda i: (0, 0)), row, row, row],
        out_specs=[pl.BlockSpec((tm, CV_WIDTH), lambda i: (i, 0)),
                   pl.BlockSpec((CV_HALO, CV_WIDTH), lambda i: (0, 0))],
        out_shape=[jax.ShapeDtypeStruct((m, CV_WIDTH), BF16),
                   jax.ShapeDtypeStruct((CV_HALO, CV_WIDTH), F32)],
        scratch_shapes=[pltpu.VMEM((CV_HALO + tm, CV_WIDTH), F32)],
        compiler_params=_params("arbitrary"),
        name="conv_seq",
    )(proj, cw, cb, ln_w, ln_b)


def _conv_step_kernel(p_ref, buf_ref, cw_ref, cb_ref, lnw_ref, lnb_ref, o_ref, nbuf_ref):
    p = p_ref[...]
    u = p[:, :, 0:CV_WIDTH] * _sigmoid(p[:, :, CV_WIDTH:2 * CV_WIDTH])
    buf = buf_ref[...]
    acc = (jnp.sum(buf * cw_ref[0:CV_K - 1, :][None], axis=1, keepdims=True)
           + u * cw_ref[CV_K - 1:CV_K, :][None])
    o_ref[...] = _conv_finish(acc, cb_ref, lnw_ref, lnb_ref)
    nbuf_ref[:, 0:CV_K - 2, :] = buf[:, 1:CV_K - 1, :]
    nbuf_ref[:, CV_K - 2:CV_K - 1, :] = u


def _conv_step(proj, buf, cw, cb, ln_w, ln_b, tb):
    m = proj.shape[0]
    row = pl.BlockSpec((1, CV_WIDTH), lambda i: (0, 0))
    buf_spec = pl.BlockSpec((tb, CV_K - 1, CV_WIDTH), lambda i: (i, 0, 0))
    y = pl.pallas_call(
        _conv_step_kernel,
        grid=(m // tb,),
        in_specs=[pl.BlockSpec((tb, 1, 2 * CV_WIDTH), lambda i: (i, 0, COL_CV)), buf_spec,
                  pl.BlockSpec((CV_K, CV_WIDTH), lambda i: (0, 0)), row, row, row],
        out_specs=[pl.BlockSpec((tb, 1, CV_WIDTH), lambda i: (i, 0, 0)), buf_spec],
        out_shape=[jax.ShapeDtypeStruct((m, 1, CV_WIDTH), BF16),
                   jax.ShapeDtypeStruct(buf.shape, F32)],
        compiler_params=_params("parallel"),
        name="conv_step",
    )(proj.reshape(m, 1, N_PROJ), buf, cw, cb, ln_w, ln_b)
    return y[0].reshape(m, CV_WIDTH), y[1]


def _bucket_thresholds():
    n = np.arange(0, 4 * MAX_DISTANCE, dtype=np.int64)
    max_exact = NUM_BUCKETS // 2
    nf = np.maximum(n, 1).astype(np.float32)
    large = max_exact + (np.log(nf / np.float32(max_exact)) / np.float32(math.log(MAX_DISTANCE / max_exact))
                         * np.float32(NUM_BUCKETS - max_exact)).astype(np.int32)
    large = np.minimum(large, NUM_BUCKETS - 1)
    bucket = np.where(n < max_exact, n, large)
    assert np.all(np.diff(bucket) >= 0) and bucket[-1] == NUM_BUCKETS - 1
    return [int(np.argmax(bucket >= b)) for b in range(NUM_BUCKETS)]


_BUCKET_THR = _bucket_thresholds()


def _bias_from_distance(n, rb_of_bucket):
    bias = jnp.zeros(n.shape, F32) + rb_of_bucket(NUM_BUCKETS - 1)
    for b in range(NUM_BUCKETS - 2, -1, -1):
        bias = jnp.where(n < _BUCKET_THR[b + 1], rb_of_bucket(b), bias)
    return bias


def _rank_select(scores, n_rows, limit):
    idx = lax.broadcasted_iota(jnp.int32, scores.shape, 0)
    beaten = jnp.zeros(scores.shape, F32)
    for j in range(n_rows):
        row = scores[j:j + 1, :]
        ahead = jnp.where(row > scores, 1.0, jnp.where(row == scores, jnp.where(idx > j, 1.0, 0.0), 0.0))
        beaten = beaten + ahead * jnp.where(j < limit, 1.0, 0.0)
    return jnp.where(idx < limit, jnp.where(beaten < MB_TOPK, 1.0, 0.0), 0.0)


MB_HEADS_PER_STEP = 4
MB_SUM_ROWS = 16
MB_BLOCKS_PER_TRIP = 2


def _softmax_blocks(carries, scores, vt_blks):
    m_new = [jnp.maximum(c[0], jnp.max(s, axis=0, keepdims=True)) for c, s in zip(carries, scores)]
    alpha = [jnp.exp2(c[0] - mn) for c, mn in zip(carries, m_new)]
    p = [jnp.exp2(s - mn).astype(BF16) for s, mn in zip(scores, m_new)]
    pv = [_dot(_with_ones_rows(vt), pp) for vt, pp in zip(vt_blks, p)]
    return [(mn, a * c[1] + x) for mn, a, c, x in zip(m_new, alpha, carries, pv)]


def _with_ones_rows(vt_blk):
    return jnp.concatenate([vt_blk, jnp.ones((MB_SUM_ROWS, vt_blk.shape[1]), BF16)], axis=0)


def _moba_seq_kernel(rb_ref, qt_ref, k_ref, vt_ref, km_ref, o_ref, bias_ref, rowterm_ref):
    hp = qt_ref.shape[0]
    h0 = pl.program_id(0) * hp
    qi = pl.program_id(1)
    nb = km_ref.shape[1]
    log2e = math.log2(math.e)
    scale = MB_HEAD_DIM ** -0.5 * log2e
    kidx = lax.broadcasted_iota(jnp.int32, (MB_BLOCK, MB_BLOCK), 0)
    qidx = lax.broadcasted_iota(jnp.int32, (MB_BLOCK, MB_BLOCK), 1)

    @pl.when(qi == 0)
    def _():
        dist = qidx - kidx
        for hh in range(hp):
            rb = lambda b: rb_ref[b, h0 + hh] * log2e
            bias_ref[hh, 0] = jnp.where(dist >= 0, _bias_from_distance(jnp.maximum(dist, 0), rb), NEG_INF)
            bias_ref[hh, 1] = _bias_from_distance(dist + MB_BLOCK, rb)

    near = jnp.maximum(qi - 1, 0)
    qtb, carries = [], []
    for hh in range(hp):
        qt = qt_ref[hh]
        block_scores = _dot_f32(km_ref[hh], qt)
        sel = _rank_select(block_scores, nb, qi)
        rowterm_ref[hh] = jnp.where(sel > 0.0, rb_ref[NUM_BUCKETS - 1, h0 + hh] * log2e, NEG_INF)
        qtb.append((qt * scale).astype(BF16))
        s = _dot(k_ref[hh, qi], qtb[hh]) + bias_ref[hh, 0]
        m0 = jnp.max(s, axis=0, keepdims=True)
        p = jnp.exp2(s - m0).astype(BF16)
        carries.append((m0, _dot(_with_ones_rows(vt_ref[hh, qi]), p)))

    heads = range(hp)
    sel_near = [jnp.where(rowterm_ref[hh, pl.ds(near, 1), :] > 0.5 * NEG_INF, 0.0, NEG_INF) for hh in heads]
    scores = [_dot(k_ref[hh, near], qtb[hh]) + bias_ref[hh, 1] + sel_near[hh] for hh in heads]
    carries = _softmax_blocks(carries, scores, [vt_ref[hh, near] for hh in heads])

    def body(t, carry):
        kjs = [t * MB_BLOCKS_PER_TRIP + u for u in range(MB_BLOCKS_PER_TRIP)]
        safe = [jnp.minimum(kj, nb - 1) for kj in kjs]
        raw = [[_dot(k_ref[hh, kc], qtb[hh]) for hh in heads] for kc in safe]
        for u, (kj, kc) in enumerate(zip(kjs, safe)):
            scores = [raw[u][hh] + jnp.where(kj < near, rowterm_ref[hh, pl.ds(kc, 1), :], NEG_INF) for hh in heads]
            carry = _softmax_blocks(carry, scores, [vt_ref[hh, kc] for hh in heads])
        return tuple(carry)

    n_trips = (near + MB_BLOCKS_PER_TRIP - 1) // MB_BLOCKS_PER_TRIP
    carries = lax.fori_loop(0, n_trips, body, tuple(carries))
    for hh in range(hp):
        acc = carries[hh][1]
        o_ref[hh] = (acc[0:MB_HEAD_DIM] / acc[MB_HEAD_DIM:MB_HEAD_DIM + 1]).astype(BF16)


def _moba_seq(rel_bias, q_t, k_t, v_t, km_tiles):
    t = q_t.shape[1]
    nb = t // MB_BLOCK
    hp = MB_HEADS_PER_STEP
    blocks = lambda x: x.reshape(MB_HEADS, MB_HEAD_DIM, nb, MB_BLOCK)
    qt = q_t.reshape(MB_HEADS, MB_HEAD_DIM, t)
    kb = blocks(k_t).transpose(0, 2, 3, 1)
    vt = blocks(v_t).transpose(0, 2, 1, 3)
    km = km_tiles.transpose(1, 0, 2).reshape(MB_HEADS, MB_HEAD_DIM, nb).transpose(0, 2, 1)
    ot = pl.pallas_call(
        _moba_seq_kernel,
        grid=(MB_HEADS // hp, nb),
        in_specs=[pl.BlockSpec(memory_space=pltpu.SMEM),
                  pl.BlockSpec((hp, MB_HEAD_DIM, MB_BLOCK), lambda h, i: (h, 0, i)),
                  pl.BlockSpec((hp, nb, MB_BLOCK, MB_HEAD_DIM), lambda h, i: (h, 0, 0, 0)),
                  pl.BlockSpec((hp, nb, MB_HEAD_DIM, MB_BLOCK), lambda h, i: (h, 0, 0, 0)),
                  pl.BlockSpec((hp, nb, MB_HEAD_DIM), lambda h, i: (h, 0, 0))],
        out_specs=pl.BlockSpec((hp, MB_HEAD_DIM, MB_BLOCK), lambda h, i: (h, 0, i)),
        out_shape=jax.ShapeDtypeStruct((MB_HEADS, MB_HEAD_DIM, t), BF16),
        scratch_shapes=[pltpu.VMEM((hp, 2, MB_BLOCK, MB_BLOCK), F32), pltpu.VMEM((hp, nb, MB_BLOCK), F32)],
        compiler_params=_params("parallel", "arbitrary"),
        name="moba_seq",
    )(rel_bias, qt, kb, vt, km)
    return ot.reshape(MB_WIDTH, t)


def _moba_paged_kernel(n_pages, pt_ref, q_ref, kn_ref, vn_ref, rbt_ref, *refs):
    k_pages = refs[:n_pages]
    v_pages = refs[n_pages:2 * n_pages]
    o_ref, bias_ref = refs[2 * n_pages:]
    past = n_pages * PAGE_SIZE
    n_blocks = past // MB_BLOCK
    pages_per_block = MB_BLOCK // PAGE_SIZE
    scale = MB_HEAD_DIM ** -0.5
    rb = lambda b: rbt_ref[:, :, b:b + 1]

    @pl.when(pl.program_id(0) == 0)
    def _():
        dist = PAGE_SIZE - lax.broadcasted_iota(jnp.int32, (MB_HEADS, 1, PAGE_SIZE), 2)
        bias_ref[...] = _bias_from_distance(dist, rb)

    di = lax.broadcasted_iota(jnp.int32, (MB_HEAD_DIM, MB_HEAD_DIM), 0)
    dj = lax.broadcasted_iota(jnp.int32, (MB_HEAD_DIM, MB_HEAD_DIM), 1)
    eye = jnp.where(di == dj, 1.0, 0.0)[None]
    q = q_ref[0]
    q_col = jnp.sum(eye * q, axis=-1, keepdims=True)
    raw = [jnp.sum(k_pages[pg][...] * q_col, axis=1, keepdims=True) for pg in range(n_pages)]
    scores = []
    for j in range(n_blocks):
        tot = raw[pages_per_block * j]
        for e in range(1, pages_per_block):
            tot = tot + raw[pages_per_block * j + e]
        scores.append(jnp.sum(tot, axis=-1, keepdims=True) * (1.0 / MB_BLOCK))
    far_bias = rb(NUM_BUCKETS - 1)
    rowterms = []
    for j in range(n_blocks):
        beaten = jnp.zeros((MB_HEADS, 1, 1), F32)
        for i in range(n_blocks):
            if i != j:
                ahead = scores[i] > scores[j] if i > j else scores[i] >= scores[j]
                beaten = beaten + jnp.where(ahead, 1.0, 0.0)
        rowterms.append(jnp.where(beaten < MB_TOPK, far_bias, NEG_INF))
    lg_own = jnp.sum(kn_ref[0] * q, axis=-1, keepdims=True) * scale + rb(0)
    logits = []
    for pg in range(n_pages):
        term = rowterms[pg // pages_per_block]
        if pg == n_pages - 1:
            term = bias_ref[...] + jnp.where(term > 0.5 * NEG_INF, 0.0, NEG_INF)
        logits.append(raw[pg] * scale + term)
    m_row = logits[0]
    for lg in logits[1:]:
        m_row = jnp.maximum(m_row, lg)
    m = jnp.maximum(jnp.max(m_row, axis=-1, keepdims=True), lg_own)
    p_own = jnp.exp(lg_own - m)
    l_row = jnp.zeros((MB_HEADS, 1, PAGE_SIZE), F32)
    acc = jnp.zeros((MB_HEADS, MB_HEAD_DIM, PAGE_SIZE), F32)
    for pg in range(n_pages):
        p = jnp.exp(logits[pg] - m)
        l_row = l_row + p
        acc = acc + v_pages[pg][...] * p
    out_col = jnp.sum(acc, axis=-1, keepdims=True)
    out = jnp.sum(eye * out_col, axis=1, keepdims=True) + p_own * vn_ref[0]
    l = jnp.sum(l_row, axis=-1, keepdims=True) + p_own
    o_ref[0] = (out / l).astype(BF16)


def _moba_paged(rel_bias, q, k_new, v_new, cache_k, cache_v, layer, page_table):
    b, n_pages = page_table.shape
    assert PAGE_SIZE >= _BUCKET_THR[NUM_BUCKETS - 1] and MB_BLOCK % PAGE_SIZE == 0
    head4 = pl.BlockSpec((1, MB_HEADS, 1, MB_HEAD_DIM), lambda i, pt: (i, 0, 0, 0))

    def page_spec(pg):
        return pl.BlockSpec((None, None, MB_HEADS, MB_HEAD_DIM, PAGE_SIZE),
                            lambda i, pt: (layer, pt[i, pg], 0, 0, 0))

    grid_spec = pltpu.PrefetchScalarGridSpec(
        num_scalar_prefetch=1,
        grid=(b,),
        in_specs=[head4, head4, head4, pl.BlockSpec((MB_HEADS, 1, NUM_BUCKETS), lambda i, pt: (0, 0, 0))]
        + [page_spec(pg) for pg in range(n_pages)] * 2,
        out_specs=head4,
        scratch_shapes=[pltpu.VMEM((MB_HEADS, 1, PAGE_SIZE), F32)],
    )
    h4 = lambda x: x.reshape(b, MB_HEADS, 1, MB_HEAD_DIM)
    kt = cache_k.transpose(0, 1, 3, 4, 2)
    vt = cache_v.transpose(0, 1, 3, 4, 2)
    out = pl.pallas_call(
        functools.partial(_moba_paged_kernel, n_pages),
        grid_spec=grid_spec,
        out_shape=jax.ShapeDtypeStruct((b, MB_HEADS, 1, MB_HEAD_DIM), BF16),
        compiler_params=_params("arbitrary"),
        name="moba_paged",
    )(page_table, h4(q), h4(k_new), h4(v_new), rel_bias.T.reshape(MB_HEADS, 1, NUM_BUCKETS),
      *([kt] * n_pages), *([vt] * n_pages))
    return out.reshape(b, MB_WIDTH)


def _merge_kernel(mb_feature_major, yrw_ref, ymb_ref, ycv_ref, g0_ref, g1_ref, g2_ref, wrw_ref, wmb_ref, wcv_ref,
                  o_ref):
    if mb_feature_major:
        y_mb = lax.dot_general(ymb_ref[...], wmb_ref[...], _TN, preferred_element_type=F32)
    else:
        y_mb = _dot(ymb_ref[...], wmb_ref[...])
    merged = (g0_ref[...] * _dot(yrw_ref[...], wrw_ref[...]) + g1_ref[...] * y_mb
              + g2_ref[...] * _dot(ycv_ref[...], wcv_ref[...]))
    o_ref[...] = merged.astype(BF16)


def _merge(y_rw, y_mb, y_cv, proj, w_rw, w_mb, w_cv, tm, mb_feature_major):
    m = proj.shape[0]
    rows = lambda n: pl.BlockSpec((tm, n), lambda i: (i, 0))
    mb_spec = pl.BlockSpec((MB_WIDTH, tm), lambda i: (0, i)) if mb_feature_major else rows(MB_WIDTH)
    gate = lambda br: pl.BlockSpec((tm, D_MODEL), lambda i: (i, COL_GATE + br))
    full = lambda n: pl.BlockSpec((n, D_MODEL), lambda i: (0, 0))
    return pl.pallas_call(
        functools.partial(_merge_kernel, mb_feature_major),
        grid=(m // tm,),
        in_specs=[rows(RW_WIDTH), mb_spec, rows(CV_WIDTH), gate(0), gate(1), gate(2),
                  full(RW_WIDTH), full(MB_WIDTH), full(CV_WIDTH)],
        out_specs=rows(D_MODEL),
        out_shape=jax.ShapeDtypeStruct((m, D_MODEL), BF16),
        compiler_params=_params("parallel"),
        name="branch_merge",
    )(y_rw, y_mb, y_cv, proj, proj, proj, w_rw, w_mb, w_cv)


def _out_proj_kernel(h_ref, x_ref, w_ref, o_ref):
    o_ref[...] = h_ref[...] + _dot(x_ref[...], w_ref[...])


def _out_proj(h, merged, w_out, tm):
    m = h.shape[0]
    rows = pl.BlockSpec((tm, D_MODEL), lambda i: (i, 0))
    return pl.pallas_call(
        _out_proj_kernel,
        grid=(m // tm,),
        in_specs=[rows, rows, pl.BlockSpec((D_MODEL, D_MODEL), lambda i: (0, 0))],
        out_specs=rows,
        out_shape=jax.ShapeDtypeStruct((m, D_MODEL), F32),
        compiler_params=_params("parallel"),
        name="out_proj",
    )(h, merged, w_out)


def _ffn_kernel(h_ref, g_ref, wg_ref, wu_ref, wd_ref, o_ref, xn_ref):
    @pl.when(pl.program_id(1) == 0)
    def _():
        x = h_ref[...]
        ms = jnp.mean(x * x, axis=-1, keepdims=True)
        xn_ref[...] = (x * lax.rsqrt(ms + NORM_EPS) * g_ref[...]).astype(BF16)
        o_ref[...] = x

    xn = xn_ref[...]
    mid = _silu(_dot(xn, wg_ref[...])) * _dot(xn, wu_ref[...])
    o_ref[...] += _dot(mid.astype(BF16), wd_ref[...])


def _ffn(h, g, wg, wu, wd, tm, tf):
    m = h.shape[0]
    d_ff = wg.shape[1]
    rows = pl.BlockSpec((tm, D_MODEL), lambda i, f: (i, 0))
    return pl.pallas_call(
        _ffn_kernel,
        grid=(m // tm, d_ff // tf),
        in_specs=[rows, pl.BlockSpec((1, D_MODEL), lambda i, f: (0, 0)),
                  pl.BlockSpec((D_MODEL, tf), lambda i, f: (0, f)),
                  pl.BlockSpec((D_MODEL, tf), lambda i, f: (0, f)),
                  pl.BlockSpec((tf, D_MODEL), lambda i, f: (f, 0))],
        out_specs=rows,
        out_shape=jax.ShapeDtypeStruct((m, D_MODEL), F32),
        scratch_shapes=[pltpu.VMEM((tm, D_MODEL), BF16)],
        compiler_params=_params("parallel", "arbitrary"),
        name="ffn",
    )(h, g, wg, wu, wd)


def _dot_f32_nt(a, b):
    nt = lambda x, y: lax.dot_general(x, y, _NT, preferred_element_type=F32)
    a_hi, a_mid, a_lo = _split3(a)
    b_hi, b_mid, b_lo = _split3(b)
    return (nt(a_hi, b_hi) + (nt(a_hi, b_mid) + nt(a_mid, b_hi))
            + (nt(a_mid, b_mid) + nt(a_hi, b_lo) + nt(a_lo, b_hi)))


def _router_kernel(h_ref, g_ref, wt_ref, b_ref, before_ref, xn_ref, gate_ref, pos_ref, cnt_ref):
    x = h_ref[...]
    ms = jnp.mean(x * x, axis=-1, keepdims=True)
    xn = x * lax.rsqrt(ms + NORM_EPS) * g_ref[...]
    xn_ref[...] = xn.astype(BF16)
    logits = _dot_f32_nt(wt_ref[...], xn) + b_ref[...]
    idx = lax.broadcasted_iota(jnp.int32, logits.shape, 0).astype(F32)
    m1 = jnp.max(logits, axis=0, keepdims=True)
    i1 = jnp.min(jnp.where(logits == m1, idx, float(N_EXPERTS)), axis=0, keepdims=True)
    rest = jnp.where(idx == i1, -jnp.inf, logits)
    m2 = jnp.max(rest, axis=0, keepdims=True)
    i2 = jnp.min(jnp.where(rest == m2, idx, float(N_EXPERTS)), axis=0, keepdims=True)
    e2 = jnp.exp(m2 - m1)
    gates = jnp.where(idx == i1, 1.0 / (1.0 + e2), 0.0) + jnp.where(idx == i2, e2 / (1.0 + e2), 0.0)
    member = jnp.where(gates > 0.0, 1.0, 0.0)
    gate_ref[0] = gates
    pos_ref[0] = _dot(member.astype(BF16), before_ref[...])
    cnt_ref[0] = jnp.sum(member, axis=-1, keepdims=True).astype(jnp.int32)


def _router(h, g, w_router, b_router, tm):
    m = h.shape[0]
    n_tiles = m // tm
    before = jnp.asarray(np.triu(np.ones((tm, tm), np.float32), 1), BF16)
    tile3 = pl.BlockSpec((1, N_EXPERTS, tm), lambda i: (i, 0, 0))
    per_tile = jax.ShapeDtypeStruct((n_tiles, N_EXPERTS, tm), F32)
    return pl.pallas_call(
        _router_kernel,
        grid=(n_tiles,),
        in_specs=[pl.BlockSpec((tm, D_MODEL), lambda i: (i, 0)),
                  pl.BlockSpec((1, D_MODEL), lambda i: (0, 0)),
                  pl.BlockSpec((N_EXPERTS, D_MODEL), lambda i: (0, 0)),
                  pl.BlockSpec((N_EXPERTS, 1), lambda i: (0, 0)),
                  pl.BlockSpec((tm, tm), lambda i: (0, 0))],
        out_specs=[pl.BlockSpec((tm, D_MODEL), lambda i: (i, 0)), tile3, tile3,
                   pl.BlockSpec((1, N_EXPERTS, 1), lambda i: (i, 0, 0))],
        out_shape=[jax.ShapeDtypeStruct((m, D_MODEL), BF16), per_tile, per_tile,
                   jax.ShapeDtypeStruct((n_tiles, N_EXPERTS, 1), jnp.int32)],
        compiler_params=_params("parallel"),
        name="moe_router",
    )(h, g, w_router.T, b_router.reshape(N_EXPERTS, 1), before)


def _moe_kernel(sub, cnt_ref, xn_ref, gate_ref, pos_ref, wg_ref, wu_ref, wd_ref, o_ref, xg_ref, gg_ref, acc_ref):
    i = pl.program_id(0)
    e = pl.program_id(1)
    f = pl.program_id(2)
    tm = xn_ref.shape[0]
    n_sub = (cnt_ref[i * N_EXPERTS + e] + (sub - 1)) // sub
    pos_row = pos_ref[0, pl.ds(e, 1), :]
    gate_row = gate_ref[0, pl.ds(e, 1), :]
    slot = lax.broadcasted_iota(jnp.int32, (sub, tm), 0)

    def one_hot(s):
        wanted = (slot + s * sub).astype(F32)
        return jnp.where(gate_row > 0.0, jnp.where(pos_row == wanted, 1.0, 0.0), 0.0)

    def rows_of(s):
        return pl.ds(pl.multiple_of(s * sub, sub), sub)

    @pl.when((e == 0) & (f == 0))
    def _():
        o_ref[...] = jnp.zeros_like(o_ref)

    @pl.when(f == 0)
    def _():
        def pack(s, carry):
            pick = one_hot(s)
            xg_ref[rows_of(s), :] = _dot(pick.astype(BF16), xn_ref[...]).astype(BF16)
            gg_ref[rows_of(s), :] = jnp.sum(pick * gate_row, axis=-1, keepdims=True)
            acc_ref[rows_of(s), :] = jnp.zeros((sub, D_MODEL), F32)
            return carry

        lax.fori_loop(0, n_sub, pack, 0)

    def expert(s, carry):
        x = xg_ref[rows_of(s), :]
        mid = _silu(_dot(x, wg_ref[0])) * _dot(x, wu_ref[0]) * gg_ref[rows_of(s), :]
        acc_ref[rows_of(s), :] += _dot(mid.astype(BF16), wd_ref[0])
        return carry

    lax.fori_loop(0, n_sub, expert, 0)

    @pl.when(f == pl.num_programs(2) - 1)
    def _():
        def unpack(s, carry):
            pick = one_hot(s).astype(BF16)
            o_ref[...] += lax.dot_general(pick, acc_ref[rows_of(s), :].astype(BF16), _TN, preferred_element_type=F32)
            return carry

        lax.fori_loop(0, n_sub, unpack, 0)


def _moe(xn, gates, pos, counts, wg, wu, wd, tm, tf):
    m = xn.shape[0]
    n_e, _, d_ff = wg.shape
    sub = min(tm, -(-(tm * TOP_K * 5 // (N_EXPERTS * 4)) // 16) * 16)
    packed = -(-tm // sub) * sub
    tile3 = pl.BlockSpec((1, N_EXPERTS, tm), lambda i, e, f, cnt: (i, 0, 0))
    grid_spec = pltpu.PrefetchScalarGridSpec(
        num_scalar_prefetch=1,
        grid=(m // tm, n_e, d_ff // tf),
        in_specs=[pl.BlockSpec((tm, D_MODEL), lambda i, e, f, cnt: (i, 0)), tile3, tile3,
                  pl.BlockSpec((1, D_MODEL, tf), lambda i, e, f, cnt: (e, 0, f)),
                  pl.BlockSpec((1, D_MODEL, tf), lambda i, e, f, cnt: (e, 0, f)),
                  pl.BlockSpec((1, tf, D_MODEL), lambda i, e, f, cnt: (e, f, 0))],
        out_specs=pl.BlockSpec((tm, D_MODEL), lambda i, e, f, cnt: (i, 0)),
        scratch_shapes=[pltpu.VMEM((packed, D_MODEL), BF16), pltpu.VMEM((packed, 1), F32),
                        pltpu.VMEM((packed, D_MODEL), F32)],
    )
    return pl.pallas_call(
        functools.partial(_moe_kernel, sub),
        grid_spec=grid_spec,
        out_shape=jax.ShapeDtypeStruct((m, D_MODEL), F32),
        compiler_params=_params("parallel", "arbitrary", "arbitrary"),
        name="moe_experts",
    )(counts.reshape(-1), xn, gates, pos, wg, wu, wd)


def _final_norm_kernel(n_in, *refs):
    g_ref, o_ref = refs[n_in], refs[n_in + 1]
    x = refs[0][...]
    for r in refs[1:n_in]:
        x = x + r[...]
    ms = jnp.mean(x * x, axis=-1, keepdims=True)
    o_ref[...] = x * lax.rsqrt(ms + NORM_EPS) * g_ref[...]


def _final_norm(parts, g, tm):
    m = parts[0].shape[0]
    rows = pl.BlockSpec((tm, D_MODEL), lambda i: (i, 0))
    return pl.pallas_call(
        functools.partial(_final_norm_kernel, len(parts)),
        grid=(m // tm,),
        in_specs=[rows] * len(parts) + [pl.BlockSpec((1, D_MODEL), lambda i: (0, 0))],
        out_specs=rows,
        out_shape=jax.ShapeDtypeStruct((m, D_MODEL), F32),
        compiler_params=_params("parallel"),
        name="final_norm",
    )(*parts, g)


def _add_kernel(a_ref, b_ref, o_ref):
    o_ref[...] = a_ref[...] + b_ref[...]


def _add(a, b, tm):
    rows = pl.BlockSpec((tm, D_MODEL), lambda i: (i, 0))
    return pl.pallas_call(
        _add_kernel,
        grid=(a.shape[0] // tm,),
        in_specs=[rows, rows],
        out_specs=rows,
        out_shape=jax.ShapeDtypeStruct(a.shape, F32),
        compiler_params=_params("parallel"),
        name="residual_add",
    )(a, b)


def _pad_rw_cols(x, axis=-1):
    x = jnp.moveaxis(x, axis, 0)
    pad = jnp.zeros((RW_PAD - RW_COLS,) + x.shape[1:], x.dtype)
    return jnp.moveaxis(jnp.concatenate([x[:RW_COLS], pad, x[RW_COLS:]], axis=0), 0, axis)


def _lora_slab(w, row0):
    slab = jnp.zeros((LORA_PAD, RW_WIDTH), F32)
    return lax.dynamic_update_slice(slab, w, (row0, 0)).astype(BF16)


def _layer_params(l, P):
    row = lambda x: x.reshape(1, -1)
    lp = dict(
        norm_mix=row(P['norm_mix'][l]), norm_ffn=row(P['norm_ffn'][l]),
        w_in_t=_pad_rw_cols(P['w_in'][l].T, axis=0).astype(BF16),
        rw_vecs=[row(_pad_rw_cols(P['rw_mu'][l])), row(P['rw_w0'][l]), row(P['rw_a0'][l]),
                 row(P['rw_kk'][l]), row(P['rw_ka'][l]), row(P['rw_rk'][l])],
        rw_mats=[_lora_slab(P['rw_w2'][l], 0), _lora_slab(P['rw_a2'][l], DECAY_LORA),
                 _lora_slab(P['rw_g2'][l], DECAY_LORA + AAA_LORA)],
        rw_ln_w=row(P['rw_ln_w'][l]), rw_ln_b=row(P['rw_ln_b'][l]),
        cv_w=P['cv_w'][l], cv_b=row(P['cv_b'][l]), cv_ln_w=row(P['cv_ln_w'][l]), cv_ln_b=row(P['cv_ln_b'][l]),
        w_br_rw=P['w_br_rw'][l].astype(BF16), w_br_mb=P['w_br_mb'][l].astype(BF16),
        w_br_cv=P['w_br_cv'][l].astype(BF16), w_out=P['w_out'][l].astype(BF16),
    )
    i = l // 2
    if l % 2 == 0:
        lp.update(wg=P['ffn_wg'][i].astype(BF16), wu=P['ffn_wu'][i].astype(BF16), wd=P['ffn_wd'][i].astype(BF16))
    else:
        lp.update(wg=P['moe_wg'][i].astype(BF16), wu=P['moe_wu'][i].astype(BF16), wd=P['moe_wd'][i].astype(BF16),
                  router=P['moe_router'][i], router_b=P['moe_router_b'][i])
    return lp


def _channel_mix(h, lp, tm_dense, tm_moe):
    if 'router' in lp:
        xn, gates, pos, counts = _router(h, lp['norm_ffn'], lp['router'], lp['router_b'], tm_moe)
        return [h, _moe(xn, gates, pos, counts, lp['wg'], lp['wu'], lp['wd'], tm_moe, 256)]
    return [_ffn(h, lp['norm_ffn'], lp['wg'], lp['wu'], lp['wd'], tm_dense, 512)]


def _as_one(parts, tm):
    return parts[0] if len(parts) == 1 else _add(parts[0], parts[1], tm)


def _run_prompt(x, layers, norm_final, rel_bias):
    t = x.shape[0]
    parts = [x]
    outs = dict(k=[], v=[], wkv=[], shift=[], conv=[])
    for lp in layers:
        h = _as_one(parts, 512)
        proj = _norm_proj(h, lp['norm_mix'], lp['w_in_t'], 1024)
        *ops, g, bonus = _rwkv_prep(proj, None, lp['rw_vecs'], lp['rw_mats'], 128)
        y_rw, s_final = _rwkv_chunked(ops, bonus, g, lp['rw_ln_w'], lp['rw_ln_b'])
        q_t, k_t, v_t, k_tb, v_tb, km_tiles = _qkv_t(h, lp['norm_mix'], lp['w_in_t'], 1024)
        y_mb_t = _moba_seq(rel_bias, q_t, k_tb, v_tb, km_tiles)
        y_cv, tail = _conv_seq(proj, lp['cv_w'], lp['cv_b'], lp['cv_ln_w'], lp['cv_ln_b'], 512)
        merged = _merge(y_rw, y_mb_t, y_cv, proj, lp['w_br_rw'], lp['w_br_mb'], lp['w_br_cv'], 256, True)
        h = _out_proj(h, merged, lp['w_out'], 512)
        parts = _channel_mix(h, lp, 512, 1024)
        tokens_major = lambda x_t: x_t.reshape(MB_HEADS, MB_HEAD_DIM, t).transpose(2, 0, 1)[None]
        outs['k'].append(tokens_major(k_t))
        outs['v'].append(tokens_major(v_t))
        outs['wkv'].append(s_final[None])
        outs['shift'].append(proj[t - 1:t, :RW_COLS])
        outs['conv'].append(tail[None, CV_HALO - (CV_K - 1):, :])
    y = _final_norm(parts, norm_final, 512)
    return (y[None],) + tuple(jnp.stack(outs[n]) for n in ('k', 'v', 'wkv', 'shift', 'conv'))


def _run_sample(x, layers, norm_final, rel_bias, cache_k, cache_v, state_wkv, state_shift, state_conv, page_table):
    b = x.shape[0]
    parts = [x]
    new_wkv = None
    outs = dict(k=[], v=[], shift=[], conv=[])
    for l, lp in enumerate(layers):
        h = _as_one(parts, b)
        proj = _norm_proj(h, lp['norm_mix'], lp['w_in_t'], b)
        *ops, g, bonus = _rwkv_prep(proj, _pad_rw_cols(state_shift[l]), lp['rw_vecs'], lp['rw_mats'], b)
        y, new_wkv = _rwkv_single_step(ops, state_wkv, l, new_wkv, 8)
        y_rw = _rwkv_post(y, bonus, g, lp['rw_ln_w'], lp['rw_ln_b'], b)
        q = proj[:, RW_PAD:RW_PAD + MB_WIDTH]
        k_new = proj[:, RW_PAD + MB_WIDTH:RW_PAD + 2 * MB_WIDTH]
        v_new = proj[:, RW_PAD + 2 * MB_WIDTH:RW_PAD + 3 * MB_WIDTH]
        y_mb = _moba_paged(rel_bias, q, k_new, v_new, cache_k, cache_v, l, page_table)
        y_cv, conv_new = _conv_step(proj, state_conv[l], lp['cv_w'], lp['cv_b'], lp['cv_ln_w'], lp['cv_ln_b'], 8)
        merged = _merge(y_rw, y_mb, y_cv, proj, lp['w_br_rw'], lp['w_br_mb'], lp['w_br_cv'], b, False)
        h = _out_proj(h, merged, lp['w_out'], b)
        parts = _channel_mix(h, lp, b, b)
        outs['k'].append(k_new.reshape(b, 1, MB_HEADS, MB_HEAD_DIM))
        outs['v'].append(v_new.reshape(b, 1, MB_HEADS, MB_HEAD_DIM))
        outs['shift'].append(proj[:, :RW_COLS])
        outs['conv'].append(conv_new)
    y = _final_norm(parts, norm_final, b)
    stacked = {n: jnp.stack(v) for n, v in outs.items()}
    return (y[:, None, :], stacked['k'], stacked['v'], new_wkv, stacked['shift'], stacked['conv'])


def kernel(x_prompt, x_sample, cache_k, cache_v, state_wkv, state_shift, state_conv, page_table, norm_mix, norm_ffn, norm_final, w_in, rw_mu, rw_w0, rw_w2, rw_a0, rw_a2, rw_g2, rw_kk, rw_ka, rw_rk, rw_ln_w, rw_ln_b, rel_bias, cv_w, cv_b, cv_ln_w, cv_ln_b, w_br_rw, w_br_mb, w_br_cv, w_out, ffn_wg, ffn_wu, ffn_wd, moe_router, moe_router_b, moe_wg, moe_wu, moe_wd):
    P = dict(norm_mix=norm_mix, norm_ffn=norm_ffn, w_in=w_in,
             rw_mu=rw_mu, rw_w0=rw_w0, rw_w2=rw_w2, rw_a0=rw_a0, rw_a2=rw_a2, rw_g2=rw_g2,
             rw_kk=rw_kk, rw_ka=rw_ka, rw_rk=rw_rk, rw_ln_w=rw_ln_w, rw_ln_b=rw_ln_b,
             cv_w=cv_w, cv_b=cv_b, cv_ln_w=cv_ln_w, cv_ln_b=cv_ln_b,
             w_br_rw=w_br_rw, w_br_mb=w_br_mb, w_br_cv=w_br_cv, w_out=w_out,
             ffn_wg=ffn_wg, ffn_wu=ffn_wu, ffn_wd=ffn_wd,
             moe_router=moe_router, moe_router_b=moe_router_b, moe_wg=moe_wg, moe_wu=moe_wu, moe_wd=moe_wd)
    depth = w_in.shape[0]
    layers = [_layer_params(l, P) for l in range(depth)]
    g_final = norm_final.reshape(1, -1)
    assert x_prompt.shape[0] == 1 and x_sample.shape[1] == 1
    prompt = _run_prompt(x_prompt[0], layers, g_final, rel_bias)
    sample = _run_sample(x_sample[:, 0, :], layers, g_final, rel_bias, cache_k, cache_v,
                         state_wkv, state_shift, state_conv, page_table)
    y_p, k_p, v_p, wkv_p, shift_p, conv_p = prompt
    y_s, k_s, v_s, wkv_s, shift_s, conv_s = sample
    return (y_p, y_s, k_p, v_p, wkv_p, shift_p, conv_p,
            k_s, v_s, wkv_s, shift_s, conv_s)
```

```python
import functools
import math

import numpy as np
import jax
import jax.numpy as jnp
from jax import lax
from jax.experimental import pallas as pl
from jax.experimental.pallas import tpu as pltpu

F32 = jnp.float32
BF16 = jnp.bfloat16

D_MODEL = 2048
RW_HEAD_DIM = 64
RW_WIDTH = 1024
RW_HEADS = 16
DECAY_LORA = 64
AAA_LORA = 64
GATE_LORA = 160
RW_COLS = 3 * RW_WIDTH + DECAY_LORA + AAA_LORA + GATE_LORA
RW_PAD = 3584
LORA_PAD = RW_PAD - 3 * RW_WIDTH
GN_EPS = 64e-5
MB_HEAD_DIM = 64
MB_WIDTH = 512
MB_HEADS = 8
MB_BLOCK = 256
MB_TOPK = 3
NUM_BUCKETS = 32
MAX_DISTANCE = 128
NEG_INF = -1e30
CV_WIDTH = 512
CV_K = 31
LN_EPS = 1e-5
N_BRANCH = 3
N_EXPERTS = 8
TOP_K = 2
NORM_EPS = 1e-6
PAGE_SIZE = 128

PROJ_TILE = 1024
N_PROJ = RW_PAD + 3 * MB_WIDTH + 2 * CV_WIDTH + N_BRANCH * D_MODEL
COL_CV = (RW_PAD + 3 * MB_WIDTH) // (2 * CV_WIDTH)
COL_GATE = (RW_PAD + 3 * MB_WIDTH + 2 * CV_WIDTH) // D_MODEL
GATE_TILE0 = (RW_PAD + 3 * MB_WIDTH + 2 * CV_WIDTH) // PROJ_TILE

VMEM_LIMIT_BYTES = 56 * 1024 * 1024


def _params(*semantics):
    return pltpu.CompilerParams(dimension_semantics=semantics, vmem_limit_bytes=VMEM_LIMIT_BYTES)


def _sigmoid(x):
    return 1.0 / (1.0 + jnp.exp(-x))


def _silu(x):
    return x * _sigmoid(x)


def _split3(x):
    hi = x.astype(BF16)
    r1 = x - hi.astype(F32)
    mid = r1.astype(BF16)
    lo = (r1 - mid.astype(F32)).astype(BF16)
    return hi, mid, lo


_NT = (((1,), (1,)), ((), ()))
_TN = (((0,), (0,)), ((), ()))


def _dot(a, b):
    return jnp.dot(a, b, preferred_element_type=F32)


def _dot_exact_rhs(x, sel):
    hi, mid, lo = _split3(x)
    return _dot(hi, sel) + _dot(mid, sel) + _dot(lo, sel)


def _dot_f32(a, b):
    a_hi, a_mid, a_lo = _split3(a)
    b_hi, b_mid, b_lo = _split3(b)
    return (_dot(a_hi, b_hi) + (_dot(a_hi, b_mid) + _dot(a_mid, b_hi))
            + (_dot(a_mid, b_mid) + _dot(a_hi, b_lo) + _dot(a_lo, b_hi)))


def _rms_norm_bf16(x, g):
    ms = jnp.mean(x * x, axis=-1, keepdims=True)
    return (x * lax.rsqrt(ms + NORM_EPS) * g).astype(BF16)


def _norm_proj_kernel(x_ref, g_ref, wt_ref, o_ref, xn_ref):
    j = pl.program_id(1)

    @pl.when(j == 0)
    def _():
        xn_ref[...] = _rms_norm_bf16(x_ref[...], g_ref[...])

    acc = lax.dot_general(xn_ref[...], wt_ref[...], _NT, preferred_element_type=F32)

    @pl.when(j < GATE_TILE0)
    def _():
        o_ref[...] = acc

    @pl.when(j >= GATE_TILE0)
    def _():
        o_ref[...] = _sigmoid(acc)


def _norm_proj(x, g, wt, tm):
    m = x.shape[0]
    return pl.pallas_call(
        _norm_proj_kernel,
        grid=(m // tm, N_PROJ // PROJ_TILE),
        in_specs=[pl.BlockSpec((tm, D_MODEL), lambda i, j: (i, 0)),
                  pl.BlockSpec((1, D_MODEL), lambda i, j: (0, 0)),
                  pl.BlockSpec((PROJ_TILE, D_MODEL), lambda i, j: (j, 0))],
        out_specs=pl.BlockSpec((tm, PROJ_TILE), lambda i, j: (i, j)),
        out_shape=jax.ShapeDtypeStruct((m, N_PROJ), F32),
        scratch_shapes=[pltpu.VMEM((tm, D_MODEL), BF16)],
        compiler_params=_params("parallel", "arbitrary"),
        name="norm_proj",
    )(x, g, wt)


def _qkv_t_kernel(x_ref, g_ref, wq_ref, wk_ref, wv_ref, q_ref, k_ref, v_ref, kb_ref, vb_ref, km_ref):
    xn = _rms_norm_bf16(x_ref[...], g_ref[...])
    nt = lambda w_ref: lax.dot_general(w_ref[...], xn, _NT, preferred_element_type=F32)
    q_ref[...] = nt(wq_ref)
    k = nt(wk_ref)
    v = nt(wv_ref)
    k_ref[...] = k
    v_ref[...] = v
    kb_ref[...] = k.astype(BF16)
    vb_ref[...] = v.astype(BF16)
    n_blk = k.shape[1] // MB_BLOCK
    sums = [jnp.sum(k[:, b * MB_BLOCK:(b + 1) * MB_BLOCK], axis=-1, keepdims=True) for b in range(n_blk)]
    km_ref[0] = jnp.concatenate(sums, axis=-1) * (1.0 / MB_BLOCK)


def _qkv_t(x, g, wt, tm):
    t = x.shape[0]
    q_row = RW_PAD // MB_WIDTH
    w_spec = lambda r: pl.BlockSpec((MB_WIDTH, D_MODEL), lambda i: (q_row + r, 0))
    col = pl.BlockSpec((MB_WIDTH, tm), lambda i: (0, i))
    f32_t = jax.ShapeDtypeStruct((MB_WIDTH, t), F32)
    bf16_t = jax.ShapeDtypeStruct((MB_WIDTH, t), BF16)
    return pl.pallas_call(
        _qkv_t_kernel,
        grid=(t // tm,),
        in_specs=[pl.BlockSpec((tm, D_MODEL), lambda i: (i, 0)), pl.BlockSpec((1, D_MODEL), lambda i: (0, 0)),
                  w_spec(0), w_spec(1), w_spec(2)],
        out_specs=[col] * 5 + [pl.BlockSpec((1, MB_WIDTH, tm // MB_BLOCK), lambda i: (i, 0, 0))],
        out_shape=[f32_t] * 3 + [bf16_t] * 2 + [jax.ShapeDtypeStruct((t // tm, MB_WIDTH, tm // MB_BLOCK), F32)],
        compiler_params=_params("parallel"),
        name="qkv_t",
    )(x, g, wt, wt, wt)


def _rwkv_prep_kernel(sequential, *refs):
    if sequential:
        (p_ref, mu_ref, w0_ref, a0_ref, kk_ref, ka_ref, rk_ref, w2_ref, a2_ref, g2_ref,
         r_o, w_o, k_o, v_o, kn_o, b_o, g_o, bonus_o, carry_ref) = refs
    else:
        (p_ref, prev_ref, mu_ref, w0_ref, a0_ref, kk_ref, ka_ref, rk_ref, w2_ref, a2_ref, g2_ref,
         r_o, w_o, k_o, v_o, kn_o, b_o, g_o, bonus_o) = refs
    p = p_ref[...]
    tm = p.shape[0]
    if sequential:
        @pl.when(pl.program_id(0) == 0)
        def _():
            carry_ref[...] = jnp.zeros_like(carry_ref)

        rows = lax.broadcasted_iota(jnp.int32, p.shape, 0)
        prev = jnp.where(rows == 0, carry_ref[...], pltpu.roll(p, 1, 0))
        carry_ref[...] = p[tm - 1:tm, :]
    else:
        prev = prev_ref[...]
    xs = p + (prev - p) * mu_ref[...]
    r = xs[:, 0:RW_WIDTH]
    k = xs[:, RW_WIDTH:2 * RW_WIDTH]
    v = xs[:, 2 * RW_WIDTH:3 * RW_WIDTH]
    lora = xs[:, 3 * RW_WIDTH:RW_PAD]
    wx = w0_ref[...] + _dot(jnp.tanh(lora).astype(BF16), w2_ref[...])
    w_log = jnp.minimum(wx, 0.0) - jnp.log(1.0 + jnp.exp(-jnp.abs(wx))) - 0.5
    log_decay = -jnp.exp(w_log)
    a = _sigmoid(a0_ref[...] + _dot(lora.astype(BF16), a2_ref[...]))
    g_o[...] = _dot(_sigmoid(lora).astype(BF16), g2_ref[...])
    kk = k * kk_ref[...]
    k_mod = k * (1.0 + (a - 1.0) * ka_ref[...])
    rkr = r * k_mod * rk_ref[...]
    kn_parts = []
    for h in range(RW_HEADS):
        sl = slice(h * RW_HEAD_DIM, (h + 1) * RW_HEAD_DIM)
        kk_h = kk[:, sl]
        nrm = jnp.sqrt(jnp.sum(kk_h * kk_h, axis=-1, keepdims=True))
        kn_parts.append(kk_h / jnp.maximum(nrm, 1e-12))
        bonus_o[:, sl] = jnp.sum(rkr[:, sl], axis=-1, keepdims=True) * v[:, sl]
    kn = jnp.concatenate(kn_parts, axis=-1)
    if sequential:
        ops = (r, log_decay, k_mod, v, kn, kn * a)
    else:
        ops = tuple(x.T for x in (r, jnp.exp(log_decay), k_mod, v, kn, kn * a))
    for o_ref, x in zip((r_o, w_o, k_o, v_o, kn_o, b_o), ops):
        o_ref[...] = x


def _rwkv_prep(proj, prev, vecs, mats, tm):
    m = proj.shape[0]
    sequential = prev is None
    row = lambda n: pl.BlockSpec((1, n), lambda i: (0, 0))
    in_specs = [pl.BlockSpec((tm, RW_PAD), lambda i: (i, 0))]
    args = [proj]
    if not sequential:
        in_specs.append(pl.BlockSpec((tm, RW_PAD), lambda i: (i, 0)))
        args.append(prev)
    in_specs += [row(RW_PAD)] + [row(RW_WIDTH)] * 5
    in_specs += [pl.BlockSpec((LORA_PAD, RW_WIDTH), lambda i: (0, 0))] * 3
    flat = jax.ShapeDtypeStruct((m, RW_WIDTH), F32)
    flat_spec = pl.BlockSpec((tm, RW_WIDTH), lambda i: (i, 0))
    if sequential:
        op_shape, op_spec = flat, flat_spec
    else:
        op_shape = jax.ShapeDtypeStruct((RW_WIDTH, m), F32)
        op_spec = pl.BlockSpec((RW_WIDTH, tm), lambda i: (0, i))
    return pl.pallas_call(
        functools.partial(_rwkv_prep_kernel, sequential),
        grid=(m // tm,),
        in_specs=in_specs,
        out_specs=[op_spec] * 6 + [flat_spec] * 2,
        out_shape=[op_shape] * 6 + [flat] * 2,
        scratch_shapes=[pltpu.VMEM((1, RW_PAD), F32)] if sequential else [],
        compiler_params=_params("arbitrary"),
        name="rwkv_prep",
    )(*args, *vecs, *mats)


RW_CHUNK = 64


def _group_norm(y):
    mean = jnp.mean(y, axis=-1, keepdims=True)
    var = jnp.mean(jnp.square(y - mean), axis=-1, keepdims=True)
    return (y - mean) * lax.rsqrt(var + GN_EPS)


def _rwkv_chunk_kernel(r_ref, lw_ref, k_ref, v_ref, kn_ref, b_ref, bonus_ref, g_ref, lnw_ref, lnb_ref,
                       y_ref, s_out_ref, h_ref):
    c = RW_CHUNK
    n = RW_HEAD_DIM

    @pl.when(pl.program_id(0) == 0)
    def _():
        h_ref[...] = jnp.zeros_like(h_ref)

    row = lax.broadcasted_iota(jnp.int32, (c, c), 0)
    col = lax.broadcasted_iota(jnp.int32, (c, c), 1)
    eye = jnp.where(row == col, 1.0, 0.0)
    lower_ones = jnp.where(col <= row, 1.0, 0.0).astype(BF16)
    lw = lw_ref[...]
    lw_hi, lw_mid, lw_lo = _split3(lw)
    cum = _dot(lower_ones, lw_hi) + _dot(lower_ones, lw_mid) + _dot(lower_ones, lw_lo)
    total = cum[c - 1:c, :]
    k = k_ref[...]
    b = b_ref[...]
    v = v_ref[...]
    g_inv = jnp.exp(-cum)
    g_rest = jnp.exp(total - cum)
    decay_total = jnp.exp(total)
    qr = jnp.concatenate([kn_ref[...] * jnp.exp(cum - lw), r_ref[...] * jnp.exp(cum)], axis=0).astype(BF16)
    kdbd = jnp.concatenate([k * g_inv, b * g_inv], axis=0).astype(BF16)
    zk = jnp.concatenate([k * g_rest, b * g_rest], axis=0).astype(BF16)
    vb = v.astype(BF16)
    row2 = lax.broadcasted_iota(jnp.int32, (2 * c, 2 * c), 0)
    col2 = lax.broadcasted_iota(jnp.int32, (2 * c, 2 * c), 1)
    t_idx = jnp.where(row2 >= c, row2 - c, row2)
    s_idx = jnp.where(col2 >= c, col2 - c, col2)
    mask = s_idx < t_idx + jnp.where(row2 >= c, 1, 0)
    heads = range(RW_HEADS)
    sls = [slice(h * n, (h + 1) * n) for h in heads]
    gms = [jnp.where(mask, lax.dot_general(qr[:, sl], kdbd[:, sl], _NT, preferred_element_type=F32), 0.0)
           for sl in sls]
    h_old = [h_ref[h] for h in heads]
    qhs = [_dot(qr[:, sl], ho.astype(BF16)) for sl, ho in zip(sls, h_old)]
    akkv = [_dot(gm[0:c, 0:c].astype(BF16), vb[:, sl]) for gm, sl in zip(gms, sls)]
    powers = [gm[0:c, c:2 * c].astype(BF16) for gm in gms]
    t_invs = [eye - gm[0:c, c:2 * c] for gm in gms]
    for _ in range(int(math.log2(c)) - 1):
        powers = [_dot(p, p).astype(BF16) for p in powers]
        t_invs = [t + _dot(p, t.astype(BF16)) for p, t in zip(powers, t_invs)]
    us = [_dot(t.astype(BF16), (qh[0:c] + av).astype(BF16)) for t, qh, av in zip(t_invs, qhs, akkv)]
    vus = [jnp.concatenate([v[:, sl], -u], axis=0).astype(BF16) for sl, u in zip(sls, us)]
    outs = [_group_norm(qh[c:2 * c] + _dot(gm[c:2 * c, :].astype(BF16), vu)) for qh, gm, vu in zip(qhs, gms, vus)]
    for h in heads:
        decay_col = jnp.sum(eye * decay_total[:, sls[h]], axis=-1, keepdims=True)
        h_ref[h] = decay_col * h_old[h] + lax.dot_general(zk[:, sls[h]], vus[h], _TN, preferred_element_type=F32)
    yn = jnp.concatenate(outs, axis=-1)
    y_ref[...] = ((yn * lnw_ref[...] + lnb_ref[...] + bonus_ref[...]) * g_ref[...]).astype(BF16)

    @pl.when(pl.program_id(0) == pl.num_programs(0) - 1)
    def _():
        for h in range(RW_HEADS):
            s_out_ref[h] = h_ref[h].T


def _rwkv_chunked(ops, bonus, g, ln_w, ln_b):
    m = bonus.shape[0]
    spec = pl.BlockSpec((RW_CHUNK, RW_WIDTH), lambda i: (i, 0))
    row = pl.BlockSpec((1, RW_WIDTH), lambda i: (0, 0))
    state_shape = (RW_HEADS, RW_HEAD_DIM, RW_HEAD_DIM)
    return pl.pallas_call(
        _rwkv_chunk_kernel,
        grid=(m // RW_CHUNK,),
        in_specs=[spec] * 8 + [row, row],
        out_specs=[spec, pl.BlockSpec(state_shape, lambda i: (0, 0, 0))],
        out_shape=[jax.ShapeDtypeStruct((m, RW_WIDTH), BF16), jax.ShapeDtypeStruct(state_shape, F32)],
        scratch_shapes=[pltpu.VMEM(state_shape, F32)],
        compiler_params=_params("arbitrary"),
        name="rwkv_chunked",
    )(*ops, bonus, g, ln_w, ln_b)


RW_STEP_HEADS = 2


def _rwkv_single_step_kernel(r_ref, w_ref, k_ref, v_ref, kn_ref, b_ref, s0_ref, *rest):
    y_ref, s_out_ref = rest[-2:]
    n = RW_HEAD_DIM
    for hh in range(s0_ref.shape[0]):
        rows = slice(hh * n, (hh + 1) * n)
        r, w, k, kn, b = (ref[rows, :] for ref in (r_ref, w_ref, k_ref, kn_ref, b_ref))

        def body(vi, carry):
            s = s0_ref[hh, vi]
            sa = jnp.sum(s * kn, axis=0, keepdims=True)
            s = s * w - sa * b + v_ref[pl.ds(hh * n + vi, 1), :] * k
            s_out_ref[hh, vi] = s
            y_ref[pl.ds(hh * n + vi, 1), :] = jnp.sum(s * r, axis=0, keepdims=True)
            return carry

        lax.fori_loop(0, n, body, 0)


def _rwkv_single_step(ops, states_t, layer, new_states_t):
    b = ops[0].shape[1]
    hs = RW_STEP_HEADS
    spec = pl.BlockSpec((hs * RW_HEAD_DIM, b), lambda i: (i, 0))
    st_spec = pl.BlockSpec((None, hs, RW_HEAD_DIM, RW_HEAD_DIM, b), lambda i: (layer, i, 0, 0, 0))
    carried = [] if new_states_t is None else [new_states_t]
    return pl.pallas_call(
        _rwkv_single_step_kernel,
        grid=(RW_HEADS // hs,),
        in_specs=[spec] * 6 + [st_spec] + [pl.BlockSpec(memory_space=pl.ANY)] * len(carried),
        out_specs=[spec, st_spec],
        out_shape=[jax.ShapeDtypeStruct((RW_WIDTH, b), F32), jax.ShapeDtypeStruct(states_t.shape, F32)],
        input_output_aliases={7: 1} if carried else {},
        compiler_params=_params("parallel"),
        name="rwkv_single_step",
    )(*ops, states_t, *carried)


def _rwkv_post_kernel(yt_ref, bonus_ref, g_ref, lnw_ref, lnb_ref, o_ref):
    y = yt_ref[...].T
    n = RW_HEAD_DIM
    yn = jnp.concatenate([_group_norm(y[:, h * n:(h + 1) * n]) for h in range(RW_HEADS)], axis=-1)
    out = (yn * lnw_ref[...] + lnb_ref[...] + bonus_ref[...]) * g_ref[...]
    o_ref[...] = out.astype(BF16)


def _rwkv_post(y_t, bonus, g, ln_w, ln_b):
    b = bonus.shape[0]
    flat_spec = pl.BlockSpec((b, RW_WIDTH), lambda i: (0, 0))
    row = pl.BlockSpec((1, RW_WIDTH), lambda i: (0, 0))
    return pl.pallas_call(
        _rwkv_post_kernel,
        grid=(1,),
        in_specs=[pl.BlockSpec((RW_WIDTH, b), lambda i: (0, 0)), flat_spec, flat_spec, row, row],
        out_specs=flat_spec,
        out_shape=jax.ShapeDtypeStruct((b, RW_WIDTH), BF16),
        compiler_params=_params("arbitrary"),
        name="rwkv_post",
    )(y_t, bonus, g, ln_w, ln_b)


CV_HALO = 32


def _conv_finish(y, cb_ref, lnw_ref, lnb_ref):
    y = y + cb_ref[...]
    mean = jnp.mean(y, axis=-1, keepdims=True)
    var = jnp.mean(jnp.square(y - mean), axis=-1, keepdims=True)
    yn = (y - mean) * lax.rsqrt(var + LN_EPS) * lnw_ref[...] + lnb_ref[...]
    return _silu(yn).astype(BF16)


def _conv_seq_kernel(p_ref, cw_ref, cb_ref, lnw_ref, lnb_ref, o_ref, tail_ref, ext_ref):
    tm = p_ref.shape[0]
    pad = CV_HALO - (CV_K - 1)

    @pl.when(pl.program_id(0) == 0)
    def _():
        ext_ref[0:CV_HALO, :] = jnp.zeros((CV_HALO, CV_WIDTH), F32)

    p = p_ref[...]
    u = p[:, 0:CV_WIDTH] * _sigmoid(p[:, CV_WIDTH:2 * CV_WIDTH])
    ext_ref[CV_HALO:CV_HALO + tm, :] = u
    acc = jnp.zeros((tm, CV_WIDTH), F32)
    for j in range(CV_K):
        acc = acc + ext_ref[pad + j:pad + j + tm, :] * cw_ref[j:j + 1, :]
    o_ref[...] = _conv_finish(acc, cb_ref, lnw_ref, lnb_ref)
    tail = ext_ref[tm:tm + CV_HALO, :]
    ext_ref[0:CV_HALO, :] = tail
    tail_ref[...] = tail


def _conv_seq(proj, cw, cb, ln_w, ln_b, tm):
    m = proj.shape[0]
    row = pl.BlockSpec((1, CV_WIDTH), lambda i: (0, 0))
    return pl.pallas_call(
        _conv_seq_kernel,
        grid=(m // tm,),
        in_specs=[pl.BlockSpec((tm, 2 * CV_WIDTH), lambda i: (i, COL_CV)),
                  pl.BlockSpec((CV_K, CV_WIDTH), lambda i: (0, 0)), row, row, row],
        out_specs=[pl.BlockSpec((tm, CV_WIDTH), lambda i: (i, 0)),
                   pl.BlockSpec((CV_HALO, CV_WIDTH), lambda i: (0, 0))],
        out_shape=[jax.ShapeDtypeStruct((m, CV_WIDTH), BF16),
                   jax.ShapeDtypeStruct((CV_HALO, CV_WIDTH), F32)],
        scratch_shapes=[pltpu.VMEM((CV_HALO + tm, CV_WIDTH), F32)],
        compiler_params=_params("arbitrary"),
        name="conv_seq",
    )(proj, cw, cb, ln_w, ln_b)


def _conv_step_kernel(p_ref, buf_ref, cw_ref, cb_ref, lnw_ref, lnb_ref, o_ref, nbuf_ref):
    p = p_ref[...]
    u = p[:, :, 0:CV_WIDTH] * _sigmoid(p[:, :, CV_WIDTH:2 * CV_WIDTH])
    buf = buf_ref[...]
    acc = (jnp.sum(buf * cw_ref[0:CV_K - 1, :][None], axis=1, keepdims=True)
           + u * cw_ref[CV_K - 1:CV_K, :][None])
    o_ref[...] = _conv_finish(acc, cb_ref, lnw_ref, lnb_ref)
    nbuf_ref[:, 0:CV_K - 2, :] = buf[:, 1:CV_K - 1, :]
    nbuf_ref[:, CV_K - 2:CV_K - 1, :] = u


def _conv_step(proj, buf, cw, cb, ln_w, ln_b, tb):
    m = proj.shape[0]
    row = pl.BlockSpec((1, CV_WIDTH), lambda i: (0, 0))
    buf_spec = pl.BlockSpec((tb, CV_K - 1, CV_WIDTH), lambda i: (i, 0, 0))
    y = pl.pallas_call(
        _conv_step_kernel,
        grid=(m // tb,),
        in_specs=[pl.BlockSpec((tb, 1, 2 * CV_WIDTH), lambda i: (i, 0, COL_CV)), buf_spec,
                  pl.BlockSpec((CV_K, CV_WIDTH), lambda i: (0, 0)), row, row, row],
        out_specs=[pl.BlockSpec((tb, 1, CV_WIDTH), lambda i: (i, 0, 0)), buf_spec],
        out_shape=[jax.ShapeDtypeStruct((m, 1, CV_WIDTH), BF16),
                   jax.ShapeDtypeStruct(buf.shape, F32)],
        compiler_params=_params("parallel"),
        name="conv_step",
    )(proj.reshape(m, 1, N_PROJ), buf, cw, cb, ln_w, ln_b)
    return y[0].reshape(m, CV_WIDTH), y[1]


def _bucket_thresholds():
    n = np.arange(0, 4 * MAX_DISTANCE, dtype=np.int64)
    max_exact = NUM_BUCKETS // 2
    nf = np.maximum(n, 1).astype(np.float32)
    large = max_exact + (np.log(nf / np.float32(max_exact)) / np.float32(math.log(MAX_DISTANCE / max_exact))
                         * np.float32(NUM_BUCKETS - max_exact)).astype(np.int32)
    large = np.minimum(large, NUM_BUCKETS - 1)
    bucket = np.where(n < max_exact, n, large)
    assert np.all(np.diff(bucket) >= 0) and bucket[-1] == NUM_BUCKETS - 1
    return [int(np.argmax(bucket >= b)) for b in range(NUM_BUCKETS)]


_BUCKET_THR = _bucket_thresholds()


def _bias_from_distance(n, rb_of_bucket):
    bias = jnp.zeros(n.shape, F32) + rb_of_bucket(NUM_BUCKETS - 1)
    for b in range(NUM_BUCKETS - 2, -1, -1):
        bias = jnp.where(n < _BUCKET_THR[b + 1], rb_of_bucket(b), bias)
    return bias


def _rank_select(scores, n_rows, limit):
    idx = lax.broadcasted_iota(jnp.int32, scores.shape, 0)
    beaten = jnp.zeros(scores.shape, F32)
    for j in range(n_rows):
        row = scores[j:j + 1, :]
        ahead = jnp.where(row > scores, 1.0, jnp.where(row == scores, jnp.where(idx > j, 1.0, 0.0), 0.0))
        beaten = beaten + ahead * jnp.where(j < limit, 1.0, 0.0)
    return jnp.where(idx < limit, jnp.where(beaten < MB_TOPK, 1.0, 0.0), 0.0)


MB_HEADS_PER_STEP = 4
MB_SUM_ROWS = 16
MB_BLOCKS_PER_TRIP = 2


def _softmax_blocks(carries, scores, vt_blks):
    m_new = [jnp.maximum(c[0], jnp.max(s, axis=0, keepdims=True)) for c, s in zip(carries, scores)]
    alpha = [jnp.exp2(c[0] - mn) for c, mn in zip(carries, m_new)]
    p = [jnp.exp2(s - mn).astype(BF16) for s, mn in zip(scores, m_new)]
    pv = [_dot(_with_ones_rows(vt), pp) for vt, pp in zip(vt_blks, p)]
    return [(mn, a * c[1] + x) for mn, a, c, x in zip(m_new, alpha, carries, pv)]


def _with_ones_rows(vt_blk):
    return jnp.concatenate([vt_blk, jnp.ones((MB_SUM_ROWS, vt_blk.shape[1]), BF16)], axis=0)


def _moba_seq_kernel(rb_ref, qt_ref, k_ref, vt_ref, km_ref, o_ref, bias_ref, rowterm_ref):
    hp = qt_ref.shape[0]
    h0 = pl.program_id(0) * hp
    qi = pl.program_id(1)
    nb = km_ref.shape[1]
    log2e = math.log2(math.e)
    scale = MB_HEAD_DIM ** -0.5 * log2e
    kidx = lax.broadcasted_iota(jnp.int32, (MB_BLOCK, MB_BLOCK), 0)
    qidx = lax.broadcasted_iota(jnp.int32, (MB_BLOCK, MB_BLOCK), 1)

    @pl.when(qi == 0)
    def _():
        dist = qidx - kidx
        for hh in range(hp):
            rb = lambda b: rb_ref[b, h0 + hh] * log2e
            bias_ref[hh, 0] = jnp.where(dist >= 0, _bias_from_distance(jnp.maximum(dist, 0), rb), NEG_INF)
            bias_ref[hh, 1] = _bias_from_distance(dist + MB_BLOCK, rb)

    near = jnp.maximum(qi - 1, 0)
    qtb, carries = [], []
    for hh in range(hp):
        qt = qt_ref[hh]
        block_scores = _dot_f32(km_ref[hh], qt)
        sel = _rank_select(block_scores, nb, qi)
        rowterm_ref[hh] = jnp.where(sel > 0.0, rb_ref[NUM_BUCKETS - 1, h0 + hh] * log2e, NEG_INF)
        qtb.append((qt * scale).astype(BF16))
        s = _dot(k_ref[hh, qi], qtb[hh]) + bias_ref[hh, 0]
        m0 = jnp.max(s, axis=0, keepdims=True)
        p = jnp.exp2(s - m0).astype(BF16)
        carries.append((m0, _dot(_with_ones_rows(vt_ref[hh, qi]), p)))

    heads = range(hp)
    sel_near = [jnp.where(rowterm_ref[hh, pl.ds(near, 1), :] > 0.5 * NEG_INF, 0.0, NEG_INF) for hh in heads]
    scores = [_dot(k_ref[hh, near], qtb[hh]) + bias_ref[hh, 1] + sel_near[hh] for hh in heads]
    carries = _softmax_blocks(carries, scores, [vt_ref[hh, near] for hh in heads])

    def body(t, carry):
        kjs = [t * MB_BLOCKS_PER_TRIP + u for u in range(MB_BLOCKS_PER_TRIP)]
        safe = [jnp.minimum(kj, nb - 1) for kj in kjs]
        raw = [[_dot(k_ref[hh, kc], qtb[hh]) for hh in heads] for kc in safe]
        for u, (kj, kc) in enumerate(zip(kjs, safe)):
            scores = [raw[u][hh] + jnp.where(kj < near, rowterm_ref[hh, pl.ds(kc, 1), :], NEG_INF) for hh in heads]
            carry = _softmax_blocks(carry, scores, [vt_ref[hh, kc] for hh in heads])
        return tuple(carry)

    n_trips = (near + MB_BLOCKS_PER_TRIP - 1) // MB_BLOCKS_PER_TRIP
    carries = lax.fori_loop(0, n_trips, body, tuple(carries))
    for hh in range(hp):
        acc = carries[hh][1]
        o_ref[hh] = (acc[0:MB_HEAD_DIM] / acc[MB_HEAD_DIM:MB_HEAD_DIM + 1]).astype(BF16)


def _moba_seq(rel_bias, q_t, k_t, v_t, km_tiles):
    t = q_t.shape[1]
    nb = t // MB_BLOCK
    hp = MB_HEADS_PER_STEP
    blocks = lambda x: x.reshape(MB_HEADS, MB_HEAD_DIM, nb, MB_BLOCK)
    qt = q_t.reshape(MB_HEADS, MB_HEAD_DIM, t)
    kb = blocks(k_t).transpose(0, 2, 3, 1)
    vt = blocks(v_t).transpose(0, 2, 1, 3)
    km = km_tiles.transpose(1, 0, 2).reshape(MB_HEADS, MB_HEAD_DIM, nb).transpose(0, 2, 1)
    ot = pl.pallas_call(
        _moba_seq_kernel,
        grid=(MB_HEADS // hp, nb),
        in_specs=[pl.BlockSpec(memory_space=pltpu.SMEM),
                  pl.BlockSpec((hp, MB_HEAD_DIM, MB_BLOCK), lambda h, i: (h, 0, i)),
                  pl.BlockSpec((hp, nb, MB_BLOCK, MB_HEAD_DIM), lambda h, i: (h, 0, 0, 0)),
                  pl.BlockSpec((hp, nb, MB_HEAD_DIM, MB_BLOCK), lambda h, i: (h, 0, 0, 0)),
                  pl.BlockSpec((hp, nb, MB_HEAD_DIM), lambda h, i: (h, 0, 0))],
        out_specs=pl.BlockSpec((hp, MB_HEAD_DIM, MB_BLOCK), lambda h, i: (h, 0, i)),
        out_shape=jax.ShapeDtypeStruct((MB_HEADS, MB_HEAD_DIM, t), BF16),
        scratch_shapes=[pltpu.VMEM((hp, 2, MB_BLOCK, MB_BLOCK), F32), pltpu.VMEM((hp, nb, MB_BLOCK), F32)],
        compiler_params=_params("parallel", "arbitrary"),
        name="moba_seq",
    )(rel_bias, qt, kb, vt, km)
    return ot.reshape(MB_WIDTH, t)


def _moba_paged_kernel(n_pages, pt_ref, q_ref, kn_ref, vn_ref, rbt_ref, *refs):
    k_pages = refs[:n_pages]
    v_pages = refs[n_pages:2 * n_pages]
    o_ref, bias_ref = refs[2 * n_pages:]
    past = n_pages * PAGE_SIZE
    n_blocks = past // MB_BLOCK
    pages_per_block = MB_BLOCK // PAGE_SIZE
    scale = MB_HEAD_DIM ** -0.5
    rb = lambda b: rbt_ref[:, :, b:b + 1]

    @pl.when(pl.program_id(0) == 0)
    def _():
        dist = PAGE_SIZE - lax.broadcasted_iota(jnp.int32, (MB_HEADS, 1, PAGE_SIZE), 2)
        bias_ref[...] = _bias_from_distance(dist, rb)

    di = lax.broadcasted_iota(jnp.int32, (MB_HEAD_DIM, MB_HEAD_DIM), 0)
    dj = lax.broadcasted_iota(jnp.int32, (MB_HEAD_DIM, MB_HEAD_DIM), 1)
    eye = jnp.where(di == dj, 1.0, 0.0)[None]
    q = q_ref[0]
    q_col = jnp.sum(eye * q, axis=-1, keepdims=True)
    raw = [jnp.sum(k_pages[pg][...] * q_col, axis=1, keepdims=True) for pg in range(n_pages)]
    scores = []
    for j in range(n_blocks):
        tot = raw[pages_per_block * j]
        for e in range(1, pages_per_block):
            tot = tot + raw[pages_per_block * j + e]
        scores.append(jnp.sum(tot, axis=-1, keepdims=True) * (1.0 / MB_BLOCK))
    far_bias = rb(NUM_BUCKETS - 1)
    rowterms = []
    for j in range(n_blocks):
        beaten = jnp.zeros((MB_HEADS, 1, 1), F32)
        for i in range(n_blocks):
            if i != j:
                ahead = scores[i] > scores[j] if i > j else scores[i] >= scores[j]
                beaten = beaten + jnp.where(ahead, 1.0, 0.0)
        rowterms.append(jnp.where(beaten < MB_TOPK, far_bias, NEG_INF))
    lg_own = jnp.sum(kn_ref[0] * q, axis=-1, keepdims=True) * scale + rb(0)
    logits = []
    for pg in range(n_pages):
        term = rowterms[pg // pages_per_block]
        if pg == n_pages - 1:
            term = bias_ref[...] + jnp.where(term > 0.5 * NEG_INF, 0.0, NEG_INF)
        logits.append(raw[pg] * scale + term)
    m_row = logits[0]
    for lg in logits[1:]:
        m_row = jnp.maximum(m_row, lg)
    m = jnp.maximum(jnp.max(m_row, axis=-1, keepdims=True), lg_own)
    p_own = jnp.exp(lg_own - m)
    l_row = jnp.zeros((MB_HEADS, 1, PAGE_SIZE), F32)
    acc = jnp.zeros((MB_HEADS, MB_HEAD_DIM, PAGE_SIZE), F32)
    for pg in range(n_pages):
        p = jnp.exp(logits[pg] - m)
        l_row = l_row + p
        acc = acc + v_pages[pg][...] * p
    out_col = jnp.sum(acc, axis=-1, keepdims=True)
    out = jnp.sum(eye * out_col, axis=1, keepdims=True) + p_own * vn_ref[0]
    l = jnp.sum(l_row, axis=-1, keepdims=True) + p_own
    o_ref[0] = (out / l).astype(BF16)


def _moba_paged(rel_bias, q, k_new, v_new, cache_k, cache_v, layer, page_table):
    b, n_pages = page_table.shape
    assert PAGE_SIZE >= _BUCKET_THR[NUM_BUCKETS - 1] and MB_BLOCK % PAGE_SIZE == 0
    head4 = pl.BlockSpec((1, MB_HEADS, 1, MB_HEAD_DIM), lambda i, pt: (i, 0, 0, 0))

    def page_spec(pg):
        return pl.BlockSpec((None, None, MB_HEADS, MB_HEAD_DIM, PAGE_SIZE),
                            lambda i, pt: (layer, pt[i, pg], 0, 0, 0))

    grid_spec = pltpu.PrefetchScalarGridSpec(
        num_scalar_prefetch=1,
        grid=(b,),
        in_specs=[head4, head4, head4, pl.BlockSpec((MB_HEADS, 1, NUM_BUCKETS), lambda i, pt: (0, 0, 0))]
        + [page_spec(pg) for pg in range(n_pages)] * 2,
        out_specs=head4,
        scratch_shapes=[pltpu.VMEM((MB_HEADS, 1, PAGE_SIZE), F32)],
    )
    h4 = lambda x: x.reshape(b, MB_HEADS, 1, MB_HEAD_DIM)
    kt = cache_k.transpose(0, 1, 3, 4, 2)
    vt = cache_v.transpose(0, 1, 3, 4, 2)
    out = pl.pallas_call(
        functools.partial(_moba_paged_kernel, n_pages),
        grid_spec=grid_spec,
        out_shape=jax.ShapeDtypeStruct((b, MB_HEADS, 1, MB_HEAD_DIM), BF16),
        compiler_params=_params("arbitrary"),
        name="moba_paged",
    )(page_table, h4(q), h4(k_new), h4(v_new), rel_bias.T.reshape(MB_HEADS, 1, NUM_BUCKETS),
      *([kt] * n_pages), *([vt] * n_pages))
    return out.reshape(b, MB_WIDTH)


def _merge_kernel(mb_feature_major, yrw_ref, ymb_ref, ycv_ref, g0_ref, g1_ref, g2_ref, wrw_ref, wmb_ref, wcv_ref,
                  o_ref):
    if mb_feature_major:
        y_mb = lax.dot_general(ymb_ref[...], wmb_ref[...], _TN, preferred_element_type=F32)
    else:
        y_mb = _dot(ymb_ref[...], wmb_ref[...])
    merged = (g0_ref[...] * _dot(yrw_ref[...], wrw_ref[...]) + g1_ref[...] * y_mb
              + g2_ref[...] * _dot(ycv_ref[...], wcv_ref[...]))
    o_ref[...] = merged.astype(BF16)


def _merge(y_rw, y_mb, y_cv, proj, w_rw, w_mb, w_cv, tm, mb_feature_major):
    m = proj.shape[0]
    rows = lambda n: pl.BlockSpec((tm, n), lambda i: (i, 0))
    mb_spec = pl.BlockSpec((MB_WIDTH, tm), lambda i: (0, i)) if mb_feature_major else rows(MB_WIDTH)
    gate = lambda br: pl.BlockSpec((tm, D_MODEL), lambda i: (i, COL_GATE + br))
    full = lambda n: pl.BlockSpec((n, D_MODEL), lambda i: (0, 0))
    return pl.pallas_call(
        functools.partial(_merge_kernel, mb_feature_major),
        grid=(m // tm,),
        in_specs=[rows(RW_WIDTH), mb_spec, rows(CV_WIDTH), gate(0), gate(1), gate(2),
                  full(RW_WIDTH), full(MB_WIDTH), full(CV_WIDTH)],
        out_specs=rows(D_MODEL),
        out_shape=jax.ShapeDtypeStruct((m, D_MODEL), BF16),
        compiler_params=_params("parallel"),
        name="branch_merge",
    )(y_rw, y_mb, y_cv, proj, proj, proj, w_rw, w_mb, w_cv)


def _out_proj_kernel(h_ref, x_ref, w_ref, o_ref):
    o_ref[...] = h_ref[...] + _dot(x_ref[...], w_ref[...])


def _out_proj(h, merged, w_out, tm):
    m = h.shape[0]
    rows = pl.BlockSpec((tm, D_MODEL), lambda i: (i, 0))
    return pl.pallas_call(
        _out_proj_kernel,
        grid=(m // tm,),
        in_specs=[rows, rows, pl.BlockSpec((D_MODEL, D_MODEL), lambda i: (0, 0))],
        out_specs=rows,
        out_shape=jax.ShapeDtypeStruct((m, D_MODEL), F32),
        compiler_params=_params("parallel"),
        name="out_proj",
    )(h, merged, w_out)


def _ffn_kernel(h_ref, g_ref, wg_ref, wu_ref, wd_ref, o_ref, xn_ref):
    @pl.when(pl.program_id(1) == 0)
    def _():
        x = h_ref[...]
        ms = jnp.mean(x * x, axis=-1, keepdims=True)
        xn_ref[...] = (x * lax.rsqrt(ms + NORM_EPS) * g_ref[...]).astype(BF16)
        o_ref[...] = x

    xn = xn_ref[...]
    mid = _silu(_dot(xn, wg_ref[...])) * _dot(xn, wu_ref[...])
    o_ref[...] += _dot(mid.astype(BF16), wd_ref[...])


def _ffn(h, g, wg, wu, wd, tm, tf):
    m = h.shape[0]
    d_ff = wg.shape[1]
    rows = pl.BlockSpec((tm, D_MODEL), lambda i, f: (i, 0))
    return pl.pallas_call(
        _ffn_kernel,
        grid=(m // tm, d_ff // tf),
        in_specs=[rows, pl.BlockSpec((1, D_MODEL), lambda i, f: (0, 0)),
                  pl.BlockSpec((D_MODEL, tf), lambda i, f: (0, f)),
                  pl.BlockSpec((D_MODEL, tf), lambda i, f: (0, f)),
                  pl.BlockSpec((tf, D_MODEL), lambda i, f: (f, 0))],
        out_specs=rows,
        out_shape=jax.ShapeDtypeStruct((m, D_MODEL), F32),
        scratch_shapes=[pltpu.VMEM((tm, D_MODEL), BF16)],
        compiler_params=_params("parallel", "arbitrary"),
        name="ffn",
    )(h, g, wg, wu, wd)


def _dot_f32_nt(a, b):
    nt = lambda x, y: lax.dot_general(x, y, _NT, preferred_element_type=F32)
    a_hi, a_mid, a_lo = _split3(a)
    b_hi, b_mid, b_lo = _split3(b)
    return (nt(a_hi, b_hi) + (nt(a_hi, b_mid) + nt(a_mid, b_hi))
            + (nt(a_mid, b_mid) + nt(a_hi, b_lo) + nt(a_lo, b_hi)))


def _router_kernel(h_ref, g_ref, wt_ref, b_ref, before_ref, xn_ref, gate_ref, pos_ref, cnt_ref):
    x = h_ref[...]
    ms = jnp.mean(x * x, axis=-1, keepdims=True)
    xn = x * lax.rsqrt(ms + NORM_EPS) * g_ref[...]
    xn_ref[...] = xn.astype(BF16)
    logits = _dot_f32_nt(wt_ref[...], xn) + b_ref[...]
    idx = lax.broadcasted_iota(jnp.int32, logits.shape, 0).astype(F32)
    m1 = jnp.max(logits, axis=0, keepdims=True)
    i1 = jnp.min(jnp.where(logits == m1, idx, float(N_EXPERTS)), axis=0, keepdims=True)
    rest = jnp.where(idx == i1, -jnp.inf, logits)
    m2 = jnp.max(rest, axis=0, keepdims=True)
    i2 = jnp.min(jnp.where(rest == m2, idx, float(N_EXPERTS)), axis=0, keepdims=True)
    e2 = jnp.exp(m2 - m1)
    gates = jnp.where(idx == i1, 1.0 / (1.0 + e2), 0.0) + jnp.where(idx == i2, e2 / (1.0 + e2), 0.0)
    member = jnp.where(gates > 0.0, 1.0, 0.0)
    gate_ref[0] = gates
    pos_ref[0] = _dot(member.astype(BF16), before_ref[...])
    cnt_ref[0] = jnp.sum(member, axis=-1, keepdims=True).astype(jnp.int32)


def _router(h, g, w_router, b_router, tm):
    m = h.shape[0]
    n_tiles = m // tm
    before = jnp.asarray(np.triu(np.ones((tm, tm), np.float32), 1), BF16)
    tile3 = pl.BlockSpec((1, N_EXPERTS, tm), lambda i: (i, 0, 0))
    per_tile = jax.ShapeDtypeStruct((n_tiles, N_EXPERTS, tm), F32)
    return pl.pallas_call(
        _router_kernel,
        grid=(n_tiles,),
        in_specs=[pl.BlockSpec((tm, D_MODEL), lambda i: (i, 0)),
                  pl.BlockSpec((1, D_MODEL), lambda i: (0, 0)),
                  pl.BlockSpec((N_EXPERTS, D_MODEL), lambda i: (0, 0)),
                  pl.BlockSpec((N_EXPERTS, 1), lambda i: (0, 0)),
                  pl.BlockSpec((tm, tm), lambda i: (0, 0))],
        out_specs=[pl.BlockSpec((tm, D_MODEL), lambda i: (i, 0)), tile3, tile3,
                   pl.BlockSpec((1, N_EXPERTS, 1), lambda i: (i, 0, 0))],
        out_shape=[jax.ShapeDtypeStruct((m, D_MODEL), BF16), per_tile, per_tile,
                   jax.ShapeDtypeStruct((n_tiles, N_EXPERTS, 1), jnp.int32)],
        compiler_params=_params("parallel"),
        name="moe_router",
    )(h, g, w_router.T, b_router.reshape(N_EXPERTS, 1), before)


def _moe_kernel(sub, cnt_ref, xn_ref, gate_ref, pos_ref, wg_ref, wu_ref, wd_ref, o_ref, xg_ref, gg_ref, acc_ref):
    i = pl.program_id(0)
    e = pl.program_id(1)
    f = pl.program_id(2)
    tm = xn_ref.shape[0]
    n_sub = (cnt_ref[i * N_EXPERTS + e] + (sub - 1)) // sub
    pos_row = pos_ref[0, pl.ds(e, 1), :]
    gate_row = gate_ref[0, pl.ds(e, 1), :]
    slot = lax.broadcasted_iota(jnp.int32, (sub, tm), 0)

    def one_hot(s):
        wanted = (slot + s * sub).astype(F32)
        return jnp.where(gate_row > 0.0, jnp.where(pos_row == wanted, 1.0, 0.0), 0.0)

    def rows_of(s):
        return pl.ds(pl.multiple_of(s * sub, sub), sub)

    @pl.when((e == 0) & (f == 0))
    def _():
        o_ref[...] = jnp.zeros_like(o_ref)

    @pl.when(f == 0)
    def _():
        def pack(s, carry):
            pick = one_hot(s)
            xg_ref[rows_of(s), :] = _dot(pick.astype(BF16), xn_ref[...]).astype(BF16)
            gg_ref[rows_of(s), :] = jnp.sum(pick * gate_row, axis=-1, keepdims=True)
            acc_ref[rows_of(s), :] = jnp.zeros((sub, D_MODEL), F32)
            return carry

        lax.fori_loop(0, n_sub, pack, 0)

    def expert(s, carry):
        x = xg_ref[rows_of(s), :]
        mid = _silu(_dot(x, wg_ref[0])) * _dot(x, wu_ref[0]) * gg_ref[rows_of(s), :]
        acc_ref[rows_of(s), :] += _dot(mid.astype(BF16), wd_ref[0])
        return carry

    lax.fori_loop(0, n_sub, expert, 0)

    @pl.when(f == pl.num_programs(2) - 1)
    def _():
        def unpack(s, carry):
            pick = one_hot(s).astype(BF16)
            o_ref[...] += lax.dot_general(pick, acc_ref[rows_of(s), :].astype(BF16), _TN, preferred_element_type=F32)
            return carry

        lax.fori_loop(0, n_sub, unpack, 0)


def _moe(xn, gates, pos, counts, wg, wu, wd, tm, tf):
    m = xn.shape[0]
    n_e, _, d_ff = wg.shape
    sub = min(tm, -(-(tm * TOP_K * 5 // (N_EXPERTS * 4)) // 16) * 16)
    packed = -(-tm // sub) * sub
    tile3 = pl.BlockSpec((1, N_EXPERTS, tm), lambda i, e, f, cnt: (i, 0, 0))
    grid_spec = pltpu.PrefetchScalarGridSpec(
        num_scalar_prefetch=1,
        grid=(m // tm, n_e, d_ff // tf),
        in_specs=[pl.BlockSpec((tm, D_MODEL), lambda i, e, f, cnt: (i, 0)), tile3, tile3,
                  pl.BlockSpec((1, D_MODEL, tf), lambda i, e, f, cnt: (e, 0, f)),
                  pl.BlockSpec((1, D_MODEL, tf), lambda i, e, f, cnt: (e, 0, f)),
                  pl.BlockSpec((1, tf, D_MODEL), lambda i, e, f, cnt: (e, f, 0))],
        out_specs=pl.BlockSpec((tm, D_MODEL), lambda i, e, f, cnt: (i, 0)),
        scratch_shapes=[pltpu.VMEM((packed, D_MODEL), BF16), pltpu.VMEM((packed, 1), F32),
                        pltpu.VMEM((packed, D_MODEL), F32)],
    )
    return pl.pallas_call(
        functools.partial(_moe_kernel, sub),
        grid_spec=grid_spec,
        out_shape=jax.ShapeDtypeStruct((m, D_MODEL), F32),
        compiler_params=_params("parallel", "arbitrary", "arbitrary"),
        name="moe_experts",
    )(counts.reshape(-1), xn, gates, pos, wg, wu, wd)


def _final_norm_kernel(n_in, *refs):
    g_ref, o_ref = refs[n_in], refs[n_in + 1]
    x = refs[0][...]
    for r in refs[1:n_in]:
        x = x + r[...]
    ms = jnp.mean(x * x, axis=-1, keepdims=True)
    o_ref[...] = x * lax.rsqrt(ms + NORM_EPS) * g_ref[...]


def _final_norm(parts, g, tm):
    m = parts[0].shape[0]
    rows = pl.BlockSpec((tm, D_MODEL), lambda i: (i, 0))
    return pl.pallas_call(
        functools.partial(_final_norm_kernel, len(parts)),
        grid=(m // tm,),
        in_specs=[rows] * len(parts) + [pl.BlockSpec((1, D_MODEL), lambda i: (0, 0))],
        out_specs=rows,
        out_shape=jax.ShapeDtypeStruct((m, D_MODEL), F32),
        compiler_params=_params("parallel"),
        name="final_norm",
    )(*parts, g)


def _add_kernel(a_ref, b_ref, o_ref):
    o_ref[...] = a_ref[...] + b_ref[...]


def _add(a, b, tm):
    rows = pl.BlockSpec((tm, D_MODEL), lambda i: (i, 0))
    return pl.pallas_call(
        _add_kernel,
        grid=(a.shape[0] // tm,),
        in_specs=[rows, rows],
        out_specs=rows,
        out_shape=jax.ShapeDtypeStruct(a.shape, F32),
        compiler_params=_params("parallel"),
        name="residual_add",
    )(a, b)


def _pad_rw_cols(x, axis=-1):
    x = jnp.moveaxis(x, axis, 0)
    pad = jnp.zeros((RW_PAD - RW_COLS,) + x.shape[1:], x.dtype)
    return jnp.moveaxis(jnp.concatenate([x[:RW_COLS], pad, x[RW_COLS:]], axis=0), 0, axis)


def _stage_w_in_kernel(w_ref, o_ref):
    o_ref[...] = w_ref[0].astype(BF16)


def _stage_w_in(w_in, layer):
    rows = 512
    assert RW_COLS // rows == (RW_PAD - 1) // rows and RW_PAD % rows == 0 and (N_PROJ - RW_PAD) % rows == 0
    shift = RW_PAD - RW_COLS

    def src_row(j):
        assert rows % 8 == 0 and shift % 8 == 0
        return 8 * jnp.where(j * rows < RW_PAD, j * (rows // 8), j * (rows // 8) - shift // 8)

    return pl.pallas_call(
        _stage_w_in_kernel,
        grid=(N_PROJ // rows,),
        in_specs=[pl.BlockSpec((pl.Element(1), pl.Element(rows), pl.Element(D_MODEL)),
                               lambda j: (layer, src_row(j), 0))],
        out_specs=pl.BlockSpec((rows, D_MODEL), lambda j: (j, 0)),
        out_shape=jax.ShapeDtypeStruct((N_PROJ, D_MODEL), BF16),
        compiler_params=_params("parallel"),
        name="stage_w_in",
    )(w_in.transpose(0, 2, 1))


def _lora_slab(w, row0):
    slab = jnp.zeros((LORA_PAD, RW_WIDTH), F32)
    return lax.dynamic_update_slice(slab, w, (row0, 0)).astype(BF16)


def _layer_params(l, P):
    row = lambda x: x.reshape(1, -1)
    lp = dict(
        norm_mix=row(P['norm_mix'][l]), norm_ffn=row(P['norm_ffn'][l]),
        w_in_t=_stage_w_in(P['w_in'], l),
        rw_vecs=[row(_pad_rw_cols(P['rw_mu'][l])), row(P['rw_w0'][l]), row(P['rw_a0'][l]),
                 row(P['rw_kk'][l]), row(P['rw_ka'][l]), row(P['rw_rk'][l])],
        rw_mats=[_lora_slab(P['rw_w2'][l], 0), _lora_slab(P['rw_a2'][l], DECAY_LORA),
                 _lora_slab(P['rw_g2'][l], DECAY_LORA + AAA_LORA)],
        rw_ln_w=row(P['rw_ln_w'][l]), rw_ln_b=row(P['rw_ln_b'][l]),
        cv_w=P['cv_w'][l], cv_b=row(P['cv_b'][l]), cv_ln_w=row(P['cv_ln_w'][l]), cv_ln_b=row(P['cv_ln_b'][l]),
        w_br_rw=P['w_br_rw'][l].astype(BF16), w_br_mb=P['w_br_mb'][l].astype(BF16),
        w_br_cv=P['w_br_cv'][l].astype(BF16), w_out=P['w_out'][l].astype(BF16),
    )
    i = l // 2
    if l % 2 == 0:
        lp.update(wg=P['ffn_wg'][i].astype(BF16), wu=P['ffn_wu'][i].astype(BF16), wd=P['ffn_wd'][i].astype(BF16))
    else:
        lp.update(wg=P['moe_wg'][i].astype(BF16), wu=P['moe_wu'][i].astype(BF16), wd=P['moe_wd'][i].astype(BF16),
                  router=P['moe_router'][i], router_b=P['moe_router_b'][i])
    return lp


def _channel_mix(h, lp, tm_dense, tm_moe):
    if 'router' in lp:
        xn, gates, pos, counts = _router(h, lp['norm_ffn'], lp['router'], lp['router_b'], tm_moe)
        return [h, _moe(xn, gates, pos, counts, lp['wg'], lp['wu'], lp['wd'], tm_moe, 256)]
    return [_ffn(h, lp['norm_ffn'], lp['wg'], lp['wu'], lp['wd'], tm_dense, 512)]


def _as_one(parts, tm):
    return parts[0] if len(parts) == 1 else _add(parts[0], parts[1], tm)


def _run_prompt(x, layers, norm_final, rel_bias):
    t = x.shape[0]
    parts = [x]
    outs = dict(k=[], v=[], wkv=[], shift=[], conv=[])
    for lp in layers:
        h = _as_one(parts, 512)
        proj = _norm_proj(h, lp['norm_mix'], lp['w_in_t'], 1024)
        *ops, g, bonus = _rwkv_prep(proj, None, lp['rw_vecs'], lp['rw_mats'], 128)
        y_rw, s_final = _rwkv_chunked(ops, bonus, g, lp['rw_ln_w'], lp['rw_ln_b'])
        q_t, k_t, v_t, k_tb, v_tb, km_tiles = _qkv_t(h, lp['norm_mix'], lp['w_in_t'], 1024)
        y_mb_t = _moba_seq(rel_bias, q_t, k_tb, v_tb, km_tiles)
        y_cv, tail = _conv_seq(proj, lp['cv_w'], lp['cv_b'], lp['cv_ln_w'], lp['cv_ln_b'], 512)
        merged = _merge(y_rw, y_mb_t, y_cv, proj, lp['w_br_rw'], lp['w_br_mb'], lp['w_br_cv'], 256, True)
        h = _out_proj(h, merged, lp['w_out'], 512)
        parts = _channel_mix(h, lp, 512, 1024)
        tokens_major = lambda x_t: x_t.reshape(MB_HEADS, MB_HEAD_DIM, t).transpose(2, 0, 1)[None]
        outs['k'].append(tokens_major(k_t))
        outs['v'].append(tokens_major(v_t))
        outs['wkv'].append(s_final[None])
        outs['shift'].append(proj[t - 1:t, :RW_COLS])
        outs['conv'].append(tail[None, CV_HALO - (CV_K - 1):, :])
    y = _final_norm(parts, norm_final, 512)
    return (y[None],) + tuple(jnp.stack(outs[n]) for n in ('k', 'v', 'wkv', 'shift', 'conv'))


def _run_sample(x, layers, norm_final, rel_bias, cache_k, cache_v, state_wkv, state_shift, state_conv, page_table):
    b = x.shape[0]
    parts = [x]
    wkv_t = state_wkv.transpose(0, 2, 3, 4, 1)
    new_wkv_t = None
    outs = dict(k=[], v=[], shift=[], conv=[])
    for l, lp in enumerate(layers):
        h = _as_one(parts, b)
        proj = _norm_proj(h, lp['norm_mix'], lp['w_in_t'], b)
        *ops, g, bonus = _rwkv_prep(proj, _pad_rw_cols(state_shift[l]), lp['rw_vecs'], lp['rw_mats'], b)
        y_t, new_wkv_t = _rwkv_single_step(ops, wkv_t, l, new_wkv_t)
        y_rw = _rwkv_post(y_t, bonus, g, lp['rw_ln_w'], lp['rw_ln_b'])
        q = proj[:, RW_PAD:RW_PAD + MB_WIDTH]
        k_new = proj[:, RW_PAD + MB_WIDTH:RW_PAD + 2 * MB_WIDTH]
        v_new = proj[:, RW_PAD + 2 * MB_WIDTH:RW_PAD + 3 * MB_WIDTH]
        y_mb = _moba_paged(rel_bias, q, k_new, v_new, cache_k, cache_v, l, page_table)
        y_cv, conv_new = _conv_step(proj, state_conv[l], lp['cv_w'], lp['cv_b'], lp['cv_ln_w'], lp['cv_ln_b'], 8)
        merged = _merge(y_rw, y_mb, y_cv, proj, lp['w_br_rw'], lp['w_br_mb'], lp['w_br_cv'], b, False)
        h = _out_proj(h, merged, lp['w_out'], b)
        parts = _channel_mix(h, lp, b, b)
        outs['k'].append(k_new.reshape(b, 1, MB_HEADS, MB_HEAD_DIM))
        outs['v'].append(v_new.reshape(b, 1, MB_HEADS, MB_HEAD_DIM))
        outs['shift'].append(proj[:, :RW_COLS])
        outs['conv'].append(conv_new)
    y = _final_norm(parts, norm_final, b)
    stacked = {n: jnp.stack(v) for n, v in outs.items()}
    return (y[:, None, :], stacked['k'], stacked['v'], new_wkv_t.transpose(0, 4, 1, 2, 3), stacked['shift'],
            stacked['conv'])


def kernel(x_prompt, x_sample, cache_k, cache_v, state_wkv, state_shift, state_conv, page_table, norm_mix, norm_ffn, norm_final, w_in, rw_mu, rw_w0, rw_w2, rw_a0, rw_a2, rw_g2, rw_kk, rw_ka, rw_rk, rw_ln_w, rw_ln_b, rel_bias, cv_w, cv_b, cv_ln_w, cv_ln_b, w_br_rw, w_br_mb, w_br_cv, w_out, ffn_wg, ffn_wu, ffn_wd, moe_router, moe_router_b, moe_wg, moe_wu, moe_wd):
    P = dict(norm_mix=norm_mix, norm_ffn=norm_ffn, w_in=w_in,
             rw_mu=rw_mu, rw_w0=rw_w0, rw_w2=rw_w2, rw_a0=rw_a0, rw_a2=rw_a2, rw_g2=rw_g2,
             rw_kk=rw_kk, rw_ka=rw_ka, rw_rk=rw_rk, rw_ln_w=rw_ln_w, rw_ln_b=rw_ln_b,
             cv_w=cv_w, cv_b=cv_b, cv_ln_w=cv_ln_w, cv_ln_b=cv_ln_b,
             w_br_rw=w_br_rw, w_br_mb=w_br_mb, w_br_cv=w_br_cv, w_out=w_out,
             ffn_wg=ffn_wg, ffn_wu=ffn_wu, ffn_wd=ffn_wd,
             moe_router=moe_router, moe_router_b=moe_router_b, moe_wg=moe_wg, moe_wu=moe_wu, moe_wd=moe_wd)
    depth = w_in.shape[0]
    layers = [_layer_params(l, P) for l in range(depth)]
    g_final = norm_final.reshape(1, -1)
    assert x_prompt.shape[0] == 1 and x_sample.shape[1] == 1
    prompt = _run_prompt(x_prompt[0], layers, g_final, rel_bias)
    sample = _run_sample(x_sample[:, 0, :], layers, g_final, rel_bias, cache_k, cache_v,
                         state_wkv, state_shift, state_conv, page_table)
    y_p, k_p, v_p, wkv_p, shift_p, conv_p = prompt
    y_s, k_s, v_s, wkv_s, shift_s, conv_s = sample
    return (y_p, y_s, k_p, v_p, wkv_p, shift_p, conv_p,
            k_s, v_s, wkv_s, shift_s, conv_s)
```

```python
import functools
import math

import numpy as np
import jax
import jax.numpy as jnp
from jax import lax
from jax.experimental import pallas as pl
from jax.experimental.pallas import tpu as pltpu

F32 = jnp.float32
BF16 = jnp.bfloat16

D_MODEL = 2048
RW_HEAD_DIM = 64
RW_WIDTH = 1024
RW_HEADS = 16
DECAY_LORA = 64
AAA_LORA = 64
GATE_LORA = 160
RW_COLS = 3 * RW_WIDTH + DECAY_LORA + AAA_LORA + GATE_LORA
RW_PAD = 3584
LORA_PAD = RW_PAD - 3 * RW_WIDTH
GN_EPS = 64e-5
MB_HEAD_DIM = 64
MB_WIDTH = 512
MB_HEADS = 8
MB_BLOCK = 256
MB_TOPK = 3
NUM_BUCKETS = 32
MAX_DISTANCE = 128
NEG_INF = -1e30
CV_WIDTH = 512
CV_K = 31
LN_EPS = 1e-5
N_BRANCH = 3
N_EXPERTS = 8
TOP_K = 2
NORM_EPS = 1e-6
PAGE_SIZE = 128

PROJ_TILE = 1024
N_PROJ = RW_PAD + 3 * MB_WIDTH + 2 * CV_WIDTH + N_BRANCH * D_MODEL
COL_CV = (RW_PAD + 3 * MB_WIDTH) // (2 * CV_WIDTH)
COL_GATE = (RW_PAD + 3 * MB_WIDTH + 2 * CV_WIDTH) // D_MODEL
GATE_TILE0 = (RW_PAD + 3 * MB_WIDTH + 2 * CV_WIDTH) // PROJ_TILE

VMEM_LIMIT_BYTES = 56 * 1024 * 1024


def _params(*semantics):
    return pltpu.CompilerParams(dimension_semantics=semantics, vmem_limit_bytes=VMEM_LIMIT_BYTES)


def _sigmoid(x):
    return 1.0 / (1.0 + jnp.exp(-x))


def _silu(x):
    return x * _sigmoid(x)


def _split3(x):
    hi = x.astype(BF16)
    r1 = x - hi.astype(F32)
    mid = r1.astype(BF16)
    lo = (r1 - mid.astype(F32)).astype(BF16)
    return hi, mid, lo


_NT = (((1,), (1,)), ((), ()))
_TN = (((0,), (0,)), ((), ()))


def _dot(a, b):
    return jnp.dot(a, b, preferred_element_type=F32)


def _dot_exact_rhs(x, sel):
    hi, mid, lo = _split3(x)
    return _dot(hi, sel) + _dot(mid, sel) + _dot(lo, sel)


def _dot_f32(a, b):
    a_hi, a_mid, a_lo = _split3(a)
    b_hi, b_mid, b_lo = _split3(b)
    return (_dot(a_hi, b_hi) + (_dot(a_hi, b_mid) + _dot(a_mid, b_hi))
            + (_dot(a_mid, b_mid) + _dot(a_hi, b_lo) + _dot(a_lo, b_hi)))


def _rms_norm_bf16(x, g):
    ms = jnp.mean(x * x, axis=-1, keepdims=True)
    return (x * lax.rsqrt(ms + NORM_EPS) * g).astype(BF16)


def _norm_proj_kernel(x_ref, g_ref, wt_ref, o_ref, xn_ref):
    j = pl.program_id(1)

    @pl.when(j == 0)
    def _():
        xn_ref[...] = _rms_norm_bf16(x_ref[...], g_ref[...])

    acc = lax.dot_general(xn_ref[...], wt_ref[...], _NT, preferred_element_type=F32)

    @pl.when(j < GATE_TILE0)
    def _():
        o_ref[...] = acc

    @pl.when(j >= GATE_TILE0)
    def _():
        o_ref[...] = _sigmoid(acc)


def _norm_proj(x, g, wt, tm):
    m = x.shape[0]
    return pl.pallas_call(
        _norm_proj_kernel,
        grid=(m // tm, N_PROJ // PROJ_TILE),
        in_specs=[pl.BlockSpec((tm, D_MODEL), lambda i, j: (i, 0)),
                  pl.BlockSpec((1, D_MODEL), lambda i, j: (0, 0)),
                  pl.BlockSpec((PROJ_TILE, D_MODEL), lambda i, j: (j, 0))],
        out_specs=pl.BlockSpec((tm, PROJ_TILE), lambda i, j: (i, j)),
        out_shape=jax.ShapeDtypeStruct((m, N_PROJ), F32),
        scratch_shapes=[pltpu.VMEM((tm, D_MODEL), BF16)],
        compiler_params=_params("parallel", "arbitrary"),
        name="norm_proj",
    )(x, g, wt)


def _qkv_t_kernel(x_ref, g_ref, wq_ref, wk_ref, wv_ref, q_ref, k_ref, v_ref, kb_ref, vb_ref, km_ref):
    xn = _rms_norm_bf16(x_ref[...], g_ref[...])
    nt = lambda w_ref: lax.dot_general(w_ref[...], xn, _NT, preferred_element_type=F32)
    q_ref[...] = nt(wq_ref)
    k = nt(wk_ref)
    v = nt(wv_ref)
    k_ref[...] = k
    v_ref[...] = v
    kb_ref[...] = k.astype(BF16)
    vb_ref[...] = v.astype(BF16)
    n_blk = k.shape[1] // MB_BLOCK
    sums = [jnp.sum(k[:, b * MB_BLOCK:(b + 1) * MB_BLOCK], axis=-1, keepdims=True) for b in range(n_blk)]
    km_ref[0] = jnp.concatenate(sums, axis=-1) * (1.0 / MB_BLOCK)


def _qkv_t(x, g, wt, tm):
    t = x.shape[0]
    q_row = RW_PAD // MB_WIDTH
    w_spec = lambda r: pl.BlockSpec((MB_WIDTH, D_MODEL), lambda i: (q_row + r, 0))
    col = pl.BlockSpec((MB_WIDTH, tm), lambda i: (0, i))
    f32_t = jax.ShapeDtypeStruct((MB_WIDTH, t), F32)
    bf16_t = jax.ShapeDtypeStruct((MB_WIDTH, t), BF16)
    return pl.pallas_call(
        _qkv_t_kernel,
        grid=(t // tm,),
        in_specs=[pl.BlockSpec((tm, D_MODEL), lambda i: (i, 0)), pl.BlockSpec((1, D_MODEL), lambda i: (0, 0)),
                  w_spec(0), w_spec(1), w_spec(2)],
        out_specs=[col] * 5 + [pl.BlockSpec((1, MB_WIDTH, tm // MB_BLOCK), lambda i: (i, 0, 0))],
        out_shape=[f32_t] * 3 + [bf16_t] * 2 + [jax.ShapeDtypeStruct((t // tm, MB_WIDTH, tm // MB_BLOCK), F32)],
        compiler_params=_params("parallel"),
        name="qkv_t",
    )(x, g, wt, wt, wt)


def _rwkv_prep_kernel(sequential, *refs):
    if sequential:
        (p_ref, mu_ref, w0_ref, a0_ref, kk_ref, ka_ref, rk_ref, w2_ref, a2_ref, g2_ref,
         r_o, w_o, k_o, v_o, kn_o, b_o, g_o, bonus_o, carry_ref) = refs
    else:
        (p_ref, prev_ref, mu_ref, w0_ref, a0_ref, kk_ref, ka_ref, rk_ref, w2_ref, a2_ref, g2_ref,
         r_o, w_o, k_o, v_o, kn_o, b_o, g_o, bonus_o) = refs
    p = p_ref[...]
    tm = p.shape[0]
    if sequential:
        @pl.when(pl.program_id(0) == 0)
        def _():
            carry_ref[...] = jnp.zeros_like(carry_ref)

        rows = lax.broadcasted_iota(jnp.int32, p.shape, 0)
        prev = jnp.where(rows == 0, carry_ref[...], pltpu.roll(p, 1, 0))
        carry_ref[...] = p[tm - 1:tm, :]
    else:
        prev = prev_ref[...]
    xs = p + (prev - p) * mu_ref[...]
    r = xs[:, 0:RW_WIDTH]
    k = xs[:, RW_WIDTH:2 * RW_WIDTH]
    v = xs[:, 2 * RW_WIDTH:3 * RW_WIDTH]
    lora = xs[:, 3 * RW_WIDTH:RW_PAD]
    wx = w0_ref[...] + _dot(jnp.tanh(lora).astype(BF16), w2_ref[...])
    w_log = jnp.minimum(wx, 0.0) - jnp.log(1.0 + jnp.exp(-jnp.abs(wx))) - 0.5
    log_decay = -jnp.exp(w_log)
    a = _sigmoid(a0_ref[...] + _dot(lora.astype(BF16), a2_ref[...]))
    g_o[...] = _dot(_sigmoid(lora).astype(BF16), g2_ref[...])
    kk = k * kk_ref[...]
    k_mod = k * (1.0 + (a - 1.0) * ka_ref[...])
    rkr = r * k_mod * rk_ref[...]
    kn_parts = []
    for h in range(RW_HEADS):
        sl = slice(h * RW_HEAD_DIM, (h + 1) * RW_HEAD_DIM)
        kk_h = kk[:, sl]
        nrm = jnp.sqrt(jnp.sum(kk_h * kk_h, axis=-1, keepdims=True))
        kn_parts.append(kk_h / jnp.maximum(nrm, 1e-12))
        bonus_o[:, sl] = jnp.sum(rkr[:, sl], axis=-1, keepdims=True) * v[:, sl]
    kn = jnp.concatenate(kn_parts, axis=-1)
    if sequential:
        ops = (r, log_decay, k_mod, v, kn, kn * a)
    else:
        ops = tuple(x.T for x in (r, jnp.exp(log_decay), k_mod, v, kn, kn * a))
    for o_ref, x in zip((r_o, w_o, k_o, v_o, kn_o, b_o), ops):
        o_ref[...] = x


def _rwkv_prep(proj, prev, vecs, mats, tm):
    m = proj.shape[0]
    sequential = prev is None
    row = lambda n: pl.BlockSpec((1, n), lambda i: (0, 0))
    in_specs = [pl.BlockSpec((tm, RW_PAD), lambda i: (i, 0))]
    args = [proj]
    if not sequential:
        in_specs.append(pl.BlockSpec((tm, RW_PAD), lambda i: (i, 0)))
        args.append(prev)
    in_specs += [row(RW_PAD)] + [row(RW_WIDTH)] * 5
    in_specs += [pl.BlockSpec((LORA_PAD, RW_WIDTH), lambda i: (0, 0))] * 3
    flat = jax.ShapeDtypeStruct((m, RW_WIDTH), F32)
    flat_spec = pl.BlockSpec((tm, RW_WIDTH), lambda i: (i, 0))
    if sequential:
        op_shape, op_spec = flat, flat_spec
    else:
        op_shape = jax.ShapeDtypeStruct((RW_WIDTH, m), F32)
        op_spec = pl.BlockSpec((RW_WIDTH, tm), lambda i: (0, i))
    return pl.pallas_call(
        functools.partial(_rwkv_prep_kernel, sequential),
        grid=(m // tm,),
        in_specs=in_specs,
        out_specs=[op_spec] * 6 + [flat_spec] * 2,
        out_shape=[op_shape] * 6 + [flat] * 2,
        scratch_shapes=[pltpu.VMEM((1, RW_PAD), F32)] if sequential else [],
        compiler_params=_params("arbitrary"),
        name="rwkv_prep",
    )(*args, *vecs, *mats)


RW_CHUNK = 64
RW_CHUNKS_PER_STEP = 2


def _group_norm(y):
    mean = jnp.mean(y, axis=-1, keepdims=True)
    var = jnp.mean(jnp.square(y - mean), axis=-1, keepdims=True)
    return (y - mean) * lax.rsqrt(var + GN_EPS)


def _rwkv_chunk_kernel(r_ref, lw_ref, k_ref, v_ref, kn_ref, b_ref, bonus_ref, g_ref, lnw_ref, lnb_ref,
                       y_ref, s_out_ref, h_ref):
    c = RW_CHUNK
    n = RW_HEAD_DIM

    @pl.when(pl.program_id(0) == 0)
    def _():
        h_ref[...] = jnp.zeros_like(h_ref)

    row = lax.broadcasted_iota(jnp.int32, (c, c), 0)
    col = lax.broadcasted_iota(jnp.int32, (c, c), 1)
    eye = jnp.where(row == col, 1.0, 0.0)
    lower_ones = jnp.where(col <= row, 1.0, 0.0).astype(BF16)
    row2 = lax.broadcasted_iota(jnp.int32, (2 * c, 2 * c), 0)
    col2 = lax.broadcasted_iota(jnp.int32, (2 * c, 2 * c), 1)
    t_idx = jnp.where(row2 >= c, row2 - c, row2)
    s_idx = jnp.where(col2 >= c, col2 - c, col2)
    mask = s_idx < t_idx + jnp.where(row2 >= c, 1, 0)
    heads = range(RW_HEADS)
    sls = [slice(h * n, (h + 1) * n) for h in heads]
    chunks = [slice(ci * c, (ci + 1) * c) for ci in range(r_ref.shape[0] // c)]

    pre = []
    for rows in chunks:
        lw = lw_ref[rows, :]
        lw_hi, lw_mid, lw_lo = _split3(lw)
        cum = _dot(lower_ones, lw_hi) + _dot(lower_ones, lw_mid) + _dot(lower_ones, lw_lo)
        total = cum[c - 1:c, :]
        k = k_ref[rows, :]
        b = b_ref[rows, :]
        v = v_ref[rows, :]
        g_inv = jnp.exp(-cum)
        g_rest = jnp.exp(total - cum)
        pre.append(dict(
            qr=jnp.concatenate([kn_ref[rows, :] * jnp.exp(cum - lw), r_ref[rows, :] * jnp.exp(cum)],
                               axis=0).astype(BF16),
            kdbd=jnp.concatenate([k * g_inv, b * g_inv], axis=0).astype(BF16),
            zk=jnp.concatenate([k * g_rest, b * g_rest], axis=0).astype(BF16),
            v=v, vb=v.astype(BF16), decay_total=jnp.exp(total)))
    pairs = [(p, sl) for p in pre for sl in sls]
    gms = [jnp.where(mask, lax.dot_general(p['qr'][:, sl], p['kdbd'][:, sl], _NT, preferred_element_type=F32), 0.0)
           for p, sl in pairs]
    akkv = [_dot(gm[0:c, 0:c].astype(BF16), p['vb'][:, sl]) for gm, (p, sl) in zip(gms, pairs)]
    powers = [gm[0:c, c:2 * c].astype(BF16) for gm in gms]
    t_invs = [eye - gm[0:c, c:2 * c] for gm in gms]
    for _ in range(int(math.log2(c)) - 1):
        powers = [_dot(pw, pw).astype(BF16) for pw in powers]
        t_invs = [t + _dot(pw, t.astype(BF16)) for pw, t in zip(powers, t_invs)]

    for ci, rows in enumerate(chunks):
        p = pre[ci]
        mine = slice(ci * RW_HEADS, (ci + 1) * RW_HEADS)
        h_old = [h_ref[h] for h in heads]
        qhs = [_dot(p['qr'][:, sl], ho.astype(BF16)) for sl, ho in zip(sls, h_old)]
        us = [_dot(t.astype(BF16), (qh[0:c] + av).astype(BF16))
              for t, qh, av in zip(t_invs[mine], qhs, akkv[mine])]
        vus = [jnp.concatenate([p['v'][:, sl], -u], axis=0).astype(BF16) for sl, u in zip(sls, us)]
        outs = [_group_norm(qh[c:2 * c] + _dot(gm[c:2 * c, :].astype(BF16), vu))
                for qh, gm, vu in zip(qhs, gms[mine], vus)]
        for h in heads:
            decay_col = jnp.sum(eye * p['decay_total'][:, sls[h]], axis=-1, keepdims=True)
            h_ref[h] = decay_col * h_old[h] + lax.dot_general(p['zk'][:, sls[h]], vus[h], _TN,
                                                              preferred_element_type=F32)
        yn = jnp.concatenate(outs, axis=-1)
        y_ref[rows, :] = ((yn * lnw_ref[...] + lnb_ref[...] + bonus_ref[rows, :]) * g_ref[rows, :]).astype(BF16)

    @pl.when(pl.program_id(0) == pl.num_programs(0) - 1)
    def _():
        for h in range(RW_HEADS):
            s_out_ref[h] = h_ref[h].T


def _rwkv_chunked(ops, bonus, g, ln_w, ln_b):
    m = bonus.shape[0]
    step_rows = RW_CHUNK * RW_CHUNKS_PER_STEP
    spec = pl.BlockSpec((step_rows, RW_WIDTH), lambda i: (i, 0))
    row = pl.BlockSpec((1, RW_WIDTH), lambda i: (0, 0))
    state_shape = (RW_HEADS, RW_HEAD_DIM, RW_HEAD_DIM)
    return pl.pallas_call(
        _rwkv_chunk_kernel,
        grid=(m // step_rows,),
        in_specs=[spec] * 8 + [row, row],
        out_specs=[spec, pl.BlockSpec(state_shape, lambda i: (0, 0, 0))],
        out_shape=[jax.ShapeDtypeStruct((m, RW_WIDTH), BF16), jax.ShapeDtypeStruct(state_shape, F32)],
        scratch_shapes=[pltpu.VMEM(state_shape, F32)],
        compiler_params=_params("arbitrary"),
        name="rwkv_chunked",
    )(*ops, bonus, g, ln_w, ln_b)


RW_STEP_HEADS = 2


def _rwkv_single_step_kernel(r_ref, w_ref, k_ref, v_ref, kn_ref, b_ref, s0_ref, *rest):
    y_ref, s_out_ref = rest[-2:]
    n = RW_HEAD_DIM
    for hh in range(s0_ref.shape[0]):
        rows = slice(hh * n, (hh + 1) * n)
        r, w, k, kn, b = (ref[rows, :] for ref in (r_ref, w_ref, k_ref, kn_ref, b_ref))

        def body(vi, carry):
            s = s0_ref[hh, vi]
            sa = jnp.sum(s * kn, axis=0, keepdims=True)
            s = s * w - sa * b + v_ref[pl.ds(hh * n + vi, 1), :] * k
            s_out_ref[hh, vi] = s
            y_ref[pl.ds(hh * n + vi, 1), :] = jnp.sum(s * r, axis=0, keepdims=True)
            return carry

        lax.fori_loop(0, n, body, 0)


def _rwkv_single_step(ops, states_t, layer, new_states_t):
    b = ops[0].shape[1]
    hs = RW_STEP_HEADS
    spec = pl.BlockSpec((hs * RW_HEAD_DIM, b), lambda i: (i, 0))
    st_spec = pl.BlockSpec((None, hs, RW_HEAD_DIM, RW_HEAD_DIM, b), lambda i: (layer, i, 0, 0, 0))
    carried = [] if new_states_t is None else [new_states_t]
    return pl.pallas_call(
        _rwkv_single_step_kernel,
        grid=(RW_HEADS // hs,),
        in_specs=[spec] * 6 + [st_spec] + [pl.BlockSpec(memory_space=pl.ANY)] * len(carried),
        out_specs=[spec, st_spec],
        out_shape=[jax.ShapeDtypeStruct((RW_WIDTH, b), F32), jax.ShapeDtypeStruct(states_t.shape, F32)],
        input_output_aliases={7: 1} if carried else {},
        compiler_params=_params("parallel"),
        name="rwkv_single_step",
    )(*ops, states_t, *carried)


def _rwkv_post_kernel(yt_ref, bonus_ref, g_ref, lnw_ref, lnb_ref, o_ref):
    y = yt_ref[...].T
    n = RW_HEAD_DIM
    yn = jnp.concatenate([_group_norm(y[:, h * n:(h + 1) * n]) for h in range(RW_HEADS)], axis=-1)
    out = (yn * lnw_ref[...] + lnb_ref[...] + bonus_ref[...]) * g_ref[...]
    o_ref[...] = out.astype(BF16)


def _rwkv_post(y_t, bonus, g, ln_w, ln_b):
    b = bonus.shape[0]
    flat_spec = pl.BlockSpec((b, RW_WIDTH), lambda i: (0, 0))
    row = pl.BlockSpec((1, RW_WIDTH), lambda i: (0, 0))
    return pl.pallas_call(
        _rwkv_post_kernel,
        grid=(1,),
        in_specs=[pl.BlockSpec((RW_WIDTH, b), lambda i: (0, 0)), flat_spec, flat_spec, row, row],
        out_specs=flat_spec,
        out_shape=jax.ShapeDtypeStruct((b, RW_WIDTH), BF16),
        compiler_params=_params("arbitrary"),
        name="rwkv_post",
    )(y_t, bonus, g, ln_w, ln_b)


CV_HALO = 32


def _conv_finish(y, cb_ref, lnw_ref, lnb_ref):
    y = y + cb_ref[...]
    mean = jnp.mean(y, axis=-1, keepdims=True)
    var = jnp.mean(jnp.square(y - mean), axis=-1, keepdims=True)
    yn = (y - mean) * lax.rsqrt(var + LN_EPS) * lnw_ref[...] + lnb_ref[...]
    return _silu(yn).astype(BF16)


def _conv_seq_kernel(p_ref, cw_ref, cb_ref, lnw_ref, lnb_ref, o_ref, tail_ref, ext_ref):
    tm = p_ref.shape[0]
    pad = CV_HALO - (CV_K - 1)

    @pl.when(pl.program_id(0) == 0)
    def _():
        ext_ref[0:CV_HALO, :] = jnp.zeros((CV_HALO, CV_WIDTH), F32)

    p = p_ref[...]
    u = p[:, 0:CV_WIDTH] * _sigmoid(p[:, CV_WIDTH:2 * CV_WIDTH])
    ext_ref[CV_HALO:CV_HALO + tm, :] = u
    acc = jnp.zeros((tm, CV_WIDTH), F32)
    for j in range(CV_K):
        acc = acc + ext_ref[pad + j:pad + j + tm, :] * cw_ref[j:j + 1, :]
    o_ref[...] = _conv_finish(acc, cb_ref, lnw_ref, lnb_ref)
    tail = ext_ref[tm:tm + CV_HALO, :]
    ext_ref[0:CV_HALO, :] = tail
    tail_ref[...] = tail


def _conv_seq(proj, cw, cb, ln_w, ln_b, tm):
    m = proj.shape[0]
    row = pl.BlockSpec((1, CV_WIDTH), lambda i: (0, 0))
    return pl.pallas_call(
        _conv_seq_kernel,
        grid=(m // tm,),
        in_specs=[pl.BlockSpec((tm, 2 * CV_WIDTH), lambda i: (i, COL_CV)),
                  pl.BlockSpec((CV_K, CV_WIDTH), lambda i: (0, 0)), row, row, row],
        out_specs=[pl.BlockSpec((tm, CV_WIDTH), lambda i: (i, 0)),
                   pl.BlockSpec((CV_HALO, CV_WIDTH), lambda i: (0, 0))],
        out_shape=[jax.ShapeDtypeStruct((m, CV_WIDTH), BF16),
                   jax.ShapeDtypeStruct((CV_HALO, CV_WIDTH), F32)],
        scratch_shapes=[pltpu.VMEM((CV_HALO + tm, CV_WIDTH), F32)],
        compiler_params=_params("arbitrary"),
        name="conv_seq",
    )(proj, cw, cb, ln_w, ln_b)


def _conv_step_kernel(p_ref, buf_ref, cw_ref, cb_ref, lnw_ref, lnb_ref, o_ref, nbuf_ref):
    p = p_ref[...]
    u = p[:, :, 0:CV_WIDTH] * _sigmoid(p[:, :, CV_WIDTH:2 * CV_WIDTH])
    buf = buf_ref[...]
    acc = (jnp.sum(buf * cw_ref[0:CV_K - 1, :][None], axis=1, keepdims=True)
           + u * cw_ref[CV_K - 1:CV_K, :][None])
    o_ref[...] = _conv_finish(acc, cb_ref, lnw_ref, lnb_ref)
    nbuf_ref[:, 0:CV_K - 2, :] = buf[:, 1:CV_K - 1, :]
    nbuf_ref[:, CV_K - 2:CV_K - 1, :] = u


def _conv_step(proj, buf, cw, cb, ln_w, ln_b, tb):
    m = proj.shape[0]
    row = pl.BlockSpec((1, CV_WIDTH), lambda i: (0, 0))
    buf_spec = pl.BlockSpec((tb, CV_K - 1, CV_WIDTH), lambda i: (i, 0, 0))
    y = pl.pallas_call(
        _conv_step_kernel,
        grid=(m // tb,),
        in_specs=[pl.BlockSpec((tb, 1, 2 * CV_WIDTH), lambda i: (i, 0, COL_CV)), buf_spec,
                  pl.BlockSpec((CV_K, CV_WIDTH), lambda i: (0, 0)), row, row, row],
        out_specs=[pl.BlockSpec((tb, 1, CV_WIDTH), lambda i: (i, 0, 0)), buf_spec],
        out_shape=[jax.ShapeDtypeStruct((m, 1, CV_WIDTH), BF16),
                   jax.ShapeDtypeStruct(buf.shape, F32)],
        compiler_params=_params("parallel"),
        name="conv_step",
    )(proj.reshape(m, 1, N_PROJ), buf, cw, cb, ln_w, ln_b)
    return y[0].reshape(m, CV_WIDTH), y[1]


def _bucket_thresholds():
    n = np.arange(0, 4 * MAX_DISTANCE, dtype=np.int64)
    max_exact = NUM_BUCKETS // 2
    nf = np.maximum(n, 1).astype(np.float32)
    large = max_exact + (np.log(nf / np.float32(max_exact)) / np.float32(math.log(MAX_DISTANCE / max_exact))
                         * np.float32(NUM_BUCKETS - max_exact)).astype(np.int32)
    large = np.minimum(large, NUM_BUCKETS - 1)
    bucket = np.where(n < max_exact, n, large)
    assert np.all(np.diff(bucket) >= 0) and bucket[-1] == NUM_BUCKETS - 1
    return [int(np.argmax(bucket >= b)) for b in range(NUM_BUCKETS)]


_BUCKET_THR = _bucket_thresholds()


def _bias_from_distance(n, rb_of_bucket):
    bias = jnp.zeros(n.shape, F32) + rb_of_bucket(NUM_BUCKETS - 1)
    for b in range(NUM_BUCKETS - 2, -1, -1):
        bias = jnp.where(n < _BUCKET_THR[b + 1], rb_of_bucket(b), bias)
    return bias


def _rank_select(scores, n_rows, limit):
    idx = lax.broadcasted_iota(jnp.int32, scores.shape, 0)
    beaten = jnp.zeros(scores.shape, F32)
    for j in range(n_rows):
        row = scores[j:j + 1, :]
        ahead = jnp.where(row > scores, 1.0, jnp.where(row == scores, jnp.where(idx > j, 1.0, 0.0), 0.0))
        beaten = beaten + ahead * jnp.where(j < limit, 1.0, 0.0)
    return jnp.where(idx < limit, jnp.where(beaten < MB_TOPK, 1.0, 0.0), 0.0)


MB_HEADS_PER_STEP = 4
MB_SUM_ROWS = 16
MB_BLOCKS_PER_TRIP = 2


def _softmax_blocks(carries, scores, vt_blks):
    m_new = [jnp.maximum(c[0], jnp.max(s, axis=0, keepdims=True)) for c, s in zip(carries, scores)]
    alpha = [jnp.exp2(c[0] - mn) for c, mn in zip(carries, m_new)]
    p = [jnp.exp2(s - mn).astype(BF16) for s, mn in zip(scores, m_new)]
    pv = [_dot(_with_ones_rows(vt), pp) for vt, pp in zip(vt_blks, p)]
    return [(mn, a * c[1] + x) for mn, a, c, x in zip(m_new, alpha, carries, pv)]


def _with_ones_rows(vt_blk):
    return jnp.concatenate([vt_blk, jnp.ones((MB_SUM_ROWS, vt_blk.shape[1]), BF16)], axis=0)


def _moba_seq_kernel(rb_ref, qt_ref, k_ref, vt_ref, km_ref, o_ref, bias_ref, rowterm_ref):
    hp = qt_ref.shape[0]
    h0 = pl.program_id(0) * hp
    qi = pl.program_id(1)
    nb = km_ref.shape[1]
    log2e = math.log2(math.e)
    scale = MB_HEAD_DIM ** -0.5 * log2e
    kidx = lax.broadcasted_iota(jnp.int32, (MB_BLOCK, MB_BLOCK), 0)
    qidx = lax.broadcasted_iota(jnp.int32, (MB_BLOCK, MB_BLOCK), 1)

    @pl.when(qi == 0)
    def _():
        dist = qidx - kidx
        for hh in range(hp):
            rb = lambda b: rb_ref[b, h0 + hh] * log2e
            bias_ref[hh, 0] = jnp.where(dist >= 0, _bias_from_distance(jnp.maximum(dist, 0), rb), NEG_INF)
            bias_ref[hh, 1] = _bias_from_distance(dist + MB_BLOCK, rb)

    near = jnp.maximum(qi - 1, 0)
    qtb, carries = [], []
    for hh in range(hp):
        qt = qt_ref[hh]
        block_scores = _dot_f32(km_ref[hh], qt)
        sel = _rank_select(block_scores, nb, qi)
        rowterm_ref[hh] = jnp.where(sel > 0.0, rb_ref[NUM_BUCKETS - 1, h0 + hh] * log2e, NEG_INF)
        qtb.append((qt * scale).astype(BF16))
        s = _dot(k_ref[hh, qi], qtb[hh]) + bias_ref[hh, 0]
        m0 = jnp.max(s, axis=0, keepdims=True)
        p = jnp.exp2(s - m0).astype(BF16)
        carries.append((m0, _dot(_with_ones_rows(vt_ref[hh, qi]), p)))

    heads = range(hp)
    sel_near = [jnp.where(rowterm_ref[hh, pl.ds(near, 1), :] > 0.5 * NEG_INF, 0.0, NEG_INF) for hh in heads]
    scores = [_dot(k_ref[hh, near], qtb[hh]) + bias_ref[hh, 1] + sel_near[hh] for hh in heads]
    carries = _softmax_blocks(carries, scores, [vt_ref[hh, near] for hh in heads])

    def body(t, carry):
        kjs = [t * MB_BLOCKS_PER_TRIP + u for u in range(MB_BLOCKS_PER_TRIP)]
        safe = [jnp.minimum(kj, nb - 1) for kj in kjs]
        raw = [[_dot(k_ref[hh, kc], qtb[hh]) for hh in heads] for kc in safe]
        for u, (kj, kc) in enumerate(zip(kjs, safe)):
            scores = [raw[u][hh] + jnp.where(kj < near, rowterm_ref[hh, pl.ds(kc, 1), :], NEG_INF) for hh in heads]
            carry = _softmax_blocks(carry, scores, [vt_ref[hh, kc] for hh in heads])
        return tuple(carry)

    n_trips = (near + MB_BLOCKS_PER_TRIP - 1) // MB_BLOCKS_PER_TRIP
    carries = lax.fori_loop(0, n_trips, body, tuple(carries))
    for hh in range(hp):
        acc = carries[hh][1]
        o_ref[hh] = (acc[0:MB_HEAD_DIM] / acc[MB_HEAD_DIM:MB_HEAD_DIM + 1]).astype(BF16)


def _moba_seq(rel_bias, q_t, k_t, v_t, km_tiles):
    t = q_t.shape[1]
    nb = t // MB_BLOCK
    hp = MB_HEADS_PER_STEP
    blocks = lambda x: x.reshape(MB_HEADS, MB_HEAD_DIM, nb, MB_BLOCK)
    qt = q_t.reshape(MB_HEADS, MB_HEAD_DIM, t)
    kb = blocks(k_t).transpose(0, 2, 3, 1)
    vt = blocks(v_t).transpose(0, 2, 1, 3)
    km = km_tiles.transpose(1, 0, 2).reshape(MB_HEADS, MB_HEAD_DIM, nb).transpose(0, 2, 1)
    ot = pl.pallas_call(
        _moba_seq_kernel,
        grid=(MB_HEADS // hp, nb),
        in_specs=[pl.BlockSpec(memory_space=pltpu.SMEM),
                  pl.BlockSpec((hp, MB_HEAD_DIM, MB_BLOCK), lambda h, i: (h, 0, i)),
                  pl.BlockSpec((hp, nb, MB_BLOCK, MB_HEAD_DIM), lambda h, i: (h, 0, 0, 0)),
                  pl.BlockSpec((hp, nb, MB_HEAD_DIM, MB_BLOCK), lambda h, i: (h, 0, 0, 0)),
                  pl.BlockSpec((hp, nb, MB_HEAD_DIM), lambda h, i: (h, 0, 0))],
        out_specs=pl.BlockSpec((hp, MB_HEAD_DIM, MB_BLOCK), lambda h, i: (h, 0, i)),
        out_shape=jax.ShapeDtypeStruct((MB_HEADS, MB_HEAD_DIM, t), BF16),
        scratch_shapes=[pltpu.VMEM((hp, 2, MB_BLOCK, MB_BLOCK), F32), pltpu.VMEM((hp, nb, MB_BLOCK), F32)],
        compiler_params=_params("parallel", "arbitrary"),
        name="moba_seq",
    )(rel_bias, qt, kb, vt, km)
    return ot.reshape(MB_WIDTH, t)


def _moba_paged_kernel(n_pages, pt_ref, q_ref, kn_ref, vn_ref, rbt_ref, *refs):
    k_pages = refs[:n_pages]
    v_pages = refs[n_pages:2 * n_pages]
    o_ref, bias_ref = refs[2 * n_pages:]
    past = n_pages * PAGE_SIZE
    n_blocks = past // MB_BLOCK
    pages_per_block = MB_BLOCK // PAGE_SIZE
    scale = MB_HEAD_DIM ** -0.5
    rb = lambda b: rbt_ref[:, :, b:b + 1]

    @pl.when(pl.program_id(0) == 0)
    def _():
        dist = PAGE_SIZE - lax.broadcasted_iota(jnp.int32, (MB_HEADS, 1, PAGE_SIZE), 2)
        bias_ref[...] = _bias_from_distance(dist, rb)

    di = lax.broadcasted_iota(jnp.int32, (MB_HEAD_DIM, MB_HEAD_DIM), 0)
    dj = lax.broadcasted_iota(jnp.int32, (MB_HEAD_DIM, MB_HEAD_DIM), 1)
    eye = jnp.where(di == dj, 1.0, 0.0)[None]
    q = q_ref[0]
    q_col = jnp.sum(eye * q, axis=-1, keepdims=True)
    raw = [jnp.sum(k_pages[pg][...] * q_col, axis=1, keepdims=True) for pg in range(n_pages)]
    scores = []
    for j in range(n_blocks):
        tot = raw[pages_per_block * j]
        for e in range(1, pages_per_block):
            tot = tot + raw[pages_per_block * j + e]
        scores.append(jnp.sum(tot, axis=-1, keepdims=True) * (1.0 / MB_BLOCK))
    far_bias = rb(NUM_BUCKETS - 1)
    rowterms = []
    for j in range(n_blocks):
        beaten = jnp.zeros((MB_HEADS, 1, 1), F32)
        for i in range(n_blocks):
            if i != j:
                ahead = scores[i] > scores[j] if i > j else scores[i] >= scores[j]
                beaten = beaten + jnp.where(ahead, 1.0, 0.0)
        rowterms.append(jnp.where(beaten < MB_TOPK, far_bias, NEG_INF))
    lg_own = jnp.sum(kn_ref[0] * q, axis=-1, keepdims=True) * scale + rb(0)
    logits = []
    for pg in range(n_pages):
        term = rowterms[pg // pages_per_block]
        if pg == n_pages - 1:
            term = bias_ref[...] + jnp.where(term > 0.5 * NEG_INF, 0.0, NEG_INF)
        logits.append(raw[pg] * scale + term)
    m_row = logits[0]
    for lg in logits[1:]:
        m_row = jnp.maximum(m_row, lg)
    m = jnp.maximum(jnp.max(m_row, axis=-1, keepdims=True), lg_own)
    p_own = jnp.exp(lg_own - m)
    l_row = jnp.zeros((MB_HEADS, 1, PAGE_SIZE), F32)
    acc = jnp.zeros((MB_HEADS, MB_HEAD_DIM, PAGE_SIZE), F32)
    for pg in range(n_pages):
        p = jnp.exp(logits[pg] - m)
        l_row = l_row + p
        acc = acc + v_pages[pg][...] * p
    out_col = jnp.sum(acc, axis=-1, keepdims=True)
    out = jnp.sum(eye * out_col, axis=1, keepdims=True) + p_own * vn_ref[0]
    l = jnp.sum(l_row, axis=-1, keepdims=True) + p_own
    o_ref[0] = (out / l).astype(BF16)


def _moba_paged(rel_bias, q, k_new, v_new, cache_k, cache_v, layer, page_table):
    b, n_pages = page_table.shape
    assert PAGE_SIZE >= _BUCKET_THR[NUM_BUCKETS - 1] and MB_BLOCK % PAGE_SIZE == 0
    head4 = pl.BlockSpec((1, MB_HEADS, 1, MB_HEAD_DIM), lambda i, pt: (i, 0, 0, 0))

    def page_spec(pg):
        return pl.BlockSpec((None, None, MB_HEADS, MB_HEAD_DIM, PAGE_SIZE),
                            lambda i, pt: (layer, pt[i, pg], 0, 0, 0))

    grid_spec = pltpu.PrefetchScalarGridSpec(
        num_scalar_prefetch=1,
        grid=(b,),
        in_specs=[head4, head4, head4, pl.BlockSpec((MB_HEADS, 1, NUM_BUCKETS), lambda i, pt: (0, 0, 0))]
        + [page_spec(pg) for pg in range(n_pages)] * 2,
        out_specs=head4,
        scratch_shapes=[pltpu.VMEM((MB_HEADS, 1, PAGE_SIZE), F32)],
    )
    h4 = lambda x: x.reshape(b, MB_HEADS, 1, MB_HEAD_DIM)
    kt = cache_k.transpose(0, 1, 3, 4, 2)
    vt = cache_v.transpose(0, 1, 3, 4, 2)
    out = pl.pallas_call(
        functools.partial(_moba_paged_kernel, n_pages),
        grid_spec=grid_spec,
        out_shape=jax.ShapeDtypeStruct((b, MB_HEADS, 1, MB_HEAD_DIM), BF16),
        compiler_params=_params("arbitrary"),
        name="moba_paged",
    )(page_table, h4(q), h4(k_new), h4(v_new), rel_bias.T.reshape(MB_HEADS, 1, NUM_BUCKETS),
      *([kt] * n_pages), *([vt] * n_pages))
    return out.reshape(b, MB_WIDTH)


def _merge_kernel(mb_feature_major, yrw_ref, ymb_ref, ycv_ref, g0_ref, g1_ref, g2_ref, wrw_ref, wmb_ref, wcv_ref,
                  o_ref):
    if mb_feature_major:
        y_mb = lax.dot_general(ymb_ref[...], wmb_ref[...], _TN, preferred_element_type=F32)
    else:
        y_mb = _dot(ymb_ref[...], wmb_ref[...])
    merged = (g0_ref[...] * _dot(yrw_ref[...], wrw_ref[...]) + g1_ref[...] * y_mb
              + g2_ref[...] * _dot(ycv_ref[...], wcv_ref[...]))
    o_ref[...] = merged.astype(BF16)


def _merge(y_rw, y_mb, y_cv, proj, w_rw, w_mb, w_cv, tm, mb_feature_major):
    m = proj.shape[0]
    rows = lambda n: pl.BlockSpec((tm, n), lambda i: (i, 0))
    mb_spec = pl.BlockSpec((MB_WIDTH, tm), lambda i: (0, i)) if mb_feature_major else rows(MB_WIDTH)
    gate = lambda br: pl.BlockSpec((tm, D_MODEL), lambda i: (i, COL_GATE + br))
    full = lambda n: pl.BlockSpec((n, D_MODEL), lambda i: (0, 0))
    return pl.pallas_call(
        functools.partial(_merge_kernel, mb_feature_major),
        grid=(m // tm,),
        in_specs=[rows(RW_WIDTH), mb_spec, rows(CV_WIDTH), gate(0), gate(1), gate(2),
                  full(RW_WIDTH), full(MB_WIDTH), full(CV_WIDTH)],
        out_specs=rows(D_MODEL),
        out_shape=jax.ShapeDtypeStruct((m, D_MODEL), BF16),
        compiler_params=_params("parallel"),
        name="branch_merge",
    )(y_rw, y_mb, y_cv, proj, proj, proj, w_rw, w_mb, w_cv)


def _out_proj_kernel(h_ref, x_ref, w_ref, o_ref):
    o_ref[...] = h_ref[...] + _dot(x_ref[...], w_ref[...])


def _out_proj(h, merged, w_out, tm):
    m = h.shape[0]
    rows = pl.BlockSpec((tm, D_MODEL), lambda i: (i, 0))
    return pl.pallas_call(
        _out_proj_kernel,
        grid=(m // tm,),
        in_specs=[rows, rows, pl.BlockSpec((D_MODEL, D_MODEL), lambda i: (0, 0))],
        out_specs=rows,
        out_shape=jax.ShapeDtypeStruct((m, D_MODEL), F32),
        compiler_params=_params("parallel"),
        name="out_proj",
    )(h, merged, w_out)


def _ffn_kernel(h_ref, g_ref, wg_ref, wu_ref, wd_ref, o_ref, xn_ref):
    @pl.when(pl.program_id(1) == 0)
    def _():
        x = h_ref[...]
        ms = jnp.mean(x * x, axis=-1, keepdims=True)
        xn_ref[...] = (x * lax.rsqrt(ms + NORM_EPS) * g_ref[...]).astype(BF16)
        o_ref[...] = x

    xn = xn_ref[...]
    mid = _silu(_dot(xn, wg_ref[...])) * _dot(xn, wu_ref[...])
    o_ref[...] += _dot(mid.astype(BF16), wd_ref[...])


def _ffn(h, g, wg, wu, wd, tm, tf):
    m = h.shape[0]
    d_ff = wg.shape[1]
    rows = pl.BlockSpec((tm, D_MODEL), lambda i, f: (i, 0))
    return pl.pallas_call(
        _ffn_kernel,
        grid=(m // tm, d_ff // tf),
        in_specs=[rows, pl.BlockSpec((1, D_MODEL), lambda i, f: (0, 0)),
                  pl.BlockSpec((D_MODEL, tf), lambda i, f: (0, f)),
                  pl.BlockSpec((D_MODEL, tf), lambda i, f: (0, f)),
                  pl.BlockSpec((tf, D_MODEL), lambda i, f: (f, 0))],
        out_specs=rows,
        out_shape=jax.ShapeDtypeStruct((m, D_MODEL), F32),
        scratch_shapes=[pltpu.VMEM((tm, D_MODEL), BF16)],
        compiler_params=_params("parallel", "arbitrary"),
        name="ffn",
    )(h, g, wg, wu, wd)


def _dot_f32_nt(a, b):
    nt = lambda x, y: lax.dot_general(x, y, _NT, preferred_element_type=F32)
    a_hi, a_mid, a_lo = _split3(a)
    b_hi, b_mid, b_lo = _split3(b)
    return (nt(a_hi, b_hi) + (nt(a_hi, b_mid) + nt(a_mid, b_hi))
            + (nt(a_mid, b_mid) + nt(a_hi, b_lo) + nt(a_lo, b_hi)))


def _router_kernel(h_ref, g_ref, wt_ref, b_ref, before_ref, xn_ref, gate_ref, pos_ref, cnt_ref):
    x = h_ref[...]
    ms = jnp.mean(x * x, axis=-1, keepdims=True)
    xn = x * lax.rsqrt(ms + NORM_EPS) * g_ref[...]
    xn_ref[...] = xn.astype(BF16)
    logits = _dot_f32_nt(wt_ref[...], xn) + b_ref[...]
    idx = lax.broadcasted_iota(jnp.int32, logits.shape, 0).astype(F32)
    m1 = jnp.max(logits, axis=0, keepdims=True)
    i1 = jnp.min(jnp.where(logits == m1, idx, float(N_EXPERTS)), axis=0, keepdims=True)
    rest = jnp.where(idx == i1, -jnp.inf, logits)
    m2 = jnp.max(rest, axis=0, keepdims=True)
    i2 = jnp.min(jnp.where(rest == m2, idx, float(N_EXPERTS)), axis=0, keepdims=True)
    e2 = jnp.exp(m2 - m1)
    gates = jnp.where(idx == i1, 1.0 / (1.0 + e2), 0.0) + jnp.where(idx == i2, e2 / (1.0 + e2), 0.0)
    member = jnp.where(gates > 0.0, 1.0, 0.0)
    gate_ref[0] = gates
    pos_ref[0] = _dot(member.astype(BF16), before_ref[...])
    cnt_ref[0] = jnp.sum(member, axis=-1, keepdims=True).astype(jnp.int32)


def _router(h, g, w_router, b_router, tm):
    m = h.shape[0]
    n_tiles = m // tm
    before = jnp.asarray(np.triu(np.ones((tm, tm), np.float32), 1), BF16)
    tile3 = pl.BlockSpec((1, N_EXPERTS, tm), lambda i: (i, 0, 0))
    per_tile = jax.ShapeDtypeStruct((n_tiles, N_EXPERTS, tm), F32)
    return pl.pallas_call(
        _router_kernel,
        grid=(n_tiles,),
        in_specs=[pl.BlockSpec((tm, D_MODEL), lambda i: (i, 0)),
                  pl.BlockSpec((1, D_MODEL), lambda i: (0, 0)),
                  pl.BlockSpec((N_EXPERTS, D_MODEL), lambda i: (0, 0)),
                  pl.BlockSpec((N_EXPERTS, 1), lambda i: (0, 0)),
                  pl.BlockSpec((tm, tm), lambda i: (0, 0))],
        out_specs=[pl.BlockSpec((tm, D_MODEL), lambda i: (i, 0)), tile3, tile3,
                   pl.BlockSpec((1, N_EXPERTS, 1), lambda i: (i, 0, 0))],
        out_shape=[jax.ShapeDtypeStruct((m, D_MODEL), BF16), per_tile, per_tile,
                   jax.ShapeDtypeStruct((n_tiles, N_EXPERTS, 1), jnp.int32)],
        compiler_params=_params("parallel"),
        name="moe_router",
    )(h, g, w_router.T, b_router.reshape(N_EXPERTS, 1), before)


def _moe_kernel(sub, n_slices, cnt_ref, xn_ref, gate_ref, pos_ref, *refs):
    weights = [refs[3 * j:3 * j + 3] for j in range(MOE_SLICES_PER_STEP)]
    o_ref, xg_ref, gg_ref, acc_ref = refs[3 * MOE_SLICES_PER_STEP:]
    i = pl.program_id(0)
    e = pl.program_id(1)
    f = pl.program_id(2)
    tm = xn_ref.shape[0]
    n_sub = (cnt_ref[i * N_EXPERTS + e] + (sub - 1)) // sub
    pos_row = pos_ref[0, pl.ds(e, 1), :]
    gate_row = gate_ref[0, pl.ds(e, 1), :]
    slot = lax.broadcasted_iota(jnp.int32, (sub, tm), 0)

    def one_hot(s):
        wanted = (slot + s * sub).astype(F32)
        return jnp.where(gate_row > 0.0, jnp.where(pos_row == wanted, 1.0, 0.0), 0.0)

    def rows_of(s):
        return pl.ds(pl.multiple_of(s * sub, sub), sub)

    @pl.when((e == 0) & (f == 0))
    def _():
        o_ref[...] = jnp.zeros_like(o_ref)

    @pl.when(f == 0)
    def _():
        def pack(s, carry):
            pick = one_hot(s)
            xg_ref[rows_of(s), :] = _dot(pick.astype(BF16), xn_ref[...]).astype(BF16)
            gg_ref[rows_of(s), :] = jnp.sum(pick * gate_row, axis=-1, keepdims=True)
            acc_ref[rows_of(s), :] = jnp.zeros((sub, D_MODEL), F32)
            return carry

        lax.fori_loop(0, n_sub, pack, 0)

    for j, (wg_ref, wu_ref, wd_ref) in enumerate(weights):
        def expert(s, carry, wg_ref=wg_ref, wu_ref=wu_ref, wd_ref=wd_ref):
            x = xg_ref[rows_of(s), :]
            mid = _silu(_dot(x, wg_ref[0])) * _dot(x, wu_ref[0]) * gg_ref[rows_of(s), :]
            acc_ref[rows_of(s), :] += _dot(mid.astype(BF16), wd_ref[0])
            return carry

        lax.fori_loop(0, jnp.where(f * MOE_SLICES_PER_STEP + j < n_slices, n_sub, 0), expert, 0)

    @pl.when(f == pl.num_programs(2) - 1)
    def _():
        def unpack(s, carry):
            pick = one_hot(s).astype(BF16)
            o_ref[...] += lax.dot_general(pick, acc_ref[rows_of(s), :].astype(BF16), _TN, preferred_element_type=F32)
            return carry

        lax.fori_loop(0, n_sub, unpack, 0)


def _moe(xn, gates, pos, counts, wg, wu, wd, tm, tf):
    m = xn.shape[0]
    n_e, _, d_ff = wg.shape
    sub = min(tm, -(-(tm * TOP_K * 5 // (N_EXPERTS * 4)) // 16) * 16)
    packed = -(-tm // sub) * sub
    n_slices = d_ff // tf
    per_step = MOE_SLICES_PER_STEP
    tile3 = pl.BlockSpec((1, N_EXPERTS, tm), lambda i, e, f, cnt: (i, 0, 0))

    def slice_specs(j):
        fj = lambda f: jnp.minimum(f * per_step + j, n_slices - 1)
        up = pl.BlockSpec((1, D_MODEL, tf), lambda i, e, f, cnt: (e, 0, fj(f)))
        down = pl.BlockSpec((1, tf, D_MODEL), lambda i, e, f, cnt: (e, fj(f), 0))
        return [up, up, down]

    grid_spec = pltpu.PrefetchScalarGridSpec(
        num_scalar_prefetch=1,
        grid=(m // tm, n_e, -(-n_slices // per_step)),
        in_specs=[pl.BlockSpec((tm, D_MODEL), lambda i, e, f, cnt: (i, 0)), tile3, tile3]
        + [spec for j in range(per_step) for spec in slice_specs(j)],
        out_specs=pl.BlockSpec((tm, D_MODEL), lambda i, e, f, cnt: (i, 0)),
        scratch_shapes=[pltpu.VMEM((packed, D_MODEL), BF16), pltpu.VMEM((packed, 1), F32),
                        pltpu.VMEM((packed, D_MODEL), F32)],
    )
    return pl.pallas_call(
        functools.partial(_moe_kernel, sub, n_slices),
        grid_spec=grid_spec,
        out_shape=jax.ShapeDtypeStruct((m, D_MODEL), F32),
        compiler_params=_params("parallel", "arbitrary", "arbitrary"),
        name="moe_experts",
    )(counts.reshape(-1), xn, gates, pos, *([wg, wu, wd] * per_step))


MOE_SLICES_PER_STEP = 2


def _final_norm_kernel(n_in, *refs):
    g_ref, o_ref = refs[n_in], refs[n_in + 1]
    x = refs[0][...]
    for r in refs[1:n_in]:
        x = x + r[...]
    ms = jnp.mean(x * x, axis=-1, keepdims=True)
    o_ref[...] = x * lax.rsqrt(ms + NORM_EPS) * g_ref[...]


def _final_norm(parts, g, tm):
    m = parts[0].shape[0]
    rows = pl.BlockSpec((tm, D_MODEL), lambda i: (i, 0))
    return pl.pallas_call(
        functools.partial(_final_norm_kernel, len(parts)),
        grid=(m // tm,),
        in_specs=[rows] * len(parts) + [pl.BlockSpec((1, D_MODEL), lambda i: (0, 0))],
        out_specs=rows,
        out_shape=jax.ShapeDtypeStruct((m, D_MODEL), F32),
        compiler_params=_params("parallel"),
        name="final_norm",
    )(*parts, g)


def _add_kernel(a_ref, b_ref, o_ref):
    o_ref[...] = a_ref[...] + b_ref[...]


def _add(a, b, tm):
    rows = pl.BlockSpec((tm, D_MODEL), lambda i: (i, 0))
    return pl.pallas_call(
        _add_kernel,
        grid=(a.shape[0] // tm,),
        in_specs=[rows, rows],
        out_specs=rows,
        out_shape=jax.ShapeDtypeStruct(a.shape, F32),
        compiler_params=_params("parallel"),
        name="residual_add",
    )(a, b)


def _pad_rw_cols(x, axis=-1):
    x = jnp.moveaxis(x, axis, 0)
    pad = jnp.zeros((RW_PAD - RW_COLS,) + x.shape[1:], x.dtype)
    return jnp.moveaxis(jnp.concatenate([x[:RW_COLS], pad, x[RW_COLS:]], axis=0), 0, axis)


def _stage_w_in_kernel(w_ref, o_ref):
    o_ref[...] = w_ref[0].astype(BF16)


def _stage_w_in(w_in, layer):
    rows = 512
    assert RW_COLS // rows == (RW_PAD - 1) // rows and RW_PAD % rows == 0 and (N_PROJ - RW_PAD) % rows == 0
    shift = RW_PAD - RW_COLS

    def src_row(j):
        assert rows % 8 == 0 and shift % 8 == 0
        return 8 * jnp.where(j * rows < RW_PAD, j * (rows // 8), j * (rows // 8) - shift // 8)

    return pl.pallas_call(
        _stage_w_in_kernel,
        grid=(N_PROJ // rows,),
        in_specs=[pl.BlockSpec((pl.Element(1), pl.Element(rows), pl.Element(D_MODEL)),
                               lambda j: (layer, src_row(j), 0))],
        out_specs=pl.BlockSpec((rows, D_MODEL), lambda j: (j, 0)),
        out_shape=jax.ShapeDtypeStruct((N_PROJ, D_MODEL), BF16),
        compiler_params=_params("parallel"),
        name="stage_w_in",
    )(w_in.transpose(0, 2, 1))


def _lora_slab(w, row0):
    slab = jnp.zeros((LORA_PAD, RW_WIDTH), F32)
    return lax.dynamic_update_slice(slab, w, (row0, 0)).astype(BF16)


def _layer_params(l, P):
    row = lambda x: x.reshape(1, -1)
    lp = dict(
        norm_mix=row(P['norm_mix'][l]), norm_ffn=row(P['norm_ffn'][l]),
        w_in_t=_stage_w_in(P['w_in'], l),
        rw_vecs=[row(_pad_rw_cols(P['rw_mu'][l])), row(P['rw_w0'][l]), row(P['rw_a0'][l]),
                 row(P['rw_kk'][l]), row(P['rw_ka'][l]), row(P['rw_rk'][l])],
        rw_mats=[_lora_slab(P['rw_w2'][l], 0), _lora_slab(P['rw_a2'][l], DECAY_LORA),
                 _lora_slab(P['rw_g2'][l], DECAY_LORA + AAA_LORA)],
        rw_ln_w=row(P['rw_ln_w'][l]), rw_ln_b=row(P['rw_ln_b'][l]),
        cv_w=P['cv_w'][l], cv_b=row(P['cv_b'][l]), cv_ln_w=row(P['cv_ln_w'][l]), cv_ln_b=row(P['cv_ln_b'][l]),
        w_br_rw=P['w_br_rw'][l].astype(BF16), w_br_mb=P['w_br_mb'][l].astype(BF16),
        w_br_cv=P['w_br_cv'][l].astype(BF16), w_out=P['w_out'][l].astype(BF16),
    )
    i = l // 2
    if l % 2 == 0:
        lp.update(wg=P['ffn_wg'][i].astype(BF16), wu=P['ffn_wu'][i].astype(BF16), wd=P['ffn_wd'][i].astype(BF16))
    else:
        lp.update(wg=P['moe_wg'][i].astype(BF16), wu=P['moe_wu'][i].astype(BF16), wd=P['moe_wd'][i].astype(BF16),
                  router=P['moe_router'][i], router_b=P['moe_router_b'][i])
    return lp


def _channel_mix(h, lp, tm_dense, tm_moe):
    if 'router' in lp:
        xn, gates, pos, counts = _router(h, lp['norm_ffn'], lp['router'], lp['router_b'], tm_moe)
        return [h, _moe(xn, gates, pos, counts, lp['wg'], lp['wu'], lp['wd'], tm_moe, 256)]
    return [_ffn(h, lp['norm_ffn'], lp['wg'], lp['wu'], lp['wd'], tm_dense, 512)]


def _as_one(parts, tm):
    return parts[0] if len(parts) == 1 else _add(parts[0], parts[1], tm)


def _run_prompt(x, layers, norm_final, rel_bias):
    t = x.shape[0]
    parts = [x]
    outs = dict(k=[], v=[], wkv=[], shift=[], conv=[])
    for lp in layers:
        h = _as_one(parts, 512)
        proj = _norm_proj(h, lp['norm_mix'], lp['w_in_t'], 1024)
        *ops, g, bonus = _rwkv_prep(proj, None, lp['rw_vecs'], lp['rw_mats'], 128)
        y_rw, s_final = _rwkv_chunked(ops, bonus, g, lp['rw_ln_w'], lp['rw_ln_b'])
        q_t, k_t, v_t, k_tb, v_tb, km_tiles = _qkv_t(h, lp['norm_mix'], lp['w_in_t'], 1024)
        y_mb_t = _moba_seq(rel_bias, q_t, k_tb, v_tb, km_tiles)
        y_cv, tail = _conv_seq(proj, lp['cv_w'], lp['cv_b'], lp['cv_ln_w'], lp['cv_ln_b'], 512)
        merged = _merge(y_rw, y_mb_t, y_cv, proj, lp['w_br_rw'], lp['w_br_mb'], lp['w_br_cv'], 256, True)
        h = _out_proj(h, merged, lp['w_out'], 512)
        parts = _channel_mix(h, lp, 1024, 1024)
        tokens_major = lambda x_t: x_t.reshape(MB_HEADS, MB_HEAD_DIM, t).transpose(2, 0, 1)[None]
        outs['k'].append(tokens_major(k_t))
        outs['v'].append(tokens_major(v_t))
        outs['wkv'].append(s_final[None])
        outs['shift'].append(proj[t - 1:t, :RW_COLS])
        outs['conv'].append(tail[None, CV_HALO - (CV_K - 1):, :])
    y = _final_norm(parts, norm_final, 512)
    return (y[None],) + tuple(jnp.stack(outs[n]) for n in ('k', 'v', 'wkv', 'shift', 'conv'))


def _run_sample(x, layers, norm_final, rel_bias, cache_k, cache_v, state_wkv, state_shift, state_conv, page_table):
    b = x.shape[0]
    parts = [x]
    wkv_t = state_wkv.transpose(0, 2, 3, 4, 1)
    new_wkv_t = None
    outs = dict(k=[], v=[], shift=[], conv=[])
    for l, lp in enumerate(layers):
        h = _as_one(parts, b)
        proj = _norm_proj(h, lp['norm_mix'], lp['w_in_t'], b)
        *ops, g, bonus = _rwkv_prep(proj, _pad_rw_cols(state_shift[l]), lp['rw_vecs'], lp['rw_mats'], b)
        y_t, new_wkv_t = _rwkv_single_step(ops, wkv_t, l, new_wkv_t)
        y_rw = _rwkv_post(y_t, bonus, g, lp['rw_ln_w'], lp['rw_ln_b'])
        q = proj[:, RW_PAD:RW_PAD + MB_WIDTH]
        k_new = proj[:, RW_PAD + MB_WIDTH:RW_PAD + 2 * MB_WIDTH]
        v_new = proj[:, RW_PAD + 2 * MB_WIDTH:RW_PAD + 3 * MB_WIDTH]
        y_mb = _moba_paged(rel_bias, q, k_new, v_new, cache_k, cache_v, l, page_table)
        y_cv, conv_new = _conv_step(proj, state_conv[l], lp['cv_w'], lp['cv_b'], lp['cv_ln_w'], lp['cv_ln_b'], 8)
        merged = _merge(y_rw, y_mb, y_cv, proj, lp['w_br_rw'], lp['w_br_mb'], lp['w_br_cv'], b, False)
        h = _out_proj(h, merged, lp['w_out'], b)
        parts = _channel_mix(h, lp, b, b)
        outs['k'].append(k_new.reshape(b, 1, MB_HEADS, MB_HEAD_DIM))
        outs['v'].append(v_new.reshape(b, 1, MB_HEADS, MB_HEAD_DIM))
        outs['shift'].append(proj[:, :RW_COLS])
        outs['conv'].append(conv_new)
    y = _final_norm(parts, norm_final, b)
    stacked = {n: jnp.stack(v) for n, v in outs.items()}
    return (y[:, None, :], stacked['k'], stacked['v'], new_wkv_t.transpose(0, 4, 1, 2, 3), stacked['shift'],
            stacked['conv'])


def kernel(x_prompt, x_sample, cache_k, cache_v, state_wkv, state_shift, state_conv, page_table, norm_mix, norm_ffn, norm_final, w_in, rw_mu, rw_w0, rw_w2, rw_a0, rw_a2, rw_g2, rw_kk, rw_ka, rw_rk, rw_ln_w, rw_ln_b, rel_bias, cv_w, cv_b, cv_ln_w, cv_ln_b, w_br_rw, w_br_mb, w_br_cv, w_out, ffn_wg, ffn_wu, ffn_wd, moe_router, moe_router_b, moe_wg, moe_wu, moe_wd):
    P = dict(norm_mix=norm_mix, norm_ffn=norm_ffn, w_in=w_in,
             rw_mu=rw_mu, rw_w0=rw_w0, rw_w2=rw_w2, rw_a0=rw_a0, rw_a2=rw_a2, rw_g2=rw_g2,
             rw_kk=rw_kk, rw_ka=rw_ka, rw_rk=rw_rk, rw_ln_w=rw_ln_w, rw_ln_b=rw_ln_b,
             cv_w=cv_w, cv_b=cv_b, cv_ln_w=cv_ln_w, cv_ln_b=cv_ln_b,
             w_br_rw=w_br_rw, w_br_mb=w_br_mb, w_br_cv=w_br_cv, w_out=w_out,
             ffn_wg=ffn_wg, ffn_wu=ffn_wu, ffn_wd=ffn_wd,
             moe_router=moe_router, moe_router_b=moe_router_b, moe_wg=moe_wg, moe_wu=moe_wu, moe_wd=moe_wd)
    depth = w_in.shape[0]
    layers = [_layer_params(l, P) for l in range(depth)]
    g_final = norm_final.reshape(1, -1)
    assert x_prompt.shape[0] == 1 and x_sample.shape[1] == 1
    prompt = _run_prompt(x_prompt[0], layers, g_final, rel_bias)
    sample = _run_sample(x_sample[:, 0, :], layers, g_final, rel_bias, cache_k, cache_v,
                         state_wkv, state_shift, state_conv, page_table)
    y_p, k_p, v_p, wkv_p, shift_p, conv_p = prompt
    y_s, k_s, v_s, wkv_s, shift_s, conv_s = sample
    return (y_p, y_s, k_p, v_p, wkv_p, shift_p, conv_p,
            k_s, v_s, wkv_s, shift_s, conv_s)
```

```python
import functools
import math

import numpy as np
import jax
import jax.numpy as jnp
from jax import lax
from jax.experimental import pallas as pl
from jax.experimental.pallas import tpu as pltpu

F32 = jnp.float32
BF16 = jnp.bfloat16

D_MODEL = 2048
RW_HEAD_DIM = 64
RW_WIDTH = 1024
RW_HEADS = 16
DECAY_LORA = 64
AAA_LORA = 64
GATE_LORA = 160
RW_COLS = 3 * RW_WIDTH + DECAY_LORA + AAA_LORA + GATE_LORA
RW_PAD = 3584
LORA_PAD = RW_PAD - 3 * RW_WIDTH
GN_EPS = 64e-5
MB_HEAD_DIM = 64
MB_WIDTH = 512
MB_HEADS = 8
MB_BLOCK = 256
MB_TOPK = 3
NUM_BUCKETS = 32
MAX_DISTANCE = 128
NEG_INF = -1e30
CV_WIDTH = 512
CV_K = 31
LN_EPS = 1e-5
N_BRANCH = 3
N_EXPERTS = 8
TOP_K = 2
NORM_EPS = 1e-6
PAGE_SIZE = 128

PROJ_TILE = 1024
N_PROJ = RW_PAD + 3 * MB_WIDTH + 2 * CV_WIDTH + N_BRANCH * D_MODEL
COL_CV = (RW_PAD + 3 * MB_WIDTH) // (2 * CV_WIDTH)
GATE_COL0 = RW_PAD + 3 * MB_WIDTH + 2 * CV_WIDTH
GATE_TILE0 = GATE_COL0 // PROJ_TILE

VMEM_LIMIT_BYTES = 56 * 1024 * 1024


def _params(*semantics):
    return pltpu.CompilerParams(dimension_semantics=semantics, vmem_limit_bytes=VMEM_LIMIT_BYTES)


def _sigmoid(x):
    return 1.0 / (1.0 + jnp.exp(-x))


def _silu(x):
    return x * _sigmoid(x)


def _split3(x):
    hi = x.astype(BF16)
    r1 = x - hi.astype(F32)
    mid = r1.astype(BF16)
    lo = (r1 - mid.astype(F32)).astype(BF16)
    return hi, mid, lo


_NT = (((1,), (1,)), ((), ()))
_TN = (((0,), (0,)), ((), ()))


def _dot(a, b):
    return jnp.dot(a, b, preferred_element_type=F32)


def _dot_exact_rhs(x, sel):
    hi, mid, lo = _split3(x)
    return _dot(hi, sel) + _dot(mid, sel) + _dot(lo, sel)


def _dot_f32(a, b):
    a_hi, a_mid, a_lo = _split3(a)
    b_hi, b_mid, b_lo = _split3(b)
    return (_dot(a_hi, b_hi) + (_dot(a_hi, b_mid) + _dot(a_mid, b_hi))
            + (_dot(a_mid, b_mid) + _dot(a_hi, b_lo) + _dot(a_lo, b_hi)))


def _rms_norm_bf16(x, g):
    ms = jnp.mean(x * x, axis=-1, keepdims=True)
    return (x * lax.rsqrt(ms + NORM_EPS) * g).astype(BF16)


def _norm_proj_kernel(x_ref, g_ref, wt_ref, o_ref, gate_ref, xn_ref):
    j = pl.program_id(1)

    @pl.when(j == 0)
    def _():
        xn_ref[...] = _rms_norm_bf16(x_ref[...], g_ref[...])

    acc = lax.dot_general(xn_ref[...], wt_ref[...], _NT, preferred_element_type=F32)

    @pl.when(j < GATE_TILE0)
    def _():
        o_ref[...] = acc

    @pl.when(j >= GATE_TILE0)
    def _():
        gate_ref[...] = _sigmoid(acc).astype(BF16)


def _norm_proj(x, g, wt, tm):
    m = x.shape[0]
    n_gate = N_PROJ - GATE_COL0
    return pl.pallas_call(
        _norm_proj_kernel,
        grid=(m // tm, N_PROJ // PROJ_TILE),
        in_specs=[pl.BlockSpec((tm, D_MODEL), lambda i, j: (i, 0)),
                  pl.BlockSpec((1, D_MODEL), lambda i, j: (0, 0)),
                  pl.BlockSpec((PROJ_TILE, D_MODEL), lambda i, j: (j, 0))],
        out_specs=[pl.BlockSpec((tm, PROJ_TILE), lambda i, j: (i, jnp.minimum(j, GATE_TILE0 - 1))),
                   pl.BlockSpec((tm, PROJ_TILE), lambda i, j: (i, jnp.maximum(j - GATE_TILE0, 0)))],
        out_shape=[jax.ShapeDtypeStruct((m, GATE_COL0), F32), jax.ShapeDtypeStruct((m, n_gate), BF16)],
        scratch_shapes=[pltpu.VMEM((tm, D_MODEL), BF16)],
        compiler_params=_params("parallel", "arbitrary"),
        name="norm_proj",
    )(x, g, wt)


def _qkv_t_kernel(x_ref, g_ref, wq_ref, wk_ref, wv_ref, q_ref, k_ref, v_ref, kb_ref, vb_ref, km_ref):
    xn = _rms_norm_bf16(x_ref[...], g_ref[...])
    nt = lambda w_ref: lax.dot_general(w_ref[...], xn, _NT, preferred_element_type=F32)
    q_ref[...] = nt(wq_ref)
    k = nt(wk_ref)
    v = nt(wv_ref)
    k_ref[...] = k
    v_ref[...] = v
    kb_ref[...] = k.astype(BF16)
    vb_ref[...] = v.astype(BF16)
    n_blk = k.shape[1] // MB_BLOCK
    sums = [jnp.sum(k[:, b * MB_BLOCK:(b + 1) * MB_BLOCK], axis=-1, keepdims=True) for b in range(n_blk)]
    km_ref[0] = jnp.concatenate(sums, axis=-1) * (1.0 / MB_BLOCK)


def _qkv_t(x, g, wt, tm):
    t = x.shape[0]
    q_row = RW_PAD // MB_WIDTH
    w_spec = lambda r: pl.BlockSpec((MB_WIDTH, D_MODEL), lambda i: (q_row + r, 0))
    col = pl.BlockSpec((MB_WIDTH, tm), lambda i: (0, i))
    f32_t = jax.ShapeDtypeStruct((MB_WIDTH, t), F32)
    bf16_t = jax.ShapeDtypeStruct((MB_WIDTH, t), BF16)
    return pl.pallas_call(
        _qkv_t_kernel,
        grid=(t // tm,),
        in_specs=[pl.BlockSpec((tm, D_MODEL), lambda i: (i, 0)), pl.BlockSpec((1, D_MODEL), lambda i: (0, 0)),
                  w_spec(0), w_spec(1), w_spec(2)],
        out_specs=[col] * 5 + [pl.BlockSpec((1, MB_WIDTH, tm // MB_BLOCK), lambda i: (i, 0, 0))],
        out_shape=[f32_t] * 3 + [bf16_t] * 2 + [jax.ShapeDtypeStruct((t // tm, MB_WIDTH, tm // MB_BLOCK), F32)],
        compiler_params=_params("parallel"),
        name="qkv_t",
    )(x, g, wt, wt, wt)


def _rwkv_prep_kernel(sequential, *refs):
    if sequential:
        (p_ref, mu_ref, w0_ref, a0_ref, kk_ref, ka_ref, rk_ref, w2_ref, a2_ref, g2_ref,
         r_o, w_o, k_o, v_o, kn_o, b_o, g_o, bonus_o, carry_ref) = refs
    else:
        (p_ref, prev_ref, mu_ref, w0_ref, a0_ref, kk_ref, ka_ref, rk_ref, w2_ref, a2_ref, g2_ref,
         r_o, w_o, k_o, v_o, kn_o, b_o, g_o, bonus_o) = refs
    p = p_ref[...]
    tm = p.shape[0]
    if sequential:
        @pl.when(pl.program_id(0) == 0)
        def _():
            carry_ref[...] = jnp.zeros_like(carry_ref)

        rows = lax.broadcasted_iota(jnp.int32, p.shape, 0)
        prev = jnp.where(rows == 0, carry_ref[...], pltpu.roll(p, 1, 0))
        carry_ref[...] = p[tm - 1:tm, :]
    else:
        prev = prev_ref[...]
    xs = p + (prev - p) * mu_ref[...]
    r = xs[:, 0:RW_WIDTH]
    k = xs[:, RW_WIDTH:2 * RW_WIDTH]
    v = xs[:, 2 * RW_WIDTH:3 * RW_WIDTH]
    lora = xs[:, 3 * RW_WIDTH:RW_PAD]
    wx = w0_ref[...] + _dot(jnp.tanh(lora).astype(BF16), w2_ref[...])
    w_log = jnp.minimum(wx, 0.0) - jnp.log(1.0 + jnp.exp(-jnp.abs(wx))) - 0.5
    log_decay = -jnp.exp(w_log)
    a = _sigmoid(a0_ref[...] + _dot(lora.astype(BF16), a2_ref[...]))
    g_o[...] = _dot(_sigmoid(lora).astype(BF16), g2_ref[...])
    kk = k * kk_ref[...]
    k_mod = k * (1.0 + (a - 1.0) * ka_ref[...])
    rkr = r * k_mod * rk_ref[...]
    kn_parts = []
    for h in range(RW_HEADS):
        sl = slice(h * RW_HEAD_DIM, (h + 1) * RW_HEAD_DIM)
        kk_h = kk[:, sl]
        nrm = jnp.sqrt(jnp.sum(kk_h * kk_h, axis=-1, keepdims=True))
        kn_parts.append(kk_h / jnp.maximum(nrm, 1e-12))
        bonus_o[:, sl] = jnp.sum(rkr[:, sl], axis=-1, keepdims=True) * v[:, sl]
    kn = jnp.concatenate(kn_parts, axis=-1)
    if sequential:
        ops = (r, log_decay, k_mod, v, kn, kn * a)
    else:
        ops = tuple(x.T for x in (r, jnp.exp(log_decay), k_mod, v, kn, kn * a))
    for o_ref, x in zip((r_o, w_o, k_o, v_o, kn_o, b_o), ops):
        o_ref[...] = x


def _rwkv_prep(proj, prev, vecs, mats, tm):
    m = proj.shape[0]
    sequential = prev is None
    row = lambda n: pl.BlockSpec((1, n), lambda i: (0, 0))
    in_specs = [pl.BlockSpec((tm, RW_PAD), lambda i: (i, 0))]
    args = [proj]
    if not sequential:
        in_specs.append(pl.BlockSpec((tm, RW_PAD), lambda i: (i, 0)))
        args.append(prev)
    in_specs += [row(RW_PAD)] + [row(RW_WIDTH)] * 5
    in_specs += [pl.BlockSpec((LORA_PAD, RW_WIDTH), lambda i: (0, 0))] * 3
    flat = jax.ShapeDtypeStruct((m, RW_WIDTH), F32)
    flat_spec = pl.BlockSpec((tm, RW_WIDTH), lambda i: (i, 0))
    if sequential:
        op_shape, op_spec = flat, flat_spec
    else:
        op_shape = jax.ShapeDtypeStruct((RW_WIDTH, m), F32)
        op_spec = pl.BlockSpec((RW_WIDTH, tm), lambda i: (0, i))
    return pl.pallas_call(
        functools.partial(_rwkv_prep_kernel, sequential),
        grid=(m // tm,),
        in_specs=in_specs,
        out_specs=[op_spec] * 6 + [flat_spec] * 2,
        out_shape=[op_shape] * 6 + [flat] * 2,
        scratch_shapes=[pltpu.VMEM((1, RW_PAD), F32)] if sequential else [],
        compiler_params=_params("arbitrary"),
        name="rwkv_prep",
    )(*args, *vecs, *mats)


RW_CHUNK = 64
RW_CHUNKS_PER_STEP = 2


def _group_norm(y):
    mean = jnp.mean(y, axis=-1, keepdims=True)
    var = jnp.mean(jnp.square(y - mean), axis=-1, keepdims=True)
    return (y - mean) * lax.rsqrt(var + GN_EPS)


def _rwkv_chunk_kernel(r_ref, lw_ref, k_ref, v_ref, kn_ref, b_ref, bonus_ref, g_ref, lnw_ref, lnb_ref,
                       y_ref, s_out_ref, h_ref):
    c = RW_CHUNK
    n = RW_HEAD_DIM

    @pl.when(pl.program_id(0) == 0)
    def _():
        h_ref[...] = jnp.zeros_like(h_ref)

    row = lax.broadcasted_iota(jnp.int32, (c, c), 0)
    col = lax.broadcasted_iota(jnp.int32, (c, c), 1)
    eye = jnp.where(row == col, 1.0, 0.0)
    lower_ones = jnp.where(col <= row, 1.0, 0.0).astype(BF16)
    row2 = lax.broadcasted_iota(jnp.int32, (2 * c, 2 * c), 0)
    col2 = lax.broadcasted_iota(jnp.int32, (2 * c, 2 * c), 1)
    t_idx = jnp.where(row2 >= c, row2 - c, row2)
    s_idx = jnp.where(col2 >= c, col2 - c, col2)
    mask = s_idx < t_idx + jnp.where(row2 >= c, 1, 0)
    heads = range(RW_HEADS)
    sls = [slice(h * n, (h + 1) * n) for h in heads]
    chunks = [slice(ci * c, (ci + 1) * c) for ci in range(r_ref.shape[0] // c)]

    pre = []
    for rows in chunks:
        lw = lw_ref[rows, :]
        lw_hi, lw_mid, lw_lo = _split3(lw)
        cum = _dot(lower_ones, lw_hi) + _dot(lower_ones, lw_mid) + _dot(lower_ones, lw_lo)
        total = cum[c - 1:c, :]
        k = k_ref[rows, :]
        b = b_ref[rows, :]
        v = v_ref[rows, :]
        g_inv = jnp.exp(-cum)
        g_rest = jnp.exp(total - cum)
        pre.append(dict(
            qr=jnp.concatenate([kn_ref[rows, :] * jnp.exp(cum - lw), r_ref[rows, :] * jnp.exp(cum)],
                               axis=0).astype(BF16),
            kdbd=jnp.concatenate([k * g_inv, b * g_inv], axis=0).astype(BF16),
            zk=jnp.concatenate([k * g_rest, b * g_rest], axis=0).astype(BF16),
            v=v, vb=v.astype(BF16), decay_total=jnp.exp(total)))
    pairs = [(p, sl) for p in pre for sl in sls]
    gms = [jnp.where(mask, lax.dot_general(p['qr'][:, sl], p['kdbd'][:, sl], _NT, preferred_element_type=F32), 0.0)
           for p, sl in pairs]
    akkv = [_dot(gm[0:c, 0:c].astype(BF16), p['vb'][:, sl]) for gm, (p, sl) in zip(gms, pairs)]
    powers = [gm[0:c, c:2 * c].astype(BF16) for gm in gms]
    t_invs = [eye - gm[0:c, c:2 * c] for gm in gms]
    for _ in range(int(math.log2(c)) - 1):
        powers = [_dot(pw, pw).astype(BF16) for pw in powers]
        t_invs = [t + _dot(pw, t.astype(BF16)) for pw, t in zip(powers, t_invs)]

    for ci, rows in enumerate(chunks):
        p = pre[ci]
        mine = slice(ci * RW_HEADS, (ci + 1) * RW_HEADS)
        h_old = [h_ref[h] for h in heads]
        qhs = [_dot(p['qr'][:, sl], ho.astype(BF16)) for sl, ho in zip(sls, h_old)]
        us = [_dot(t.astype(BF16), (qh[0:c] + av).astype(BF16))
              for t, qh, av in zip(t_invs[mine], qhs, akkv[mine])]
        vus = [jnp.concatenate([p['v'][:, sl], -u], axis=0).astype(BF16) for sl, u in zip(sls, us)]
        outs = [_group_norm(qh[c:2 * c] + _dot(gm[c:2 * c, :].astype(BF16), vu))
                for qh, gm, vu in zip(qhs, gms[mine], vus)]
        for h in heads:
            decay_col = jnp.sum(eye * p['decay_total'][:, sls[h]], axis=-1, keepdims=True)
            h_ref[h] = decay_col * h_old[h] + lax.dot_general(p['zk'][:, sls[h]], vus[h], _TN,
                                                              preferred_element_type=F32)
        yn = jnp.concatenate(outs, axis=-1)
        y_ref[rows, :] = ((yn * lnw_ref[...] + lnb_ref[...] + bonus_ref[rows, :]) * g_ref[rows, :]).astype(BF16)

    @pl.when(pl.program_id(0) == pl.num_programs(0) - 1)
    def _():
        for h in range(RW_HEADS):
            s_out_ref[h] = h_ref[h].T


def _rwkv_chunked(ops, bonus, g, ln_w, ln_b):
    m = bonus.shape[0]
    step_rows = RW_CHUNK * RW_CHUNKS_PER_STEP
    spec = pl.BlockSpec((step_rows, RW_WIDTH), lambda i: (i, 0))
    row = pl.BlockSpec((1, RW_WIDTH), lambda i: (0, 0))
    state_shape = (RW_HEADS, RW_HEAD_DIM, RW_HEAD_DIM)
    return pl.pallas_call(
        _rwkv_chunk_kernel,
        grid=(m // step_rows,),
        in_specs=[spec] * 8 + [row, row],
        out_specs=[spec, pl.BlockSpec(state_shape, lambda i: (0, 0, 0))],
        out_shape=[jax.ShapeDtypeStruct((m, RW_WIDTH), BF16), jax.ShapeDtypeStruct(state_shape, F32)],
        scratch_shapes=[pltpu.VMEM(state_shape, F32)],
        compiler_params=_params("arbitrary"),
        name="rwkv_chunked",
    )(*ops, bonus, g, ln_w, ln_b)


RW_STEP_HEADS = 2


def _rwkv_single_step_kernel(r_ref, w_ref, k_ref, v_ref, kn_ref, b_ref, s0_ref, *rest):
    y_ref, s_out_ref = rest[-2:]
    n = RW_HEAD_DIM
    for hh in range(s0_ref.shape[0]):
        rows = slice(hh * n, (hh + 1) * n)
        r, w, k, kn, b = (ref[rows, :] for ref in (r_ref, w_ref, k_ref, kn_ref, b_ref))

        def body(vi, carry):
            s = s0_ref[hh, vi]
            sa = jnp.sum(s * kn, axis=0, keepdims=True)
            s = s * w - sa * b + v_ref[pl.ds(hh * n + vi, 1), :] * k
            s_out_ref[hh, vi] = s
            y_ref[pl.ds(hh * n + vi, 1), :] = jnp.sum(s * r, axis=0, keepdims=True)
            return carry

        lax.fori_loop(0, n, body, 0)


def _rwkv_single_step(ops, states_t, layer, new_states_t):
    b = ops[0].shape[1]
    hs = RW_STEP_HEADS
    spec = pl.BlockSpec((hs * RW_HEAD_DIM, b), lambda i: (i, 0))
    st_spec = pl.BlockSpec((None, hs, RW_HEAD_DIM, RW_HEAD_DIM, b), lambda i: (layer, i, 0, 0, 0))
    carried = [] if new_states_t is None else [new_states_t]
    return pl.pallas_call(
        _rwkv_single_step_kernel,
        grid=(RW_HEADS // hs,),
        in_specs=[spec] * 6 + [st_spec] + [pl.BlockSpec(memory_space=pl.ANY)] * len(carried),
        out_specs=[spec, st_spec],
        out_shape=[jax.ShapeDtypeStruct((RW_WIDTH, b), F32), jax.ShapeDtypeStruct(states_t.shape, F32)],
        input_output_aliases={7: 1} if carried else {},
        compiler_params=_params("parallel"),
        name="rwkv_single_step",
    )(*ops, states_t, *carried)


def _rwkv_post_kernel(yt_ref, bonus_ref, g_ref, lnw_ref, lnb_ref, o_ref):
    y = yt_ref[...].T
    n = RW_HEAD_DIM
    yn = jnp.concatenate([_group_norm(y[:, h * n:(h + 1) * n]) for h in range(RW_HEADS)], axis=-1)
    out = (yn * lnw_ref[...] + lnb_ref[...] + bonus_ref[...]) * g_ref[...]
    o_ref[...] = out.astype(BF16)


def _rwkv_post(y_t, bonus, g, ln_w, ln_b):
    b = bonus.shape[0]
    flat_spec = pl.BlockSpec((b, RW_WIDTH), lambda i: (0, 0))
    row = pl.BlockSpec((1, RW_WIDTH), lambda i: (0, 0))
    return pl.pallas_call(
        _rwkv_post_kernel,
        grid=(1,),
        in_specs=[pl.BlockSpec((RW_WIDTH, b), lambda i: (0, 0)), flat_spec, flat_spec, row, row],
        out_specs=flat_spec,
        out_shape=jax.ShapeDtypeStruct((b, RW_WIDTH), BF16),
        compiler_params=_params("arbitrary"),
        name="rwkv_post",
    )(y_t, bonus, g, ln_w, ln_b)


CV_HALO = 32


def _conv_finish(y, cb_ref, lnw_ref, lnb_ref):
    y = y + cb_ref[...]
    mean = jnp.mean(y, axis=-1, keepdims=True)
    var = jnp.mean(jnp.square(y - mean), axis=-1, keepdims=True)
    yn = (y - mean) * lax.rsqrt(var + LN_EPS) * lnw_ref[...] + lnb_ref[...]
    return _silu(yn).astype(BF16)


def _conv_seq_kernel(p_ref, cw_ref, cb_ref, lnw_ref, lnb_ref, o_ref, tail_ref, ext_ref):
    tm = p_ref.shape[0]
    pad = CV_HALO - (CV_K - 1)

    @pl.when(pl.program_id(0) == 0)
    def _():
        ext_ref[0:CV_HALO, :] = jnp.zeros((CV_HALO, CV_WIDTH), F32)

    p = p_ref[...]
    u = p[:, 0:CV_WIDTH] * _sigmoid(p[:, CV_WIDTH:2 * CV_WIDTH])
    ext_ref[CV_HALO:CV_HALO + tm, :] = u
    acc = jnp.zeros((tm, CV_WIDTH), F32)
    for j in range(CV_K):
        acc = acc + ext_ref[pad + j:pad + j + tm, :] * cw_ref[j:j + 1, :]
    o_ref[...] = _conv_finish(acc, cb_ref, lnw_ref, lnb_ref)
    tail = ext_ref[tm:tm + CV_HALO, :]
    ext_ref[0:CV_HALO, :] = tail
    tail_ref[...] = tail


def _conv_seq(proj, cw, cb, ln_w, ln_b, tm):
    m = proj.shape[0]
    row = pl.BlockSpec((1, CV_WIDTH), lambda i: (0, 0))
    return pl.pallas_call(
        _conv_seq_kernel,
        grid=(m // tm,),
        in_specs=[pl.BlockSpec((tm, 2 * CV_WIDTH), lambda i: (i, COL_CV)),
                  pl.BlockSpec((CV_K, CV_WIDTH), lambda i: (0, 0)), row, row, row],
        out_specs=[pl.BlockSpec((tm, CV_WIDTH), lambda i: (i, 0)),
                   pl.BlockSpec((CV_HALO, CV_WIDTH), lambda i: (0, 0))],
        out_shape=[jax.ShapeDtypeStruct((m, CV_WIDTH), BF16),
                   jax.ShapeDtypeStruct((CV_HALO, CV_WIDTH), F32)],
        scratch_shapes=[pltpu.VMEM((CV_HALO + tm, CV_WIDTH), F32)],
        compiler_params=_params("arbitrary"),
        name="conv_seq",
    )(proj, cw, cb, ln_w, ln_b)


def _conv_step_kernel(p_ref, buf_ref, cw_ref, cb_ref, lnw_ref, lnb_ref, o_ref, nbuf_ref):
    p = p_ref[...]
    u = p[:, :, 0:CV_WIDTH] * _sigmoid(p[:, :, CV_WIDTH:2 * CV_WIDTH])
    buf = buf_ref[...]
    acc = (jnp.sum(buf * cw_ref[0:CV_K - 1, :][None], axis=1, keepdims=True)
           + u * cw_ref[CV_K - 1:CV_K, :][None])
    o_ref[...] = _conv_finish(acc, cb_ref, lnw_ref, lnb_ref)
    nbuf_ref[:, 0:CV_K - 2, :] = buf[:, 1:CV_K - 1, :]
    nbuf_ref[:, CV_K - 2:CV_K - 1, :] = u


def _conv_step(proj, buf, cw, cb, ln_w, ln_b, tb):
    m = proj.shape[0]
    row = pl.BlockSpec((1, CV_WIDTH), lambda i: (0, 0))
    buf_spec = pl.BlockSpec((tb, CV_K - 1, CV_WIDTH), lambda i: (i, 0, 0))
    y = pl.pallas_call(
        _conv_step_kernel,
        grid=(m // tb,),
        in_specs=[pl.BlockSpec((tb, 1, 2 * CV_WIDTH), lambda i: (i, 0, COL_CV)), buf_spec,
                  pl.BlockSpec((CV_K, CV_WIDTH), lambda i: (0, 0)), row, row, row],
        out_specs=[pl.BlockSpec((tb, 1, CV_WIDTH), lambda i: (i, 0, 0)), buf_spec],
        out_shape=[jax.ShapeDtypeStruct((m, 1, CV_WIDTH), BF16),
                   jax.ShapeDtypeStruct(buf.shape, F32)],
        compiler_params=_params("parallel"),
        name="conv_step",
    )(proj.reshape(m, 1, proj.shape[1]), buf, cw, cb, ln_w, ln_b)
    return y[0].reshape(m, CV_WIDTH), y[1]


def _bucket_thresholds():
    n = np.arange(0, 4 * MAX_DISTANCE, dtype=np.int64)
    max_exact = NUM_BUCKETS // 2
    nf = np.maximum(n, 1).astype(np.float32)
    large = max_exact + (np.log(nf / np.float32(max_exact)) / np.float32(math.log(MAX_DISTANCE / max_exact))
                         * np.float32(NUM_BUCKETS - max_exact)).astype(np.int32)
    large = np.minimum(large, NUM_BUCKETS - 1)
    bucket = np.where(n < max_exact, n, large)
    assert np.all(np.diff(bucket) >= 0) and bucket[-1] == NUM_BUCKETS - 1
    return [int(np.argmax(bucket >= b)) for b in range(NUM_BUCKETS)]


_BUCKET_THR = _bucket_thresholds()


def _bias_from_distance(n, rb_of_bucket):
    bias = jnp.zeros(n.shape, F32) + rb_of_bucket(NUM_BUCKETS - 1)
    for b in range(NUM_BUCKETS - 2, -1, -1):
        bias = jnp.where(n < _BUCKET_THR[b + 1], rb_of_bucket(b), bias)
    return bias


def _rank_select(scores, n_rows, limit):
    idx = lax.broadcasted_iota(jnp.int32, scores.shape, 0)
    beaten = jnp.zeros(scores.shape, F32)
    for j in range(n_rows):
        row = scores[j:j + 1, :]
        ahead = jnp.where(row > scores, 1.0, jnp.where(row == scores, jnp.where(idx > j, 1.0, 0.0), 0.0))
        beaten = beaten + ahead * jnp.where(j < limit, 1.0, 0.0)
    return jnp.where(idx < limit, jnp.where(beaten < MB_TOPK, 1.0, 0.0), 0.0)


MB_HEADS_PER_STEP = 4
MB_SUM_ROWS = 16
MB_BLOCKS_PER_TRIP = 2


def _softmax_blocks(carries, scores, vt_blks):
    m_new = [jnp.maximum(c[0], jnp.max(s, axis=0, keepdims=True)) for c, s in zip(carries, scores)]
    alpha = [jnp.exp2(c[0] - mn) for c, mn in zip(carries, m_new)]
    p = [jnp.exp2(s - mn).astype(BF16) for s, mn in zip(scores, m_new)]
    pv = [_dot(_with_ones_rows(vt), pp) for vt, pp in zip(vt_blks, p)]
    return [(mn, a * c[1] + x) for mn, a, c, x in zip(m_new, alpha, carries, pv)]


def _with_ones_rows(vt_blk):
    return jnp.concatenate([vt_blk, jnp.ones((MB_SUM_ROWS, vt_blk.shape[1]), BF16)], axis=0)


def _moba_seq_kernel(rb_ref, qt_ref, k_ref, vt_ref, km_ref, o_ref, bias_ref, rowterm_ref):
    hp = qt_ref.shape[0]
    h0 = pl.program_id(0) * hp
    qi = pl.program_id(1)
    nb = km_ref.shape[1]
    log2e = math.log2(math.e)
    scale = MB_HEAD_DIM ** -0.5 * log2e
    kidx = lax.broadcasted_iota(jnp.int32, (MB_BLOCK, MB_BLOCK), 0)
    qidx = lax.broadcasted_iota(jnp.int32, (MB_BLOCK, MB_BLOCK), 1)

    @pl.when(qi == 0)
    def _():
        dist = qidx - kidx
        for hh in range(hp):
            rb = lambda b: rb_ref[b, h0 + hh] * log2e
            bias_ref[hh, 0] = jnp.where(dist >= 0, _bias_from_distance(jnp.maximum(dist, 0), rb), NEG_INF)
            bias_ref[hh, 1] = _bias_from_distance(dist + MB_BLOCK, rb)

    near = jnp.maximum(qi - 1, 0)
    qtb, carries = [], []
    for hh in range(hp):
        qt = qt_ref[hh]
        block_scores = _dot_f32(km_ref[hh], qt)
        sel = _rank_select(block_scores, nb, qi)
        rowterm_ref[hh] = jnp.where(sel > 0.0, rb_ref[NUM_BUCKETS - 1, h0 + hh] * log2e, NEG_INF)
        qtb.append((qt * scale).astype(BF16))
        s = _dot(k_ref[hh, qi], qtb[hh]) + bias_ref[hh, 0]
        m0 = jnp.max(s, axis=0, keepdims=True)
        p = jnp.exp2(s - m0).astype(BF16)
        carries.append((m0, _dot(_with_ones_rows(vt_ref[hh, qi]), p)))

    heads = range(hp)
    sel_near = [jnp.where(rowterm_ref[hh, pl.ds(near, 1), :] > 0.5 * NEG_INF, 0.0, NEG_INF) for hh in heads]
    scores = [_dot(k_ref[hh, near], qtb[hh]) + bias_ref[hh, 1] + sel_near[hh] for hh in heads]
    carries = _softmax_blocks(carries, scores, [vt_ref[hh, near] for hh in heads])

    def body(t, carry):
        kjs = [t * MB_BLOCKS_PER_TRIP + u for u in range(MB_BLOCKS_PER_TRIP)]
        safe = [jnp.minimum(kj, nb - 1) for kj in kjs]
        raw = [[_dot(k_ref[hh, kc], qtb[hh]) for hh in heads] for kc in safe]
        for u, (kj, kc) in enumerate(zip(kjs, safe)):
            scores = [raw[u][hh] + jnp.where(kj < near, rowterm_ref[hh, pl.ds(kc, 1), :], NEG_INF) for hh in heads]
            carry = _softmax_blocks(carry, scores, [vt_ref[hh, kc] for hh in heads])
        return tuple(carry)

    n_trips = (near + MB_BLOCKS_PER_TRIP - 1) // MB_BLOCKS_PER_TRIP
    carries = lax.fori_loop(0, n_trips, body, tuple(carries))
    for hh in range(hp):
        acc = carries[hh][1]
        o_ref[hh] = (acc[0:MB_HEAD_DIM] / acc[MB_HEAD_DIM:MB_HEAD_DIM + 1]).astype(BF16)


def _moba_seq(rel_bias, q_t, k_t, v_t, km_tiles):
    t = q_t.shape[1]
    nb = t // MB_BLOCK
    hp = MB_HEADS_PER_STEP
    blocks = lambda x: x.reshape(MB_HEADS, MB_HEAD_DIM, nb, MB_BLOCK)
    qt = q_t.reshape(MB_HEADS, MB_HEAD_DIM, t)
    kb = blocks(k_t).transpose(0, 2, 3, 1)
    vt = blocks(v_t).transpose(0, 2, 1, 3)
    km = km_tiles.transpose(1, 0, 2).reshape(MB_HEADS, MB_HEAD_DIM, nb).transpose(0, 2, 1)
    ot = pl.pallas_call(
        _moba_seq_kernel,
        grid=(MB_HEADS // hp, nb),
        in_specs=[pl.BlockSpec(memory_space=pltpu.SMEM),
                  pl.BlockSpec((hp, MB_HEAD_DIM, MB_BLOCK), lambda h, i: (h, 0, i)),
                  pl.BlockSpec((hp, nb, MB_BLOCK, MB_HEAD_DIM), lambda h, i: (h, 0, 0, 0)),
                  pl.BlockSpec((hp, nb, MB_HEAD_DIM, MB_BLOCK), lambda h, i: (h, 0, 0, 0)),
                  pl.BlockSpec((hp, nb, MB_HEAD_DIM), lambda h, i: (h, 0, 0))],
        out_specs=pl.BlockSpec((hp, MB_HEAD_DIM, MB_BLOCK), lambda h, i: (h, 0, i)),
        out_shape=jax.ShapeDtypeStruct((MB_HEADS, MB_HEAD_DIM, t), BF16),
        scratch_shapes=[pltpu.VMEM((hp, 2, MB_BLOCK, MB_BLOCK), F32), pltpu.VMEM((hp, nb, MB_BLOCK), F32)],
        compiler_params=_params("parallel", "arbitrary"),
        name="moba_seq",
    )(rel_bias, qt, kb, vt, km)
    return ot.reshape(MB_WIDTH, t)


def _moba_paged_kernel(n_pages, pt_ref, q_ref, kn_ref, vn_ref, rbt_ref, *refs):
    k_pages = refs[:n_pages]
    v_pages = refs[n_pages:2 * n_pages]
    o_ref, bias_ref = refs[2 * n_pages:]
    past = n_pages * PAGE_SIZE
    n_blocks = past // MB_BLOCK
    pages_per_block = MB_BLOCK // PAGE_SIZE
    scale = MB_HEAD_DIM ** -0.5
    rb = lambda b: rbt_ref[:, :, b:b + 1]

    @pl.when(pl.program_id(0) == 0)
    def _():
        dist = PAGE_SIZE - lax.broadcasted_iota(jnp.int32, (MB_HEADS, 1, PAGE_SIZE), 2)
        bias_ref[...] = _bias_from_distance(dist, rb)

    di = lax.broadcasted_iota(jnp.int32, (MB_HEAD_DIM, MB_HEAD_DIM), 0)
    dj = lax.broadcasted_iota(jnp.int32, (MB_HEAD_DIM, MB_HEAD_DIM), 1)
    eye = jnp.where(di == dj, 1.0, 0.0)[None]
    q = q_ref[0]
    q_col = jnp.sum(eye * q, axis=-1, keepdims=True)
    raw = [jnp.sum(k_pages[pg][...] * q_col, axis=1, keepdims=True) for pg in range(n_pages)]
    scores = []
    for j in range(n_blocks):
        tot = raw[pages_per_block * j]
        for e in range(1, pages_per_block):
            tot = tot + raw[pages_per_block * j + e]
        scores.append(jnp.sum(tot, axis=-1, keepdims=True) * (1.0 / MB_BLOCK))
    far_bias = rb(NUM_BUCKETS - 1)
    rowterms = []
    for j in range(n_blocks):
        beaten = jnp.zeros((MB_HEADS, 1, 1), F32)
        for i in range(n_blocks):
            if i != j:
                ahead = scores[i] > scores[j] if i > j else scores[i] >= scores[j]
                beaten = beaten + jnp.where(ahead, 1.0, 0.0)
        rowterms.append(jnp.where(beaten < MB_TOPK, far_bias, NEG_INF))
    lg_own = jnp.sum(kn_ref[0] * q, axis=-1, keepdims=True) * scale + rb(0)
    logits = []
    for pg in range(n_pages):
        term = rowterms[pg // pages_per_block]
        if pg == n_pages - 1:
            term = bias_ref[...] + jnp.where(term > 0.5 * NEG_INF, 0.0, NEG_INF)
        logits.append(raw[pg] * scale + term)
    m_row = logits[0]
    for lg in logits[1:]:
        m_row = jnp.maximum(m_row, lg)
    m = jnp.maximum(jnp.max(m_row, axis=-1, keepdims=True), lg_own)
    p_own = jnp.exp(lg_own - m)
    l_row = jnp.zeros((MB_HEADS, 1, PAGE_SIZE), F32)
    acc = jnp.zeros((MB_HEADS, MB_HEAD_DIM, PAGE_SIZE), F32)
    for pg in range(n_pages):
        p = jnp.exp(logits[pg] - m)
        l_row = l_row + p
        acc = acc + v_pages[pg][...] * p
    out_col = jnp.sum(acc, axis=-1, keepdims=True)
    out = jnp.sum(eye * out_col, axis=1, keepdims=True) + p_own * vn_ref[0]
    l = jnp.sum(l_row, axis=-1, keepdims=True) + p_own
    o_ref[0] = (out / l).astype(BF16)


def _moba_paged(rel_bias, q, k_new, v_new, cache_k, cache_v, layer, page_table):
    b, n_pages = page_table.shape
    assert PAGE_SIZE >= _BUCKET_THR[NUM_BUCKETS - 1] and MB_BLOCK % PAGE_SIZE == 0
    head4 = pl.BlockSpec((1, MB_HEADS, 1, MB_HEAD_DIM), lambda i, pt: (i, 0, 0, 0))

    def page_spec(pg):
        return pl.BlockSpec((None, None, MB_HEADS, MB_HEAD_DIM, PAGE_SIZE),
                            lambda i, pt: (layer, pt[i, pg], 0, 0, 0))

    grid_spec = pltpu.PrefetchScalarGridSpec(
        num_scalar_prefetch=1,
        grid=(b,),
        in_specs=[head4, head4, head4, pl.BlockSpec((MB_HEADS, 1, NUM_BUCKETS), lambda i, pt: (0, 0, 0))]
        + [page_spec(pg) for pg in range(n_pages)] * 2,
        out_specs=head4,
        scratch_shapes=[pltpu.VMEM((MB_HEADS, 1, PAGE_SIZE), F32)],
    )
    h4 = lambda x: x.reshape(b, MB_HEADS, 1, MB_HEAD_DIM)
    kt = cache_k.transpose(0, 1, 3, 4, 2)
    vt = cache_v.transpose(0, 1, 3, 4, 2)
    out = pl.pallas_call(
        functools.partial(_moba_paged_kernel, n_pages),
        grid_spec=grid_spec,
        out_shape=jax.ShapeDtypeStruct((b, MB_HEADS, 1, MB_HEAD_DIM), BF16),
        compiler_params=_params("arbitrary"),
        name="moba_paged",
    )(page_table, h4(q), h4(k_new), h4(v_new), rel_bias.T.reshape(MB_HEADS, 1, NUM_BUCKETS),
      *([kt] * n_pages), *([vt] * n_pages))
    return out.reshape(b, MB_WIDTH)


def _merge_kernel(mb_feature_major, yrw_ref, ymb_ref, ycv_ref, g0_ref, g1_ref, g2_ref, wrw_ref, wmb_ref, wcv_ref,
                  o_ref):
    if mb_feature_major:
        y_mb = lax.dot_general(ymb_ref[...], wmb_ref[...], _TN, preferred_element_type=F32)
    else:
        y_mb = _dot(ymb_ref[...], wmb_ref[...])
    merged = (g0_ref[...] * _dot(yrw_ref[...], wrw_ref[...]) + g1_ref[...] * y_mb
              + g2_ref[...] * _dot(ycv_ref[...], wcv_ref[...]))
    o_ref[...] = merged.astype(BF16)


def _merge(y_rw, y_mb, y_cv, gates, w_rw, w_mb, w_cv, tm, mb_feature_major):
    m = gates.shape[0]
    rows = lambda n: pl.BlockSpec((tm, n), lambda i: (i, 0))
    mb_spec = pl.BlockSpec((MB_WIDTH, tm), lambda i: (0, i)) if mb_feature_major else rows(MB_WIDTH)
    gate = lambda br: pl.BlockSpec((tm, D_MODEL), lambda i: (i, br))
    full = lambda n: pl.BlockSpec((n, D_MODEL), lambda i: (0, 0))
    return pl.pallas_call(
        functools.partial(_merge_kernel, mb_feature_major),
        grid=(m // tm,),
        in_specs=[rows(RW_WIDTH), mb_spec, rows(CV_WIDTH), gate(0), gate(1), gate(2),
                  full(RW_WIDTH), full(MB_WIDTH), full(CV_WIDTH)],
        out_specs=rows(D_MODEL),
        out_shape=jax.ShapeDtypeStruct((m, D_MODEL), BF16),
        compiler_params=_params("parallel"),
        name="branch_merge",
    )(y_rw, y_mb, y_cv, gates, gates, gates, w_rw, w_mb, w_cv)


def _out_proj_kernel(h_ref, x_ref, w_ref, o_ref):
    o_ref[...] = h_ref[...] + _dot(x_ref[...], w_ref[...])


def _out_proj(h, merged, w_out, tm):
    m = h.shape[0]
    rows = pl.BlockSpec((tm, D_MODEL), lambda i: (i, 0))
    return pl.pallas_call(
        _out_proj_kernel,
        grid=(m // tm,),
        in_specs=[rows, rows, pl.BlockSpec((D_MODEL, D_MODEL), lambda i: (0, 0))],
        out_specs=rows,
        out_shape=jax.ShapeDtypeStruct((m, D_MODEL), F32),
        compiler_params=_params("parallel"),
        name="out_proj",
    )(h, merged, w_out)


def _ffn_kernel(h_ref, g_ref, wg_ref, wu_ref, wd_ref, o_ref, xn_ref):
    @pl.when(pl.program_id(1) == 0)
    def _():
        x = h_ref[...]
        ms = jnp.mean(x * x, axis=-1, keepdims=True)
        xn_ref[...] = (x * lax.rsqrt(ms + NORM_EPS) * g_ref[...]).astype(BF16)
        o_ref[...] = x

    xn = xn_ref[...]
    mid = _silu(_dot(xn, wg_ref[...])) * _dot(xn, wu_ref[...])
    o_ref[...] += _dot(mid.astype(BF16), wd_ref[...])


def _ffn(h, g, wg, wu, wd, tm, tf):
    m = h.shape[0]
    d_ff = wg.shape[1]
    rows = pl.BlockSpec((tm, D_MODEL), lambda i, f: (i, 0))
    return pl.pallas_call(
        _ffn_kernel,
        grid=(m // tm, d_ff // tf),
        in_specs=[rows, pl.BlockSpec((1, D_MODEL), lambda i, f: (0, 0)),
                  pl.BlockSpec((D_MODEL, tf), lambda i, f: (0, f)),
                  pl.BlockSpec((D_MODEL, tf), lambda i, f: (0, f)),
                  pl.BlockSpec((tf, D_MODEL), lambda i, f: (f, 0))],
        out_specs=rows,
        out_shape=jax.ShapeDtypeStruct((m, D_MODEL), F32),
        scratch_shapes=[pltpu.VMEM((tm, D_MODEL), BF16)],
        compiler_params=_params("parallel", "arbitrary"),
        name="ffn",
    )(h, g, wg, wu, wd)


def _dot_f32_nt(a, b):
    nt = lambda x, y: lax.dot_general(x, y, _NT, preferred_element_type=F32)
    a_hi, a_mid, a_lo = _split3(a)
    b_hi, b_mid, b_lo = _split3(b)
    return (nt(a_hi, b_hi) + (nt(a_hi, b_mid) + nt(a_mid, b_hi))
            + (nt(a_mid, b_mid) + nt(a_hi, b_lo) + nt(a_lo, b_hi)))


def _router_kernel(h_ref, g_ref, wt_ref, b_ref, before_ref, xn_ref, gate_ref, pos_ref, cnt_ref):
    x = h_ref[...]
    ms = jnp.mean(x * x, axis=-1, keepdims=True)
    xn = x * lax.rsqrt(ms + NORM_EPS) * g_ref[...]
    xn_ref[...] = xn.astype(BF16)
    logits = _dot_f32_nt(wt_ref[...], xn) + b_ref[...]
    idx = lax.broadcasted_iota(jnp.int32, logits.shape, 0).astype(F32)
    m1 = jnp.max(logits, axis=0, keepdims=True)
    i1 = jnp.min(jnp.where(logits == m1, idx, float(N_EXPERTS)), axis=0, keepdims=True)
    rest = jnp.where(idx == i1, -jnp.inf, logits)
    m2 = jnp.max(rest, axis=0, keepdims=True)
    i2 = jnp.min(jnp.where(rest == m2, idx, float(N_EXPERTS)), axis=0, keepdims=True)
    e2 = jnp.exp(m2 - m1)
    gates = jnp.where(idx == i1, 1.0 / (1.0 + e2), 0.0) + jnp.where(idx == i2, e2 / (1.0 + e2), 0.0)
    member = jnp.where(gates > 0.0, 1.0, 0.0)
    gate_ref[0] = gates
    pos_ref[0] = _dot(member.astype(BF16), before_ref[...])
    cnt_ref[0] = jnp.sum(member, axis=-1, keepdims=True).astype(jnp.int32)


def _router(h, g, w_router, b_router, tm):
    m = h.shape[0]
    n_tiles = m // tm
    before = jnp.asarray(np.triu(np.ones((tm, tm), np.float32), 1), BF16)
    tile3 = pl.BlockSpec((1, N_EXPERTS, tm), lambda i: (i, 0, 0))
    per_tile = jax.ShapeDtypeStruct((n_tiles, N_EXPERTS, tm), F32)
    return pl.pallas_call(
        _router_kernel,
        grid=(n_tiles,),
        in_specs=[pl.BlockSpec((tm, D_MODEL), lambda i: (i, 0)),
                  pl.BlockSpec((1, D_MODEL), lambda i: (0, 0)),
                  pl.BlockSpec((N_EXPERTS, D_MODEL), lambda i: (0, 0)),
                  pl.BlockSpec((N_EXPERTS, 1), lambda i: (0, 0)),
                  pl.BlockSpec((tm, tm), lambda i: (0, 0))],
        out_specs=[pl.BlockSpec((tm, D_MODEL), lambda i: (i, 0)), tile3, tile3,
                   pl.BlockSpec((1, N_EXPERTS, 1), lambda i: (i, 0, 0))],
        out_shape=[jax.ShapeDtypeStruct((m, D_MODEL), BF16), per_tile, per_tile,
                   jax.ShapeDtypeStruct((n_tiles, N_EXPERTS, 1), jnp.int32)],
        compiler_params=_params("parallel"),
        name="moe_router",
    )(h, g, w_router.T, b_router.reshape(N_EXPERTS, 1), before)


def _moe_kernel(sub, n_slices, cnt_ref, xn_ref, gate_ref, pos_ref, *refs):
    weights = [refs[3 * j:3 * j + 3] for j in range(MOE_SLICES_PER_STEP)]
    o_ref, xg_ref, gg_ref, acc_ref = refs[3 * MOE_SLICES_PER_STEP:]
    i = pl.program_id(0)
    e = pl.program_id(1)
    f = pl.program_id(2)
    tm = xn_ref.shape[0]
    n_sub = (cnt_ref[i * N_EXPERTS + e] + (sub - 1)) // sub
    pos_row = pos_ref[0, pl.ds(e, 1), :]
    gate_row = gate_ref[0, pl.ds(e, 1), :]
    slot = lax.broadcasted_iota(jnp.int32, (sub, tm), 0)

    def one_hot(s):
        wanted = (slot + s * sub).astype(F32)
        return jnp.where(gate_row > 0.0, jnp.where(pos_row == wanted, 1.0, 0.0), 0.0)

    def rows_of(s):
        return pl.ds(pl.multiple_of(s * sub, sub), sub)

    @pl.when((e == 0) & (f == 0))
    def _():
        o_ref[...] = jnp.zeros_like(o_ref)

    @pl.when(f == 0)
    def _():
        def pack(s, carry):
            pick = one_hot(s)
            xg_ref[rows_of(s), :] = _dot(pick.astype(BF16), xn_ref[...]).astype(BF16)
            gg_ref[rows_of(s), :] = jnp.sum(pick * gate_row, axis=-1, keepdims=True)
            acc_ref[rows_of(s), :] = jnp.zeros((sub, D_MODEL), F32)
            return carry

        lax.fori_loop(0, n_sub, pack, 0)

    for j, (wg_ref, wu_ref, wd_ref) in enumerate(weights):
        def expert(s, carry, wg_ref=wg_ref, wu_ref=wu_ref, wd_ref=wd_ref):
            x = xg_ref[rows_of(s), :]
            mid = _silu(_dot(x, wg_ref[0])) * _dot(x, wu_ref[0]) * gg_ref[rows_of(s), :]
            acc_ref[rows_of(s), :] += _dot(mid.astype(BF16), wd_ref[0])
            return carry

        lax.fori_loop(0, jnp.where(f * MOE_SLICES_PER_STEP + j < n_slices, n_sub, 0), expert, 0)

    @pl.when(f == pl.num_programs(2) - 1)
    def _():
        def unpack(s, carry):
            pick = one_hot(s).astype(BF16)
            o_ref[...] += lax.dot_general(pick, acc_ref[rows_of(s), :].astype(BF16), _TN, preferred_element_type=F32)
            return carry

        lax.fori_loop(0, n_sub, unpack, 0)


def _moe(xn, gates, pos, counts, wg, wu, wd, tm, tf):
    m = xn.shape[0]
    n_e, _, d_ff = wg.shape
    sub = min(tm, -(-(tm * TOP_K * 5 // (N_EXPERTS * 4)) // 16) * 16)
    packed = -(-tm // sub) * sub
    n_slices = d_ff // tf
    per_step = MOE_SLICES_PER_STEP
    tile3 = pl.BlockSpec((1, N_EXPERTS, tm), lambda i, e, f, cnt: (i, 0, 0))

    def slice_specs(j):
        fj = lambda f: jnp.minimum(f * per_step + j, n_slices - 1)
        up = pl.BlockSpec((1, D_MODEL, tf), lambda i, e, f, cnt: (e, 0, fj(f)))
        down = pl.BlockSpec((1, tf, D_MODEL), lambda i, e, f, cnt: (e, fj(f), 0))
        return [up, up, down]

    grid_spec = pltpu.PrefetchScalarGridSpec(
        num_scalar_prefetch=1,
        grid=(m // tm, n_e, -(-n_slices // per_step)),
        in_specs=[pl.BlockSpec((tm, D_MODEL), lambda i, e, f, cnt: (i, 0)), tile3, tile3]
        + [spec for j in range(per_step) for spec in slice_specs(j)],
        out_specs=pl.BlockSpec((tm, D_MODEL), lambda i, e, f, cnt: (i, 0)),
        scratch_shapes=[pltpu.VMEM((packed, D_MODEL), BF16), pltpu.VMEM((packed, 1), F32),
                        pltpu.VMEM((packed, D_MODEL), F32)],
    )
    return pl.pallas_call(
        functools.partial(_moe_kernel, sub, n_slices),
        grid_spec=grid_spec,
        out_shape=jax.ShapeDtypeStruct((m, D_MODEL), F32),
        compiler_params=_params("parallel", "arbitrary", "arbitrary"),
        name="moe_experts",
    )(counts.reshape(-1), xn, gates, pos, *([wg, wu, wd] * per_step))


MOE_SLICES_PER_STEP = 2


def _final_norm_kernel(n_in, *refs):
    g_ref, o_ref = refs[n_in], refs[n_in + 1]
    x = refs[0][...]
    for r in refs[1:n_in]:
        x = x + r[...]
    ms = jnp.mean(x * x, axis=-1, keepdims=True)
    o_ref[...] = x * lax.rsqrt(ms + NORM_EPS) * g_ref[...]


def _final_norm(parts, g, tm):
    m = parts[0].shape[0]
    rows = pl.BlockSpec((tm, D_MODEL), lambda i: (i, 0))
    return pl.pallas_call(
        functools.partial(_final_norm_kernel, len(parts)),
        grid=(m // tm,),
        in_specs=[rows] * len(parts) + [pl.BlockSpec((1, D_MODEL), lambda i: (0, 0))],
        out_specs=rows,
        out_shape=jax.ShapeDtypeStruct((m, D_MODEL), F32),
        compiler_params=_params("parallel"),
        name="final_norm",
    )(*parts, g)


def _add_kernel(a_ref, b_ref, o_ref):
    o_ref[...] = a_ref[...] + b_ref[...]


def _add(a, b, tm):
    rows = pl.BlockSpec((tm, D_MODEL), lambda i: (i, 0))
    return pl.pallas_call(
        _add_kernel,
        grid=(a.shape[0] // tm,),
        in_specs=[rows, rows],
        out_specs=rows,
        out_shape=jax.ShapeDtypeStruct(a.shape, F32),
        compiler_params=_params("parallel"),
        name="residual_add",
    )(a, b)


def _pad_rw_cols(x, axis=-1):
    x = jnp.moveaxis(x, axis, 0)
    pad = jnp.zeros((RW_PAD - RW_COLS,) + x.shape[1:], x.dtype)
    return jnp.moveaxis(jnp.concatenate([x[:RW_COLS], pad, x[RW_COLS:]], axis=0), 0, axis)


def _stage_w_in_kernel(w_ref, o_ref):
    o_ref[...] = w_ref[0].astype(BF16)


def _stage_w_in(w_in, layer):
    rows = 512
    assert RW_COLS // rows == (RW_PAD - 1) // rows and RW_PAD % rows == 0 and (N_PROJ - RW_PAD) % rows == 0
    shift = RW_PAD - RW_COLS

    def src_row(j):
        assert rows % 8 == 0 and shift % 8 == 0
        return 8 * jnp.where(j * rows < RW_PAD, j * (rows // 8), j * (rows // 8) - shift // 8)

    return pl.pallas_call(
        _stage_w_in_kernel,
        grid=(N_PROJ // rows,),
        in_specs=[pl.BlockSpec((pl.Element(1), pl.Element(rows), pl.Element(D_MODEL)),
                               lambda j: (layer, src_row(j), 0))],
        out_specs=pl.BlockSpec((rows, D_MODEL), lambda j: (j, 0)),
        out_shape=jax.ShapeDtypeStruct((N_PROJ, D_MODEL), BF16),
        compiler_params=_params("parallel"),
        name="stage_w_in",
    )(w_in.transpose(0, 2, 1))


def _lora_slab(w, row0):
    slab = jnp.zeros((LORA_PAD, RW_WIDTH), F32)
    return lax.dynamic_update_slice(slab, w, (row0, 0)).astype(BF16)


def _layer_params(l, P):
    row = lambda x: x.reshape(1, -1)
    lp = dict(
        norm_mix=row(P['norm_mix'][l]), norm_ffn=row(P['norm_ffn'][l]),
        w_in_t=_stage_w_in(P['w_in'], l),
        rw_vecs=[row(_pad_rw_cols(P['rw_mu'][l])), row(P['rw_w0'][l]), row(P['rw_a0'][l]),
                 row(P['rw_kk'][l]), row(P['rw_ka'][l]), row(P['rw_rk'][l])],
        rw_mats=[_lora_slab(P['rw_w2'][l], 0), _lora_slab(P['rw_a2'][l], DECAY_LORA),
                 _lora_slab(P['rw_g2'][l], DECAY_LORA + AAA_LORA)],
        rw_ln_w=row(P['rw_ln_w'][l]), rw_ln_b=row(P['rw_ln_b'][l]),
        cv_w=P['cv_w'][l], cv_b=row(P['cv_b'][l]), cv_ln_w=row(P['cv_ln_w'][l]), cv_ln_b=row(P['cv_ln_b'][l]),
        w_br_rw=P['w_br_rw'][l].astype(BF16), w_br_mb=P['w_br_mb'][l].astype(BF16),
        w_br_cv=P['w_br_cv'][l].astype(BF16), w_out=P['w_out'][l].astype(BF16),
    )
    i = l // 2
    if l % 2 == 0:
        lp.update(wg=P['ffn_wg'][i].astype(BF16), wu=P['ffn_wu'][i].astype(BF16), wd=P['ffn_wd'][i].astype(BF16))
    else:
        lp.update(wg=P['moe_wg'][i].astype(BF16), wu=P['moe_wu'][i].astype(BF16), wd=P['moe_wd'][i].astype(BF16),
                  router=P['moe_router'][i], router_b=P['moe_router_b'][i])
    return lp


def _channel_mix(h, lp, tm_dense, tm_moe):
    if 'router' in lp:
        xn, gates, pos, counts = _router(h, lp['norm_ffn'], lp['router'], lp['router_b'], tm_moe)
        return [h, _moe(xn, gates, pos, counts, lp['wg'], lp['wu'], lp['wd'], tm_moe, 256)]
    return [_ffn(h, lp['norm_ffn'], lp['wg'], lp['wu'], lp['wd'], tm_dense, 512)]


def _as_one(parts, tm):
    return parts[0] if len(parts) == 1 else _add(parts[0], parts[1], tm)


def _run_prompt(x, layers, norm_final, rel_bias):
    t = x.shape[0]
    parts = [x]
    outs = dict(k=[], v=[], wkv=[], shift=[], conv=[])
    for lp in layers:
        h = _as_one(parts, 512)
        proj, gates = _norm_proj(h, lp['norm_mix'], lp['w_in_t'], 1024)
        *ops, g, bonus = _rwkv_prep(proj, None, lp['rw_vecs'], lp['rw_mats'], 256)
        y_rw, s_final = _rwkv_chunked(ops, bonus, g, lp['rw_ln_w'], lp['rw_ln_b'])
        q_t, k_t, v_t, k_tb, v_tb, km_tiles = _qkv_t(h, lp['norm_mix'], lp['w_in_t'], 1024)
        y_mb_t = _moba_seq(rel_bias, q_t, k_tb, v_tb, km_tiles)
        y_cv, tail = _conv_seq(proj, lp['cv_w'], lp['cv_b'], lp['cv_ln_w'], lp['cv_ln_b'], 512)
        merged = _merge(y_rw, y_mb_t, y_cv, gates, lp['w_br_rw'], lp['w_br_mb'], lp['w_br_cv'], 512, True)
        h = _out_proj(h, merged, lp['w_out'], 512)
        parts = _channel_mix(h, lp, 1024, 1024)
        tokens_major = lambda x_t: x_t.reshape(MB_HEADS, MB_HEAD_DIM, t).transpose(2, 0, 1)[None]
        outs['k'].append(tokens_major(k_t))
        outs['v'].append(tokens_major(v_t))
        outs['wkv'].append(s_final[None])
        outs['shift'].append(proj[t - 1:t, :RW_COLS])
        outs['conv'].append(tail[None, CV_HALO - (CV_K - 1):, :])
    y = _final_norm(parts, norm_final, 512)
    return (y[None],) + tuple(jnp.stack(outs[n]) for n in ('k', 'v', 'wkv', 'shift', 'conv'))


def _run_sample(x, layers, norm_final, rel_bias, cache_k, cache_v, state_wkv, state_shift, state_conv, page_table):
    b = x.shape[0]
    parts = [x]
    wkv_t = state_wkv.transpose(0, 2, 3, 4, 1)
    new_wkv_t = None
    outs = dict(k=[], v=[], shift=[], conv=[])
    for l, lp in enumerate(layers):
        h = _as_one(parts, b)
        proj, gates = _norm_proj(h, lp['norm_mix'], lp['w_in_t'], b)
        *ops, g, bonus = _rwkv_prep(proj, _pad_rw_cols(state_shift[l]), lp['rw_vecs'], lp['rw_mats'], b)
        y_t, new_wkv_t = _rwkv_single_step(ops, wkv_t, l, new_wkv_t)
        y_rw = _rwkv_post(y_t, bonus, g, lp['rw_ln_w'], lp['rw_ln_b'])
        q = proj[:, RW_PAD:RW_PAD + MB_WIDTH]
        k_new = proj[:, RW_PAD + MB_WIDTH:RW_PAD + 2 * MB_WIDTH]
        v_new = proj[:, RW_PAD + 2 * MB_WIDTH:RW_PAD + 3 * MB_WIDTH]
        y_mb = _moba_paged(rel_bias, q, k_new, v_new, cache_k, cache_v, l, page_table)
        y_cv, conv_new = _conv_step(proj, state_conv[l], lp['cv_w'], lp['cv_b'], lp['cv_ln_w'], lp['cv_ln_b'], 8)
        merged = _merge(y_rw, y_mb, y_cv, gates, lp['w_br_rw'], lp['w_br_mb'], lp['w_br_cv'], b, False)
        h = _out_proj(h, merged, lp['w_out'], b)
        parts = _channel_mix(h, lp, b, b)
        outs['k'].append(k_new.reshape(b, 1, MB_HEADS, MB_HEAD_DIM))
        outs['v'].append(v_new.reshape(b, 1, MB_HEADS, MB_HEAD_DIM))
        outs['shift'].append(proj[:, :RW_COLS])
        outs['conv'].append(conv_new)
    y = _final_norm(parts, norm_final, b)
    stacked = {n: jnp.stack(v) for n, v in outs.items()}
    return (y[:, None, :], stacked['k'], stacked['v'], new_wkv_t.transpose(0, 4, 1, 2, 3), stacked['shift'],
            stacked['conv'])


def kernel(x_prompt, x_sample, cache_k, cache_v, state_wkv, state_shift, state_conv, page_table, norm_mix, norm_ffn, norm_final, w_in, rw_mu, rw_w0, rw_w2, rw_a0, rw_a2, rw_g2, rw_kk, rw_ka, rw_rk, rw_ln_w, rw_ln_b, rel_bias, cv_w, cv_b, cv_ln_w, cv_ln_b, w_br_rw, w_br_mb, w_br_cv, w_out, ffn_wg, ffn_wu, ffn_wd, moe_router, moe_router_b, moe_wg, moe_wu, moe_wd):
    P = dict(norm_mix=norm_mix, norm_ffn=norm_ffn, w_in=w_in,
             rw_mu=rw_mu, rw_w0=rw_w0, rw_w2=rw_w2, rw_a0=rw_a0, rw_a2=rw_a2, rw_g2=rw_g2,
             rw_kk=rw_kk, rw_ka=rw_ka, rw_rk=rw_rk, rw_ln_w=rw_ln_w, rw_ln_b=rw_ln_b,
             cv_w=cv_w, cv_b=cv_b, cv_ln_w=cv_ln_w, cv_ln_b=cv_ln_b,
             w_br_rw=w_br_rw, w_br_mb=w_br_mb, w_br_cv=w_br_cv, w_out=w_out,
             ffn_wg=ffn_wg, ffn_wu=ffn_wu, ffn_wd=ffn_wd,
             moe_router=moe_router, moe_router_b=moe_router_b, moe_wg=moe_wg, moe_wu=moe_wu, moe_wd=moe_wd)
    depth = w_in.shape[0]
    layers = [_layer_params(l, P) for l in range(depth)]
    g_final = norm_final.reshape(1, -1)
    assert x_prompt.shape[0] == 1 and x_sample.shape[1] == 1
    prompt = _run_prompt(x_prompt[0], layers, g_final, rel_bias)
    sample = _run_sample(x_sample[:, 0, :], layers, g_final, rel_bias, cache_k, cache_v,
                         state_wkv, state_shift, state_conv, page_table)
    y_p, k_p, v_p, wkv_p, shift_p, conv_p = prompt
    y_s, k_s, v_s, wkv_s, shift_s, conv_s = sample
    return (y_p, y_s, k_p, v_p, wkv_p, shift_p, conv_p,
            k_s, v_s, wkv_s, shift_s, conv_s)
```

```python
import functools
import math

import numpy as np
import jax
import jax.numpy as jnp
from jax import lax
from jax.experimental import pallas as pl
from jax.experimental.pallas import tpu as pltpu

F32 = jnp.float32
BF16 = jnp.bfloat16

D_MODEL = 2048
RW_HEAD_DIM = 64
RW_WIDTH = 1024
RW_HEADS = 16
DECAY_LORA = 64
AAA_LORA = 64
GATE_LORA = 160
RW_COLS = 3 * RW_WIDTH + DECAY_LORA + AAA_LORA + GATE_LORA
RW_PAD = 3584
LORA_PAD = RW_PAD - 3 * RW_WIDTH
GN_EPS = 64e-5
MB_HEAD_DIM = 64
MB_WIDTH = 512
MB_HEADS = 8
MB_BLOCK = 256
MB_TOPK = 3
NUM_BUCKETS = 32
MAX_DISTANCE = 128
NEG_INF = -1e30
CV_WIDTH = 512
CV_K = 31
LN_EPS = 1e-5
N_BRANCH = 3
N_EXPERTS = 8
TOP_K = 2
NORM_EPS = 1e-6
PAGE_SIZE = 128

PROJ_TILE = 1024
N_PROJ = RW_PAD + 3 * MB_WIDTH + 2 * CV_WIDTH + N_BRANCH * D_MODEL
COL_CV = (RW_PAD + 3 * MB_WIDTH) // (2 * CV_WIDTH)
GATE_COL0 = RW_PAD + 3 * MB_WIDTH + 2 * CV_WIDTH
GATE_TILE0 = GATE_COL0 // PROJ_TILE

VMEM_LIMIT_BYTES = 56 * 1024 * 1024


def _params(*semantics):
    return pltpu.CompilerParams(dimension_semantics=semantics, vmem_limit_bytes=VMEM_LIMIT_BYTES)


def _sigmoid(x):
    return 1.0 / (1.0 + jnp.exp(-x))


def _silu(x):
    return x * _sigmoid(x)


def _split3(x):
    hi = x.astype(BF16)
    r1 = x - hi.astype(F32)
    mid = r1.astype(BF16)
    lo = (r1 - mid.astype(F32)).astype(BF16)
    return hi, mid, lo


_NT = (((1,), (1,)), ((), ()))
_TN = (((0,), (0,)), ((), ()))


def _dot(a, b):
    return jnp.dot(a, b, preferred_element_type=F32)


def _dot_exact_rhs(x, sel):
    hi, mid, lo = _split3(x)
    return _dot(hi, sel) + _dot(mid, sel) + _dot(lo, sel)


def _dot_f32(a, b):
    a_hi, a_mid, a_lo = _split3(a)
    b_hi, b_mid, b_lo = _split3(b)
    return (_dot(a_hi, b_hi) + (_dot(a_hi, b_mid) + _dot(a_mid, b_hi))
            + (_dot(a_mid, b_mid) + _dot(a_hi, b_lo) + _dot(a_lo, b_hi)))


def _rms_norm_bf16(x, g):
    ms = jnp.mean(x * x, axis=-1, keepdims=True)
    return (x * lax.rsqrt(ms + NORM_EPS) * g).astype(BF16)


def _norm_proj_kernel(x_ref, g_ref, wt_ref, o_ref, gate_ref, xn_ref):
    j = pl.program_id(1)

    @pl.when(j == 0)
    def _():
        xn_ref[...] = _rms_norm_bf16(x_ref[...], g_ref[...])

    acc = lax.dot_general(xn_ref[...], wt_ref[...], _NT, preferred_element_type=F32)

    @pl.when(j < GATE_TILE0)
    def _():
        o_ref[...] = acc

    @pl.when(j >= GATE_TILE0)
    def _():
        gate_ref[...] = _sigmoid(acc).astype(BF16)


def _norm_proj(x, g, wt, tm):
    m = x.shape[0]
    n_gate = N_PROJ - GATE_COL0
    return pl.pallas_call(
        _norm_proj_kernel,
        grid=(m // tm, N_PROJ // PROJ_TILE),
        in_specs=[pl.BlockSpec((tm, D_MODEL), lambda i, j: (i, 0)),
                  pl.BlockSpec((1, D_MODEL), lambda i, j: (0, 0)),
                  pl.BlockSpec((PROJ_TILE, D_MODEL), lambda i, j: (j, 0))],
        out_specs=[pl.BlockSpec((tm, PROJ_TILE), lambda i, j: (i, jnp.minimum(j, GATE_TILE0 - 1))),
                   pl.BlockSpec((tm, PROJ_TILE), lambda i, j: (i, jnp.maximum(j - GATE_TILE0, 0)))],
        out_shape=[jax.ShapeDtypeStruct((m, GATE_COL0), F32), jax.ShapeDtypeStruct((m, n_gate), BF16)],
        scratch_shapes=[pltpu.VMEM((tm, D_MODEL), BF16)],
        compiler_params=_params("parallel", "arbitrary"),
        name="norm_proj",
    )(x, g, wt)


def _qkv_t_kernel(x_ref, g_ref, wq_ref, wk_ref, wv_ref, q_ref, k_ref, v_ref, kb_ref, vb_ref, km_ref):
    xn = _rms_norm_bf16(x_ref[...], g_ref[...])
    nt = lambda w_ref: lax.dot_general(w_ref[...], xn, _NT, preferred_element_type=F32)
    q_ref[...] = nt(wq_ref)
    k = nt(wk_ref)
    v = nt(wv_ref)
    k_ref[...] = k
    v_ref[...] = v
    kb_ref[...] = k.astype(BF16)
    vb_ref[...] = v.astype(BF16)
    n_blk = k.shape[1] // MB_BLOCK
    sums = [jnp.sum(k[:, b * MB_BLOCK:(b + 1) * MB_BLOCK], axis=-1, keepdims=True) for b in range(n_blk)]
    km_ref[0] = jnp.concatenate(sums, axis=-1) * (1.0 / MB_BLOCK)


def _qkv_t(x, g, wt, tm):
    t = x.shape[0]
    q_row = RW_PAD // MB_WIDTH
    w_spec = lambda r: pl.BlockSpec((MB_WIDTH, D_MODEL), lambda i: (q_row + r, 0))
    col = pl.BlockSpec((MB_WIDTH, tm), lambda i: (0, i))
    f32_t = jax.ShapeDtypeStruct((MB_WIDTH, t), F32)
    bf16_t = jax.ShapeDtypeStruct((MB_WIDTH, t), BF16)
    return pl.pallas_call(
        _qkv_t_kernel,
        grid=(t // tm,),
        in_specs=[pl.BlockSpec((tm, D_MODEL), lambda i: (i, 0)), pl.BlockSpec((1, D_MODEL), lambda i: (0, 0)),
                  w_spec(0), w_spec(1), w_spec(2)],
        out_specs=[col] * 5 + [pl.BlockSpec((1, MB_WIDTH, tm // MB_BLOCK), lambda i: (i, 0, 0))],
        out_shape=[f32_t] * 3 + [bf16_t] * 2 + [jax.ShapeDtypeStruct((t // tm, MB_WIDTH, tm // MB_BLOCK), F32)],
        compiler_params=_params("parallel"),
        name="qkv_t",
    )(x, g, wt, wt, wt)


def _rwkv_prep_kernel(sequential, *refs):
    if sequential:
        (p_ref, mu_ref, w0_ref, a0_ref, kk_ref, ka_ref, rk_ref, w2_ref, a2_ref, g2_ref,
         r_o, w_o, k_o, v_o, kn_o, b_o, g_o, bonus_o, carry_ref) = refs
    else:
        (p_ref, prev_ref, mu_ref, w0_ref, a0_ref, kk_ref, ka_ref, rk_ref, w2_ref, a2_ref, g2_ref,
         r_o, w_o, k_o, v_o, kn_o, b_o, g_o, bonus_o) = refs
    p = p_ref[...]
    tm = p.shape[0]
    if sequential:
        @pl.when(pl.program_id(0) == 0)
        def _():
            carry_ref[...] = jnp.zeros_like(carry_ref)

        rows = lax.broadcasted_iota(jnp.int32, p.shape, 0)
        prev = jnp.where(rows == 0, carry_ref[...], pltpu.roll(p, 1, 0))
        carry_ref[...] = p[tm - 1:tm, :]
    else:
        prev = prev_ref[...]
    xs = p + (prev - p) * mu_ref[...]
    r = xs[:, 0:RW_WIDTH]
    k = xs[:, RW_WIDTH:2 * RW_WIDTH]
    v = xs[:, 2 * RW_WIDTH:3 * RW_WIDTH]
    lora = xs[:, 3 * RW_WIDTH:RW_PAD]
    wx = w0_ref[...] + _dot(jnp.tanh(lora).astype(BF16), w2_ref[...])
    w_log = jnp.minimum(wx, 0.0) - jnp.log(1.0 + jnp.exp(-jnp.abs(wx))) - 0.5
    log_decay = -jnp.exp(w_log)
    a = _sigmoid(a0_ref[...] + _dot(lora.astype(BF16), a2_ref[...]))
    g_o[...] = _dot(_sigmoid(lora).astype(BF16), g2_ref[...])
    kk = k * kk_ref[...]
    k_mod = k * (1.0 + (a - 1.0) * ka_ref[...])
    rkr = r * k_mod * rk_ref[...]
    kn_parts = []
    for h in range(RW_HEADS):
        sl = slice(h * RW_HEAD_DIM, (h + 1) * RW_HEAD_DIM)
        kk_h = kk[:, sl]
        nrm = jnp.sqrt(jnp.sum(kk_h * kk_h, axis=-1, keepdims=True))
        kn_parts.append(kk_h / jnp.maximum(nrm, 1e-12))
        bonus_o[:, sl] = jnp.sum(rkr[:, sl], axis=-1, keepdims=True) * v[:, sl]
    kn = jnp.concatenate(kn_parts, axis=-1)
    if sequential:
        ops = (r, log_decay, k_mod, v, kn, kn * a)
    else:
        ops = tuple(x.T for x in (r, jnp.exp(log_decay), k_mod, v, kn, kn * a))
    for o_ref, x in zip((r_o, w_o, k_o, v_o, kn_o, b_o), ops):
        o_ref[...] = x


def _rwkv_prep(proj, prev, vecs, mats, tm):
    m = proj.shape[0]
    sequential = prev is None
    row = lambda n: pl.BlockSpec((1, n), lambda i: (0, 0))
    in_specs = [pl.BlockSpec((tm, RW_PAD), lambda i: (i, 0))]
    args = [proj]
    if not sequential:
        in_specs.append(pl.BlockSpec((tm, RW_PAD), lambda i: (i, 0)))
        args.append(prev)
    in_specs += [row(RW_PAD)] + [row(RW_WIDTH)] * 5
    in_specs += [pl.BlockSpec((LORA_PAD, RW_WIDTH), lambda i: (0, 0))] * 3
    flat = jax.ShapeDtypeStruct((m, RW_WIDTH), F32)
    flat_spec = pl.BlockSpec((tm, RW_WIDTH), lambda i: (i, 0))
    if sequential:
        op_shape, op_spec = flat, flat_spec
    else:
        op_shape = jax.ShapeDtypeStruct((RW_WIDTH, m), F32)
        op_spec = pl.BlockSpec((RW_WIDTH, tm), lambda i: (0, i))
    return pl.pallas_call(
        functools.partial(_rwkv_prep_kernel, sequential),
        grid=(m // tm,),
        in_specs=in_specs,
        out_specs=[op_spec] * 6 + [flat_spec] * 2,
        out_shape=[op_shape] * 6 + [flat] * 2,
        scratch_shapes=[pltpu.VMEM((1, RW_PAD), F32)] if sequential else [],
        compiler_params=_params("arbitrary"),
        name="rwkv_prep",
    )(*args, *vecs, *mats)


RW_CHUNK = 64
RW_CHUNKS_PER_STEP = 4


def _group_norm(y):
    mean = jnp.mean(y, axis=-1, keepdims=True)
    var = jnp.mean(jnp.square(y - mean), axis=-1, keepdims=True)
    return (y - mean) * lax.rsqrt(var + GN_EPS)


def _rwkv_chunk_kernel(r_ref, lw_ref, k_ref, v_ref, kn_ref, b_ref, bonus_ref, g_ref, lnw_ref, lnb_ref,
                       y_ref, s_out_ref, h_ref):
    c = RW_CHUNK
    n = RW_HEAD_DIM

    @pl.when(pl.program_id(0) == 0)
    def _():
        h_ref[...] = jnp.zeros_like(h_ref)

    row = lax.broadcasted_iota(jnp.int32, (c, c), 0)
    col = lax.broadcasted_iota(jnp.int32, (c, c), 1)
    eye = jnp.where(row == col, 1.0, 0.0)
    lower_ones = jnp.where(col <= row, 1.0, 0.0).astype(BF16)
    row2 = lax.broadcasted_iota(jnp.int32, (2 * c, 2 * c), 0)
    col2 = lax.broadcasted_iota(jnp.int32, (2 * c, 2 * c), 1)
    t_idx = jnp.where(row2 >= c, row2 - c, row2)
    s_idx = jnp.where(col2 >= c, col2 - c, col2)
    mask = s_idx < t_idx + jnp.where(row2 >= c, 1, 0)
    heads = range(RW_HEADS)
    sls = [slice(h * n, (h + 1) * n) for h in heads]
    chunks = [slice(ci * c, (ci + 1) * c) for ci in range(r_ref.shape[0] // c)]

    pre = []
    for rows in chunks:
        lw = lw_ref[rows, :]
        lw_hi, lw_mid, lw_lo = _split3(lw)
        cum = _dot(lower_ones, lw_hi) + _dot(lower_ones, lw_mid) + _dot(lower_ones, lw_lo)
        total = cum[c - 1:c, :]
        k = k_ref[rows, :]
        b = b_ref[rows, :]
        v = v_ref[rows, :]
        g_inv = jnp.exp(-cum)
        g_rest = jnp.exp(total - cum)
        pre.append(dict(
            qr=jnp.concatenate([kn_ref[rows, :] * jnp.exp(cum - lw), r_ref[rows, :] * jnp.exp(cum)],
                               axis=0).astype(BF16),
            kdbd=jnp.concatenate([k * g_inv, b * g_inv], axis=0).astype(BF16),
            zk=jnp.concatenate([k * g_rest, b * g_rest], axis=0).astype(BF16),
            v=v, vb=v.astype(BF16), decay_total=jnp.exp(total)))
    pairs = [(p, sl) for p in pre for sl in sls]
    gms = [jnp.where(mask, lax.dot_general(p['qr'][:, sl], p['kdbd'][:, sl], _NT, preferred_element_type=F32), 0.0)
           for p, sl in pairs]
    akkv = [_dot(gm[0:c, 0:c].astype(BF16), p['vb'][:, sl]) for gm, (p, sl) in zip(gms, pairs)]
    powers = [gm[0:c, c:2 * c].astype(BF16) for gm in gms]
    t_invs = [eye - gm[0:c, c:2 * c] for gm in gms]
    for _ in range(int(math.log2(c)) - 1):
        powers = [_dot(pw, pw).astype(BF16) for pw in powers]
        t_invs = [t + _dot(pw, t.astype(BF16)) for pw, t in zip(powers, t_invs)]

    for ci, rows in enumerate(chunks):
        p = pre[ci]
        mine = slice(ci * RW_HEADS, (ci + 1) * RW_HEADS)
        h_old = [h_ref[h] for h in heads]
        qhs = [_dot(p['qr'][:, sl], ho.astype(BF16)) for sl, ho in zip(sls, h_old)]
        us = [_dot(t.astype(BF16), (qh[0:c] + av).astype(BF16))
              for t, qh, av in zip(t_invs[mine], qhs, akkv[mine])]
        vus = [jnp.concatenate([p['v'][:, sl], -u], axis=0).astype(BF16) for sl, u in zip(sls, us)]
        outs = [_group_norm(qh[c:2 * c] + _dot(gm[c:2 * c, :].astype(BF16), vu))
                for qh, gm, vu in zip(qhs, gms[mine], vus)]
        for h in heads:
            decay_col = jnp.sum(eye * p['decay_total'][:, sls[h]], axis=-1, keepdims=True)
            h_ref[h] = decay_col * h_old[h] + lax.dot_general(p['zk'][:, sls[h]], vus[h], _TN,
                                                              preferred_element_type=F32)
        yn = jnp.concatenate(outs, axis=-1)
        y_ref[rows, :] = ((yn * lnw_ref[...] + lnb_ref[...] + bonus_ref[rows, :]) * g_ref[rows, :]).astype(BF16)

    @pl.when(pl.program_id(0) == pl.num_programs(0) - 1)
    def _():
        for h in range(RW_HEADS):
            s_out_ref[h] = h_ref[h].T


def _rwkv_chunked(ops, bonus, g, ln_w, ln_b):
    m = bonus.shape[0]
    step_rows = RW_CHUNK * RW_CHUNKS_PER_STEP
    spec = pl.BlockSpec((step_rows, RW_WIDTH), lambda i: (i, 0))
    row = pl.BlockSpec((1, RW_WIDTH), lambda i: (0, 0))
    state_shape = (RW_HEADS, RW_HEAD_DIM, RW_HEAD_DIM)
    return pl.pallas_call(
        _rwkv_chunk_kernel,
        grid=(m // step_rows,),
        in_specs=[spec] * 8 + [row, row],
        out_specs=[spec, pl.BlockSpec(state_shape, lambda i: (0, 0, 0))],
        out_shape=[jax.ShapeDtypeStruct((m, RW_WIDTH), BF16), jax.ShapeDtypeStruct(state_shape, F32)],
        scratch_shapes=[pltpu.VMEM(state_shape, F32)],
        compiler_params=_params("arbitrary"),
        name="rwkv_chunked",
    )(*ops, bonus, g, ln_w, ln_b)


RW_STEP_HEADS = 2


def _rwkv_single_step_kernel(r_ref, w_ref, k_ref, v_ref, kn_ref, b_ref, s0_ref, *rest):
    y_ref, s_out_ref = rest[-2:]
    n = RW_HEAD_DIM
    for hh in range(s0_ref.shape[0]):
        rows = slice(hh * n, (hh + 1) * n)
        r, w, k, kn, b = (ref[rows, :] for ref in (r_ref, w_ref, k_ref, kn_ref, b_ref))

        def body(vi, carry):
            s = s0_ref[hh, vi]
            sa = jnp.sum(s * kn, axis=0, keepdims=True)
            s = s * w - sa * b + v_ref[pl.ds(hh * n + vi, 1), :] * k
            s_out_ref[hh, vi] = s
            y_ref[pl.ds(hh * n + vi, 1), :] = jnp.sum(s * r, axis=0, keepdims=True)
            return carry

        lax.fori_loop(0, n, body, 0)


def _rwkv_single_step(ops, states_t, layer, new_states_t):
    b = ops[0].shape[1]
    hs = RW_STEP_HEADS
    spec = pl.BlockSpec((hs * RW_HEAD_DIM, b), lambda i: (i, 0))
    st_spec = pl.BlockSpec((None, hs, RW_HEAD_DIM, RW_HEAD_DIM, b), lambda i: (layer, i, 0, 0, 0))
    carried = [] if new_states_t is None else [new_states_t]
    return pl.pallas_call(
        _rwkv_single_step_kernel,
        grid=(RW_HEADS // hs,),
        in_specs=[spec] * 6 + [st_spec] + [pl.BlockSpec(memory_space=pl.ANY)] * len(carried),
        out_specs=[spec, st_spec],
        out_shape=[jax.ShapeDtypeStruct((RW_WIDTH, b), F32), jax.ShapeDtypeStruct(states_t.shape, F32)],
        input_output_aliases={7: 1} if carried else {},
        compiler_params=_params("parallel"),
        name="rwkv_single_step",
    )(*ops, states_t, *carried)


def _rwkv_post_kernel(yt_ref, bonus_ref, g_ref, lnw_ref, lnb_ref, o_ref):
    y = yt_ref[...].T
    n = RW_HEAD_DIM
    yn = jnp.concatenate([_group_norm(y[:, h * n:(h + 1) * n]) for h in range(RW_HEADS)], axis=-1)
    out = (yn * lnw_ref[...] + lnb_ref[...] + bonus_ref[...]) * g_ref[...]
    o_ref[...] = out.astype(BF16)


def _rwkv_post(y_t, bonus, g, ln_w, ln_b):
    b = bonus.shape[0]
    flat_spec = pl.BlockSpec((b, RW_WIDTH), lambda i: (0, 0))
    row = pl.BlockSpec((1, RW_WIDTH), lambda i: (0, 0))
    return pl.pallas_call(
        _rwkv_post_kernel,
        grid=(1,),
        in_specs=[pl.BlockSpec((RW_WIDTH, b), lambda i: (0, 0)), flat_spec, flat_spec, row, row],
        out_specs=flat_spec,
        out_shape=jax.ShapeDtypeStruct((b, RW_WIDTH), BF16),
        compiler_params=_params("arbitrary"),
        name="rwkv_post",
    )(y_t, bonus, g, ln_w, ln_b)


CV_HALO = 32


def _conv_finish(y, cb_ref, lnw_ref, lnb_ref):
    y = y + cb_ref[...]
    mean = jnp.mean(y, axis=-1, keepdims=True)
    var = jnp.mean(jnp.square(y - mean), axis=-1, keepdims=True)
    yn = (y - mean) * lax.rsqrt(var + LN_EPS) * lnw_ref[...] + lnb_ref[...]
    return _silu(yn).astype(BF16)


def _conv_seq_kernel(p_ref, cw_ref, cb_ref, lnw_ref, lnb_ref, o_ref, tail_ref, ext_ref):
    tm = p_ref.shape[0]
    pad = CV_HALO - (CV_K - 1)

    @pl.when(pl.program_id(0) == 0)
    def _():
        ext_ref[0:CV_HALO, :] = jnp.zeros((CV_HALO, CV_WIDTH), F32)

    p = p_ref[...]
    u = p[:, 0:CV_WIDTH] * _sigmoid(p[:, CV_WIDTH:2 * CV_WIDTH])
    ext_ref[CV_HALO:CV_HALO + tm, :] = u
    acc = jnp.zeros((tm, CV_WIDTH), F32)
    for j in range(CV_K):
        acc = acc + ext_ref[pad + j:pad + j + tm, :] * cw_ref[j:j + 1, :]
    o_ref[...] = _conv_finish(acc, cb_ref, lnw_ref, lnb_ref)
    tail = ext_ref[tm:tm + CV_HALO, :]
    ext_ref[0:CV_HALO, :] = tail
    tail_ref[...] = tail


def _conv_seq(proj, cw, cb, ln_w, ln_b, tm):
    m = proj.shape[0]
    row = pl.BlockSpec((1, CV_WIDTH), lambda i: (0, 0))
    return pl.pallas_call(
        _conv_seq_kernel,
        grid=(m // tm,),
        in_specs=[pl.BlockSpec((tm, 2 * CV_WIDTH), lambda i: (i, COL_CV)),
                  pl.BlockSpec((CV_K, CV_WIDTH), lambda i: (0, 0)), row, row, row],
        out_specs=[pl.BlockSpec((tm, CV_WIDTH), lambda i: (i, 0)),
                   pl.BlockSpec((CV_HALO, CV_WIDTH), lambda i: (0, 0))],
        out_shape=[jax.ShapeDtypeStruct((m, CV_WIDTH), BF16),
                   jax.ShapeDtypeStruct((CV_HALO, CV_WIDTH), F32)],
        scratch_shapes=[pltpu.VMEM((CV_HALO + tm, CV_WIDTH), F32)],
        compiler_params=_params("arbitrary"),
        name="conv_seq",
    )(proj, cw, cb, ln_w, ln_b)


def _conv_step_kernel(p_ref, buf_ref, cw_ref, cb_ref, lnw_ref, lnb_ref, o_ref, nbuf_ref):
    p = p_ref[...]
    u = p[:, :, 0:CV_WIDTH] * _sigmoid(p[:, :, CV_WIDTH:2 * CV_WIDTH])
    buf = buf_ref[...]
    acc = (jnp.sum(buf * cw_ref[0:CV_K - 1, :][None], axis=1, keepdims=True)
           + u * cw_ref[CV_K - 1:CV_K, :][None])
    o_ref[...] = _conv_finish(acc, cb_ref, lnw_ref, lnb_ref)
    nbuf_ref[:, 0:CV_K - 2, :] = buf[:, 1:CV_K - 1, :]
    nbuf_ref[:, CV_K - 2:CV_K - 1, :] = u


def _conv_step(proj, buf, cw, cb, ln_w, ln_b, tb):
    m = proj.shape[0]
    row = pl.BlockSpec((1, CV_WIDTH), lambda i: (0, 0))
    buf_spec = pl.BlockSpec((tb, CV_K - 1, CV_WIDTH), lambda i: (i, 0, 0))
    y = pl.pallas_call(
        _conv_step_kernel,
        grid=(m // tb,),
        in_specs=[pl.BlockSpec((tb, 1, 2 * CV_WIDTH), lambda i: (i, 0, COL_CV)), buf_spec,
                  pl.BlockSpec((CV_K, CV_WIDTH), lambda i: (0, 0)), row, row, row],
        out_specs=[pl.BlockSpec((tb, 1, CV_WIDTH), lambda i: (i, 0, 0)), buf_spec],
        out_shape=[jax.ShapeDtypeStruct((m, 1, CV_WIDTH), BF16),
                   jax.ShapeDtypeStruct(buf.shape, F32)],
        compiler_params=_params("parallel"),
        name="conv_step",
    )(proj.reshape(m, 1, proj.shape[1]), buf, cw, cb, ln_w, ln_b)
    return y[0].reshape(m, CV_WIDTH), y[1]


def _bucket_thresholds():
    n = np.arange(0, 4 * MAX_DISTANCE, dtype=np.int64)
    max_exact = NUM_BUCKETS // 2
    nf = np.maximum(n, 1).astype(np.float32)
    large = max_exact + (np.log(nf / np.float32(max_exact)) / np.float32(math.log(MAX_DISTANCE / max_exact))
                         * np.float32(NUM_BUCKETS - max_exact)).astype(np.int32)
    large = np.minimum(large, NUM_BUCKETS - 1)
    bucket = np.where(n < max_exact, n, large)
    assert np.all(np.diff(bucket) >= 0) and bucket[-1] == NUM_BUCKETS - 1
    return [int(np.argmax(bucket >= b)) for b in range(NUM_BUCKETS)]


_BUCKET_THR = _bucket_thresholds()


def _bias_from_distance(n, rb_of_bucket):
    bias = jnp.zeros(n.shape, F32) + rb_of_bucket(NUM_BUCKETS - 1)
    for b in range(NUM_BUCKETS - 2, -1, -1):
        bias = jnp.where(n < _BUCKET_THR[b + 1], rb_of_bucket(b), bias)
    return bias


def _rank_select(scores, n_rows, limit):
    idx = lax.broadcasted_iota(jnp.int32, scores.shape, 0)
    beaten = jnp.zeros(scores.shape, F32)
    for j in range(n_rows):
        row = scores[j:j + 1, :]
        ahead = jnp.where(row > scores, 1.0, jnp.where(row == scores, jnp.where(idx > j, 1.0, 0.0), 0.0))
        beaten = beaten + ahead * jnp.where(j < limit, 1.0, 0.0)
    return jnp.where(idx < limit, jnp.where(beaten < MB_TOPK, 1.0, 0.0), 0.0)


MB_HEADS_PER_STEP = 4
MB_SUM_ROWS = 16
MB_BLOCKS_PER_TRIP = 2


def _softmax_blocks(carries, scores, vt_blks):
    m_new = [jnp.maximum(c[0], jnp.max(s, axis=0, keepdims=True)) for c, s in zip(carries, scores)]
    alpha = [jnp.exp2(c[0] - mn) for c, mn in zip(carries, m_new)]
    p = [jnp.exp2(s - mn).astype(BF16) for s, mn in zip(scores, m_new)]
    pv = [_dot(_with_ones_rows(vt), pp) for vt, pp in zip(vt_blks, p)]
    return [(mn, a * c[1] + x) for mn, a, c, x in zip(m_new, alpha, carries, pv)]


def _with_ones_rows(vt_blk):
    return jnp.concatenate([vt_blk, jnp.ones((MB_SUM_ROWS, vt_blk.shape[1]), BF16)], axis=0)


def _moba_seq_kernel(rb_ref, qt_ref, k_ref, vt_ref, km_ref, o_ref, bias_ref, rowterm_ref):
    hp = qt_ref.shape[0]
    h0 = pl.program_id(0) * hp
    qi = pl.program_id(1)
    nb = km_ref.shape[1]
    log2e = math.log2(math.e)
    scale = MB_HEAD_DIM ** -0.5 * log2e
    kidx = lax.broadcasted_iota(jnp.int32, (MB_BLOCK, MB_BLOCK), 0)
    qidx = lax.broadcasted_iota(jnp.int32, (MB_BLOCK, MB_BLOCK), 1)

    @pl.when(qi == 0)
    def _():
        dist = qidx - kidx
        for hh in range(hp):
            rb = lambda b: rb_ref[b, h0 + hh] * log2e
            bias_ref[hh, 0] = jnp.where(dist >= 0, _bias_from_distance(jnp.maximum(dist, 0), rb), NEG_INF)
            bias_ref[hh, 1] = _bias_from_distance(dist + MB_BLOCK, rb)

    near = jnp.maximum(qi - 1, 0)
    qtb, carries = [], []
    for hh in range(hp):
        qt = qt_ref[hh]
        block_scores = _dot_f32(km_ref[hh], qt)
        sel = _rank_select(block_scores, nb, qi)
        rowterm_ref[hh] = jnp.where(sel > 0.0, rb_ref[NUM_BUCKETS - 1, h0 + hh] * log2e, NEG_INF)
        qtb.append((qt * scale).astype(BF16))
        s = _dot(k_ref[hh, qi], qtb[hh]) + bias_ref[hh, 0]
        m0 = jnp.max(s, axis=0, keepdims=True)
        p = jnp.exp2(s - m0).astype(BF16)
        carries.append((m0, _dot(_with_ones_rows(vt_ref[hh, qi]), p)))

    heads = range(hp)
    sel_near = [jnp.where(rowterm_ref[hh, pl.ds(near, 1), :] > 0.5 * NEG_INF, 0.0, NEG_INF) for hh in heads]
    scores = [_dot(k_ref[hh, near], qtb[hh]) + bias_ref[hh, 1] + sel_near[hh] for hh in heads]
    carries = _softmax_blocks(carries, scores, [vt_ref[hh, near] for hh in heads])

    def body(t, carry):
        kjs = [t * MB_BLOCKS_PER_TRIP + u for u in range(MB_BLOCKS_PER_TRIP)]
        safe = [jnp.minimum(kj, nb - 1) for kj in kjs]
        raw = [[_dot(k_ref[hh, kc], qtb[hh]) for hh in heads] for kc in safe]
        for u, (kj, kc) in enumerate(zip(kjs, safe)):
            scores = [raw[u][hh] + jnp.where(kj < near, rowterm_ref[hh, pl.ds(kc, 1), :], NEG_INF) for hh in heads]
            carry = _softmax_blocks(carry, scores, [vt_ref[hh, kc] for hh in heads])
        return tuple(carry)

    n_trips = (near + MB_BLOCKS_PER_TRIP - 1) // MB_BLOCKS_PER_TRIP
    carries = lax.fori_loop(0, n_trips, body, tuple(carries))
    for hh in range(hp):
        acc = carries[hh][1]
        o_ref[hh] = (acc[0:MB_HEAD_DIM] / acc[MB_HEAD_DIM:MB_HEAD_DIM + 1]).astype(BF16)


def _moba_seq(rel_bias, q_t, k_t, v_t, km_tiles):
    t = q_t.shape[1]
    nb = t // MB_BLOCK
    hp = MB_HEADS_PER_STEP
    blocks = lambda x: x.reshape(MB_HEADS, MB_HEAD_DIM, nb, MB_BLOCK)
    qt = q_t.reshape(MB_HEADS, MB_HEAD_DIM, t)
    kb = blocks(k_t).transpose(0, 2, 3, 1)
    vt = blocks(v_t).transpose(0, 2, 1, 3)
    km = km_tiles.transpose(1, 0, 2).reshape(MB_HEADS, MB_HEAD_DIM, nb).transpose(0, 2, 1)
    ot = pl.pallas_call(
        _moba_seq_kernel,
        grid=(MB_HEADS // hp, nb),
        in_specs=[pl.BlockSpec(memory_space=pltpu.SMEM),
                  pl.BlockSpec((hp, MB_HEAD_DIM, MB_BLOCK), lambda h, i: (h, 0, i)),
                  pl.BlockSpec((hp, nb, MB_BLOCK, MB_HEAD_DIM), lambda h, i: (h, 0, 0, 0)),
                  pl.BlockSpec((hp, nb, MB_HEAD_DIM, MB_BLOCK), lambda h, i: (h, 0, 0, 0)),
                  pl.BlockSpec((hp, nb, MB_HEAD_DIM), lambda h, i: (h, 0, 0))],
        out_specs=pl.BlockSpec((hp, MB_HEAD_DIM, MB_BLOCK), lambda h, i: (h, 0, i)),
        out_shape=jax.ShapeDtypeStruct((MB_HEADS, MB_HEAD_DIM, t), BF16),
        scratch_shapes=[pltpu.VMEM((hp, 2, MB_BLOCK, MB_BLOCK), F32), pltpu.VMEM((hp, nb, MB_BLOCK), F32)],
        compiler_params=_params("parallel", "arbitrary"),
        name="moba_seq",
    )(rel_bias, qt, kb, vt, km)
    return ot.reshape(MB_WIDTH, t)


def _moba_paged_kernel(n_pages, pt_ref, q_ref, kn_ref, vn_ref, rbt_ref, *refs):
    k_pages = refs[:n_pages]
    v_pages = refs[n_pages:2 * n_pages]
    o_ref, bias_ref = refs[2 * n_pages:]
    past = n_pages * PAGE_SIZE
    n_blocks = past // MB_BLOCK
    pages_per_block = MB_BLOCK // PAGE_SIZE
    scale = MB_HEAD_DIM ** -0.5
    rb = lambda b: rbt_ref[:, :, b:b + 1]

    @pl.when(pl.program_id(0) == 0)
    def _():
        dist = PAGE_SIZE - lax.broadcasted_iota(jnp.int32, (MB_HEADS, 1, PAGE_SIZE), 2)
        bias_ref[...] = _bias_from_distance(dist, rb)

    di = lax.broadcasted_iota(jnp.int32, (MB_HEAD_DIM, MB_HEAD_DIM), 0)
    dj = lax.broadcasted_iota(jnp.int32, (MB_HEAD_DIM, MB_HEAD_DIM), 1)
    eye = jnp.where(di == dj, 1.0, 0.0)[None]
    q = q_ref[0]
    q_col = jnp.sum(eye * q, axis=-1, keepdims=True)
    raw = [jnp.sum(k_pages[pg][...] * q_col, axis=1, keepdims=True) for pg in range(n_pages)]
    scores = []
    for j in range(n_blocks):
        tot = raw[pages_per_block * j]
        for e in range(1, pages_per_block):
            tot = tot + raw[pages_per_block * j + e]
        scores.append(jnp.sum(tot, axis=-1, keepdims=True) * (1.0 / MB_BLOCK))
    far_bias = rb(NUM_BUCKETS - 1)
    rowterms = []
    for j in range(n_blocks):
        beaten = jnp.zeros((MB_HEADS, 1, 1), F32)
        for i in range(n_blocks):
            if i != j:
                ahead = scores[i] > scores[j] if i > j else scores[i] >= scores[j]
                beaten = beaten + jnp.where(ahead, 1.0, 0.0)
        rowterms.append(jnp.where(beaten < MB_TOPK, far_bias, NEG_INF))
    lg_own = jnp.sum(kn_ref[0] * q, axis=-1, keepdims=True) * scale + rb(0)
    logits = []
    for pg in range(n_pages):
        term = rowterms[pg // pages_per_block]
        if pg == n_pages - 1:
            term = bias_ref[...] + jnp.where(term > 0.5 * NEG_INF, 0.0, NEG_INF)
        logits.append(raw[pg] * scale + term)
    m_row = logits[0]
    for lg in logits[1:]:
        m_row = jnp.maximum(m_row, lg)
    m = jnp.maximum(jnp.max(m_row, axis=-1, keepdims=True), lg_own)
    p_own = jnp.exp(lg_own - m)
    l_row = jnp.zeros((MB_HEADS, 1, PAGE_SIZE), F32)
    acc = jnp.zeros((MB_HEADS, MB_HEAD_DIM, PAGE_SIZE), F32)
    for pg in range(n_pages):
        p = jnp.exp(logits[pg] - m)
        l_row = l_row + p
        acc = acc + v_pages[pg][...] * p
    out_col = jnp.sum(acc, axis=-1, keepdims=True)
    out = jnp.sum(eye * out_col, axis=1, keepdims=True) + p_own * vn_ref[0]
    l = jnp.sum(l_row, axis=-1, keepdims=True) + p_own
    o_ref[0] = (out / l).astype(BF16)


def _moba_paged(rel_bias, q, k_new, v_new, cache_k, cache_v, layer, page_table):
    b, n_pages = page_table.shape
    assert PAGE_SIZE >= _BUCKET_THR[NUM_BUCKETS - 1] and MB_BLOCK % PAGE_SIZE == 0
    head4 = pl.BlockSpec((1, MB_HEADS, 1, MB_HEAD_DIM), lambda i, pt: (i, 0, 0, 0))

    def page_spec(pg):
        return pl.BlockSpec((None, None, MB_HEADS, MB_HEAD_DIM, PAGE_SIZE),
                            lambda i, pt: (layer, pt[i, pg], 0, 0, 0))

    grid_spec = pltpu.PrefetchScalarGridSpec(
        num_scalar_prefetch=1,
        grid=(b,),
        in_specs=[head4, head4, head4, pl.BlockSpec((MB_HEADS, 1, NUM_BUCKETS), lambda i, pt: (0, 0, 0))]
        + [page_spec(pg) for pg in range(n_pages)] * 2,
        out_specs=head4,
        scratch_shapes=[pltpu.VMEM((MB_HEADS, 1, PAGE_SIZE), F32)],
    )
    h4 = lambda x: x.reshape(b, MB_HEADS, 1, MB_HEAD_DIM)
    kt = cache_k.transpose(0, 1, 3, 4, 2)
    vt = cache_v.transpose(0, 1, 3, 4, 2)
    out = pl.pallas_call(
        functools.partial(_moba_paged_kernel, n_pages),
        grid_spec=grid_spec,
        out_shape=jax.ShapeDtypeStruct((b, MB_HEADS, 1, MB_HEAD_DIM), BF16),
        compiler_params=_params("arbitrary"),
        name="moba_paged",
    )(page_table, h4(q), h4(k_new), h4(v_new), rel_bias.T.reshape(MB_HEADS, 1, NUM_BUCKETS),
      *([kt] * n_pages), *([vt] * n_pages))
    return out.reshape(b, MB_WIDTH)


def _merge_kernel(mb_feature_major, yrw_ref, ymb_ref, ycv_ref, g0_ref, g1_ref, g2_ref, wrw_ref, wmb_ref, wcv_ref,
                  o_ref):
    if mb_feature_major:
        y_mb = lax.dot_general(ymb_ref[...], wmb_ref[...], _TN, preferred_element_type=F32)
    else:
        y_mb = _dot(ymb_ref[...], wmb_ref[...])
    merged = (g0_ref[...] * _dot(yrw_ref[...], wrw_ref[...]) + g1_ref[...] * y_mb
              + g2_ref[...] * _dot(ycv_ref[...], wcv_ref[...]))
    o_ref[...] = merged.astype(BF16)


def _merge(y_rw, y_mb, y_cv, gates, w_rw, w_mb, w_cv, tm, mb_feature_major):
    m = gates.shape[0]
    rows = lambda n: pl.BlockSpec((tm, n), lambda i: (i, 0))
    mb_spec = pl.BlockSpec((MB_WIDTH, tm), lambda i: (0, i)) if mb_feature_major else rows(MB_WIDTH)
    gate = lambda br: pl.BlockSpec((tm, D_MODEL), lambda i: (i, br))
    full = lambda n: pl.BlockSpec((n, D_MODEL), lambda i: (0, 0))
    return pl.pallas_call(
        functools.partial(_merge_kernel, mb_feature_major),
        grid=(m // tm,),
        in_specs=[rows(RW_WIDTH), mb_spec, rows(CV_WIDTH), gate(0), gate(1), gate(2),
                  full(RW_WIDTH), full(MB_WIDTH), full(CV_WIDTH)],
        out_specs=rows(D_MODEL),
        out_shape=jax.ShapeDtypeStruct((m, D_MODEL), BF16),
        compiler_params=_params("parallel"),
        name="branch_merge",
    )(y_rw, y_mb, y_cv, gates, gates, gates, w_rw, w_mb, w_cv)


def _out_proj_kernel(h_ref, x_ref, w_ref, o_ref):
    o_ref[...] = h_ref[...] + _dot(x_ref[...], w_ref[...])


def _out_proj(h, merged, w_out, tm):
    m = h.shape[0]
    rows = pl.BlockSpec((tm, D_MODEL), lambda i: (i, 0))
    return pl.pallas_call(
        _out_proj_kernel,
        grid=(m // tm,),
        in_specs=[rows, rows, pl.BlockSpec((D_MODEL, D_MODEL), lambda i: (0, 0))],
        out_specs=rows,
        out_shape=jax.ShapeDtypeStruct((m, D_MODEL), F32),
        compiler_params=_params("parallel"),
        name="out_proj",
    )(h, merged, w_out)


def _ffn_kernel(h_ref, g_ref, wg_ref, wu_ref, wd_ref, o_ref, xn_ref):
    @pl.when(pl.program_id(1) == 0)
    def _():
        x = h_ref[...]
        ms = jnp.mean(x * x, axis=-1, keepdims=True)
        xn_ref[...] = (x * lax.rsqrt(ms + NORM_EPS) * g_ref[...]).astype(BF16)
        o_ref[...] = x

    xn = xn_ref[...]
    mid = _silu(_dot(xn, wg_ref[...])) * _dot(xn, wu_ref[...])
    o_ref[...] += _dot(mid.astype(BF16), wd_ref[...])


def _ffn(h, g, wg, wu, wd, tm, tf):
    m = h.shape[0]
    d_ff = wg.shape[1]
    rows = pl.BlockSpec((tm, D_MODEL), lambda i, f: (i, 0))
    return pl.pallas_call(
        _ffn_kernel,
        grid=(m // tm, d_ff // tf),
        in_specs=[rows, pl.BlockSpec((1, D_MODEL), lambda i, f: (0, 0)),
                  pl.BlockSpec((D_MODEL, tf), lambda i, f: (0, f)),
                  pl.BlockSpec((D_MODEL, tf), lambda i, f: (0, f)),
                  pl.BlockSpec((tf, D_MODEL), lambda i, f: (f, 0))],
        out_specs=rows,
        out_shape=jax.ShapeDtypeStruct((m, D_MODEL), F32),
        scratch_shapes=[pltpu.VMEM((tm, D_MODEL), BF16)],
        compiler_params=_params("parallel", "arbitrary"),
        name="ffn",
    )(h, g, wg, wu, wd)


def _dot_f32_nt(a, b):
    nt = lambda x, y: lax.dot_general(x, y, _NT, preferred_element_type=F32)
    a_hi, a_mid, a_lo = _split3(a)
    b_hi, b_mid, b_lo = _split3(b)
    return (nt(a_hi, b_hi) + (nt(a_hi, b_mid) + nt(a_mid, b_hi))
            + (nt(a_mid, b_mid) + nt(a_hi, b_lo) + nt(a_lo, b_hi)))


def _router_kernel(h_ref, g_ref, wt_ref, b_ref, before_ref, xn_ref, gate_ref, pos_ref, cnt_ref):
    x = h_ref[...]
    ms = jnp.mean(x * x, axis=-1, keepdims=True)
    xn = x * lax.rsqrt(ms + NORM_EPS) * g_ref[...]
    xn_ref[...] = xn.astype(BF16)
    logits = _dot_f32_nt(wt_ref[...], xn) + b_ref[...]
    idx = lax.broadcasted_iota(jnp.int32, logits.shape, 0).astype(F32)
    m1 = jnp.max(logits, axis=0, keepdims=True)
    i1 = jnp.min(jnp.where(logits == m1, idx, float(N_EXPERTS)), axis=0, keepdims=True)
    rest = jnp.where(idx == i1, -jnp.inf, logits)
    m2 = jnp.max(rest, axis=0, keepdims=True)
    i2 = jnp.min(jnp.where(rest == m2, idx, float(N_EXPERTS)), axis=0, keepdims=True)
    e2 = jnp.exp(m2 - m1)
    gates = jnp.where(idx == i1, 1.0 / (1.0 + e2), 0.0) + jnp.where(idx == i2, e2 / (1.0 + e2), 0.0)
    member = jnp.where(gates > 0.0, 1.0, 0.0)
    gate_ref[0] = gates
    pos_ref[0] = _dot(member.astype(BF16), before_ref[...])
    cnt_ref[0] = jnp.sum(member, axis=-1, keepdims=True).astype(jnp.int32)


def _router(h, g, w_router, b_router, tm):
    m = h.shape[0]
    n_tiles = m // tm
    before = jnp.asarray(np.triu(np.ones((tm, tm), np.float32), 1), BF16)
    tile3 = pl.BlockSpec((1, N_EXPERTS, tm), lambda i: (i, 0, 0))
    per_tile = jax.ShapeDtypeStruct((n_tiles, N_EXPERTS, tm), F32)
    return pl.pallas_call(
        _router_kernel,
        grid=(n_tiles,),
        in_specs=[pl.BlockSpec((tm, D_MODEL), lambda i: (i, 0)),
                  pl.BlockSpec((1, D_MODEL), lambda i: (0, 0)),
                  pl.BlockSpec((N_EXPERTS, D_MODEL), lambda i: (0, 0)),
                  pl.BlockSpec((N_EXPERTS, 1), lambda i: (0, 0)),
                  pl.BlockSpec((tm, tm), lambda i: (0, 0))],
        out_specs=[pl.BlockSpec((tm, D_MODEL), lambda i: (i, 0)), tile3, tile3,
                   pl.BlockSpec((1, N_EXPERTS, 1), lambda i: (i, 0, 0))],
        out_shape=[jax.ShapeDtypeStruct((m, D_MODEL), BF16), per_tile, per_tile,
                   jax.ShapeDtypeStruct((n_tiles, N_EXPERTS, 1), jnp.int32)],
        compiler_params=_params("parallel"),
        name="moe_router",
    )(h, g, w_router.T, b_router.reshape(N_EXPERTS, 1), before)


def _moe_kernel(sub, n_slices, cnt_ref, xn_ref, gate_ref, pos_ref, *refs):
    weights = [refs[3 * j:3 * j + 3] for j in range(MOE_SLICES_PER_STEP)]
    o_ref, xg_ref, gg_ref, acc_ref = refs[3 * MOE_SLICES_PER_STEP:]
    i = pl.program_id(0)
    e = pl.program_id(1)
    f = pl.program_id(2)
    tm = xn_ref.shape[0]
    n_sub = (cnt_ref[i * N_EXPERTS + e] + (sub - 1)) // sub
    pos_row = pos_ref[0, pl.ds(e, 1), :]
    gate_row = gate_ref[0, pl.ds(e, 1), :]
    slot = lax.broadcasted_iota(jnp.int32, (sub, tm), 0)

    def one_hot(s):
        wanted = (slot + s * sub).astype(F32)
        return jnp.where(gate_row > 0.0, jnp.where(pos_row == wanted, 1.0, 0.0), 0.0)

    def rows_of(s):
        return pl.ds(pl.multiple_of(s * sub, sub), sub)

    @pl.when((e == 0) & (f == 0))
    def _():
        o_ref[...] = jnp.zeros_like(o_ref)

    @pl.when(f == 0)
    def _():
        def pack(s, carry):
            pick = one_hot(s)
            xg_ref[rows_of(s), :] = _dot(pick.astype(BF16), xn_ref[...]).astype(BF16)
            gg_ref[rows_of(s), :] = jnp.sum(pick * gate_row, axis=-1, keepdims=True)
            acc_ref[rows_of(s), :] = jnp.zeros((sub, D_MODEL), F32)
            return carry

        lax.fori_loop(0, n_sub, pack, 0)

    for j, (wg_ref, wu_ref, wd_ref) in enumerate(weights):
        def expert(s, carry, wg_ref=wg_ref, wu_ref=wu_ref, wd_ref=wd_ref):
            x = xg_ref[rows_of(s), :]
            mid = _silu(_dot(x, wg_ref[0])) * _dot(x, wu_ref[0]) * gg_ref[rows_of(s), :]
            acc_ref[rows_of(s), :] += _dot(mid.astype(BF16), wd_ref[0])
            return carry

        lax.fori_loop(0, jnp.where(f * MOE_SLICES_PER_STEP + j < n_slices, n_sub, 0), expert, 0)

    @pl.when(f == pl.num_programs(2) - 1)
    def _():
        def unpack(s, carry):
            pick = one_hot(s).astype(BF16)
            o_ref[...] += lax.dot_general(pick, acc_ref[rows_of(s), :].astype(BF16), _TN, preferred_element_type=F32)
            return carry

        lax.fori_loop(0, n_sub, unpack, 0)


def _moe(xn, gates, pos, counts, wg, wu, wd, tm, tf):
    m = xn.shape[0]
    n_e, _, d_ff = wg.shape
    sub = min(tm, -(-(tm * TOP_K * 5 // (N_EXPERTS * 4)) // 16) * 16)
    packed = -(-tm // sub) * sub
    n_slices = d_ff // tf
    per_step = MOE_SLICES_PER_STEP
    tile3 = pl.BlockSpec((1, N_EXPERTS, tm), lambda i, e, f, cnt: (i, 0, 0))

    def slice_specs(j):
        fj = lambda f: jnp.minimum(f * per_step + j, n_slices - 1)
        up = pl.BlockSpec((1, D_MODEL, tf), lambda i, e, f, cnt: (e, 0, fj(f)))
        down = pl.BlockSpec((1, tf, D_MODEL), lambda i, e, f, cnt: (e, fj(f), 0))
        return [up, up, down]

    grid_spec = pltpu.PrefetchScalarGridSpec(
        num_scalar_prefetch=1,
        grid=(m // tm, n_e, -(-n_slices // per_step)),
        in_specs=[pl.BlockSpec((tm, D_MODEL), lambda i, e, f, cnt: (i, 0)), tile3, tile3]
        + [spec for j in range(per_step) for spec in slice_specs(j)],
        out_specs=pl.BlockSpec((tm, D_MODEL), lambda i, e, f, cnt: (i, 0)),
        scratch_shapes=[pltpu.VMEM((packed, D_MODEL), BF16), pltpu.VMEM((packed, 1), F32),
                        pltpu.VMEM((packed, D_MODEL), F32)],
    )
    return pl.pallas_call(
        functools.partial(_moe_kernel, sub, n_slices),
        grid_spec=grid_spec,
        out_shape=jax.ShapeDtypeStruct((m, D_MODEL), F32),
        compiler_params=_params("parallel", "arbitrary", "arbitrary"),
        name="moe_experts",
    )(counts.reshape(-1), xn, gates, pos, *([wg, wu, wd] * per_step))


MOE_SLICES_PER_STEP = 2
MOE_FF_SLICE = 256
FFN_FF_SLICE = 512
SAMPLE_CONV_ROWS = 8


def _final_norm_kernel(n_in, *refs):
    g_ref, o_ref = refs[n_in], refs[n_in + 1]
    x = refs[0][...]
    for r in refs[1:n_in]:
        x = x + r[...]
    ms = jnp.mean(x * x, axis=-1, keepdims=True)
    o_ref[...] = x * lax.rsqrt(ms + NORM_EPS) * g_ref[...]


def _final_norm(parts, g, tm):
    m = parts[0].shape[0]
    rows = pl.BlockSpec((tm, D_MODEL), lambda i: (i, 0))
    return pl.pallas_call(
        functools.partial(_final_norm_kernel, len(parts)),
        grid=(m // tm,),
        in_specs=[rows] * len(parts) + [pl.BlockSpec((1, D_MODEL), lambda i: (0, 0))],
        out_specs=rows,
        out_shape=jax.ShapeDtypeStruct((m, D_MODEL), F32),
        compiler_params=_params("parallel"),
        name="final_norm",
    )(*parts, g)


def _add_kernel(a_ref, b_ref, o_ref):
    o_ref[...] = a_ref[...] + b_ref[...]


def _add(a, b, tm):
    rows = pl.BlockSpec((tm, D_MODEL), lambda i: (i, 0))
    return pl.pallas_call(
        _add_kernel,
        grid=(a.shape[0] // tm,),
        in_specs=[rows, rows],
        out_specs=rows,
        out_shape=jax.ShapeDtypeStruct(a.shape, F32),
        compiler_params=_params("parallel"),
        name="residual_add",
    )(a, b)


def _pad_rw_cols(x, axis=-1):
    x = jnp.moveaxis(x, axis, 0)
    pad = jnp.zeros((RW_PAD - RW_COLS,) + x.shape[1:], x.dtype)
    return jnp.moveaxis(jnp.concatenate([x[:RW_COLS], pad, x[RW_COLS:]], axis=0), 0, axis)


def _stage_w_in_kernel(w_ref, o_ref):
    o_ref[...] = w_ref[0].astype(BF16)


def _stage_w_in(w_in, layer):
    rows = 512
    assert RW_COLS // rows == (RW_PAD - 1) // rows and RW_PAD % rows == 0 and (N_PROJ - RW_PAD) % rows == 0
    shift = RW_PAD - RW_COLS

    def src_row(j):
        assert rows % 8 == 0 and shift % 8 == 0
        return 8 * jnp.where(j * rows < RW_PAD, j * (rows // 8), j * (rows // 8) - shift // 8)

    return pl.pallas_call(
        _stage_w_in_kernel,
        grid=(N_PROJ // rows,),
        in_specs=[pl.BlockSpec((pl.Element(1), pl.Element(rows), pl.Element(D_MODEL)),
                               lambda j: (layer, src_row(j), 0))],
        out_specs=pl.BlockSpec((rows, D_MODEL), lambda j: (j, 0)),
        out_shape=jax.ShapeDtypeStruct((N_PROJ, D_MODEL), BF16),
        compiler_params=_params("parallel"),
        name="stage_w_in",
    )(w_in.transpose(0, 2, 1))


def _lora_slab(w, row0):
    slab = jnp.zeros((LORA_PAD, RW_WIDTH), F32)
    return lax.dynamic_update_slice(slab, w, (row0, 0)).astype(BF16)


def _layer_params(l, P):
    row = lambda x: x.reshape(1, -1)
    lp = dict(
        norm_mix=row(P['norm_mix'][l]), norm_ffn=row(P['norm_ffn'][l]),
        w_in_t=_stage_w_in(P['w_in'], l),
        rw_vecs=[row(_pad_rw_cols(P['rw_mu'][l])), row(P['rw_w0'][l]), row(P['rw_a0'][l]),
                 row(P['rw_kk'][l]), row(P['rw_ka'][l]), row(P['rw_rk'][l])],
        rw_mats=[_lora_slab(P['rw_w2'][l], 0), _lora_slab(P['rw_a2'][l], DECAY_LORA),
                 _lora_slab(P['rw_g2'][l], DECAY_LORA + AAA_LORA)],
        rw_ln_w=row(P['rw_ln_w'][l]), rw_ln_b=row(P['rw_ln_b'][l]),
        cv_w=P['cv_w'][l], cv_b=row(P['cv_b'][l]), cv_ln_w=row(P['cv_ln_w'][l]), cv_ln_b=row(P['cv_ln_b'][l]),
        w_br_rw=P['w_br_rw'][l].astype(BF16), w_br_mb=P['w_br_mb'][l].astype(BF16),
        w_br_cv=P['w_br_cv'][l].astype(BF16), w_out=P['w_out'][l].astype(BF16),
    )
    i = l // 2
    if l % 2 == 0:
        lp.update(wg=P['ffn_wg'][i].astype(BF16), wu=P['ffn_wu'][i].astype(BF16), wd=P['ffn_wd'][i].astype(BF16))
    else:
        lp.update(wg=P['moe_wg'][i].astype(BF16), wu=P['moe_wu'][i].astype(BF16), wd=P['moe_wd'][i].astype(BF16),
                  router=P['moe_router'][i], router_b=P['moe_router_b'][i])
    return lp


def _channel_mix(h, lp, tm_dense, tm_moe):
    if 'router' in lp:
        xn, gates, pos, counts = _router(h, lp['norm_ffn'], lp['router'], lp['router_b'], tm_moe)
        return [h, _moe(xn, gates, pos, counts, lp['wg'], lp['wu'], lp['wd'], tm_moe, MOE_FF_SLICE)]
    return [_ffn(h, lp['norm_ffn'], lp['wg'], lp['wu'], lp['wd'], tm_dense, FFN_FF_SLICE)]


def _as_one(parts, tm):
    return parts[0] if len(parts) == 1 else _add(parts[0], parts[1], tm)


PROMPT_ROWS = dict(proj=1024, qkv=1024, prep=256, conv=512, merge=512, out_proj=512, ffn=1024, moe=1024, norm=512)


def _run_prompt(x, layers, norm_final, rel_bias):
    t = x.shape[0]
    tm = PROMPT_ROWS
    parts = [x]
    outs = dict(k=[], v=[], wkv=[], shift=[], conv=[])
    for lp in layers:
        h = _as_one(parts, tm['norm'])
        proj, gates = _norm_proj(h, lp['norm_mix'], lp['w_in_t'], tm['proj'])
        *ops, g, bonus = _rwkv_prep(proj, None, lp['rw_vecs'], lp['rw_mats'], tm['prep'])
        y_rw, s_final = _rwkv_chunked(ops, bonus, g, lp['rw_ln_w'], lp['rw_ln_b'])
        q_t, k_t, v_t, k_tb, v_tb, km_tiles = _qkv_t(h, lp['norm_mix'], lp['w_in_t'], tm['qkv'])
        y_mb_t = _moba_seq(rel_bias, q_t, k_tb, v_tb, km_tiles)
        y_cv, tail = _conv_seq(proj, lp['cv_w'], lp['cv_b'], lp['cv_ln_w'], lp['cv_ln_b'], tm['conv'])
        merged = _merge(y_rw, y_mb_t, y_cv, gates, lp['w_br_rw'], lp['w_br_mb'], lp['w_br_cv'], tm['merge'], True)
        h = _out_proj(h, merged, lp['w_out'], tm['out_proj'])
        parts = _channel_mix(h, lp, tm['ffn'], tm['moe'])
        tokens_major = lambda x_t: x_t.reshape(MB_HEADS, MB_HEAD_DIM, t).transpose(2, 0, 1)[None]
        outs['k'].append(tokens_major(k_t))
        outs['v'].append(tokens_major(v_t))
        outs['wkv'].append(s_final[None])
        outs['shift'].append(proj[t - 1:t, :RW_COLS])
        outs['conv'].append(tail[None, CV_HALO - (CV_K - 1):, :])
    y = _final_norm(parts, norm_final, tm['norm'])
    return (y[None],) + tuple(jnp.stack(outs[n]) for n in ('k', 'v', 'wkv', 'shift', 'conv'))


def _run_sample(x, layers, norm_final, rel_bias, cache_k, cache_v, state_wkv, state_shift, state_conv, page_table):
    b = x.shape[0]
    parts = [x]
    wkv_t = state_wkv.transpose(0, 2, 3, 4, 1)
    new_wkv_t = None
    outs = dict(k=[], v=[], shift=[], conv=[])
    for l, lp in enumerate(layers):
        h = _as_one(parts, b)
        proj, gates = _norm_proj(h, lp['norm_mix'], lp['w_in_t'], b)
        *ops, g, bonus = _rwkv_prep(proj, _pad_rw_cols(state_shift[l]), lp['rw_vecs'], lp['rw_mats'], b)
        y_t, new_wkv_t = _rwkv_single_step(ops, wkv_t, l, new_wkv_t)
        y_rw = _rwkv_post(y_t, bonus, g, lp['rw_ln_w'], lp['rw_ln_b'])
        q = proj[:, RW_PAD:RW_PAD + MB_WIDTH]
        k_new = proj[:, RW_PAD + MB_WIDTH:RW_PAD + 2 * MB_WIDTH]
        v_new = proj[:, RW_PAD + 2 * MB_WIDTH:RW_PAD + 3 * MB_WIDTH]
        y_mb = _moba_paged(rel_bias, q, k_new, v_new, cache_k, cache_v, l, page_table)
        y_cv, conv_new = _conv_step(proj, state_conv[l], lp['cv_w'], lp['cv_b'], lp['cv_ln_w'], lp['cv_ln_b'],
                                    SAMPLE_CONV_ROWS)
        merged = _merge(y_rw, y_mb, y_cv, gates, lp['w_br_rw'], lp['w_br_mb'], lp['w_br_cv'], b, False)
        h = _out_proj(h, merged, lp['w_out'], b)
        parts = _channel_mix(h, lp, b, b)
        outs['k'].append(k_new.reshape(b, 1, MB_HEADS, MB_HEAD_DIM))
        outs['v'].append(v_new.reshape(b, 1, MB_HEADS, MB_HEAD_DIM))
        outs['shift'].append(proj[:, :RW_COLS])
        outs['conv'].append(conv_new)
    y = _final_norm(parts, norm_final, b)
    stacked = {n: jnp.stack(v) for n, v in outs.items()}
    return (y[:, None, :], stacked['k'], stacked['v'], new_wkv_t.transpose(0, 4, 1, 2, 3), stacked['shift'],
            stacked['conv'])


def kernel(x_prompt, x_sample, cache_k, cache_v, state_wkv, state_shift, state_conv, page_table, norm_mix, norm_ffn, norm_final, w_in, rw_mu, rw_w0, rw_w2, rw_a0, rw_a2, rw_g2, rw_kk, rw_ka, rw_rk, rw_ln_w, rw_ln_b, rel_bias, cv_w, cv_b, cv_ln_w, cv_ln_b, w_br_rw, w_br_mb, w_br_cv, w_out, ffn_wg, ffn_wu, ffn_wd, moe_router, moe_router_b, moe_wg, moe_wu, moe_wd):
    P = dict(norm_mix=norm_mix, norm_ffn=norm_ffn, w_in=w_in,
             rw_mu=rw_mu, rw_w0=rw_w0, rw_w2=rw_w2, rw_a0=rw_a0, rw_a2=rw_a2, rw_g2=rw_g2,
             rw_kk=rw_kk, rw_ka=rw_ka, rw_rk=rw_rk, rw_ln_w=rw_ln_w, rw_ln_b=rw_ln_b,
             cv_w=cv_w, cv_b=cv_b, cv_ln_w=cv_ln_w, cv_ln_b=cv_ln_b,
             w_br_rw=w_br_rw, w_br_mb=w_br_mb, w_br_cv=w_br_cv, w_out=w_out,
             ffn_wg=ffn_wg, ffn_wu=ffn_wu, ffn_wd=ffn_wd,
             moe_router=moe_router, moe_router_b=moe_router_b, moe_wg=moe_wg, moe_wu=moe_wu, moe_wd=moe_wd)
    depth = w_in.shape[0]
    layers = [_layer_params(l, P) for l in range(depth)]
    g_final = norm_final.reshape(1, -1)
    assert x_prompt.shape[0] == 1 and x_sample.shape[1] == 1
    prompt = _run_prompt(x_prompt[0], layers, g_final, rel_bias)
    sample = _run_sample(x_sample[:, 0, :], layers, g_final, rel_bias, cache_k, cache_v,
                         state_wkv, state_shift, state_conv, page_table)
    y_p, k_p, v_p, wkv_p, shift_p, conv_p = prompt
    y_s, k_s, v_s, wkv_s, shift_s, conv_s = sample
    return (y_p, y_s, k_p, v_p, wkv_p, shift_p, conv_p,
            k_s, v_s, wkv_s, shift_s, conv_s)
```

```python
import functools
import math

import numpy as np
import jax
import jax.numpy as jnp
from jax import lax
from jax.experimental import pallas as pl
from jax.experimental.pallas import tpu as pltpu

F32 = jnp.float32
BF16 = jnp.bfloat16

D_MODEL = 2048
RW_HEAD_DIM = 64
RW_WIDTH = 1024
RW_HEADS = 16
DECAY_LORA = 64
AAA_LORA = 64
GATE_LORA = 160
RW_COLS = 3 * RW_WIDTH + DECAY_LORA + AAA_LORA + GATE_LORA
RW_PAD = 3584
LORA_PAD = RW_PAD - 3 * RW_WIDTH
GN_EPS = 64e-5
MB_HEAD_DIM = 64
MB_WIDTH = 512
MB_HEADS = 8
MB_BLOCK = 256
MB_TOPK = 3
NUM_BUCKETS = 32
MAX_DISTANCE = 128
NEG_INF = -1e30
CV_WIDTH = 512
CV_K = 31
LN_EPS = 1e-5
N_BRANCH = 3
N_EXPERTS = 8
TOP_K = 2
NORM_EPS = 1e-6
PAGE_SIZE = 128

PROJ_TILE = 1024
N_PROJ = RW_PAD + 3 * MB_WIDTH + 2 * CV_WIDTH + N_BRANCH * D_MODEL
COL_CV = (RW_PAD + 3 * MB_WIDTH) // (2 * CV_WIDTH)
GATE_COL0 = RW_PAD + 3 * MB_WIDTH + 2 * CV_WIDTH
GATE_TILE0 = GATE_COL0 // PROJ_TILE

VMEM_LIMIT_BYTES = 56 * 1024 * 1024


def _params(*semantics):
    return pltpu.CompilerParams(dimension_semantics=semantics, vmem_limit_bytes=VMEM_LIMIT_BYTES)


def _sigmoid(x):
    return 1.0 / (1.0 + jnp.exp(-x))


def _silu(x):
    return x * _sigmoid(x)


def _split3(x):
    hi = x.astype(BF16)
    r1 = x - hi.astype(F32)
    mid = r1.astype(BF16)
    lo = (r1 - mid.astype(F32)).astype(BF16)
    return hi, mid, lo


_NT = (((1,), (1,)), ((), ()))
_TN = (((0,), (0,)), ((), ()))


def _dot(a, b):
    return jnp.dot(a, b, preferred_element_type=F32)


def _dot_exact_rhs(x, sel):
    hi, mid, lo = _split3(x)
    return _dot(hi, sel) + _dot(mid, sel) + _dot(lo, sel)


def _dot_f32(a, b):
    a_hi, a_mid, a_lo = _split3(a)
    b_hi, b_mid, b_lo = _split3(b)
    return (_dot(a_hi, b_hi) + (_dot(a_hi, b_mid) + _dot(a_mid, b_hi))
            + (_dot(a_mid, b_mid) + _dot(a_hi, b_lo) + _dot(a_lo, b_hi)))


def _rms_norm_bf16(x, g):
    ms = jnp.mean(x * x, axis=-1, keepdims=True)
    return (x * lax.rsqrt(ms + NORM_EPS) * g).astype(BF16)


def _norm_proj_kernel(x_ref, g_ref, wt_ref, o_ref, gate_ref, xn_ref):
    j = pl.program_id(1)

    @pl.when(j == 0)
    def _():
        xn_ref[...] = _rms_norm_bf16(x_ref[...], g_ref[...])

    acc = lax.dot_general(xn_ref[...], wt_ref[...], _NT, preferred_element_type=F32)

    @pl.when(j < GATE_TILE0)
    def _():
        o_ref[...] = acc

    @pl.when(j >= GATE_TILE0)
    def _():
        gate_ref[...] = _sigmoid(acc).astype(BF16)


def _norm_proj(x, g, wt, tm):
    m = x.shape[0]
    n_gate = N_PROJ - GATE_COL0
    return pl.pallas_call(
        _norm_proj_kernel,
        grid=(m // tm, N_PROJ // PROJ_TILE),
        in_specs=[pl.BlockSpec((tm, D_MODEL), lambda i, j: (i, 0)),
                  pl.BlockSpec((1, D_MODEL), lambda i, j: (0, 0)),
                  pl.BlockSpec((PROJ_TILE, D_MODEL), lambda i, j: (j, 0))],
        out_specs=[pl.BlockSpec((tm, PROJ_TILE), lambda i, j: (i, jnp.minimum(j, GATE_TILE0 - 1))),
                   pl.BlockSpec((tm, PROJ_TILE), lambda i, j: (i, jnp.maximum(j - GATE_TILE0, 0)))],
        out_shape=[jax.ShapeDtypeStruct((m, GATE_COL0), F32), jax.ShapeDtypeStruct((m, n_gate), BF16)],
        scratch_shapes=[pltpu.VMEM((tm, D_MODEL), BF16)],
        compiler_params=_params("parallel", "arbitrary"),
        name="norm_proj",
    )(x, g, wt)


def _qkv_t_kernel(x_ref, g_ref, wq_ref, wk_ref, wv_ref, q_ref, k_ref, v_ref, kb_ref, vb_ref, km_ref):
    xn = _rms_norm_bf16(x_ref[...], g_ref[...])
    nt = lambda w_ref: lax.dot_general(w_ref[...], xn, _NT, preferred_element_type=F32)
    q_ref[...] = nt(wq_ref)
    k = nt(wk_ref)
    v = nt(wv_ref)
    k_ref[...] = k
    v_ref[...] = v
    kb_ref[...] = k.astype(BF16)
    vb_ref[...] = v.astype(BF16)
    n_blk = k.shape[1] // MB_BLOCK
    sums = [jnp.sum(k[:, b * MB_BLOCK:(b + 1) * MB_BLOCK], axis=-1, keepdims=True) for b in range(n_blk)]
    km_ref[0] = jnp.concatenate(sums, axis=-1) * (1.0 / MB_BLOCK)


def _qkv_t(x, g, wt, tm):
    t = x.shape[0]
    q_row = RW_PAD // MB_WIDTH
    w_spec = lambda r: pl.BlockSpec((MB_WIDTH, D_MODEL), lambda i: (q_row + r, 0))
    col = pl.BlockSpec((MB_WIDTH, tm), lambda i: (0, i))
    f32_t = jax.ShapeDtypeStruct((MB_WIDTH, t), F32)
    bf16_t = jax.ShapeDtypeStruct((MB_WIDTH, t), BF16)
    return pl.pallas_call(
        _qkv_t_kernel,
        grid=(t // tm,),
        in_specs=[pl.BlockSpec((tm, D_MODEL), lambda i: (i, 0)), pl.BlockSpec((1, D_MODEL), lambda i: (0, 0)),
                  w_spec(0), w_spec(1), w_spec(2)],
        out_specs=[col] * 5 + [pl.BlockSpec((1, MB_WIDTH, tm // MB_BLOCK), lambda i: (i, 0, 0))],
        out_shape=[f32_t] * 3 + [bf16_t] * 2 + [jax.ShapeDtypeStruct((t // tm, MB_WIDTH, tm // MB_BLOCK), F32)],
        compiler_params=_params("parallel"),
        name="qkv_t",
    )(x, g, wt, wt, wt)


def _rwkv_prep_kernel(sequential, *refs):
    if sequential:
        (p_ref, mu_ref, w0_ref, a0_ref, kk_ref, ka_ref, rk_ref, w2_ref, a2_ref, g2_ref,
         r_o, w_o, k_o, v_o, kn_o, b_o, g_o, bonus_o, carry_ref) = refs
    else:
        (p_ref, prev_ref, mu_ref, w0_ref, a0_ref, kk_ref, ka_ref, rk_ref, w2_ref, a2_ref, g2_ref,
         r_o, w_o, k_o, v_o, kn_o, b_o, g_o, bonus_o) = refs
    p = p_ref[...]
    tm = p.shape[0]
    if sequential:
        @pl.when(pl.program_id(0) == 0)
        def _():
            carry_ref[...] = jnp.zeros_like(carry_ref)

        rows = lax.broadcasted_iota(jnp.int32, p.shape, 0)
        prev = jnp.where(rows == 0, carry_ref[...], pltpu.roll(p, 1, 0))
        carry_ref[...] = p[tm - 1:tm, :]
    else:
        prev = prev_ref[...]
    xs = p + (prev - p) * mu_ref[...]
    r = xs[:, 0:RW_WIDTH]
    k = xs[:, RW_WIDTH:2 * RW_WIDTH]
    v = xs[:, 2 * RW_WIDTH:3 * RW_WIDTH]
    lora = xs[:, 3 * RW_WIDTH:RW_PAD]
    wx = w0_ref[...] + _dot(jnp.tanh(lora).astype(BF16), w2_ref[...])
    w_log = jnp.minimum(wx, 0.0) - jnp.log(1.0 + jnp.exp(-jnp.abs(wx))) - 0.5
    log_decay = -jnp.exp(w_log)
    a = _sigmoid(a0_ref[...] + _dot(lora.astype(BF16), a2_ref[...]))
    g_o[...] = _dot(_sigmoid(lora).astype(BF16), g2_ref[...])
    kk = k * kk_ref[...]
    k_mod = k * (1.0 + (a - 1.0) * ka_ref[...])
    rkr = r * k_mod * rk_ref[...]
    kn_parts = []
    for h in range(RW_HEADS):
        sl = slice(h * RW_HEAD_DIM, (h + 1) * RW_HEAD_DIM)
        kk_h = kk[:, sl]
        nrm = jnp.sqrt(jnp.sum(kk_h * kk_h, axis=-1, keepdims=True))
        kn_parts.append(kk_h / jnp.maximum(nrm, 1e-12))
        bonus_o[:, sl] = jnp.sum(rkr[:, sl], axis=-1, keepdims=True) * v[:, sl]
    kn = jnp.concatenate(kn_parts, axis=-1)
    if sequential:
        ops = (r, log_decay, k_mod, v, kn, kn * a)
    else:
        ops = tuple(x.T for x in (r, jnp.exp(log_decay), k_mod, v, kn, kn * a))
    for o_ref, x in zip((r_o, w_o, k_o, v_o, kn_o, b_o), ops):
        o_ref[...] = x


def _rwkv_prep(proj, prev, vecs, mats, tm):
    m = proj.shape[0]
    sequential = prev is None
    row = lambda n: pl.BlockSpec((1, n), lambda i: (0, 0))
    in_specs = [pl.BlockSpec((tm, RW_PAD), lambda i: (i, 0))]
    args = [proj]
    if not sequential:
        in_specs.append(pl.BlockSpec((tm, RW_PAD), lambda i: (i, 0)))
        args.append(prev)
    in_specs += [row(RW_PAD)] + [row(RW_WIDTH)] * 5
    in_specs += [pl.BlockSpec((LORA_PAD, RW_WIDTH), lambda i: (0, 0))] * 3
    flat = jax.ShapeDtypeStruct((m, RW_WIDTH), F32)
    flat_spec = pl.BlockSpec((tm, RW_WIDTH), lambda i: (i, 0))
    if sequential:
        op_shape, op_spec = flat, flat_spec
    else:
        op_shape = jax.ShapeDtypeStruct((RW_WIDTH, m), F32)
        op_spec = pl.BlockSpec((RW_WIDTH, tm), lambda i: (0, i))
    return pl.pallas_call(
        functools.partial(_rwkv_prep_kernel, sequential),
        grid=(m // tm,),
        in_specs=in_specs,
        out_specs=[op_spec] * 6 + [flat_spec] * 2,
        out_shape=[op_shape] * 6 + [flat] * 2,
        scratch_shapes=[pltpu.VMEM((1, RW_PAD), F32)] if sequential else [],
        compiler_params=_params("arbitrary"),
        name="rwkv_prep",
    )(*args, *vecs, *mats)


RW_CHUNK = 64
RW_CHUNKS_PER_STEP = 4


def _group_norm(y):
    mean = jnp.mean(y, axis=-1, keepdims=True)
    var = jnp.mean(jnp.square(y - mean), axis=-1, keepdims=True)
    return (y - mean) * lax.rsqrt(var + GN_EPS)


def _rwkv_chunk_kernel(r_ref, lw_ref, k_ref, v_ref, kn_ref, b_ref, bonus_ref, g_ref, lnw_ref, lnb_ref,
                       y_ref, s_out_ref, h_ref):
    c = RW_CHUNK
    n = RW_HEAD_DIM

    @pl.when(pl.program_id(0) == 0)
    def _():
        h_ref[...] = jnp.zeros_like(h_ref)

    row = lax.broadcasted_iota(jnp.int32, (c, c), 0)
    col = lax.broadcasted_iota(jnp.int32, (c, c), 1)
    eye = jnp.where(row == col, 1.0, 0.0)
    lower_ones = jnp.where(col <= row, 1.0, 0.0).astype(BF16)
    row2 = lax.broadcasted_iota(jnp.int32, (2 * c, 2 * c), 0)
    col2 = lax.broadcasted_iota(jnp.int32, (2 * c, 2 * c), 1)
    t_idx = jnp.where(row2 >= c, row2 - c, row2)
    s_idx = jnp.where(col2 >= c, col2 - c, col2)
    mask = s_idx < t_idx + jnp.where(row2 >= c, 1, 0)
    heads = range(RW_HEADS)
    sls = [slice(h * n, (h + 1) * n) for h in heads]
    chunks = [slice(ci * c, (ci + 1) * c) for ci in range(r_ref.shape[0] // c)]

    pre = []
    for rows in chunks:
        lw = lw_ref[rows, :]
        lw_hi, lw_mid, lw_lo = _split3(lw)
        cum = _dot(lower_ones, lw_hi) + _dot(lower_ones, lw_mid) + _dot(lower_ones, lw_lo)
        total = cum[c - 1:c, :]
        k = k_ref[rows, :]
        b = b_ref[rows, :]
        v = v_ref[rows, :]
        g_inv = jnp.exp(-cum)
        g_rest = jnp.exp(total - cum)
        pre.append(dict(
            qr=jnp.concatenate([kn_ref[rows, :] * jnp.exp(cum - lw), r_ref[rows, :] * jnp.exp(cum)],
                               axis=0).astype(BF16),
            kdbd=jnp.concatenate([k * g_inv, b * g_inv], axis=0).astype(BF16),
            zk=jnp.concatenate([k * g_rest, b * g_rest], axis=0).astype(BF16),
            v=v, vb=v.astype(BF16), decay_total=jnp.exp(total)))
    pairs = [(p, sl) for p in pre for sl in sls]
    gms = [jnp.where(mask, lax.dot_general(p['qr'][:, sl], p['kdbd'][:, sl], _NT, preferred_element_type=F32), 0.0)
           for p, sl in pairs]
    akkv = [_dot(gm[0:c, 0:c].astype(BF16), p['vb'][:, sl]) for gm, (p, sl) in zip(gms, pairs)]
    powers = [gm[0:c, c:2 * c].astype(BF16) for gm in gms]
    t_invs = [eye - gm[0:c, c:2 * c] for gm in gms]
    for _ in range(int(math.log2(c)) - 1):
        powers = [_dot(pw, pw).astype(BF16) for pw in powers]
        t_invs = [t + _dot(pw, t.astype(BF16)) for pw, t in zip(powers, t_invs)]

    for ci, rows in enumerate(chunks):
        p = pre[ci]
        mine = slice(ci * RW_HEADS, (ci + 1) * RW_HEADS)
        h_old = [h_ref[h] for h in heads]
        qhs = [_dot(p['qr'][:, sl], ho.astype(BF16)) for sl, ho in zip(sls, h_old)]
        us = [_dot(t.astype(BF16), (qh[0:c] + av).astype(BF16))
              for t, qh, av in zip(t_invs[mine], qhs, akkv[mine])]
        vus = [jnp.concatenate([p['v'][:, sl], -u], axis=0).astype(BF16) for sl, u in zip(sls, us)]
        outs = [_group_norm(qh[c:2 * c] + _dot(gm[c:2 * c, :].astype(BF16), vu))
                for qh, gm, vu in zip(qhs, gms[mine], vus)]
        for h in heads:
            decay_col = jnp.sum(eye * p['decay_total'][:, sls[h]], axis=-1, keepdims=True)
            h_ref[h] = decay_col * h_old[h] + lax.dot_general(p['zk'][:, sls[h]], vus[h], _TN,
                                                              preferred_element_type=F32)
        yn = jnp.concatenate(outs, axis=-1)
        y_ref[rows, :] = ((yn * lnw_ref[...] + lnb_ref[...] + bonus_ref[rows, :]) * g_ref[rows, :]).astype(BF16)

    @pl.when(pl.program_id(0) == pl.num_programs(0) - 1)
    def _():
        for h in range(RW_HEADS):
            s_out_ref[h] = h_ref[h].T


def _rwkv_chunked(ops, bonus, g, ln_w, ln_b):
    m = bonus.shape[0]
    step_rows = RW_CHUNK * RW_CHUNKS_PER_STEP
    spec = pl.BlockSpec((step_rows, RW_WIDTH), lambda i: (i, 0))
    row = pl.BlockSpec((1, RW_WIDTH), lambda i: (0, 0))
    state_shape = (RW_HEADS, RW_HEAD_DIM, RW_HEAD_DIM)
    return pl.pallas_call(
        _rwkv_chunk_kernel,
        grid=(m // step_rows,),
        in_specs=[spec] * 8 + [row, row],
        out_specs=[spec, pl.BlockSpec(state_shape, lambda i: (0, 0, 0))],
        out_shape=[jax.ShapeDtypeStruct((m, RW_WIDTH), BF16), jax.ShapeDtypeStruct(state_shape, F32)],
        scratch_shapes=[pltpu.VMEM(state_shape, F32)],
        compiler_params=_params("arbitrary"),
        name="rwkv_chunked",
    )(*ops, bonus, g, ln_w, ln_b)


RW_STEP_HEADS = 2


def _rwkv_single_step_kernel(r_ref, w_ref, k_ref, v_ref, kn_ref, b_ref, s0_ref, *rest):
    y_ref, s_out_ref = rest[-2:]
    n = RW_HEAD_DIM
    for hh in range(s0_ref.shape[0]):
        rows = slice(hh * n, (hh + 1) * n)
        r, w, k, kn, b = (ref[rows, :] for ref in (r_ref, w_ref, k_ref, kn_ref, b_ref))

        def body(vi, carry):
            s = s0_ref[hh, vi]
            sa = jnp.sum(s * kn, axis=0, keepdims=True)
            s = s * w - sa * b + v_ref[pl.ds(hh * n + vi, 1), :] * k
            s_out_ref[hh, vi] = s
            y_ref[pl.ds(hh * n + vi, 1), :] = jnp.sum(s * r, axis=0, keepdims=True)
            return carry

        lax.fori_loop(0, n, body, 0)


def _rwkv_single_step(ops, states_t, layer, new_states_t):
    b = ops[0].shape[1]
    hs = RW_STEP_HEADS
    spec = pl.BlockSpec((hs * RW_HEAD_DIM, b), lambda i: (i, 0))
    st_spec = pl.BlockSpec((None, hs, RW_HEAD_DIM, RW_HEAD_DIM, b), lambda i: (layer, i, 0, 0, 0))
    carried = [] if new_states_t is None else [new_states_t]
    return pl.pallas_call(
        _rwkv_single_step_kernel,
        grid=(RW_HEADS // hs,),
        in_specs=[spec] * 6 + [st_spec] + [pl.BlockSpec(memory_space=pl.ANY)] * len(carried),
        out_specs=[spec, st_spec],
        out_shape=[jax.ShapeDtypeStruct((RW_WIDTH, b), F32), jax.ShapeDtypeStruct(states_t.shape, F32)],
        input_output_aliases={7: 1} if carried else {},
        compiler_params=_params("parallel"),
        name="rwkv_single_step",
    )(*ops, states_t, *carried)


def _rwkv_post_kernel(yt_ref, bonus_ref, g_ref, lnw_ref, lnb_ref, o_ref):
    y = yt_ref[...].T
    n = RW_HEAD_DIM
    yn = jnp.concatenate([_group_norm(y[:, h * n:(h + 1) * n]) for h in range(RW_HEADS)], axis=-1)
    out = (yn * lnw_ref[...] + lnb_ref[...] + bonus_ref[...]) * g_ref[...]
    o_ref[...] = out.astype(BF16)


def _rwkv_post(y_t, bonus, g, ln_w, ln_b):
    b = bonus.shape[0]
    flat_spec = pl.BlockSpec((b, RW_WIDTH), lambda i: (0, 0))
    row = pl.BlockSpec((1, RW_WIDTH), lambda i: (0, 0))
    return pl.pallas_call(
        _rwkv_post_kernel,
        grid=(1,),
        in_specs=[pl.BlockSpec((RW_WIDTH, b), lambda i: (0, 0)), flat_spec, flat_spec, row, row],
        out_specs=flat_spec,
        out_shape=jax.ShapeDtypeStruct((b, RW_WIDTH), BF16),
        compiler_params=_params("arbitrary"),
        name="rwkv_post",
    )(y_t, bonus, g, ln_w, ln_b)


CV_HALO = 32


def _conv_finish(y, cb_ref, lnw_ref, lnb_ref):
    y = y + cb_ref[...]
    mean = jnp.mean(y, axis=-1, keepdims=True)
    var = jnp.mean(jnp.square(y - mean), axis=-1, keepdims=True)
    yn = (y - mean) * lax.rsqrt(var + LN_EPS) * lnw_ref[...] + lnb_ref[...]
    return _silu(yn).astype(BF16)


def _conv_seq_kernel(p_ref, cw_ref, cb_ref, lnw_ref, lnb_ref, o_ref, tail_ref, ext_ref):
    tm = p_ref.shape[0]
    pad = CV_HALO - (CV_K - 1)

    @pl.when(pl.program_id(0) == 0)
    def _():
        ext_ref[0:CV_HALO, :] = jnp.zeros((CV_HALO, CV_WIDTH), F32)

    p = p_ref[...]
    u = p[:, 0:CV_WIDTH] * _sigmoid(p[:, CV_WIDTH:2 * CV_WIDTH])
    ext_ref[CV_HALO:CV_HALO + tm, :] = u
    acc = jnp.zeros((tm, CV_WIDTH), F32)
    for j in range(CV_K):
        acc = acc + ext_ref[pad + j:pad + j + tm, :] * cw_ref[j:j + 1, :]
    o_ref[...] = _conv_finish(acc, cb_ref, lnw_ref, lnb_ref)
    tail = ext_ref[tm:tm + CV_HALO, :]
    ext_ref[0:CV_HALO, :] = tail
    tail_ref[...] = tail


def _conv_seq(proj, cw, cb, ln_w, ln_b, tm):
    m = proj.shape[0]
    row = pl.BlockSpec((1, CV_WIDTH), lambda i: (0, 0))
    return pl.pallas_call(
        _conv_seq_kernel,
        grid=(m // tm,),
        in_specs=[pl.BlockSpec((tm, 2 * CV_WIDTH), lambda i: (i, COL_CV)),
                  pl.BlockSpec((CV_K, CV_WIDTH), lambda i: (0, 0)), row, row, row],
        out_specs=[pl.BlockSpec((tm, CV_WIDTH), lambda i: (i, 0)),
                   pl.BlockSpec((CV_HALO, CV_WIDTH), lambda i: (0, 0))],
        out_shape=[jax.ShapeDtypeStruct((m, CV_WIDTH), BF16),
                   jax.ShapeDtypeStruct((CV_HALO, CV_WIDTH), F32)],
        scratch_shapes=[pltpu.VMEM((CV_HALO + tm, CV_WIDTH), F32)],
        compiler_params=_params("arbitrary"),
        name="conv_seq",
    )(proj, cw, cb, ln_w, ln_b)


def _conv_step_kernel(p_ref, buf_ref, cw_ref, cb_ref, lnw_ref, lnb_ref, o_ref, nbuf_ref):
    p = p_ref[...]
    u = p[:, :, 0:CV_WIDTH] * _sigmoid(p[:, :, CV_WIDTH:2 * CV_WIDTH])
    buf = buf_ref[...]
    acc = (jnp.sum(buf * cw_ref[0:CV_K - 1, :][None], axis=1, keepdims=True)
           + u * cw_ref[CV_K - 1:CV_K, :][None])
    o_ref[...] = _conv_finish(acc, cb_ref, lnw_ref, lnb_ref)
    nbuf_ref[:, 0:CV_K - 2, :] = buf[:, 1:CV_K - 1, :]
    nbuf_ref[:, CV_K - 2:CV_K - 1, :] = u


def _conv_step(proj, buf, cw, cb, ln_w, ln_b, tb):
    m = proj.shape[0]
    row = pl.BlockSpec((1, CV_WIDTH), lambda i: (0, 0))
    buf_spec = pl.BlockSpec((tb, CV_K - 1, CV_WIDTH), lambda i: (i, 0, 0))
    y = pl.pallas_call(
        _conv_step_kernel,
        grid=(m // tb,),
        in_specs=[pl.BlockSpec((tb, 1, 2 * CV_WIDTH), lambda i: (i, 0, COL_CV)), buf_spec,
                  pl.BlockSpec((CV_K, CV_WIDTH), lambda i: (0, 0)), row, row, row],
        out_specs=[pl.BlockSpec((tb, 1, CV_WIDTH), lambda i: (i, 0, 0)), buf_spec],
        out_shape=[jax.ShapeDtypeStruct((m, 1, CV_WIDTH), BF16),
                   jax.ShapeDtypeStruct(buf.shape, F32)],
        compiler_params=_params("parallel"),
        name="conv_step",
    )(proj.reshape(m, 1, proj.shape[1]), buf, cw, cb, ln_w, ln_b)
    return y[0].reshape(m, CV_WIDTH), y[1]


def _bucket_thresholds():
    n = np.arange(0, 4 * MAX_DISTANCE, dtype=np.int64)
    max_exact = NUM_BUCKETS // 2
    nf = np.maximum(n, 1).astype(np.float32)
    large = max_exact + (np.log(nf / np.float32(max_exact)) / np.float32(math.log(MAX_DISTANCE / max_exact))
                         * np.float32(NUM_BUCKETS - max_exact)).astype(np.int32)
    large = np.minimum(large, NUM_BUCKETS - 1)
    bucket = np.where(n < max_exact, n, large)
    assert np.all(np.diff(bucket) >= 0) and bucket[-1] == NUM_BUCKETS - 1
    return [int(np.argmax(bucket >= b)) for b in range(NUM_BUCKETS)]


_BUCKET_THR = _bucket_thresholds()


def _bias_from_distance(n, rb_of_bucket):
    bias = jnp.zeros(n.shape, F32) + rb_of_bucket(NUM_BUCKETS - 1)
    for b in range(NUM_BUCKETS - 2, -1, -1):
        bias = jnp.where(n < _BUCKET_THR[b + 1], rb_of_bucket(b), bias)
    return bias


def _rank_select(scores, n_rows, limit):
    idx = lax.broadcasted_iota(jnp.int32, scores.shape, 0)
    beaten = jnp.zeros(scores.shape, F32)
    for j in range(n_rows):
        row = scores[j:j + 1, :]
        ahead = jnp.where(row > scores, 1.0, jnp.where(row == scores, jnp.where(idx > j, 1.0, 0.0), 0.0))
        beaten = beaten + ahead * jnp.where(j < limit, 1.0, 0.0)
    return jnp.where(idx < limit, jnp.where(beaten < MB_TOPK, 1.0, 0.0), 0.0)


MB_HEADS_PER_STEP = 4
MB_SUM_ROWS = 16
MB_BLOCKS_PER_TRIP = 2


def _softmax_blocks(carries, scores, vt_blks):
    m_new = [jnp.maximum(c[0], jnp.max(s, axis=0, keepdims=True)) for c, s in zip(carries, scores)]
    alpha = [jnp.exp2(c[0] - mn) for c, mn in zip(carries, m_new)]
    p = [jnp.exp2(s - mn).astype(BF16) for s, mn in zip(scores, m_new)]
    pv = [_dot(_with_ones_rows(vt), pp) for vt, pp in zip(vt_blks, p)]
    return [(mn, a * c[1] + x) for mn, a, c, x in zip(m_new, alpha, carries, pv)]


def _with_ones_rows(vt_blk):
    return jnp.concatenate([vt_blk, jnp.ones((MB_SUM_ROWS, vt_blk.shape[1]), BF16)], axis=0)


def _moba_seq_kernel(rb_ref, qt_ref, k_ref, vt_ref, km_ref, o_ref, bias_ref, rowterm_ref):
    hp = qt_ref.shape[0]
    h0 = pl.program_id(0) * hp
    qi = pl.program_id(1)
    nb = km_ref.shape[1]
    log2e = math.log2(math.e)
    scale = MB_HEAD_DIM ** -0.5 * log2e
    kidx = lax.broadcasted_iota(jnp.int32, (MB_BLOCK, MB_BLOCK), 0)
    qidx = lax.broadcasted_iota(jnp.int32, (MB_BLOCK, MB_BLOCK), 1)

    @pl.when(qi == 0)
    def _():
        dist = qidx - kidx
        for hh in range(hp):
            rb = lambda b: rb_ref[b, h0 + hh] * log2e
            bias_ref[hh, 0] = jnp.where(dist >= 0, _bias_from_distance(jnp.maximum(dist, 0), rb), NEG_INF)
            bias_ref[hh, 1] = _bias_from_distance(dist + MB_BLOCK, rb)

    near = jnp.maximum(qi - 1, 0)
    qtb, carries = [], []
    for hh in range(hp):
        qt = qt_ref[hh]
        block_scores = _dot_f32(km_ref[hh], qt)
        sel = _rank_select(block_scores, nb, qi)
        rowterm_ref[hh] = jnp.where(sel > 0.0, rb_ref[NUM_BUCKETS - 1, h0 + hh] * log2e, NEG_INF)
        qtb.append((qt * scale).astype(BF16))
        s = _dot(k_ref[hh, qi], qtb[hh]) + bias_ref[hh, 0]
        m0 = jnp.max(s, axis=0, keepdims=True)
        p = jnp.exp2(s - m0).astype(BF16)
        carries.append((m0, _dot(_with_ones_rows(vt_ref[hh, qi]), p)))

    heads = range(hp)
    sel_near = [jnp.where(rowterm_ref[hh, pl.ds(near, 1), :] > 0.5 * NEG_INF, 0.0, NEG_INF) for hh in heads]
    scores = [_dot(k_ref[hh, near], qtb[hh]) + bias_ref[hh, 1] + sel_near[hh] for hh in heads]
    carries = _softmax_blocks(carries, scores, [vt_ref[hh, near] for hh in heads])

    def body(t, carry):
        kjs = [t * MB_BLOCKS_PER_TRIP + u for u in range(MB_BLOCKS_PER_TRIP)]
        safe = [jnp.minimum(kj, nb - 1) for kj in kjs]
        raw = [[_dot(k_ref[hh, kc], qtb[hh]) for hh in heads] for kc in safe]
        for u, (kj, kc) in enumerate(zip(kjs, safe)):
            scores = [raw[u][hh] + jnp.where(kj < near, rowterm_ref[hh, pl.ds(kc, 1), :], NEG_INF) for hh in heads]
            carry = _softmax_blocks(carry, scores, [vt_ref[hh, kc] for hh in heads])
        return tuple(carry)

    n_trips = (near + MB_BLOCKS_PER_TRIP - 1) // MB_BLOCKS_PER_TRIP
    carries = lax.fori_loop(0, n_trips, body, tuple(carries))
    for hh in range(hp):
        acc = carries[hh][1]
        o_ref[hh] = (acc[0:MB_HEAD_DIM] / acc[MB_HEAD_DIM:MB_HEAD_DIM + 1]).astype(BF16)


def _moba_seq(rel_bias, q_t, k_t, v_t, km_tiles):
    t = q_t.shape[1]
    nb = t // MB_BLOCK
    hp = MB_HEADS_PER_STEP
    blocks = lambda x: x.reshape(MB_HEADS, MB_HEAD_DIM, nb, MB_BLOCK)
    qt = q_t.reshape(MB_HEADS, MB_HEAD_DIM, t)
    kb = blocks(k_t).transpose(0, 2, 3, 1)
    vt = blocks(v_t).transpose(0, 2, 1, 3)
    km = km_tiles.transpose(1, 0, 2).reshape(MB_HEADS, MB_HEAD_DIM, nb).transpose(0, 2, 1)
    ot = pl.pallas_call(
        _moba_seq_kernel,
        grid=(MB_HEADS // hp, nb),
        in_specs=[pl.BlockSpec(memory_space=pltpu.SMEM),
                  pl.BlockSpec((hp, MB_HEAD_DIM, MB_BLOCK), lambda h, i: (h, 0, i)),
                  pl.BlockSpec((hp, nb, MB_BLOCK, MB_HEAD_DIM), lambda h, i: (h, 0, 0, 0)),
                  pl.BlockSpec((hp, nb, MB_HEAD_DIM, MB_BLOCK), lambda h, i: (h, 0, 0, 0)),
                  pl.BlockSpec((hp, nb, MB_HEAD_DIM), lambda h, i: (h, 0, 0))],
        out_specs=pl.BlockSpec((hp, MB_HEAD_DIM, MB_BLOCK), lambda h, i: (h, 0, i)),
        out_shape=jax.ShapeDtypeStruct((MB_HEADS, MB_HEAD_DIM, t), BF16),
        scratch_shapes=[pltpu.VMEM((hp, 2, MB_BLOCK, MB_BLOCK), F32), pltpu.VMEM((hp, nb, MB_BLOCK), F32)],
        compiler_params=_params("parallel", "arbitrary"),
        name="moba_seq",
    )(rel_bias, qt, kb, vt, km)
    return ot.reshape(MB_WIDTH, t)


def _moba_paged_kernel(n_pages, pt_ref, q_ref, kn_ref, vn_ref, rbt_ref, *refs):
    k_pages = refs[:n_pages]
    v_pages = refs[n_pages:2 * n_pages]
    o_ref, bias_ref = refs[2 * n_pages:]
    past = n_pages * PAGE_SIZE
    n_blocks = past // MB_BLOCK
    pages_per_block = MB_BLOCK // PAGE_SIZE
    scale = MB_HEAD_DIM ** -0.5
    rb = lambda b: rbt_ref[:, :, b:b + 1]

    @pl.when(pl.program_id(0) == 0)
    def _():
        dist = PAGE_SIZE - lax.broadcasted_iota(jnp.int32, (MB_HEADS, 1, PAGE_SIZE), 2)
        bias_ref[...] = _bias_from_distance(dist, rb)

    di = lax.broadcasted_iota(jnp.int32, (MB_HEAD_DIM, MB_HEAD_DIM), 0)
    dj = lax.broadcasted_iota(jnp.int32, (MB_HEAD_DIM, MB_HEAD_DIM), 1)
    eye = jnp.where(di == dj, 1.0, 0.0)[None]
    q = q_ref[0]
    q_col = jnp.sum(eye * q, axis=-1, keepdims=True)
    raw = [jnp.sum(k_pages[pg][...] * q_col, axis=1, keepdims=True) for pg in range(n_pages)]
    scores = []
    for j in range(n_blocks):
        tot = raw[pages_per_block * j]
        for e in range(1, pages_per_block):
            tot = tot + raw[pages_per_block * j + e]
        scores.append(jnp.sum(tot, axis=-1, keepdims=True) * (1.0 / MB_BLOCK))
    far_bias = rb(NUM_BUCKETS - 1)
    rowterms = []
    for j in range(n_blocks):
        beaten = jnp.zeros((MB_HEADS, 1, 1), F32)
        for i in range(n_blocks):
            if i != j:
                ahead = scores[i] > scores[j] if i > j else scores[i] >= scores[j]
                beaten = beaten + jnp.where(ahead, 1.0, 0.0)
        rowterms.append(jnp.where(beaten < MB_TOPK, far_bias, NEG_INF))
    lg_own = jnp.sum(kn_ref[0] * q, axis=-1, keepdims=True) * scale + rb(0)
    logits = []
    for pg in range(n_pages):
        term = rowterms[pg // pages_per_block]
        if pg == n_pages - 1:
            term = bias_ref[...] + jnp.where(term > 0.5 * NEG_INF, 0.0, NEG_INF)
        logits.append(raw[pg] * scale + term)
    m_row = logits[0]
    for lg in logits[1:]:
        m_row = jnp.maximum(m_row, lg)
    m = jnp.maximum(jnp.max(m_row, axis=-1, keepdims=True), lg_own)
    p_own = jnp.exp(lg_own - m)
    l_row = jnp.zeros((MB_HEADS, 1, PAGE_SIZE), F32)
    acc = jnp.zeros((MB_HEADS, MB_HEAD_DIM, PAGE_SIZE), F32)
    for pg in range(n_pages):
        p = jnp.exp(logits[pg] - m)
        l_row = l_row + p
        acc = acc + v_pages[pg][...] * p
    out_col = jnp.sum(acc, axis=-1, keepdims=True)
    out = jnp.sum(eye * out_col, axis=1, keepdims=True) + p_own * vn_ref[0]
    l = jnp.sum(l_row, axis=-1, keepdims=True) + p_own
    o_ref[0] = (out / l).astype(BF16)


def _moba_paged(rel_bias, q, k_new, v_new, cache_k, cache_v, layer, page_table):
    b, n_pages = page_table.shape
    assert PAGE_SIZE >= _BUCKET_THR[NUM_BUCKETS - 1] and MB_BLOCK % PAGE_SIZE == 0
    head4 = pl.BlockSpec((1, MB_HEADS, 1, MB_HEAD_DIM), lambda i, pt: (i, 0, 0, 0))

    def page_spec(pg):
        return pl.BlockSpec((None, None, MB_HEADS, MB_HEAD_DIM, PAGE_SIZE),
                            lambda i, pt: (layer, pt[i, pg], 0, 0, 0))

    grid_spec = pltpu.PrefetchScalarGridSpec(
        num_scalar_prefetch=1,
        grid=(b,),
        in_specs=[head4, head4, head4, pl.BlockSpec((MB_HEADS, 1, NUM_BUCKETS), lambda i, pt: (0, 0, 0))]
        + [page_spec(pg) for pg in range(n_pages)] * 2,
        out_specs=head4,
        scratch_shapes=[pltpu.VMEM((MB_HEADS, 1, PAGE_SIZE), F32)],
    )
    h4 = lambda x: x.reshape(b, MB_HEADS, 1, MB_HEAD_DIM)
    kt = cache_k.transpose(0, 1, 3, 4, 2)
    vt = cache_v.transpose(0, 1, 3, 4, 2)
    out = pl.pallas_call(
        functools.partial(_moba_paged_kernel, n_pages),
        grid_spec=grid_spec,
        out_shape=jax.ShapeDtypeStruct((b, MB_HEADS, 1, MB_HEAD_DIM), BF16),
        compiler_params=_params("arbitrary"),
        name="moba_paged",
    )(page_table, h4(q), h4(k_new), h4(v_new), rel_bias.T.reshape(MB_HEADS, 1, NUM_BUCKETS),
      *([kt] * n_pages), *([vt] * n_pages))
    return out.reshape(b, MB_WIDTH)


def _merge_kernel(mb_feature_major, yrw_ref, ymb_ref, ycv_ref, g0_ref, g1_ref, g2_ref, wrw_ref, wmb_ref, wcv_ref,
                  o_ref):
    if mb_feature_major:
        y_mb = lax.dot_general(ymb_ref[...], wmb_ref[...], _TN, preferred_element_type=F32)
    else:
        y_mb = _dot(ymb_ref[...], wmb_ref[...])
    merged = (g0_ref[...] * _dot(yrw_ref[...], wrw_ref[...]) + g1_ref[...] * y_mb
              + g2_ref[...] * _dot(ycv_ref[...], wcv_ref[...]))
    o_ref[...] = merged.astype(BF16)


def _merge(y_rw, y_mb, y_cv, gates, w_rw, w_mb, w_cv, tm, mb_feature_major):
    m = gates.shape[0]
    rows = lambda n: pl.BlockSpec((tm, n), lambda i: (i, 0))
    mb_spec = pl.BlockSpec((MB_WIDTH, tm), lambda i: (0, i)) if mb_feature_major else rows(MB_WIDTH)
    gate = lambda br: pl.BlockSpec((tm, D_MODEL), lambda i: (i, br))
    full = lambda n: pl.BlockSpec((n, D_MODEL), lambda i: (0, 0))
    return pl.pallas_call(
        functools.partial(_merge_kernel, mb_feature_major),
        grid=(m // tm,),
        in_specs=[rows(RW_WIDTH), mb_spec, rows(CV_WIDTH), gate(0), gate(1), gate(2),
                  full(RW_WIDTH), full(MB_WIDTH), full(CV_WIDTH)],
        out_specs=rows(D_MODEL),
        out_shape=jax.ShapeDtypeStruct((m, D_MODEL), BF16),
        compiler_params=_params("parallel"),
        name="branch_merge",
    )(y_rw, y_mb, y_cv, gates, gates, gates, w_rw, w_mb, w_cv)


def _out_proj_kernel(h_ref, x_ref, w_ref, o_ref):
    o_ref[...] = h_ref[...] + _dot(x_ref[...], w_ref[...])


def _out_proj(h, merged, w_out, tm):
    m = h.shape[0]
    rows = pl.BlockSpec((tm, D_MODEL), lambda i: (i, 0))
    return pl.pallas_call(
        _out_proj_kernel,
        grid=(m // tm,),
        in_specs=[rows, rows, pl.BlockSpec((D_MODEL, D_MODEL), lambda i: (0, 0))],
        out_specs=rows,
        out_shape=jax.ShapeDtypeStruct((m, D_MODEL), F32),
        compiler_params=_params("parallel"),
        name="out_proj",
    )(h, merged, w_out)


def _ffn_kernel(h_ref, g_ref, wg_ref, wu_ref, wd_ref, o_ref, xn_ref):
    @pl.when(pl.program_id(1) == 0)
    def _():
        x = h_ref[...]
        ms = jnp.mean(x * x, axis=-1, keepdims=True)
        xn_ref[...] = (x * lax.rsqrt(ms + NORM_EPS) * g_ref[...]).astype(BF16)
        o_ref[...] = x

    xn = xn_ref[...]
    mid = _silu(_dot(xn, wg_ref[...])) * _dot(xn, wu_ref[...])
    o_ref[...] += _dot(mid.astype(BF16), wd_ref[...])


def _ffn(h, g, wg, wu, wd, tm, tf):
    m = h.shape[0]
    d_ff = wg.shape[1]
    rows = pl.BlockSpec((tm, D_MODEL), lambda i, f: (i, 0))
    return pl.pallas_call(
        _ffn_kernel,
        grid=(m // tm, d_ff // tf),
        in_specs=[rows, pl.BlockSpec((1, D_MODEL), lambda i, f: (0, 0)),
                  pl.BlockSpec((D_MODEL, tf), lambda i, f: (0, f)),
                  pl.BlockSpec((D_MODEL, tf), lambda i, f: (0, f)),
                  pl.BlockSpec((tf, D_MODEL), lambda i, f: (f, 0))],
        out_specs=rows,
        out_shape=jax.ShapeDtypeStruct((m, D_MODEL), F32),
        scratch_shapes=[pltpu.VMEM((tm, D_MODEL), BF16)],
        compiler_params=_params("parallel", "arbitrary"),
        name="ffn",
    )(h, g, wg, wu, wd)


def _dot_f32_nt(a, b):
    nt = lambda x, y: lax.dot_general(x, y, _NT, preferred_element_type=F32)
    a_hi, a_mid, a_lo = _split3(a)
    b_hi, b_mid, b_lo = _split3(b)
    return (nt(a_hi, b_hi) + (nt(a_hi, b_mid) + nt(a_mid, b_hi))
            + (nt(a_mid, b_mid) + nt(a_hi, b_lo) + nt(a_lo, b_hi)))


def _router_kernel(h_ref, g_ref, wt_ref, b_ref, before_ref, xn_ref, gate_ref, pos_ref, cnt_ref):
    x = h_ref[...]
    ms = jnp.mean(x * x, axis=-1, keepdims=True)
    xn = x * lax.rsqrt(ms + NORM_EPS) * g_ref[...]
    xn_ref[...] = xn.astype(BF16)
    logits = _dot_f32_nt(wt_ref[...], xn) + b_ref[...]
    idx = lax.broadcasted_iota(jnp.int32, logits.shape, 0).astype(F32)
    m1 = jnp.max(logits, axis=0, keepdims=True)
    i1 = jnp.min(jnp.where(logits == m1, idx, float(N_EXPERTS)), axis=0, keepdims=True)
    rest = jnp.where(idx == i1, -jnp.inf, logits)
    m2 = jnp.max(rest, axis=0, keepdims=True)
    i2 = jnp.min(jnp.where(rest == m2, idx, float(N_EXPERTS)), axis=0, keepdims=True)
    e2 = jnp.exp(m2 - m1)
    gates = jnp.where(idx == i1, 1.0 / (1.0 + e2), 0.0) + jnp.where(idx == i2, e2 / (1.0 + e2), 0.0)
    member = jnp.where(gates > 0.0, 1.0, 0.0)
    gate_ref[0] = gates
    pos_ref[0] = _dot(member.astype(BF16), before_ref[...])
    cnt_ref[0] = jnp.sum(member, axis=-1, keepdims=True).astype(jnp.int32)


def _router(h, g, w_router, b_router, tm):
    m = h.shape[0]
    n_tiles = m // tm
    before = jnp.asarray(np.triu(np.ones((tm, tm), np.float32), 1), BF16)
    tile3 = pl.BlockSpec((1, N_EXPERTS, tm), lambda i: (i, 0, 0))
    per_tile = jax.ShapeDtypeStruct((n_tiles, N_EXPERTS, tm), F32)
    return pl.pallas_call(
        _router_kernel,
        grid=(n_tiles,),
        in_specs=[pl.BlockSpec((tm, D_MODEL), lambda i: (i, 0)),
                  pl.BlockSpec((1, D_MODEL), lambda i: (0, 0)),
                  pl.BlockSpec((N_EXPERTS, D_MODEL), lambda i: (0, 0)),
                  pl.BlockSpec((N_EXPERTS, 1), lambda i: (0, 0)),
                  pl.BlockSpec((tm, tm), lambda i: (0, 0))],
        out_specs=[pl.BlockSpec((tm, D_MODEL), lambda i: (i, 0)), tile3, tile3,
                   pl.BlockSpec((1, N_EXPERTS, 1), lambda i: (i, 0, 0))],
        out_shape=[jax.ShapeDtypeStruct((m, D_MODEL), BF16), per_tile, per_tile,
                   jax.ShapeDtypeStruct((n_tiles, N_EXPERTS, 1), jnp.int32)],
        compiler_params=_params("parallel"),
        name="moe_router",
    )(h, g, w_router.T, b_router.reshape(N_EXPERTS, 1), before)


def _moe_kernel(subs, n_slices, cnt_ref, xn_ref, gate_ref, pos_ref, *refs):
    weights = [refs[3 * j:3 * j + 3] for j in range(MOE_SLICES_PER_STEP)]
    o_ref, xg_ref, gg_ref, acc_ref = refs[3 * MOE_SLICES_PER_STEP:]
    i = pl.program_id(0)
    e = pl.program_id(1)
    f = pl.program_id(2)
    tm = xn_ref.shape[0]
    count = cnt_ref[i * N_EXPERTS + e]
    pos_row = pos_ref[0, pl.ds(e, 1), :]
    gate_row = gate_ref[0, pl.ds(e, 1), :]

    @pl.when((e == 0) & (f == 0))
    def _():
        o_ref[...] = jnp.zeros_like(o_ref)

    def run(sub):
        n_sub = (count + (sub - 1)) // sub
        slot = lax.broadcasted_iota(jnp.int32, (sub, tm), 0)

        def one_hot(s):
            wanted = (slot + s * sub).astype(F32)
            return jnp.where(gate_row > 0.0, jnp.where(pos_row == wanted, 1.0, 0.0), 0.0)

        def rows_of(s):
            return pl.ds(pl.multiple_of(s * sub, sub), sub)

        @pl.when(f == 0)
        def _():
            def pack(s, carry):
                pick = one_hot(s)
                xg_ref[rows_of(s), :] = _dot(pick.astype(BF16), xn_ref[...]).astype(BF16)
                gg_ref[rows_of(s), :] = jnp.sum(pick * gate_row, axis=-1, keepdims=True)
                acc_ref[rows_of(s), :] = jnp.zeros((sub, D_MODEL), F32)
                return carry

            lax.fori_loop(0, n_sub, pack, 0)

        for j, (wg_ref, wu_ref, wd_ref) in enumerate(weights):
            def expert(s, carry, wg_ref=wg_ref, wu_ref=wu_ref, wd_ref=wd_ref):
                x = xg_ref[rows_of(s), :]
                mid = _silu(_dot(x, wg_ref[0])) * _dot(x, wu_ref[0]) * gg_ref[rows_of(s), :]
                acc_ref[rows_of(s), :] += _dot(mid.astype(BF16), wd_ref[0])
                return carry

            lax.fori_loop(0, jnp.where(f * MOE_SLICES_PER_STEP + j < n_slices, n_sub, 0), expert, 0)

        @pl.when(f == pl.num_programs(2) - 1)
        def _():
            def unpack(s, carry):
                pick = one_hot(s).astype(BF16)
                o_ref[...] += lax.dot_general(pick, acc_ref[rows_of(s), :].astype(BF16), _TN,
                                              preferred_element_type=F32)
                return carry

            lax.fori_loop(0, n_sub, unpack, 0)

    small, big = subs

    @pl.when(count <= small)
    def _():
        run(small)

    @pl.when(count > small)
    def _():
        run(big)


def _moe(xn, gates, pos, counts, wg, wu, wd, tm, tf):
    m = xn.shape[0]
    n_e, _, d_ff = wg.shape
    share = tm * TOP_K // N_EXPERTS
    round16 = lambda n: min(tm, -(-n // 16) * 16)
    subs = (round16(share), round16(share * 5 // 4))
    packed = -(-tm // subs[1]) * subs[1]
    n_slices = d_ff // tf
    per_step = MOE_SLICES_PER_STEP
    tile3 = pl.BlockSpec((1, N_EXPERTS, tm), lambda i, e, f, cnt: (i, 0, 0))

    def slice_specs(j):
        fj = lambda f: jnp.minimum(f * per_step + j, n_slices - 1)
        up = pl.BlockSpec((1, D_MODEL, tf), lambda i, e, f, cnt: (e, 0, fj(f)))
        down = pl.BlockSpec((1, tf, D_MODEL), lambda i, e, f, cnt: (e, fj(f), 0))
        return [up, up, down]

    grid_spec = pltpu.PrefetchScalarGridSpec(
        num_scalar_prefetch=1,
        grid=(m // tm, n_e, -(-n_slices // per_step)),
        in_specs=[pl.BlockSpec((tm, D_MODEL), lambda i, e, f, cnt: (i, 0)), tile3, tile3]
        + [spec for j in range(per_step) for spec in slice_specs(j)],
        out_specs=pl.BlockSpec((tm, D_MODEL), lambda i, e, f, cnt: (i, 0)),
        scratch_shapes=[pltpu.VMEM((packed, D_MODEL), BF16), pltpu.VMEM((packed, 1), F32),
                        pltpu.VMEM((packed, D_MODEL), F32)],
    )
    return pl.pallas_call(
        functools.partial(_moe_kernel, subs, n_slices),
        grid_spec=grid_spec,
        out_shape=jax.ShapeDtypeStruct((m, D_MODEL), F32),
        compiler_params=_params("parallel", "arbitrary", "arbitrary"),
        name="moe_experts",
    )(counts.reshape(-1), xn, gates, pos, *([wg, wu, wd] * per_step))


MOE_SLICES_PER_STEP = 2
MOE_FF_SLICE = 256
FFN_FF_SLICE = 512
SAMPLE_CONV_ROWS = 8


def _final_norm_kernel(n_in, *refs):
    g_ref, o_ref = refs[n_in], refs[n_in + 1]
    x = refs[0][...]
    for r in refs[1:n_in]:
        x = x + r[...]
    ms = jnp.mean(x * x, axis=-1, keepdims=True)
    o_ref[...] = x * lax.rsqrt(ms + NORM_EPS) * g_ref[...]


def _final_norm(parts, g, tm):
    m = parts[0].shape[0]
    rows = pl.BlockSpec((tm, D_MODEL), lambda i: (i, 0))
    return pl.pallas_call(
        functools.partial(_final_norm_kernel, len(parts)),
        grid=(m // tm,),
        in_specs=[rows] * len(parts) + [pl.BlockSpec((1, D_MODEL), lambda i: (0, 0))],
        out_specs=rows,
        out_shape=jax.ShapeDtypeStruct((m, D_MODEL), F32),
        compiler_params=_params("parallel"),
        name="final_norm",
    )(*parts, g)


def _add_kernel(a_ref, b_ref, o_ref):
    o_ref[...] = a_ref[...] + b_ref[...]


def _add(a, b, tm):
    rows = pl.BlockSpec((tm, D_MODEL), lambda i: (i, 0))
    return pl.pallas_call(
        _add_kernel,
        grid=(a.shape[0] // tm,),
        in_specs=[rows, rows],
        out_specs=rows,
        out_shape=jax.ShapeDtypeStruct(a.shape, F32),
        compiler_params=_params("parallel"),
        name="residual_add",
    )(a, b)


def _pad_rw_cols(x, axis=-1):
    x = jnp.moveaxis(x, axis, 0)
    pad = jnp.zeros((RW_PAD - RW_COLS,) + x.shape[1:], x.dtype)
    return jnp.moveaxis(jnp.concatenate([x[:RW_COLS], pad, x[RW_COLS:]], axis=0), 0, axis)


def _stage_w_in_kernel(w_ref, o_ref):
    o_ref[...] = w_ref[0].astype(BF16)


def _stage_w_in(w_in, layer):
    rows = 512
    assert RW_COLS // rows == (RW_PAD - 1) // rows and RW_PAD % rows == 0 and (N_PROJ - RW_PAD) % rows == 0
    shift = RW_PAD - RW_COLS

    def src_row(j):
        assert rows % 8 == 0 and shift % 8 == 0
        return 8 * jnp.where(j * rows < RW_PAD, j * (rows // 8), j * (rows // 8) - shift // 8)

    return pl.pallas_call(
        _stage_w_in_kernel,
        grid=(N_PROJ // rows,),
        in_specs=[pl.BlockSpec((pl.Element(1), pl.Element(rows), pl.Element(D_MODEL)),
                               lambda j: (layer, src_row(j), 0))],
        out_specs=pl.BlockSpec((rows, D_MODEL), lambda j: (j, 0)),
        out_shape=jax.ShapeDtypeStruct((N_PROJ, D_MODEL), BF16),
        compiler_params=_params("parallel"),
        name="stage_w_in",
    )(w_in.transpose(0, 2, 1))


def _lora_slab(w, row0):
    slab = jnp.zeros((LORA_PAD, RW_WIDTH), F32)
    return lax.dynamic_update_slice(slab, w, (row0, 0)).astype(BF16)


def _layer_params(l, P):
    row = lambda x: x.reshape(1, -1)
    lp = dict(
        norm_mix=row(P['norm_mix'][l]), norm_ffn=row(P['norm_ffn'][l]),
        w_in_t=_stage_w_in(P['w_in'], l),
        rw_vecs=[row(_pad_rw_cols(P['rw_mu'][l])), row(P['rw_w0'][l]), row(P['rw_a0'][l]),
                 row(P['rw_kk'][l]), row(P['rw_ka'][l]), row(P['rw_rk'][l])],
        rw_mats=[_lora_slab(P['rw_w2'][l], 0), _lora_slab(P['rw_a2'][l], DECAY_LORA),
                 _lora_slab(P['rw_g2'][l], DECAY_LORA + AAA_LORA)],
        rw_ln_w=row(P['rw_ln_w'][l]), rw_ln_b=row(P['rw_ln_b'][l]),
        cv_w=P['cv_w'][l], cv_b=row(P['cv_b'][l]), cv_ln_w=row(P['cv_ln_w'][l]), cv_ln_b=row(P['cv_ln_b'][l]),
        w_br_rw=P['w_br_rw'][l].astype(BF16), w_br_mb=P['w_br_mb'][l].astype(BF16),
        w_br_cv=P['w_br_cv'][l].astype(BF16), w_out=P['w_out'][l].astype(BF16),
    )
    i = l // 2
    if l % 2 == 0:
        lp.update(wg=P['ffn_wg'][i].astype(BF16), wu=P['ffn_wu'][i].astype(BF16), wd=P['ffn_wd'][i].astype(BF16))
    else:
        lp.update(wg=P['moe_wg'][i].astype(BF16), wu=P['moe_wu'][i].astype(BF16), wd=P['moe_wd'][i].astype(BF16),
                  router=P['moe_router'][i], router_b=P['moe_router_b'][i])
    return lp


def _channel_mix(h, lp, tm_dense, tm_moe):
    if 'router' in lp:
        xn, gates, pos, counts = _router(h, lp['norm_ffn'], lp['router'], lp['router_b'], tm_moe)
        return [h, _moe(xn, gates, pos, counts, lp['wg'], lp['wu'], lp['wd'], tm_moe, MOE_FF_SLICE)]
    return [_ffn(h, lp['norm_ffn'], lp['wg'], lp['wu'], lp['wd'], tm_dense, FFN_FF_SLICE)]


def _as_one(parts, tm):
    return parts[0] if len(parts) == 1 else _add(parts[0], parts[1], tm)


PROMPT_ROWS = dict(proj=1024, qkv=1024, prep=256, conv=512, merge=512, out_proj=512, ffn=1024, moe=1024, norm=512)


def _run_prompt(x, layers, norm_final, rel_bias):
    t = x.shape[0]
    tm = PROMPT_ROWS
    parts = [x]
    outs = dict(k=[], v=[], wkv=[], shift=[], conv=[])
    for lp in layers:
        h = _as_one(parts, tm['norm'])
        proj, gates = _norm_proj(h, lp['norm_mix'], lp['w_in_t'], tm['proj'])
        *ops, g, bonus = _rwkv_prep(proj, None, lp['rw_vecs'], lp['rw_mats'], tm['prep'])
        y_rw, s_final = _rwkv_chunked(ops, bonus, g, lp['rw_ln_w'], lp['rw_ln_b'])
        q_t, k_t, v_t, k_tb, v_tb, km_tiles = _qkv_t(h, lp['norm_mix'], lp['w_in_t'], tm['qkv'])
        y_mb_t = _moba_seq(rel_bias, q_t, k_tb, v_tb, km_tiles)
        y_cv, tail = _conv_seq(proj, lp['cv_w'], lp['cv_b'], lp['cv_ln_w'], lp['cv_ln_b'], tm['conv'])
        merged = _merge(y_rw, y_mb_t, y_cv, gates, lp['w_br_rw'], lp['w_br_mb'], lp['w_br_cv'], tm['merge'], True)
        h = _out_proj(h, merged, lp['w_out'], tm['out_proj'])
        parts = _channel_mix(h, lp, tm['ffn'], tm['moe'])
        tokens_major = lambda x_t: x_t.reshape(MB_HEADS, MB_HEAD_DIM, t).transpose(2, 0, 1)[None]
        outs['k'].append(tokens_major(k_t))
        outs['v'].append(tokens_major(v_t))
        outs['wkv'].append(s_final[None])
        outs['shift'].append(proj[t - 1:t, :RW_COLS])
        outs['conv'].append(tail[None, CV_HALO - (CV_K - 1):, :])
    y = _final_norm(parts, norm_final, tm['norm'])
    return (y[None],) + tuple(jnp.stack(outs[n]) for n in ('k', 'v', 'wkv', 'shift', 'conv'))


def _run_sample(x, layers, norm_final, rel_bias, cache_k, cache_v, state_wkv, state_shift, state_conv, page_table):
    b = x.shape[0]
    parts = [x]
    wkv_t = state_wkv.transpose(0, 2, 3, 4, 1)
    new_wkv_t = None
    outs = dict(k=[], v=[], shift=[], conv=[])
    for l, lp in enumerate(layers):
        h = _as_one(parts, b)
        proj, gates = _norm_proj(h, lp['norm_mix'], lp['w_in_t'], b)
        *ops, g, bonus = _rwkv_prep(proj, _pad_rw_cols(state_shift[l]), lp['rw_vecs'], lp['rw_mats'], b)
        y_t, new_wkv_t = _rwkv_single_step(ops, wkv_t, l, new_wkv_t)
        y_rw = _rwkv_post(y_t, bonus, g, lp['rw_ln_w'], lp['rw_ln_b'])
        q = proj[:, RW_PAD:RW_PAD + MB_WIDTH]
        k_new = proj[:, RW_PAD + MB_WIDTH:RW_PAD + 2 * MB_WIDTH]
        v_new = proj[:, RW_PAD + 2 * MB_WIDTH:RW_PAD + 3 * MB_WIDTH]
        y_mb = _moba_paged(rel_bias, q, k_new, v_new, cache_k, cache_v, l, page_table)
        y_cv, conv_new = _conv_step(proj, state_conv[l], lp['cv_w'], lp['cv_b'], lp['cv_ln_w'], lp['cv_ln_b'],
                                    SAMPLE_CONV_ROWS)
        merged = _merge(y_rw, y_mb, y_cv, gates, lp['w_br_rw'], lp['w_br_mb'], lp['w_br_cv'], b, False)
        h = _out_proj(h, merged, lp['w_out'], b)
        parts = _channel_mix(h, lp, b, b)
        outs['k'].append(k_new.reshape(b, 1, MB_HEADS, MB_HEAD_DIM))
        outs['v'].append(v_new.reshape(b, 1, MB_HEADS, MB_HEAD_DIM))
        outs['shift'].append(proj[:, :RW_COLS])
        outs['conv'].append(conv_new)
    y = _final_norm(parts, norm_final, b)
    stacked = {n: jnp.stack(v) for n, v in outs.items()}
    return (y[:, None, :], stacked['k'], stacked['v'], new_wkv_t.transpose(0, 4, 1, 2, 3), stacked['shift'],
            stacked['conv'])


def kernel(x_prompt, x_sample, cache_k, cache_v, state_wkv, state_shift, state_conv, page_table, norm_mix, norm_ffn, norm_final, w_in, rw_mu, rw_w0, rw_w2, rw_a0, rw_a2, rw_g2, rw_kk, rw_ka, rw_rk, rw_ln_w, rw_ln_b, rel_bias, cv_w, cv_b, cv_ln_w, cv_ln_b, w_br_rw, w_br_mb, w_br_cv, w_out, ffn_wg, ffn_wu, ffn_wd, moe_router, moe_router_b, moe_wg, moe_wu, moe_wd):
    P = dict(norm_mix=norm_mix, norm_ffn=norm_ffn, w_in=w_in,
             rw_mu=rw_mu, rw_w0=rw_w0, rw_w2=rw_w2, rw_a0=rw_a0, rw_a2=rw_a2, rw_g2=rw_g2,
             rw_kk=rw_kk, rw_ka=rw_ka, rw_rk=rw_rk, rw_ln_w=rw_ln_w, rw_ln_b=rw_ln_b,
             cv_w=cv_w, cv_b=cv_b, cv_ln_w=cv_ln_w, cv_ln_b=cv_ln_b,
             w_br_rw=w_br_rw, w_br_mb=w_br_mb, w_br_cv=w_br_cv, w_out=w_out,
             ffn_wg=ffn_wg, ffn_wu=ffn_wu, ffn_wd=ffn_wd,
             moe_router=moe_router, moe_router_b=moe_router_b, moe_wg=moe_wg, moe_wu=moe_wu, moe_wd=moe_wd)
    depth = w_in.shape[0]
    layers = [_layer_params(l, P) for l in range(depth)]
    g_final = norm_final.reshape(1, -1)
    assert x_prompt.shape[0] == 1 and x_sample.shape[1] == 1
    prompt = _run_prompt(x_prompt[0], layers, g_final, rel_bias)
    sample = _run_sample(x_sample[:, 0, :], layers, g_final, rel_bias, cache_k, cache_v,
                         state_wkv, state_shift, state_conv, page_table)
    y_p, k_p, v_p, wkv_p, shift_p, conv_p = prompt
    y_s, k_s, v_s, wkv_s, shift_s, conv_s = sample
    return (y_p, y_s, k_p, v_p, wkv_p, shift_p, conv_p,
            k_s, v_s, wkv_s, shift_s, conv_s)
```
